```python
import jax, jax.numpy as jnp
from jax import lax
import numpy as np

D_MODEL = 1024
BATCH = 4
SEQ = 4096
DEPTH = 1
DEC_BATCH = 128
DEC_SEQ = 1
PAST_LEN = 16384
PAGE_SIZE = 128

MIX_WIDTH = D_MODEL
N_ATT_HEADS = 8
QK_NOPE = 64
QK_ROPE = 32
V_HEAD = 64
ATT_WIDTH = N_ATT_HEADS * V_HEAD
Q_LORA = 384
KV_LORA = 256
ROPE_THETA = 10000.0
SM_SCALE = (QK_NOPE + QK_ROPE) ** -0.5
REC_WIDTH = MIX_WIDTH - ATT_WIDTH
REC_BLOCKS = 8
REC_BLOCK_W = REC_WIDTH // REC_BLOCKS
CONV_W = 4
LRU_C = 8.0
N_GROUPS = 4
EXPERTS_PER_GROUP = 8
N_EXPERTS = N_GROUPS * EXPERTS_PER_GROUP
TOP_K = 2
D_EXPERT = 256
Q_BLOCK = 128
ALPHA = (2.0 * DEPTH) ** 0.25
BETA = (8.0 * DEPTH) ** -0.25
LN_EPS = 1e-5
RMS_EPS = 1e-6
NEG_INF = -1e30
IN_WIDTH = Q_LORA + KV_LORA + QK_ROPE + 2 * REC_WIDTH
_SPLITS = (Q_LORA, Q_LORA + KV_LORA, Q_LORA + KV_LORA + QK_ROPE, Q_LORA + KV_LORA + QK_ROPE + REC_WIDTH)

kernel_name = "hymba_mla_rglru_hmoe_deepnorm_step"


def _rmsnorm(x, g):
    xf = x.astype(jnp.float32)
    y = xf * lax.rsqrt(jnp.mean(xf * xf, axis=-1, keepdims=True) + RMS_EPS)
    return (y * g.astype(jnp.float32)).astype(x.dtype)


def _layernorm(x, g, b):
    xf = x.astype(jnp.float32)
    mu = jnp.mean(xf, axis=-1, keepdims=True)
    xc = xf - mu
    var = jnp.mean(xc * xc, axis=-1, keepdims=True)
    return (xc * lax.rsqrt(var + LN_EPS) * g.astype(jnp.float32) + b.astype(jnp.float32)).astype(x.dtype)


def _rope_tables(pos):
    half = QK_ROPE // 2
    inv = ROPE_THETA ** (-(jnp.arange(half, dtype=jnp.float32) * 2.0 / QK_ROPE))
    ang = pos.astype(jnp.float32)[:, None] * inv[None, :]
    return jnp.cos(ang), jnp.sin(ang)


def _rope(x, cos, sin):
    x1, x2 = jnp.split(x.astype(jnp.float32), 2, axis=-1)
    return jnp.concatenate([x1 * cos - x2 * sin, x2 * cos + x1 * sin], axis=-1).astype(x.dtype)


def _project(x, pos, w_in, q_norm_g, w_uq, kv_norm_g):
    z = jnp.einsum('btd,de->bte', x, w_in)
    q_lat, kv_lat, k_pe, rec_x, rec_gate = jnp.split(z, _SPLITS, axis=-1)
    q = jnp.einsum('btq,qhe->bthe', _rmsnorm(q_lat, q_norm_g), w_uq)
    q_nope, q_pe = q[..., :QK_NOPE], q[..., QK_NOPE:]
    cos, sin = _rope_tables(pos)
    q_pe = _rope(q_pe, cos[None, :, None, :], sin[None, :, None, :])
    k_pe = _rope(k_pe, cos[None], sin[None])
    c_kv = _rmsnorm(kv_lat, kv_norm_g)
    return q_nope, q_pe, c_kv, k_pe, rec_x, rec_gate


def _mla_prompt(q_nope, q_pe, c_kv, k_pe, w_uk, w_uv):
    B, S = c_kv.shape[0], c_kv.shape[1]
    k_nope = jnp.einsum('bsc,chn->bshn', c_kv, w_uk)
    v = jnp.einsum('bsc,chv->bshv', c_kv, w_uv)
    nb = S // Q_BLOCK
    qn = q_nope.reshape(B, nb, Q_BLOCK, N_ATT_HEADS, QK_NOPE).transpose(1, 0, 2, 3, 4)
    qp = q_pe.reshape(B, nb, Q_BLOCK, N_ATT_HEADS, QK_ROPE).transpose(1, 0, 2, 3, 4)
    kpos = jnp.arange(S)

    def block(args):
        qn_b, qp_b, i = args
        s = (jnp.einsum('bqhn,bkhn->bhqk', qn_b, k_nope)
             + jnp.einsum('bqhr,bkr->bhqk', qp_b, k_pe)).astype(jnp.float32) * SM_SCALE
        qpos = i * Q_BLOCK + jnp.arange(Q_BLOCK)
        s = jnp.where(kpos[None, :] <= qpos[:, None], s, NEG_INF)
        p = jax.nn.softmax(s, axis=-1).astype(v.dtype)
        return jnp.einsum('bhqk,bkhv->bqhv', p, v)

    o = lax.map(block, (qn, qp, jnp.arange(nb)))
    return o.transpose(1, 0, 2, 3, 4).reshape(B, S, N_ATT_HEADS * V_HEAD)


def _mla_sample(q_nope, q_pe, c_new, kpe_new, cache_kv_latent, cache_k_rope, page_table, w_uk, w_uv):
    Bd, T = c_new.shape[0], c_new.shape[1]
    c_past = cache_kv_latent[page_table].reshape(Bd, -1, KV_LORA)
    kpe_past = cache_k_rope[page_table].reshape(Bd, -1, QK_ROPE)
    P = c_past.shape[1]
    q_lat = jnp.einsum('bthn,chn->bthc', q_nope, w_uk)
    s_past = (jnp.einsum('bthc,bpc->bhtp', q_lat, c_past)
              + jnp.einsum('bthr,bpr->bhtp', q_pe, kpe_past)).astype(jnp.float32)
    s_new = (jnp.einsum('bthc,buc->bhtu', q_lat, c_new)
             + jnp.einsum('bthr,bur->bhtu', q_pe, kpe_new)).astype(jnp.float32)
    causal = jnp.arange(T)[None, :] <= jnp.arange(T)[:, None]
    s_new = jnp.where(causal, s_new, NEG_INF)
    p = jax.nn.softmax(jnp.concatenate([s_past, s_new], axis=-1) * SM_SCALE, axis=-1)
    p_past = p[..., :P].astype(c_past.dtype)
    p_new = p[..., P:].astype(c_new.dtype)
    o_lat = (jnp.einsum('bhtp,bpc->bthc', p_past, c_past)
             + jnp.einsum('bhtu,buc->bthc', p_new, c_new))
    o = jnp.einsum('bthc,chv->bthv', o_lat, w_uv)
    return o.reshape(Bd, T, N_ATT_HEADS * V_HEAD)


def _rglru_block(rec_x, rec_gate, conv_state, h0, conv_w, conv_b, w_rg, b_rg, w_ig, b_ig, lru_lambda):
    B, T = rec_x.shape[0], rec_x.shape[1]
    xp = jnp.concatenate([conv_state.astype(rec_x.dtype), rec_x], axis=1)
    xc = conv_b + sum(xp[:, k:k + T] * conv_w[k] for k in range(CONV_W))
    new_conv = xp[:, T:]
    xb = xc.reshape(B, T, REC_BLOCKS, REC_BLOCK_W)
    r = jax.nn.sigmoid((jnp.einsum('btnd,nde->btne', xb, w_rg).reshape(B, T, REC_WIDTH) + b_rg).astype(jnp.float32))
    i = jax.nn.sigmoid((jnp.einsum('btnd,nde->btne', xb, w_ig).reshape(B, T, REC_WIDTH) + b_ig).astype(jnp.float32))
    log_a = -LRU_C * jax.nn.softplus(-lru_lambda.astype(jnp.float32)) * r
    a = jnp.exp(log_a)
    u = jnp.sqrt(-jnp.expm1(2.0 * log_a)) * (i * xc.astype(jnp.float32))

    def step(h, au):
        a_t, u_t = au
        h = a_t * h + u_t
        return h, h

    h_last, hs = lax.scan(step, h0.astype(jnp.float32), (jnp.swapaxes(a, 0, 1), jnp.swapaxes(u, 0, 1)))
    y = jnp.swapaxes(hs, 0, 1) * jax.nn.gelu(rec_gate.astype(jnp.float32))
    return y.astype(rec_x.dtype), new_conv, h_last.astype(h0.dtype)


def _hier_moe(x, w_group, b_group, w_expert, b_expert, w_gate_up, w_down):
    B, T, D = x.shape
    xt = x.reshape(B * T, D)
    n = xt.shape[0]
    rows = jnp.arange(n)
    g_logits = jnp.einsum('nd,dg->ng', xt, w_group).astype(jnp.float32)
    g_prob = jax.nn.softmax(g_logits, axis=-1)
    g_sel = jnp.argmax(g_logits + b_group.astype(jnp.float32), axis=-1)
    e_logits = jnp.einsum('nd,de->ne', xt, w_expert).astype(jnp.float32).reshape(n, N_GROUPS, EXPERTS_PER_GROUP)
    e_sel = e_logits[rows, g_sel]
    e_bias = b_expert.astype(jnp.float32).reshape(N_GROUPS, EXPERTS_PER_GROUP)[g_sel]
    _, top_i = lax.top_k(e_sel + e_bias, TOP_K)
    top_p = jax.nn.softmax(jnp.take_along_axis(e_sel, top_i, axis=-1), axis=-1)
    gates_k = g_prob[rows, g_sel][:, None] * top_p
    expert_idx = g_sel[:, None] * EXPERTS_PER_GROUP + top_i
    gate = jnp.sum(jax.nn.one_hot(expert_idx, N_EXPERTS, dtype=jnp.float32) * gates_k[..., None], axis=1)
    y = jnp.zeros((n, D), jnp.float32)
    for e in range(N_EXPERTS):
        gu = xt @ w_gate_up[e]
        h = jax.nn.silu(gu[:, :D_EXPERT]) * gu[:, D_EXPERT:]
        y = y + gate[:, e:e + 1] * (h @ w_down[e]).astype(jnp.float32)
    return y.astype(x.dtype).reshape(B, T, D)


def _merge_and_ffn(x, att_o, rec_o, att_out_g, rec_out_g, w_out, ln1_g, ln1_b,
                   w_group, b_group, w_expert, b_expert, w_gate_up, w_down, ln2_g, ln2_b):
    mixed = jnp.concatenate([_rmsnorm(att_o, att_out_g), _rmsnorm(rec_o, rec_out_g)], axis=-1)
    mix = jnp.einsum('bte,ed->btd', mixed, w_out)
    x1 = _layernorm(ALPHA * x + mix, ln1_g, ln1_b)
    ffn = _hier_moe(x1, w_group, b_group, w_expert, b_expert, w_gate_up, w_down)
    return _layernorm(ALPHA * x1 + ffn, ln2_g, ln2_b)


def setup_inputs(seed: int = 0) -> dict:
    key = jax.random.key(seed)
    ks = jax.random.split(key, 40)
    f32 = jnp.float32
    n_pages = PAST_LEN // PAGE_SIZE
    n_used = DEC_BATCH * n_pages
    n_pool = (n_used * 5) // 4

    def nrm(k, shape, scale):
        return jax.random.normal(k, shape, f32) * scale

    u = jax.random.uniform(ks[19], (REC_WIDTH,), f32, 0.9, 0.999)
    a0 = u ** (1.0 / LRU_C)
    return {
        'x_prompt': nrm(ks[0], (BATCH, SEQ, D_MODEL), 1.0),
        'x_sample': nrm(ks[1], (DEC_BATCH, DEC_SEQ, D_MODEL), 1.0),
        'cache_kv_latent': nrm(ks[2], (n_pool, PAGE_SIZE, KV_LORA), 1.0),
        'cache_k_rope': nrm(ks[3], (n_pool, PAGE_SIZE, QK_ROPE), 1.0),
        'state_conv': nrm(ks[4], (DEC_BATCH, CONV_W - 1, REC_WIDTH), 1.0),
        'state_rec': nrm(ks[5], (DEC_BATCH, REC_WIDTH), 0.5),
        'page_table': jax.random.permutation(ks[6], n_pool)[:n_used].reshape(DEC_BATCH, n_pages).astype(jnp.int32),
        'w_in': nrm(ks[7], (D_MODEL, IN_WIDTH), D_MODEL ** -0.5),
        'q_norm_g': 1.0 + nrm(ks[8], (Q_LORA,), 0.01),
        'w_uq': nrm(ks[9], (Q_LORA, N_ATT_HEADS, QK_NOPE + QK_ROPE), Q_LORA ** -0.5),
        'kv_norm_g': 1.0 + nrm(ks[10], (KV_LORA,), 0.01),
        'w_uk': nrm(ks[11], (KV_LORA, N_ATT_HEADS, QK_NOPE), KV_LORA ** -0.5),
        'w_uv': nrm(ks[12], (KV_LORA, N_ATT_HEADS, V_HEAD), KV_LORA ** -0.5 * BETA),
        'conv_w': nrm(ks[13], (CONV_W, REC_WIDTH), CONV_W ** -0.5),
        'conv_b': nrm(ks[14], (REC_WIDTH,), 0.01),
        'w_rg': nrm(ks[15], (REC_BLOCKS, REC_BLOCK_W, REC_BLOCK_W), REC_BLOCK_W ** -0.5),
        'b_rg': nrm(ks[16], (REC_WIDTH,), 0.01),
        'w_ig': nrm(ks[17], (REC_BLOCKS, REC_BLOCK_W, REC_BLOCK_W), REC_BLOCK_W ** -0.5),
        'b_ig': nrm(ks[18], (REC_WIDTH,), 0.01),
        'lru_lambda': jnp.log(a0) - jnp.log1p(-a0),
        'att_out_g': 1.0 + nrm(ks[20], (ATT_WIDTH,), 0.01),
        'rec_out_g': 1.0 + nrm(ks[21], (REC_WIDTH,), 0.01),
        'w_out': nrm(ks[22], (MIX_WIDTH, D_MODEL), MIX_WIDTH ** -0.5 * BETA),
        'ln1_g': 1.0 + nrm(ks[23], (D_MODEL,), 0.01),
        'ln1_b': nrm(ks[24], (D_MODEL,), 0.01),
        'w_group': nrm(ks[25], (D_MODEL, N_GROUPS), D_MODEL ** -0.5),
        'b_group': nrm(ks[26], (N_GROUPS,), 0.01),
        'w_expert': nrm(ks[27], (D_MODEL, N_EXPERTS), D_MODEL ** -0.5),
        'b_expert': nrm(ks[28], (N_EXPERTS,), 0.01),
        'w_gate_up': nrm(ks[29], (N_EXPERTS, D_MODEL, 2 * D_EXPERT), D_MODEL ** -0.5 * BETA),
        'w_down': nrm(ks[30], (N_EXPERTS, D_EXPERT, D_MODEL), D_EXPERT ** -0.5 * BETA),
        'ln2_g': 1.0 + nrm(ks[31], (D_MODEL,), 0.01),
        'ln2_b': nrm(ks[32], (D_MODEL,), 0.01),
    }


def reference(x_prompt, x_sample, cache_kv_latent, cache_k_rope, state_conv, state_rec, page_table,
              w_in, q_norm_g, w_uq, kv_norm_g, w_uk, w_uv, conv_w, conv_b, w_rg, b_rg, w_ig, b_ig,
              lru_lambda, att_out_g, rec_out_g, w_out, ln1_g, ln1_b, w_group, b_group, w_expert,
              b_expert, w_gate_up, w_down, ln2_g, ln2_b):
    past_len = page_table.shape[1] * PAGE_SIZE
    y_p, y_s = x_prompt, x_sample
    for _layer in range(DEPTH):
        Bp, Sp = y_p.shape[0], y_p.shape[1]
        pos_p = jnp.arange(Sp, dtype=jnp.int32)
        qn, qp, c_p, kpe_p, rx, rg = _project(y_p, pos_p, w_in, q_norm_g, w_uq, kv_norm_g)
        att_p = _mla_prompt(qn, qp, c_p, kpe_p, w_uk, w_uv)
        conv0 = jnp.zeros((Bp, CONV_W - 1, REC_WIDTH), rx.dtype)
        h0 = jnp.zeros((Bp, REC_WIDTH), state_rec.dtype)
        rec_p, conv_p, h_p = _rglru_block(rx, rg, conv0, h0, conv_w, conv_b, w_rg, b_rg, w_ig, b_ig, lru_lambda)
        y_p = _merge_and_ffn(y_p, att_p, rec_p, att_out_g, rec_out_g, w_out, ln1_g, ln1_b,
                             w_group, b_group, w_expert, b_expert, w_gate_up, w_down, ln2_g, ln2_b)
        pos_s = past_len + jnp.arange(y_s.shape[1], dtype=jnp.int32)
        qn, qp, c_s, kpe_s, rx, rg = _project(y_s, pos_s, w_in, q_norm_g, w_uq, kv_norm_g)
        att_s = _mla_sample(qn, qp, c_s, kpe_s, cache_kv_latent, cache_k_rope, page_table, w_uk, w_uv)
        rec_s, conv_s, h_s = _rglru_block(rx, rg, state_conv, state_rec, conv_w, conv_b, w_rg, b_rg, w_ig, b_ig, lru_lambda)
        y_s = _merge_and_ffn(y_s, att_s, rec_s, att_out_g, rec_out_g, w_out, ln1_g, ln1_b,
                             w_group, b_group, w_expert, b_expert, w_gate_up, w_down, ln2_g, ln2_b)
    return (y_p, y_s, c_p, kpe_p, conv_p, h_p, c_s, kpe_s, conv_s, h_s)
```

```python
import functools
import math

import jax
import jax.numpy as jnp
from jax import lax
from jax.experimental import pallas as pl
from jax.experimental.pallas import tpu as pltpu

F32 = jnp.float32
BF16 = jnp.bfloat16
I32 = jnp.int32

N_HEADS = 8
QK_NOPE = 64
QK_ROPE = 32
V_HEAD = 64
Q_LORA = 384
KV_LORA = 256
ROPE_THETA = 10000.0
SM_SCALE = (QK_NOPE + QK_ROPE) ** -0.5
REC_WIDTH = 512
REC_BLOCKS = 8
REC_BLOCK_W = REC_WIDTH // REC_BLOCKS
CONV_W = 4
LRU_C = 8.0
N_GROUPS = 4
EXPERTS_PER_GROUP = 8
D_EXPERT = 256
DEPTH = 1
ALPHA = (2.0 * DEPTH) ** 0.25
LN_EPS = 1e-5
RMS_EPS = 1e-6
NEG_INF = -1e30
PAGE_SIZE = 128

LANE = 128
SUBLANE = 8
HEAD_PAD = LANE
N_PAIRS = N_HEADS * V_HEAD // LANE
VMEM_LIMIT = 56 * 1024 * 1024

TM_PROJ = 512
TM_LRU = 256
TQ_ATT = 512
TK_ATT = 512
TM_MERGE = 512
TM_MOE = 1024
CH_MOE = 128
PAGES_PER_CHUNK = 16


def _dot(a, b):
    return jnp.dot(a, b, preferred_element_type=F32)


def _dot_nt(a, b, precision=None):
    return lax.dot_general(a, b, (((1,), (1,)), ((), ())), preferred_element_type=F32, precision=precision)


def _dot_tn(a, b):
    return lax.dot_general(a, b, (((0,), (0,)), ((), ())), preferred_element_type=F32)


def _cparams(semantics):
    return pltpu.CompilerParams(dimension_semantics=semantics, vmem_limit_bytes=VMEM_LIMIT)


def _rmsnorm(x, g):
    return x * lax.rsqrt(jnp.mean(x * x, axis=-1, keepdims=True) + RMS_EPS) * g


def _layernorm(x, g, b):
    mu = jnp.mean(x, axis=-1, keepdims=True)
    xc = x - mu
    var = jnp.mean(xc * xc, axis=-1, keepdims=True)
    return xc * lax.rsqrt(var + LN_EPS) * g + b


def _gelu_tanh(x):
    return x * (0.5 * (1.0 + jnp.tanh(math.sqrt(2.0 / math.pi) * (x + 0.044715 * (x * x * x)))))


def _lru_coeffs(xc, pre_r, pre_i, b_r, b_i, lam):
    r = jax.nn.sigmoid(pre_r + b_r)
    i = jax.nn.sigmoid(pre_i + b_i)
    neg_lam = -lam
    softplus = jnp.maximum(neg_lam, 0.0) + jnp.log1p(jnp.exp(-jnp.abs(neg_lam)))
    log_a = (-LRU_C * softplus) * r
    a = jnp.exp(log_a)
    u = jnp.sqrt(-jnp.tanh(log_a) * (a * a + 1.0)) * (i * xc)
    return a, u


def _gate_preacts(xc, wg_ref):
    half = REC_WIDTH // 2
    g0 = _dot(xc[:, :half].astype(BF16), wg_ref[0])
    g1 = _dot(xc[:, half:].astype(BF16), wg_ref[1])
    pre_r = jnp.concatenate([g0[:, :half], g1[:, :half]], axis=1)
    pre_i = jnp.concatenate([g0[:, half:], g1[:, half:]], axis=1)
    return pre_r, pre_i


def _proj_kernel(x_ref, cos_ref, sin_ref, win_ref, qg_ref, wuq_ref, kvg_ref, wukv_ref,
                 q_ref, k_ref, v_ref, ckv_ref, kpe_ref, rx_ref, rg_ref):
    x = x_ref[0].astype(BF16)
    z = _dot(x, win_ref[...])
    o_kv = Q_LORA
    o_ka = o_kv + KV_LORA
    o_kb = o_ka + LANE
    o_rx = o_kb + LANE
    o_rg = o_rx + REC_WIDTH
    cos = cos_ref[...]
    sin = sin_ref[...]
    qn = _rmsnorm(z[:, :o_kv], qg_ref[...])
    qq = _dot(qn.astype(BF16), wuq_ref[...])
    sw = N_HEADS * HEAD_PAD
    for h in range(N_HEADS):
        lo = h * HEAD_PAD
        q_ref[0, h] = (qq[:, lo:lo + HEAD_PAD] * cos + qq[:, sw + lo:sw + lo + HEAD_PAD] * sin).astype(BF16)
    ckv = _rmsnorm(z[:, o_kv:o_ka], kvg_ref[...])
    ckv_ref[0] = ckv
    kpe = z[:, o_ka:o_kb] * cos + z[:, o_kb:o_rx] * sin
    kpe_ref[0] = kpe
    kv = _dot(ckv.astype(BF16), wukv_ref[...])
    for h in range(N_HEADS):
        lo = h * HEAD_PAD
        k_ref[0, h] = (kv[:, lo:lo + HEAD_PAD] + kpe).astype(BF16)
    for p in range(N_PAIRS):
        lo = sw + p * LANE
        v_ref[0, p] = kv[:, lo:lo + LANE].astype(BF16)
    rx_ref[0] = z[:, o_rx:o_rg]
    rg_ref[0] = z[:, o_rg:o_rg + REC_WIDTH]


def _project(x, cos_t, sin_t, wts, tm):
    b, s, d = x.shape
    in_w = wts["w_in"].shape[1]
    full = lambda shape: pl.BlockSpec(shape, lambda bi, si: (0,) * len(shape))
    return pl.pallas_call(
        _proj_kernel,
        grid=(b, s // tm),
        in_specs=[
            pl.BlockSpec((1, tm, d), lambda bi, si: (bi, si, 0)),
            pl.BlockSpec((tm, LANE), lambda bi, si: (si, 0)),
            pl.BlockSpec((tm, LANE), lambda bi, si: (si, 0)),
            full((d, in_w)),
            full((1, Q_LORA)),
            full((Q_LORA, 2 * N_HEADS * HEAD_PAD)),
            full((1, KV_LORA)),
            full((KV_LORA, N_HEADS * HEAD_PAD + N_PAIRS * LANE)),
        ],
        out_specs=[
            pl.BlockSpec((1, N_HEADS, tm, HEAD_PAD), lambda bi, si: (bi, 0, si, 0)),
            pl.BlockSpec((1, N_HEADS, tm, HEAD_PAD), lambda bi, si: (bi, 0, si, 0)),
            pl.BlockSpec((1, N_PAIRS, tm, LANE), lambda bi, si: (bi, 0, si, 0)),
            pl.BlockSpec((1, tm, KV_LORA), lambda bi, si: (bi, si, 0)),
            pl.BlockSpec((1, tm, LANE), lambda bi, si: (bi, si, 0)),
            pl.BlockSpec((1, tm, REC_WIDTH), lambda bi, si: (bi, si, 0)),
            pl.BlockSpec((1, tm, REC_WIDTH), lambda bi, si: (bi, si, 0)),
        ],
        out_shape=[
            jax.ShapeDtypeStruct((b, N_HEADS, s, HEAD_PAD), BF16),
            jax.ShapeDtypeStruct((b, N_HEADS, s, HEAD_PAD), BF16),
            jax.ShapeDtypeStruct((b, N_PAIRS, s, LANE), BF16),
            jax.ShapeDtypeStruct((b, s, KV_LORA), F32),
            jax.ShapeDtypeStruct((b, s, LANE), F32),
            jax.ShapeDtypeStruct((b, s, REC_WIDTH), F32),
            jax.ShapeDtypeStruct((b, s, REC_WIDTH), F32),
        ],
        compiler_params=_cparams(("parallel", "parallel")),
        name="proj",
    )(x, cos_t, sin_t, wts["w_in"], wts["q_norm_g"], wts["w_uq"], wts["kv_norm_g"], wts["w_ukv"])


def _lru_prompt_kernel(rx_ref, rg_ref, cw_ref, cb_ref, wg_ref, br_ref, bi_ref, lam_ref,
                       y_ref, hlast_ref, xp_ref, h_ref):
    tm = rx_ref.shape[1]
    si = pl.program_id(1)

    @pl.when(si == 0)
    def _():
        xp_ref[0:SUBLANE, :] = jnp.zeros((SUBLANE, REC_WIDTH), F32)
        h_ref[...] = jnp.zeros_like(h_ref)

    x = rx_ref[0]
    xp_ref[SUBLANE:SUBLANE + tm, :] = x
    xc = cb_ref[...] + x * cw_ref[CONV_W - 1:CONV_W, :]
    for m in range(1, CONV_W):
        xc = xc + xp_ref[pl.ds(SUBLANE - m, tm), :] * cw_ref[CONV_W - 1 - m:CONV_W - m, :]
    xp_ref[0:SUBLANE, :] = x[tm - SUBLANE:, :]

    pre_r, pre_i = _gate_preacts(xc, wg_ref)
    a, u = _lru_coeffs(xc, pre_r, pre_i, br_ref[...], bi_ref[...], lam_ref[...])

    row = lax.broadcasted_iota(I32, (tm, REC_WIDTH), 0)
    d = 1
    while d < tm:
        keep = row >= d
        a_sh = jnp.where(keep, pltpu.roll(a, d, 0), 1.0)
        u_sh = jnp.where(keep, pltpu.roll(u, d, 0), 0.0)
        u = u + a * u_sh
        a = a * a_sh
        d *= 2
    h = a * h_ref[...] + u
    h_ref[...] = h[tm - 1:tm, :]
    hlast_ref[0] = h[tm - 1:tm, :]
    y_ref[0] = h * _gelu_tanh(rg_ref[0])


def _lru_prompt(rx, rg, wts, tm):
    b, s, w = rx.shape
    full = lambda shape: pl.BlockSpec(shape, lambda bi, si: (0,) * len(shape))
    return pl.pallas_call(
        _lru_prompt_kernel,
        grid=(b, s // tm),
        in_specs=[
            pl.BlockSpec((1, tm, w), lambda bi, si: (bi, si, 0)),
            pl.BlockSpec((1, tm, w), lambda bi, si: (bi, si, 0)),
            full((CONV_W, w)), full((1, w)), full((2, w // 2, w)), full((1, w)), full((1, w)), full((1, w)),
        ],
        out_specs=[
            pl.BlockSpec((1, tm, w), lambda bi, si: (bi, si, 0)),
            pl.BlockSpec((1, 1, w), lambda bi, si: (bi, 0, 0)),
        ],
        out_shape=[jax.ShapeDtypeStruct((b, s, w), F32), jax.ShapeDtypeStruct((b, 1, w), F32)],
        scratch_shapes=[pltpu.VMEM((tm + SUBLANE, w), F32), pltpu.VMEM((1, w), F32)],
        compiler_params=_cparams(("arbitrary", "arbitrary")),
        name="lru_prompt",
    )(rx, rg, wts["conv_w"], wts["conv_b"], wts["w_gate"], wts["b_rg"], wts["b_ig"], wts["lru_lambda"])


def _lru_step_kernel(rx_ref, rg_ref, conv_ref, h0_ref, cw_ref, cb_ref, wg_ref, br_ref, bi_ref, lam_ref,
                     y_ref, newconv_ref, h_ref):
    x = rx_ref[...]
    xc = cb_ref[...] + x * cw_ref[CONV_W - 1:CONV_W, :]
    for k in range(CONV_W - 1):
        xc = xc + conv_ref[k] * cw_ref[k:k + 1, :]
    pre_r, pre_i = _gate_preacts(xc, wg_ref)
    a, u = _lru_coeffs(xc, pre_r, pre_i, br_ref[...], bi_ref[...], lam_ref[...])
    h = a * h0_ref[...] + u
    h_ref[...] = h
    y_ref[...] = h * _gelu_tanh(rg_ref[...])
    for k in range(CONV_W - 2):
        newconv_ref[k] = conv_ref[k + 1]
    newconv_ref[CONV_W - 2] = x


def _lru_step(rx, rg, conv_t, h0, wts):
    n, w = rx.shape
    return pl.pallas_call(
        _lru_step_kernel,
        out_shape=[jax.ShapeDtypeStruct((n, w), F32),
                   jax.ShapeDtypeStruct((CONV_W - 1, n, w), F32),
                   jax.ShapeDtypeStruct((n, w), F32)],
        name="lru_step",
    )(rx, rg, conv_t, h0, wts["conv_w"], wts["conv_b"], wts["w_gate"], wts["b_rg"], wts["b_ig"], wts["lru_lambda"])


def _attn_kernel(q_ref, k_ref, v_ref, o_ref, *, tk):
    tq = q_ref.shape[2]
    qi = pl.program_id(2)
    n_full = qi * (tq // tk)
    row = lax.broadcasted_iota(I32, (tq, tk), 0)
    col = lax.broadcasted_iota(I32, (tq, tk), 1)
    outs = []
    for e in range(2):
        q = q_ref[0, e]

        def step(j, carry, masked, e=e, q=q):
            m, l, acc = carry
            kb = k_ref[0, e, pl.ds(pl.multiple_of(j * tk, tk), tk), :]
            vb = v_ref[0, 0, pl.ds(pl.multiple_of(j * tk, tk), tk), :]
            s = _dot_nt(q, kb) * SM_SCALE
            if masked is not None:
                s = jnp.where(col + masked * tk <= row, s, NEG_INF)
            m_new = jnp.maximum(m, jnp.max(s, axis=-1, keepdims=True))
            alpha = jnp.exp(m - m_new)
            p = jnp.exp(s - m_new)
            l = alpha * l + jnp.sum(p, axis=-1, keepdims=True)
            acc = alpha * acc + _dot(p.astype(BF16), vb)
            return m_new, l, acc

        carry = (jnp.full((tq, 1), NEG_INF, F32), jnp.zeros((tq, 1), F32), jnp.zeros((tq, LANE), F32))
        carry = lax.fori_loop(0, n_full, functools.partial(step, masked=None), carry)
        for dj in range(tq // tk):
            carry = step(n_full + dj, carry, masked=dj)
        m, l, acc = carry
        outs.append(acc / l)
    lane = lax.broadcasted_iota(I32, (tq, LANE), 1)
    o_ref[0, 0] = jnp.where(lane < V_HEAD, outs[0], outs[1])


def _attention(q, k, v, tq, tk):
    b, _, s, _ = q.shape
    return pl.pallas_call(
        functools.partial(_attn_kernel, tk=tk),
        grid=(b, N_PAIRS, s // tq),
        in_specs=[
            pl.BlockSpec((1, 2, tq, HEAD_PAD), lambda bi, pi, qi: (bi, pi, qi, 0)),
            pl.BlockSpec((1, 2, s, HEAD_PAD), lambda bi, pi, qi: (bi, pi, 0, 0)),
            pl.BlockSpec((1, 1, s, LANE), lambda bi, pi, qi: (bi, pi, 0, 0)),
        ],
        out_specs=pl.BlockSpec((1, 1, tq, LANE), lambda bi, pi, qi: (bi, pi, qi, 0)),
        out_shape=jax.ShapeDtypeStruct((b, N_PAIRS, s, LANE), F32),
        compiler_params=_cparams(("parallel", "parallel", "parallel")),
        name="attn_prompt",
    )(q, k, v)


def _absorb_kernel(q_ref, wlat_ref, wpe_ref, qlat_ref, qpe_ref):
    for h in range(N_HEADS):
        q = q_ref[h]
        qlat_ref[h] = _dot(q, wlat_ref[h])
        qpe_ref[h] = _dot(q, wpe_ref[...])


def _absorb(q, wts):
    _, n, _ = q.shape
    return pl.pallas_call(
        _absorb_kernel,
        out_shape=[jax.ShapeDtypeStruct((N_HEADS, n, KV_LORA), F32),
                   jax.ShapeDtypeStruct((N_HEADS, n, QK_ROPE), F32)],
        name="absorb_q",
    )(q, wts["w_uk_t"], wts["w_pe_sel"])


def _sample_attn_kernel(pt_ref, qlat_ref, qpe_ref, cnew_ref, knew_ref, ckv_hbm, kr_hbm, o_ref,
                        cbuf, kbuf, sem, *, n_chunks, total_chunks):
    b = pl.program_id(0)
    cp = PAGES_PER_CHUNK

    def copies(g, slot):
        out = []
        for p in range(cp):
            page = pt_ref[g * cp + p]
            rows = pl.ds(p * PAGE_SIZE, PAGE_SIZE)
            out.append(pltpu.make_async_copy(ckv_hbm.at[page], cbuf.at[slot, rows], sem.at[0, slot]))
            out.append(pltpu.make_async_copy(kr_hbm.at[page], kbuf.at[slot, rows], sem.at[1, slot]))
        return out

    def start(g, slot):
        for c in copies(g, slot):
            c.start()

    def wait(g, slot):
        for c in copies(g, slot):
            c.wait()

    @pl.when(b == 0)
    def _():
        start(0, 0)

    qlat = qlat_ref[0].astype(BF16)
    qpe = qpe_ref[0].astype(BF16)

    def chunk(c, carry):
        m, l, acc = carry
        g = b * n_chunks + c
        slot = lax.rem(g, 2)

        @pl.when(g + 1 < total_chunks)
        def _():
            start(g + 1, 1 - slot)

        wait(g, slot)
        cb = cbuf[slot].astype(BF16)
        kb = kbuf[slot].astype(BF16)
        s = (_dot_nt(qlat, cb) + _dot_nt(qpe, kb)) * SM_SCALE
        m_new = jnp.maximum(m, jnp.max(s, axis=-1, keepdims=True))
        alpha = jnp.exp(m - m_new)
        p = jnp.exp(s - m_new)
        l = alpha * l + jnp.sum(p, axis=-1, keepdims=True)
        acc = alpha * acc + _dot(p.astype(BF16), cb)
        return m_new, l, acc

    carry = (jnp.full((N_HEADS, 1), NEG_INF, F32), jnp.zeros((N_HEADS, 1), F32),
             jnp.zeros((N_HEADS, KV_LORA), F32))
    m, l, acc = lax.fori_loop(0, n_chunks, chunk, carry)

    cnew = cnew_ref[0]
    knew = knew_ref[0]
    s_new = (jnp.sum(qlat_ref[0] * cnew, axis=-1, keepdims=True)
             + jnp.sum(qpe_ref[0] * knew, axis=-1, keepdims=True)) * SM_SCALE
    m_fin = jnp.maximum(m, s_new)
    alpha = jnp.exp(m - m_fin)
    p_new = jnp.exp(s_new - m_fin)
    l = alpha * l + p_new
    acc = alpha * acc + p_new * cnew
    o_ref[0] = acc / l


def _sample_attention(page_table, qlat, qpe, c_new, k_new, cache_kv, cache_kr):
    bd, n_pages = page_table.shape
    n_chunks = n_pages // PAGES_PER_CHUNK
    rows = PAGES_PER_CHUNK * PAGE_SIZE
    kern = functools.partial(_sample_attn_kernel, n_chunks=n_chunks, total_chunks=bd * n_chunks)
    grid_spec = pltpu.PrefetchScalarGridSpec(
        num_scalar_prefetch=1,
        grid=(bd,),
        in_specs=[
            pl.BlockSpec((1, N_HEADS, KV_LORA), lambda bi, pt: (bi, 0, 0)),
            pl.BlockSpec((1, N_HEADS, QK_ROPE), lambda bi, pt: (bi, 0, 0)),
            pl.BlockSpec((1, 1, KV_LORA), lambda bi, pt: (bi, 0, 0)),
            pl.BlockSpec((1, 1, QK_ROPE), lambda bi, pt: (bi, 0, 0)),
            pl.BlockSpec(memory_space=pl.ANY),
            pl.BlockSpec(memory_space=pl.ANY),
        ],
        out_specs=pl.BlockSpec((1, N_HEADS, KV_LORA), lambda bi, pt: (bi, 0, 0)),
        scratch_shapes=[
            pltpu.VMEM((2, rows, KV_LORA), F32),
            pltpu.VMEM((2, rows, QK_ROPE), F32),
            pltpu.SemaphoreType.DMA((2, 2)),
        ],
    )
    return pl.pallas_call(
        kern,
        grid_spec=grid_spec,
        out_shape=jax.ShapeDtypeStruct((bd, N_HEADS, KV_LORA), F32),
        compiler_params=_cparams(("arbitrary",)),
        name="attn_sample",
    )(page_table.reshape(-1), qlat, qpe, c_new, k_new, cache_kv, cache_kr)


def _value_up_kernel(olat_ref, wv_ref, o_ref):
    w = 2 * KV_LORA
    for p in range(N_PAIRS):
        o_ref[0, p] = _dot(olat_ref[:, p * w:(p + 1) * w].astype(BF16), wv_ref[p])


def _value_up(olat2d, wts):
    n = olat2d.shape[0]
    return pl.pallas_call(
        _value_up_kernel,
        out_shape=jax.ShapeDtypeStruct((1, N_PAIRS, n, LANE), F32),
        name="value_up",
    )(olat2d, wts["w_uv_pair"])


def _merge_kernel(x_ref, att_ref, rec_ref, ag_ref, rgn_ref, wout_ref, g1_ref, b1_ref, wr_ref, bg_ref, be_ref,
                  x1_ref, gsel_ref, gate_ref, cnt_ref):
    tm = x_ref.shape[1]
    att = [att_ref[0, p] for p in range(N_PAIRS)]
    ss = att[0] * att[0]
    for p in range(1, N_PAIRS):
        ss = ss + att[p] * att[p]
    inv = lax.rsqrt(jnp.sum(ss, axis=-1, keepdims=True) / (N_PAIRS * LANE) + RMS_EPS)
    parts = [(att[p] * inv * ag_ref[:, p * LANE:(p + 1) * LANE]).astype(BF16) for p in range(N_PAIRS)]
    parts.append(_rmsnorm(rec_ref[0], rgn_ref[...]).astype(BF16))
    mixed = jnp.concatenate(parts, axis=-1)
    mix = _dot(mixed, wout_ref[...])
    x1 = _layernorm(ALPHA * x_ref[0] + mix, g1_ref[...], b1_ref[...])
    x1_ref[0] = x1

    lt = _dot_nt(wr_ref[...], x1, precision=lax.Precision.HIGHEST)
    g = [lt[k:k + 1, :] for k in range(N_GROUPS)]
    gmax = functools.reduce(jnp.maximum, g)
    ex = [jnp.exp(gk - gmax) for gk in g]
    den = functools.reduce(lambda p, q: p + q, ex)
    best = g[0] + bg_ref[0:1, :]
    idx = jnp.zeros((1, tm), I32)
    for k in range(1, N_GROUPS):
        cand = g[k] + bg_ref[k:k + 1, :]
        upd = cand > best
        idx = jnp.where(upd, k, idx)
        best = jnp.where(upd, cand, best)
    gp = ex[0]
    e_sel = lt[SUBLANE:SUBLANE + EXPERTS_PER_GROUP, :]
    e_bias = jnp.broadcast_to(be_ref[0:EXPERTS_PER_GROUP, :], (EXPERTS_PER_GROUP, tm))
    for k in range(1, N_GROUPS):
        hit = idx == k
        lo = SUBLANE + k * EXPERTS_PER_GROUP
        gp = jnp.where(hit, ex[k], gp)
        e_sel = jnp.where(hit, lt[lo:lo + EXPERTS_PER_GROUP, :], e_sel)
        e_bias = jnp.where(hit, be_ref[k * EXPERTS_PER_GROUP:(k + 1) * EXPERTS_PER_GROUP, :], e_bias)
    g_prob = gp / den
    sc = e_sel + e_bias
    sub = lax.broadcasted_iota(I32, (EXPERTS_PER_GROUP, tm), 0)
    m1 = jnp.max(sc, axis=0, keepdims=True)
    i1 = jnp.min(jnp.where(sc == m1, sub, EXPERTS_PER_GROUP), axis=0, keepdims=True)
    mask1 = sub == i1
    sc2 = jnp.where(mask1, -jnp.inf, sc)
    m2 = jnp.max(sc2, axis=0, keepdims=True)
    i2 = jnp.min(jnp.where(sc2 == m2, sub, EXPERTS_PER_GROUP), axis=0, keepdims=True)
    mask2 = sub == i2
    v1 = jnp.sum(jnp.where(mask1, e_sel, 0.0), axis=0, keepdims=True)
    v2 = jnp.sum(jnp.where(mask2, e_sel, 0.0), axis=0, keepdims=True)
    vm = jnp.maximum(v1, v2)
    e1 = jnp.exp(v1 - vm)
    e2 = jnp.exp(v2 - vm)
    esum = e1 + e2
    gate = g_prob * (jnp.where(mask1, e1 / esum, 0.0) + jnp.where(mask2, e2 / esum, 0.0))
    gsel_ref[0] = idx
    gate_ref[0] = gate
    rowi = lax.broadcasted_iota(I32, (SUBLANE, LANE), 0)
    cnt = jnp.zeros((SUBLANE, LANE), F32)
    for k in range(N_GROUPS):
        ck = jnp.sum(jnp.where(idx == k, 1.0, 0.0), axis=-1, keepdims=True)
        cnt = jnp.where(rowi == k, ck, cnt)
    cnt_ref[0] = cnt


def _merge(x, att, rec, wts, tm):
    b, s, d = x.shape
    nt = s // tm
    mw = wts["w_out"].shape[0]
    n_r = wts["w_router_t"].shape[0]
    full = lambda shape: pl.BlockSpec(shape, lambda bi, si: (0,) * len(shape))
    return pl.pallas_call(
        _merge_kernel,
        grid=(b, nt),
        in_specs=[
            pl.BlockSpec((1, tm, d), lambda bi, si: (bi, si, 0)),
            pl.BlockSpec((1, N_PAIRS, tm, LANE), lambda bi, si: (bi, 0, si, 0)),
            pl.BlockSpec((1, tm, REC_WIDTH), lambda bi, si: (bi, si, 0)),
            full((1, N_PAIRS * LANE)), full((1, REC_WIDTH)), full((mw, d)), full((1, d)), full((1, d)),
            full((n_r, d)), full((N_GROUPS, 1)), full((N_GROUPS * EXPERTS_PER_GROUP, 1)),
        ],
        out_specs=[
            pl.BlockSpec((1, tm, d), lambda bi, si: (bi, si, 0)),
            pl.BlockSpec((1, 1, tm), lambda bi, si: (bi * nt + si, 0, 0)),
            pl.BlockSpec((1, EXPERTS_PER_GROUP, tm), lambda bi, si: (bi * nt + si, 0, 0)),
            pl.BlockSpec((1, SUBLANE, LANE), lambda bi, si: (bi * nt + si, 0, 0)),
        ],
        out_shape=[
            jax.ShapeDtypeStruct((b, s, d), F32),
            jax.ShapeDtypeStruct((b * nt, 1, tm), I32),
            jax.ShapeDtypeStruct((b * nt, EXPERTS_PER_GROUP, tm), F32),
            jax.ShapeDtypeStruct((b * nt, SUBLANE, LANE), F32),
        ],
        compiler_params=_cparams(("parallel", "parallel")),
        name="merge_router",
    )(x, att, rec, wts["att_out_g"], wts["rec_out_g"], wts["w_out"], wts["ln1_g"], wts["ln1_b"],
      wts["w_router_t"], wts["b_group"], wts["b_expert"])


def _moe_kernel(nch_ref, x1_ref, gsel_ref, gate_ref, wgu_ref, wd_ref, g2_ref, b2_ref, o_ref, xb_ref, tri_ref,
                *, ch):
    tm = x1_ref.shape[0]
    ti = pl.program_id(0)
    gi = pl.program_id(1)

    @pl.when(jnp.logical_and(ti == 0, gi == 0))
    def _():
        r = lax.broadcasted_iota(I32, (tm, tm), 0)
        c = lax.broadcasted_iota(I32, (tm, tm), 1)
        tri_ref[...] = jnp.where(r < c, 1.0, 0.0).astype(BF16)

    @pl.when(gi == 0)
    def _():
        xb_ref[...] = x1_ref[...].astype(BF16)
        o_ref[...] = jnp.zeros_like(o_ref)

    in_group = gsel_ref[0] == gi
    member = jnp.broadcast_to(jnp.where(in_group, 1.0, 0.0), (SUBLANE, tm)).astype(BF16)
    before = _dot(member, tri_ref[...])
    rank = jnp.where(in_group, before[0:1, :].astype(I32), -1)
    gate_pad = jnp.concatenate([gate_ref[0], jnp.zeros((LANE - EXPERTS_PER_GROUP, tm), F32)], axis=0)

    def chunk(c, carry):
        slot_id = lax.broadcasted_iota(I32, (ch, tm), 0) + c * ch
        onehot = jnp.where(slot_id == rank, 1.0, 0.0)
        onehot_b = onehot.astype(BF16)
        xg = _dot(onehot_b, xb_ref[...]).astype(BF16)
        gc = _dot_nt(onehot, gate_pad, precision=lax.Precision.HIGHEST)
        acc = jnp.zeros((ch, o_ref.shape[1]), F32)
        for j in range(EXPERTS_PER_GROUP):
            gu = _dot(xg, wgu_ref[0, j])
            hid = jax.nn.silu(gu[:, :D_EXPERT]) * gu[:, D_EXPERT:]
            acc = acc + gc[:, j:j + 1] * _dot(hid.astype(BF16), wd_ref[0, j])
        hi = acc.astype(BF16)
        lo = (acc - hi.astype(F32)).astype(BF16)
        o_ref[...] += _dot_tn(onehot_b, hi) + _dot_tn(onehot_b, lo)
        return carry

    lax.fori_loop(0, nch_ref[ti * N_GROUPS + gi], chunk, 0)

    @pl.when(gi == N_GROUPS - 1)
    def _():
        o_ref[...] = _layernorm(ALPHA * x1_ref[...] + o_ref[...], g2_ref[...], b2_ref[...])


def _moe(x1, gsel, gate, nch, wts, tm, ch):
    n, d = x1.shape
    nt = n // tm
    e2 = 2 * D_EXPERT
    grid_spec = pltpu.PrefetchScalarGridSpec(
        num_scalar_prefetch=1,
        grid=(nt, N_GROUPS),
        in_specs=[
            pl.BlockSpec((tm, d), lambda ti, gi, nc: (ti, 0)),
            pl.BlockSpec((1, 1, tm), lambda ti, gi, nc: (ti, 0, 0)),
            pl.BlockSpec((1, EXPERTS_PER_GROUP, tm), lambda ti, gi, nc: (ti, 0, 0)),
            pl.BlockSpec((1, EXPERTS_PER_GROUP, d, e2), lambda ti, gi, nc: (gi, 0, 0, 0)),
            pl.BlockSpec((1, EXPERTS_PER_GROUP, D_EXPERT, d), lambda ti, gi, nc: (gi, 0, 0, 0)),
            pl.BlockSpec((1, d), lambda ti, gi, nc: (0, 0)),
            pl.BlockSpec((1, d), lambda ti, gi, nc: (0, 0)),
        ],
        out_specs=pl.BlockSpec((tm, d), lambda ti, gi, nc: (ti, 0)),
        scratch_shapes=[pltpu.VMEM((tm, d), BF16), pltpu.VMEM((tm, tm), BF16)],
    )
    return pl.pallas_call(
        functools.partial(_moe_kernel, ch=ch),
        grid_spec=grid_spec,
        out_shape=jax.ShapeDtypeStruct((n, d), F32),
        compiler_params=_cparams(("arbitrary", "arbitrary")),
        name="moe",
    )(nch, x1, gsel, gate, wts["w_gate_up"], wts["w_down"], wts["ln2_g"], wts["ln2_b"])


def _merge_and_ffn(x, att, rec, wts, tm_merge, tm_moe, ch):
    b, s, d = x.shape
    x1, gsel, gate, cnt = _merge(x, att, rec, wts, tm_merge)
    n = b * s
    nt = n // tm_moe
    f = tm_moe // tm_merge
    gsel = gsel.reshape(nt, 1, tm_moe)
    gate = gate.reshape(nt, f, EXPERTS_PER_GROUP, tm_merge).transpose(0, 2, 1, 3).reshape(nt, EXPERTS_PER_GROUP, tm_moe)
    counts = cnt[:, :N_GROUPS, 0].reshape(nt, f, N_GROUPS).sum(axis=1).astype(I32)
    nch = ((counts + (ch - 1)) // ch).reshape(-1)
    y = _moe(x1.reshape(n, d), gsel, gate, nch, wts, tm_moe, ch)
    return y.reshape(b, s, d)


def _swap_halves(w):
    half = QK_ROPE // 2
    return jnp.concatenate([w[..., half:], w[..., :half]], axis=-1)


def _prep_weights(w_in, q_norm_g, w_uq, kv_norm_g, w_uk, w_uv, conv_w, conv_b, w_rg, b_rg, w_ig, b_ig,
                  lru_lambda, att_out_g, rec_out_g, w_out, ln1_g, ln1_b, w_group, b_group, w_expert,
                  b_expert, w_gate_up, w_down, ln2_g, ln2_b):
    d = w_in.shape[0]
    o1, o2, o3, o4 = Q_LORA, Q_LORA + KV_LORA, Q_LORA + KV_LORA + QK_ROPE, Q_LORA + KV_LORA + QK_ROPE + REC_WIDTH
    w_kpe = w_in[:, o2:o3]
    pad_lo = jnp.zeros((d, QK_NOPE), F32)
    pad_hi = jnp.zeros((d, HEAD_PAD - QK_NOPE - QK_ROPE), F32)
    w_in_ext = jnp.concatenate([
        w_in[:, :o2],
        pad_lo, w_kpe, pad_hi,
        pad_lo, _swap_halves(w_kpe), pad_hi,
        w_in[:, o3:o4], w_in[:, o4:],
    ], axis=1).astype(BF16)

    nope, pe = w_uq[..., :QK_NOPE], w_uq[..., QK_NOPE:]
    zq = lambda n: jnp.zeros((Q_LORA, N_HEADS, n), F32)
    q_main = jnp.concatenate([nope, pe, zq(HEAD_PAD - QK_NOPE - QK_ROPE)], axis=-1)
    q_swap = jnp.concatenate([zq(QK_NOPE), _swap_halves(pe), zq(HEAD_PAD - QK_NOPE - QK_ROPE)], axis=-1)
    w_uq_ext = jnp.concatenate([q_main.reshape(Q_LORA, -1), q_swap.reshape(Q_LORA, -1)], axis=1).astype(BF16)

    k_pad = jnp.concatenate([w_uk, jnp.zeros((KV_LORA, N_HEADS, HEAD_PAD - QK_NOPE), F32)], axis=-1)
    w_ukv = jnp.concatenate([k_pad.reshape(KV_LORA, -1), w_uv.reshape(KV_LORA, -1)], axis=1).astype(BF16)

    w_uk_t = jnp.concatenate([w_uk.transpose(1, 2, 0),
                              jnp.zeros((N_HEADS, HEAD_PAD - QK_NOPE, KV_LORA), F32)], axis=1).astype(BF16)
    sel = jnp.zeros((HEAD_PAD, QK_ROPE), F32).at[QK_NOPE + jnp.arange(QK_ROPE), jnp.arange(QK_ROPE)].set(1.0)
    w_uv_h = w_uv.transpose(1, 0, 2)
    zero_v = jnp.zeros((KV_LORA, V_HEAD), F32)
    w_uv_pair = jnp.stack([
        jnp.concatenate([jnp.concatenate([w_uv_h[2 * p], zero_v], axis=1),
                         jnp.concatenate([zero_v, w_uv_h[2 * p + 1]], axis=1)], axis=0)
        for p in range(N_PAIRS)]).astype(BF16)

    def block_diag(w):
        eye = jnp.eye(REC_BLOCKS, dtype=F32)
        return jnp.einsum('nde,nm->ndme', w, eye).reshape(REC_WIDTH, REC_WIDTH)

    bd_r, bd_i = block_diag(w_rg), block_diag(w_ig)
    half = REC_WIDTH // 2
    w_gate = jnp.stack([
        jnp.concatenate([bd_r[j * half:(j + 1) * half, j * half:(j + 1) * half],
                         bd_i[j * half:(j + 1) * half, j * half:(j + 1) * half]], axis=1)
        for j in range(2)]).astype(BF16)

    w_router_t = jnp.concatenate([w_group.T, jnp.zeros((SUBLANE - N_GROUPS, d), F32), w_expert.T], axis=0)
    row = lambda v: v.reshape(1, -1)
    return {
        "w_in": w_in_ext, "q_norm_g": row(q_norm_g), "w_uq": w_uq_ext, "kv_norm_g": row(kv_norm_g),
        "w_ukv": w_ukv, "w_uk_t": w_uk_t, "w_pe_sel": sel.astype(BF16), "w_uv_pair": w_uv_pair,
        "conv_w": conv_w, "conv_b": row(conv_b), "w_gate": w_gate, "b_rg": row(b_rg), "b_ig": row(b_ig),
        "lru_lambda": row(lru_lambda), "att_out_g": row(att_out_g), "rec_out_g": row(rec_out_g),
        "w_out": w_out.astype(BF16), "ln1_g": row(ln1_g), "ln1_b": row(ln1_b),
        "w_router_t": w_router_t, "b_group": b_group.reshape(-1, 1), "b_expert": b_expert.reshape(-1, 1),
        "w_gate_up": w_gate_up.astype(BF16).reshape(N_GROUPS, EXPERTS_PER_GROUP, d, 2 * D_EXPERT),
        "w_down": w_down.astype(BF16).reshape(N_GROUPS, EXPERTS_PER_GROUP, D_EXPERT, d),
        "ln2_g": row(ln2_g), "ln2_b": row(ln2_b),
    }


def _rope_tables(pos):
    half = QK_ROPE // 2
    inv = ROPE_THETA ** (-(jnp.arange(half, dtype=F32) * 2.0 / QK_ROPE))
    ang = pos.astype(F32)[:, None] * inv[None, :]
    cos, sin = jnp.cos(ang), jnp.sin(ang)
    t = pos.shape[0]
    cos_t = jnp.concatenate([jnp.ones((t, QK_NOPE), F32), cos, cos,
                             jnp.zeros((t, HEAD_PAD - QK_NOPE - QK_ROPE), F32)], axis=1)
    sin_t = jnp.concatenate([jnp.zeros((t, QK_NOPE), F32), -sin, sin,
                             jnp.zeros((t, HEAD_PAD - QK_NOPE - QK_ROPE), F32)], axis=1)
    return cos_t, sin_t


def kernel(x_prompt, x_sample, cache_kv_latent, cache_k_rope, state_conv, state_rec, page_table,
           w_in, q_norm_g, w_uq, kv_norm_g, w_uk, w_uv, conv_w, conv_b, w_rg, b_rg, w_ig, b_ig,
           lru_lambda, att_out_g, rec_out_g, w_out, ln1_g, ln1_b, w_group, b_group, w_expert,
           b_expert, w_gate_up, w_down, ln2_g, ln2_b):
    wts = _prep_weights(w_in, q_norm_g, w_uq, kv_norm_g, w_uk, w_uv, conv_w, conv_b, w_rg, b_rg, w_ig, b_ig,
                        lru_lambda, att_out_g, rec_out_g, w_out, ln1_g, ln1_b, w_group, b_group, w_expert,
                        b_expert, w_gate_up, w_down, ln2_g, ln2_b)
    bp, sp, d = x_prompt.shape
    bd, td, _ = x_sample.shape
    assert td == 1, "the sample path handles one new token per sequence"
    past_len = page_table.shape[1] * PAGE_SIZE
    ko = QK_NOPE

    tm_proj = min(TM_PROJ, sp)
    cos_p, sin_p = _rope_tables(jnp.arange(sp, dtype=I32))
    q, k, v, c_p, kpe_blk, rx, rg = _project(x_prompt, cos_p, sin_p, wts, tm_proj)
    att_p = _attention(q, k, v, min(TQ_ATT, sp), min(TK_ATT, sp))
    rec_p, h_p = _lru_prompt(rx, rg, wts, min(TM_LRU, sp))
    tm_moe = min(TM_MOE, bp * sp)
    y_p = _merge_and_ffn(x_prompt, att_p, rec_p, wts, min(TM_MERGE, sp, tm_moe), tm_moe, CH_MOE)
    kpe_p = kpe_blk[..., ko:ko + QK_ROPE]
    conv_p = rx[:, sp - (CONV_W - 1):, :]

    xs = x_sample.reshape(1, bd, d)
    cos_s, sin_s = _rope_tables(jnp.full((bd,), past_len, I32))
    q_s, _, _, c_s, kpe_s_blk, rx_s, rg_s = _project(xs, cos_s, sin_s, wts, bd)
    kpe_s = kpe_s_blk[0, :, ko:ko + QK_ROPE]
    qlat, qpe = _absorb(q_s[0], wts)
    o_lat = _sample_attention(page_table, qlat.transpose(1, 0, 2), qpe.transpose(1, 0, 2),
                              c_s.reshape(bd, 1, KV_LORA), kpe_s.reshape(bd, 1, QK_ROPE),
                              cache_kv_latent, cache_k_rope)
    att_s = _value_up(o_lat.reshape(bd, N_HEADS * KV_LORA), wts)
    rec_s, conv_s_t, h_s = _lru_step(rx_s[0], rg_s[0], state_conv.transpose(1, 0, 2), state_rec, wts)
    y_s = _merge_and_ffn(xs, att_s, rec_s.reshape(1, bd, REC_WIDTH), wts, bd, bd, min(CH_MOE, bd))

    return (y_p, y_s.reshape(bd, 1, d), c_p, kpe_p, conv_p, h_p.reshape(bp, REC_WIDTH),
            c_s.reshape(bd, 1, KV_LORA), kpe_s.reshape(bd, 1, QK_ROPE), conv_s_t.transpose(1, 0, 2), h_s)
```

```python
import functools
import math

import jax
import jax.numpy as jnp
from jax import lax
from jax.experimental import pallas as pl
from jax.experimental.pallas import tpu as pltpu

F32 = jnp.float32
BF16 = jnp.bfloat16
I32 = jnp.int32

N_HEADS = 8
QK_NOPE = 64
QK_ROPE = 32
V_HEAD = 64
Q_LORA = 384
KV_LORA = 256
ROPE_THETA = 10000.0
SM_SCALE = (QK_NOPE + QK_ROPE) ** -0.5
REC_WIDTH = 512
REC_BLOCKS = 8
REC_BLOCK_W = REC_WIDTH // REC_BLOCKS
CONV_W = 4
LRU_C = 8.0
N_GROUPS = 4
EXPERTS_PER_GROUP = 8
D_EXPERT = 256
DEPTH = 1
ALPHA = (2.0 * DEPTH) ** 0.25
LN_EPS = 1e-5
RMS_EPS = 1e-6
NEG_INF = -1e30
PAGE_SIZE = 128

LANE = 128
SUBLANE = 8
HEAD_PAD = LANE
N_PAIRS = N_HEADS * V_HEAD // LANE
VMEM_LIMIT = 56 * 1024 * 1024

TM_PROJ = 512
TM_LRU = 256
TQ_ATT = 512
TK_ATT = 512
TM_MERGE = 512
TM_MOE = 1024
CH_MOE = 256
PAGES_PER_CHUNK = 32
EXP2_SCALE = SM_SCALE * math.log2(math.e)


def _dot(a, b):
    return jnp.dot(a, b, preferred_element_type=F32)


def _dot_nt(a, b, precision=None):
    return lax.dot_general(a, b, (((1,), (1,)), ((), ())), preferred_element_type=F32, precision=precision)


def _dot_tn(a, b):
    return lax.dot_general(a, b, (((0,), (0,)), ((), ())), preferred_element_type=F32)


def _cparams(semantics):
    return pltpu.CompilerParams(dimension_semantics=semantics, vmem_limit_bytes=VMEM_LIMIT)


def _rmsnorm(x, g):
    return x * lax.rsqrt(jnp.mean(x * x, axis=-1, keepdims=True) + RMS_EPS) * g


def _layernorm(x, g, b):
    mu = jnp.mean(x, axis=-1, keepdims=True)
    xc = x - mu
    var = jnp.mean(xc * xc, axis=-1, keepdims=True)
    return xc * lax.rsqrt(var + LN_EPS) * g + b


def _gelu_tanh(x):
    return x * (0.5 * (1.0 + jnp.tanh(math.sqrt(2.0 / math.pi) * (x + 0.044715 * (x * x * x)))))


def _lru_coeffs(xc, pre_r, pre_i, b_r, b_i, lam):
    r = jax.nn.sigmoid(pre_r + b_r)
    i = jax.nn.sigmoid(pre_i + b_i)
    neg_lam = -lam
    softplus = jnp.maximum(neg_lam, 0.0) + jnp.log1p(jnp.exp(-jnp.abs(neg_lam)))
    log_a = (-LRU_C * softplus) * r
    a = jnp.exp(log_a)
    u = jnp.sqrt(-jnp.tanh(log_a) * (a * a + 1.0)) * (i * xc)
    return a, u


def _gate_preacts(xc, wg_ref):
    half = REC_WIDTH // 2
    g0 = _dot(xc[:, :half].astype(BF16), wg_ref[0])
    g1 = _dot(xc[:, half:].astype(BF16), wg_ref[1])
    pre_r = jnp.concatenate([g0[:, :half], g1[:, :half]], axis=1)
    pre_i = jnp.concatenate([g0[:, half:], g1[:, half:]], axis=1)
    return pre_r, pre_i


def _proj_kernel(x_ref, cos_ref, sin_ref, win_ref, qg_ref, wuq_ref, kvg_ref, wukv_ref,
                 q_ref, k_ref, v_ref, ckv_ref, kpe_ref, rx_ref, rg_ref):
    x = x_ref[0].astype(BF16)
    z = _dot(x, win_ref[...])
    o_kv = Q_LORA
    o_ka = o_kv + KV_LORA
    o_kb = o_ka + LANE
    o_rx = o_kb + LANE
    o_rg = o_rx + REC_WIDTH
    cos = cos_ref[...]
    sin = sin_ref[...]
    qn = _rmsnorm(z[:, :o_kv], qg_ref[...])
    qq = _dot(qn.astype(BF16), wuq_ref[...])
    sw = N_HEADS * HEAD_PAD
    for h in range(N_HEADS):
        lo = h * HEAD_PAD
        q_ref[0, h] = (qq[:, lo:lo + HEAD_PAD] * cos + qq[:, sw + lo:sw + lo + HEAD_PAD] * sin).astype(BF16)
    ckv = _rmsnorm(z[:, o_kv:o_ka], kvg_ref[...])
    ckv_ref[0] = ckv
    kpe = z[:, o_ka:o_kb] * cos + z[:, o_kb:o_rx] * sin
    kpe_ref[0] = kpe
    kv = _dot(ckv.astype(BF16), wukv_ref[...])
    for h in range(N_HEADS):
        lo = h * HEAD_PAD
        k_ref[0, h] = (kv[:, lo:lo + HEAD_PAD] + kpe).astype(BF16)
    for p in range(N_PAIRS):
        lo = sw + p * LANE
        v_ref[0, p] = kv[:, lo:lo + LANE].astype(BF16)
    rx_ref[0] = z[:, o_rx:o_rg]
    rg_ref[0] = z[:, o_rg:o_rg + REC_WIDTH]


def _project(x, cos_t, sin_t, wts, tm):
    b, s, d = x.shape
    in_w = wts["w_in"].shape[1]
    full = lambda shape: pl.BlockSpec(shape, lambda bi, si: (0,) * len(shape))
    return pl.pallas_call(
        _proj_kernel,
        grid=(b, s // tm),
        in_specs=[
            pl.BlockSpec((1, tm, d), lambda bi, si: (bi, si, 0)),
            pl.BlockSpec((tm, LANE), lambda bi, si: (si, 0)),
            pl.BlockSpec((tm, LANE), lambda bi, si: (si, 0)),
            full((d, in_w)),
            full((1, Q_LORA)),
            full((Q_LORA, 2 * N_HEADS * HEAD_PAD)),
            full((1, KV_LORA)),
            full((KV_LORA, N_HEADS * HEAD_PAD + N_PAIRS * LANE)),
        ],
        out_specs=[
            pl.BlockSpec((1, N_HEADS, tm, HEAD_PAD), lambda bi, si: (bi, 0, si, 0)),
            pl.BlockSpec((1, N_HEADS, tm, HEAD_PAD), lambda bi, si: (bi, 0, si, 0)),
            pl.BlockSpec((1, N_PAIRS, tm, LANE), lambda bi, si: (bi, 0, si, 0)),
            pl.BlockSpec((1, tm, KV_LORA), lambda bi, si: (bi, si, 0)),
            pl.BlockSpec((1, tm, LANE), lambda bi, si: (bi, si, 0)),
            pl.BlockSpec((1, tm, REC_WIDTH), lambda bi, si: (bi, si, 0)),
            pl.BlockSpec((1, tm, REC_WIDTH), lambda bi, si: (bi, si, 0)),
        ],
        out_shape=[
            jax.ShapeDtypeStruct((b, N_HEADS, s, HEAD_PAD), BF16),
            jax.ShapeDtypeStruct((b, N_HEADS, s, HEAD_PAD), BF16),
            jax.ShapeDtypeStruct((b, N_PAIRS, s, LANE), BF16),
            jax.ShapeDtypeStruct((b, s, KV_LORA), F32),
            jax.ShapeDtypeStruct((b, s, LANE), F32),
            jax.ShapeDtypeStruct((b, s, REC_WIDTH), F32),
            jax.ShapeDtypeStruct((b, s, REC_WIDTH), F32),
        ],
        compiler_params=_cparams(("parallel", "parallel")),
        name="proj",
    )(x, cos_t, sin_t, wts["w_in"], wts["q_norm_g"], wts["w_uq"], wts["kv_norm_g"], wts["w_ukv"])


def _lru_prompt_kernel(rx_ref, rg_ref, cw_ref, cb_ref, wg_ref, br_ref, bi_ref, lam_ref,
                       y_ref, hlast_ref, xp_ref, h_ref):
    tm = rx_ref.shape[1]
    si = pl.program_id(1)

    @pl.when(si == 0)
    def _():
        xp_ref[0:SUBLANE, :] = jnp.zeros((SUBLANE, REC_WIDTH), F32)
        h_ref[...] = jnp.zeros_like(h_ref)

    x = rx_ref[0]
    xp_ref[SUBLANE:SUBLANE + tm, :] = x
    xc = cb_ref[...] + x * cw_ref[CONV_W - 1:CONV_W, :]
    for m in range(1, CONV_W):
        xc = xc + xp_ref[pl.ds(SUBLANE - m, tm), :] * cw_ref[CONV_W - 1 - m:CONV_W - m, :]
    xp_ref[0:SUBLANE, :] = x[tm - SUBLANE:, :]

    pre_r, pre_i = _gate_preacts(xc, wg_ref)
    a, u = _lru_coeffs(xc, pre_r, pre_i, br_ref[...], bi_ref[...], lam_ref[...])

    row = lax.broadcasted_iota(I32, (tm, REC_WIDTH), 0)
    d = 1
    while d < tm:
        keep = row >= d
        a_sh = jnp.where(keep, pltpu.roll(a, d, 0), 1.0)
        u_sh = jnp.where(keep, pltpu.roll(u, d, 0), 0.0)
        u = u + a * u_sh
        a = a * a_sh
        d *= 2
    h = a * h_ref[...] + u
    h_ref[...] = h[tm - 1:tm, :]
    hlast_ref[0] = h[tm - 1:tm, :]
    y_ref[0] = h * _gelu_tanh(rg_ref[0])


def _lru_prompt(rx, rg, wts, tm):
    b, s, w = rx.shape
    full = lambda shape: pl.BlockSpec(shape, lambda bi, si: (0,) * len(shape))
    return pl.pallas_call(
        _lru_prompt_kernel,
        grid=(b, s // tm),
        in_specs=[
            pl.BlockSpec((1, tm, w), lambda bi, si: (bi, si, 0)),
            pl.BlockSpec((1, tm, w), lambda bi, si: (bi, si, 0)),
            full((CONV_W, w)), full((1, w)), full((2, w // 2, w)), full((1, w)), full((1, w)), full((1, w)),
        ],
        out_specs=[
            pl.BlockSpec((1, tm, w), lambda bi, si: (bi, si, 0)),
            pl.BlockSpec((1, 1, w), lambda bi, si: (bi, 0, 0)),
        ],
        out_shape=[jax.ShapeDtypeStruct((b, s, w), F32), jax.ShapeDtypeStruct((b, 1, w), F32)],
        scratch_shapes=[pltpu.VMEM((tm + SUBLANE, w), F32), pltpu.VMEM((1, w), F32)],
        compiler_params=_cparams(("arbitrary", "arbitrary")),
        name="lru_prompt",
    )(rx, rg, wts["conv_w"], wts["conv_b"], wts["w_gate"], wts["b_rg"], wts["b_ig"], wts["lru_lambda"])


def _lru_step_kernel(rx_ref, rg_ref, conv_ref, h0_ref, cw_ref, cb_ref, wg_ref, br_ref, bi_ref, lam_ref,
                     y_ref, newconv_ref, h_ref):
    x = rx_ref[...]
    xc = cb_ref[...] + x * cw_ref[CONV_W - 1:CONV_W, :]
    for k in range(CONV_W - 1):
        xc = xc + conv_ref[k] * cw_ref[k:k + 1, :]
    pre_r, pre_i = _gate_preacts(xc, wg_ref)
    a, u = _lru_coeffs(xc, pre_r, pre_i, br_ref[...], bi_ref[...], lam_ref[...])
    h = a * h0_ref[...] + u
    h_ref[...] = h
    y_ref[...] = h * _gelu_tanh(rg_ref[...])
    for k in range(CONV_W - 2):
        newconv_ref[k] = conv_ref[k + 1]
    newconv_ref[CONV_W - 2] = x


def _lru_step(rx, rg, conv_t, h0, wts):
    n, w = rx.shape
    return pl.pallas_call(
        _lru_step_kernel,
        out_shape=[jax.ShapeDtypeStruct((n, w), F32),
                   jax.ShapeDtypeStruct((CONV_W - 1, n, w), F32),
                   jax.ShapeDtypeStruct((n, w), F32)],
        name="lru_step",
    )(rx, rg, conv_t, h0, wts["conv_w"], wts["conv_b"], wts["w_gate"], wts["b_rg"], wts["b_ig"], wts["lru_lambda"])


def _attn_kernel(q_ref, k_ref, v_ref, o_ref, *, tk):
    tq = q_ref.shape[2]
    qi = pl.program_id(2)
    n_full = qi * (tq // tk)
    row = lax.broadcasted_iota(I32, (tq, tk), 0)
    col = lax.broadcasted_iota(I32, (tq, tk), 1)
    outs = []
    for e in range(2):
        q = q_ref[0, e]

        def step(j, carry, masked, e=e, q=q):
            m, l, acc = carry
            kb = k_ref[0, e, pl.ds(pl.multiple_of(j * tk, tk), tk), :]
            vb = v_ref[0, 0, pl.ds(pl.multiple_of(j * tk, tk), tk), :]
            s = _dot_nt(q, kb)
            if masked is not None:
                s = jnp.where(col + masked * tk <= row, s, NEG_INF)
            m_new = jnp.maximum(m, jnp.max(s, axis=-1, keepdims=True))
            alpha = jnp.exp2((m - m_new) * EXP2_SCALE)
            p = jnp.exp2((s - m_new) * EXP2_SCALE)
            l = alpha * l + jnp.sum(p, axis=-1, keepdims=True)
            acc = alpha * acc + _dot(p.astype(BF16), vb)
            return m_new, l, acc

        carry = (jnp.full((tq, 1), NEG_INF, F32), jnp.zeros((tq, 1), F32), jnp.zeros((tq, LANE), F32))
        carry = lax.fori_loop(0, n_full, functools.partial(step, masked=None), carry)
        for dj in range(tq // tk):
            carry = step(n_full + dj, carry, masked=dj)
        m, l, acc = carry
        outs.append(acc / l)
    lane = lax.broadcasted_iota(I32, (tq, LANE), 1)
    o_ref[0, 0] = jnp.where(lane < V_HEAD, outs[0], outs[1])


def _attention(q, k, v, tq, tk):
    b, _, s, _ = q.shape
    return pl.pallas_call(
        functools.partial(_attn_kernel, tk=tk),
        grid=(b, N_PAIRS, s // tq),
        in_specs=[
            pl.BlockSpec((1, 2, tq, HEAD_PAD), lambda bi, pi, qi: (bi, pi, qi, 0)),
            pl.BlockSpec((1, 2, s, HEAD_PAD), lambda bi, pi, qi: (bi, pi, 0, 0)),
            pl.BlockSpec((1, 1, s, LANE), lambda bi, pi, qi: (bi, pi, 0, 0)),
        ],
        out_specs=pl.BlockSpec((1, 1, tq, LANE), lambda bi, pi, qi: (bi, pi, qi, 0)),
        out_shape=jax.ShapeDtypeStruct((b, N_PAIRS, s, LANE), F32),
        compiler_params=_cparams(("parallel", "parallel", "parallel")),
        name="attn_prompt",
    )(q, k, v)


def _absorb_kernel(q_ref, wlat_ref, wpe_ref, qlat_ref, qpe_ref):
    for h in range(N_HEADS):
        q = q_ref[h]
        qlat_ref[h] = _dot(q, wlat_ref[h])
        qpe_ref[h] = _dot(q, wpe_ref[...])


def _absorb(q, wts):
    _, n, _ = q.shape
    return pl.pallas_call(
        _absorb_kernel,
        out_shape=[jax.ShapeDtypeStruct((N_HEADS, n, KV_LORA), F32),
                   jax.ShapeDtypeStruct((N_HEADS, n, QK_ROPE), F32)],
        name="absorb_q",
    )(q, wts["w_uk_t"], wts["w_pe_sel"])


def _sample_attn_kernel(pt_ref, qlat_ref, qpe_ref, cnew_ref, knew_ref, ckv_hbm, kr_hbm, o_ref,
                        cbuf, kbuf, sem, *, n_chunks, total_chunks):
    b = pl.program_id(0)
    cp = PAGES_PER_CHUNK

    def copies(g, slot):
        out = []
        for p in range(cp):
            page = pt_ref[g * cp + p]
            rows = pl.ds(p * PAGE_SIZE, PAGE_SIZE)
            out.append(pltpu.make_async_copy(ckv_hbm.at[page], cbuf.at[slot, rows], sem.at[0, slot]))
            out.append(pltpu.make_async_copy(kr_hbm.at[page], kbuf.at[slot, :, rows], sem.at[1, slot]))
        return out

    def start(g, slot):
        for c in copies(g, slot):
            c.start()

    def wait(g, slot):
        for c in copies(g, slot):
            c.wait()

    @pl.when(b == 0)
    def _():
        start(0, 0)

    qlat = qlat_ref[0].astype(BF16)
    qpe = qpe_ref[0].astype(BF16)

    def chunk(c, carry):
        m, l, acc = carry
        g = b * n_chunks + c
        slot = lax.rem(g, 2)

        @pl.when(g + 1 < total_chunks)
        def _():
            start(g + 1, 1 - slot)

        wait(g, slot)
        cb = cbuf[slot].astype(BF16)
        kb = kbuf[slot].astype(BF16)
        s = _dot_nt(qlat, cb) + _dot(qpe, kb)
        m_new = jnp.maximum(m, jnp.max(s, axis=-1, keepdims=True))
        alpha = jnp.exp2((m - m_new) * EXP2_SCALE)
        p = jnp.exp2((s - m_new) * EXP2_SCALE)
        l = alpha * l + jnp.sum(p, axis=-1, keepdims=True)
        acc = alpha * acc + _dot(p.astype(BF16), cb)
        return m_new, l, acc

    carry = (jnp.full((N_HEADS, 1), NEG_INF, F32), jnp.zeros((N_HEADS, 1), F32),
             jnp.zeros((N_HEADS, KV_LORA), F32))
    m, l, acc = lax.fori_loop(0, n_chunks, chunk, carry)

    cnew = cnew_ref[0]
    knew = knew_ref[0]
    s_new = (jnp.sum(qlat_ref[0] * cnew, axis=-1, keepdims=True)
             + jnp.sum(qpe_ref[0] * knew, axis=-1, keepdims=True))
    m_fin = jnp.maximum(m, s_new)
    alpha = jnp.exp2((m - m_fin) * EXP2_SCALE)
    p_new = jnp.exp2((s_new - m_fin) * EXP2_SCALE)
    l = alpha * l + p_new
    acc = alpha * acc + p_new * cnew
    o_ref[0] = acc / l


def _sample_attention(page_table, qlat, qpe, c_new, k_new, cache_kv, cache_kr):
    bd, n_pages = page_table.shape
    n_chunks = n_pages // PAGES_PER_CHUNK
    rows = PAGES_PER_CHUNK * PAGE_SIZE
    kern = functools.partial(_sample_attn_kernel, n_chunks=n_chunks, total_chunks=bd * n_chunks)
    grid_spec = pltpu.PrefetchScalarGridSpec(
        num_scalar_prefetch=1,
        grid=(bd,),
        in_specs=[
            pl.BlockSpec((1, N_HEADS, KV_LORA), lambda bi, pt: (bi, 0, 0)),
            pl.BlockSpec((1, N_HEADS, QK_ROPE), lambda bi, pt: (bi, 0, 0)),
            pl.BlockSpec((1, 1, KV_LORA), lambda bi, pt: (bi, 0, 0)),
            pl.BlockSpec((1, 1, QK_ROPE), lambda bi, pt: (bi, 0, 0)),
            pl.BlockSpec(memory_space=pl.ANY),
            pl.BlockSpec(memory_space=pl.ANY),
        ],
        out_specs=pl.BlockSpec((1, N_HEADS, KV_LORA), lambda bi, pt: (bi, 0, 0)),
        scratch_shapes=[
            pltpu.VMEM((2, rows, KV_LORA), F32),
            pltpu.VMEM((2, QK_ROPE, rows), F32),
            pltpu.SemaphoreType.DMA((2, 2)),
        ],
    )
    return pl.pallas_call(
        kern,
        grid_spec=grid_spec,
        out_shape=jax.ShapeDtypeStruct((bd, N_HEADS, KV_LORA), F32),
        compiler_params=_cparams(("arbitrary",)),
        name="attn_sample",
    )(page_table.reshape(-1), qlat, qpe, c_new, k_new, cache_kv, cache_kr)


def _value_up_kernel(olat_ref, wv_ref, o_ref):
    w = 2 * KV_LORA
    for p in range(N_PAIRS):
        o_ref[0, p] = _dot(olat_ref[:, p * w:(p + 1) * w].astype(BF16), wv_ref[p])


def _value_up(olat2d, wts):
    n = olat2d.shape[0]
    return pl.pallas_call(
        _value_up_kernel,
        out_shape=jax.ShapeDtypeStruct((1, N_PAIRS, n, LANE), F32),
        name="value_up",
    )(olat2d, wts["w_uv_pair"])


def _merge_kernel(x_ref, att_ref, rec_ref, ag_ref, rgn_ref, wout_ref, g1_ref, b1_ref, wr_ref, bg_ref, be_ref,
                  x1_ref, gsel_ref, gate_ref, cnt_ref):
    tm = x_ref.shape[1]
    att = [att_ref[0, p] for p in range(N_PAIRS)]
    ss = att[0] * att[0]
    for p in range(1, N_PAIRS):
        ss = ss + att[p] * att[p]
    inv = lax.rsqrt(jnp.sum(ss, axis=-1, keepdims=True) / (N_PAIRS * LANE) + RMS_EPS)
    parts = [(att[p] * inv * ag_ref[:, p * LANE:(p + 1) * LANE]).astype(BF16) for p in range(N_PAIRS)]
    parts.append(_rmsnorm(rec_ref[0], rgn_ref[...]).astype(BF16))
    mixed = jnp.concatenate(parts, axis=-1)
    mix = _dot(mixed, wout_ref[...])
    x1 = _layernorm(ALPHA * x_ref[0] + mix, g1_ref[...], b1_ref[...])
    x1_ref[0] = x1

    lt = _dot_nt(wr_ref[...], x1, precision=lax.Precision.HIGHEST)
    g = [lt[k:k + 1, :] for k in range(N_GROUPS)]
    gmax = functools.reduce(jnp.maximum, g)
    ex = [jnp.exp(gk - gmax) for gk in g]
    den = functools.reduce(lambda p, q: p + q, ex)
    best = g[0] + bg_ref[0:1, :]
    idx = jnp.zeros((1, tm), I32)
    for k in range(1, N_GROUPS):
        cand = g[k] + bg_ref[k:k + 1, :]
        upd = cand > best
        idx = jnp.where(upd, k, idx)
        best = jnp.where(upd, cand, best)
    gp = ex[0]
    e_sel = lt[SUBLANE:SUBLANE + EXPERTS_PER_GROUP, :]
    e_bias = jnp.broadcast_to(be_ref[0:EXPERTS_PER_GROUP, :], (EXPERTS_PER_GROUP, tm))
    for k in range(1, N_GROUPS):
        hit = idx == k
        lo = SUBLANE + k * EXPERTS_PER_GROUP
        gp = jnp.where(hit, ex[k], gp)
        e_sel = jnp.where(hit, lt[lo:lo + EXPERTS_PER_GROUP, :], e_sel)
        e_bias = jnp.where(hit, be_ref[k * EXPERTS_PER_GROUP:(k + 1) * EXPERTS_PER_GROUP, :], e_bias)
    g_prob = gp / den
    sc = e_sel + e_bias
    sub = lax.broadcasted_iota(I32, (EXPERTS_PER_GROUP, tm), 0)
    m1 = jnp.max(sc, axis=0, keepdims=True)
    i1 = jnp.min(jnp.where(sc == m1, sub, EXPERTS_PER_GROUP), axis=0, keepdims=True)
    mask1 = sub == i1
    sc2 = jnp.where(mask1, -jnp.inf, sc)
    m2 = jnp.max(sc2, axis=0, keepdims=True)
    i2 = jnp.min(jnp.where(sc2 == m2, sub, EXPERTS_PER_GROUP), axis=0, keepdims=True)
    mask2 = sub == i2
    v1 = jnp.sum(jnp.where(mask1, e_sel, 0.0), axis=0, keepdims=True)
    v2 = jnp.sum(jnp.where(mask2, e_sel, 0.0), axis=0, keepdims=True)
    vm = jnp.maximum(v1, v2)
    e1 = jnp.exp(v1 - vm)
    e2 = jnp.exp(v2 - vm)
    esum = e1 + e2
    gate = g_prob * (jnp.where(mask1, e1 / esum, 0.0) + jnp.where(mask2, e2 / esum, 0.0))
    gsel_ref[0] = idx
    gate_ref[0] = gate
    rowi = lax.broadcasted_iota(I32, (SUBLANE, LANE), 0)
    cnt = jnp.zeros((SUBLANE, LANE), F32)
    for k in range(N_GROUPS):
        ck = jnp.sum(jnp.where(idx == k, 1.0, 0.0), axis=-1, keepdims=True)
        cnt = jnp.where(rowi == k, ck, cnt)
    cnt_ref[0] = cnt


def _merge(x, att, rec, wts, tm):
    b, s, d = x.shape
    nt = s // tm
    mw = wts["w_out"].shape[0]
    n_r = wts["w_router_t"].shape[0]
    full = lambda shape: pl.BlockSpec(shape, lambda bi, si: (0,) * len(shape))
    return pl.pallas_call(
        _merge_kernel,
        grid=(b, nt),
        in_specs=[
            pl.BlockSpec((1, tm, d), lambda bi, si: (bi, si, 0)),
            pl.BlockSpec((1, N_PAIRS, tm, LANE), lambda bi, si: (bi, 0, si, 0)),
            pl.BlockSpec((1, tm, REC_WIDTH), lambda bi, si: (bi, si, 0)),
            full((1, N_PAIRS * LANE)), full((1, REC_WIDTH)), full((mw, d)), full((1, d)), full((1, d)),
            full((n_r, d)), full((N_GROUPS, 1)), full((N_GROUPS * EXPERTS_PER_GROUP, 1)),
        ],
        out_specs=[
            pl.BlockSpec((1, tm, d), lambda bi, si: (bi, si, 0)),
            pl.BlockSpec((1, 1, tm), lambda bi, si: (bi * nt + si, 0, 0)),
            pl.BlockSpec((1, EXPERTS_PER_GROUP, tm), lambda bi, si: (bi * nt + si, 0, 0)),
            pl.BlockSpec((1, SUBLANE, LANE), lambda bi, si: (bi * nt + si, 0, 0)),
        ],
        out_shape=[
            jax.ShapeDtypeStruct((b, s, d), F32),
            jax.ShapeDtypeStruct((b * nt, 1, tm), I32),
            jax.ShapeDtypeStruct((b * nt, EXPERTS_PER_GROUP, tm), F32),
            jax.ShapeDtypeStruct((b * nt, SUBLANE, LANE), F32),
        ],
        compiler_params=_cparams(("parallel", "parallel")),
        name="merge_router",
    )(x, att, rec, wts["att_out_g"], wts["rec_out_g"], wts["w_out"], wts["ln1_g"], wts["ln1_b"],
      wts["w_router_t"], wts["b_group"], wts["b_expert"])


def _moe_kernel(nch_ref, x1_ref, gsel_ref, gate_ref, wgu_ref, wd_ref, g2_ref, b2_ref, o_ref, xb_ref, tri_ref,
                *, ch):
    tm = x1_ref.shape[0]
    ti = pl.program_id(0)
    gi = pl.program_id(1)

    @pl.when(jnp.logical_and(ti == 0, gi == 0))
    def _():
        r = lax.broadcasted_iota(I32, (tm, tm), 0)
        c = lax.broadcasted_iota(I32, (tm, tm), 1)
        tri_ref[...] = jnp.where(r < c, 1.0, 0.0).astype(BF16)

    @pl.when(gi == 0)
    def _():
        xb_ref[...] = x1_ref[...].astype(BF16)
        o_ref[...] = jnp.zeros_like(o_ref)

    in_group = gsel_ref[0] == gi
    member = jnp.broadcast_to(jnp.where(in_group, 1.0, 0.0), (SUBLANE, tm)).astype(BF16)
    before = _dot(member, tri_ref[...])
    rank = jnp.where(in_group, before[0:1, :].astype(I32), -1)
    gate = gate_ref[0]
    g_hi = gate.astype(BF16).astype(F32)
    g_mid = (gate - g_hi).astype(BF16).astype(F32)
    g_lo = (gate - g_hi) - g_mid
    n_terms = 3
    gate_terms = jnp.concatenate(
        [g_hi, g_mid, g_lo, jnp.zeros((LANE - n_terms * EXPERTS_PER_GROUP, tm), F32)], axis=0).astype(BF16)

    def chunk(c, carry):
        slot_id = lax.broadcasted_iota(I32, (ch, tm), 0) + c * ch
        onehot_b = jnp.where(slot_id == rank, 1.0, 0.0).astype(BF16)
        xg = _dot(onehot_b, xb_ref[...]).astype(BF16)
        gt = _dot_nt(onehot_b, gate_terms)
        gc = gt
        for k in range(1, n_terms):
            gc = gc + pltpu.roll(gt, LANE - k * EXPERTS_PER_GROUP, 1)
        acc = jnp.zeros((ch, o_ref.shape[1]), F32)
        for j in range(EXPERTS_PER_GROUP):
            gu = _dot(xg, wgu_ref[0, j])
            hid = jax.nn.silu(gu[:, :D_EXPERT]) * gu[:, D_EXPERT:]
            acc = acc + gc[:, j:j + 1] * _dot(hid.astype(BF16), wd_ref[0, j])
        hi = acc.astype(BF16)
        lo = (acc - hi.astype(F32)).astype(BF16)
        o_ref[...] += _dot_tn(onehot_b, hi) + _dot_tn(onehot_b, lo)
        return carry

    lax.fori_loop(0, nch_ref[ti * N_GROUPS + gi], chunk, 0)

    @pl.when(gi == N_GROUPS - 1)
    def _():
        o_ref[...] = _layernorm(ALPHA * x1_ref[...] + o_ref[...], g2_ref[...], b2_ref[...])


def _moe(x1, gsel, gate, nch, wts, tm, ch):
    n, d = x1.shape
    nt = n // tm
    e2 = 2 * D_EXPERT
    grid_spec = pltpu.PrefetchScalarGridSpec(
        num_scalar_prefetch=1,
        grid=(nt, N_GROUPS),
        in_specs=[
            pl.BlockSpec((tm, d), lambda ti, gi, nc: (ti, 0)),
            pl.BlockSpec((1, 1, tm), lambda ti, gi, nc: (ti, 0, 0)),
            pl.BlockSpec((1, EXPERTS_PER_GROUP, tm), lambda ti, gi, nc: (ti, 0, 0)),
            pl.BlockSpec((1, EXPERTS_PER_GROUP, d, e2), lambda ti, gi, nc: (gi, 0, 0, 0)),
            pl.BlockSpec((1, EXPERTS_PER_GROUP, D_EXPERT, d), lambda ti, gi, nc: (gi, 0, 0, 0)),
            pl.BlockSpec((1, d), lambda ti, gi, nc: (0, 0)),
            pl.BlockSpec((1, d), lambda ti, gi, nc: (0, 0)),
        ],
        out_specs=pl.BlockSpec((tm, d), lambda ti, gi, nc: (ti, 0)),
        scratch_shapes=[pltpu.VMEM((tm, d), BF16), pltpu.VMEM((tm, tm), BF16)],
    )
    return pl.pallas_call(
        functools.partial(_moe_kernel, ch=ch),
        grid_spec=grid_spec,
        out_shape=jax.ShapeDtypeStruct((n, d), F32),
        compiler_params=_cparams(("arbitrary", "arbitrary")),
        name="moe",
    )(nch, x1, gsel, gate, wts["w_gate_up"], wts["w_down"], wts["ln2_g"], wts["ln2_b"])


def _merge_and_ffn(x, att, rec, wts, tm_merge, tm_moe, ch):
    b, s, d = x.shape
    x1, gsel, gate, cnt = _merge(x, att, rec, wts, tm_merge)
    n = b * s
    nt = n // tm_moe
    f = tm_moe // tm_merge
    gsel = gsel.reshape(nt, 1, tm_moe)
    gate = gate.reshape(nt, f, EXPERTS_PER_GROUP, tm_merge).transpose(0, 2, 1, 3).reshape(nt, EXPERTS_PER_GROUP, tm_moe)
    counts = cnt[:, :N_GROUPS, 0].reshape(nt, f, N_GROUPS).sum(axis=1).astype(I32)
    nch = ((counts + (ch - 1)) // ch).reshape(-1)
    y = _moe(x1.reshape(n, d), gsel, gate, nch, wts, tm_moe, ch)
    return y.reshape(b, s, d)


def _swap_halves(w):
    half = QK_ROPE // 2
    return jnp.concatenate([w[..., half:], w[..., :half]], axis=-1)


def _prep_weights(w_in, q_norm_g, w_uq, kv_norm_g, w_uk, w_uv, conv_w, conv_b, w_rg, b_rg, w_ig, b_ig,
                  lru_lambda, att_out_g, rec_out_g, w_out, ln1_g, ln1_b, w_group, b_group, w_expert,
                  b_expert, w_gate_up, w_down, ln2_g, ln2_b):
    d = w_in.shape[0]
    o1, o2, o3, o4 = Q_LORA, Q_LORA + KV_LORA, Q_LORA + KV_LORA + QK_ROPE, Q_LORA + KV_LORA + QK_ROPE + REC_WIDTH
    w_kpe = w_in[:, o2:o3]
    pad_lo = jnp.zeros((d, QK_NOPE), F32)
    pad_hi = jnp.zeros((d, HEAD_PAD - QK_NOPE - QK_ROPE), F32)
    w_in_ext = jnp.concatenate([
        w_in[:, :o2],
        pad_lo, w_kpe, pad_hi,
        pad_lo, _swap_halves(w_kpe), pad_hi,
        w_in[:, o3:o4], w_in[:, o4:],
    ], axis=1).astype(BF16)

    nope, pe = w_uq[..., :QK_NOPE], w_uq[..., QK_NOPE:]
    zq = lambda n: jnp.zeros((Q_LORA, N_HEADS, n), F32)
    q_main = jnp.concatenate([nope, pe, zq(HEAD_PAD - QK_NOPE - QK_ROPE)], axis=-1)
    q_swap = jnp.concatenate([zq(QK_NOPE), _swap_halves(pe), zq(HEAD_PAD - QK_NOPE - QK_ROPE)], axis=-1)
    w_uq_ext = jnp.concatenate([q_main.reshape(Q_LORA, -1), q_swap.reshape(Q_LORA, -1)], axis=1).astype(BF16)

    k_pad = jnp.concatenate([w_uk, jnp.zeros((KV_LORA, N_HEADS, HEAD_PAD - QK_NOPE), F32)], axis=-1)
    w_ukv = jnp.concatenate([k_pad.reshape(KV_LORA, -1), w_uv.reshape(KV_LORA, -1)], axis=1).astype(BF16)

    w_uk_t = jnp.concatenate([w_uk.transpose(1, 2, 0),
                              jnp.zeros((N_HEADS, HEAD_PAD - QK_NOPE, KV_LORA), F32)], axis=1).astype(BF16)
    sel = jnp.zeros((HEAD_PAD, QK_ROPE), F32).at[QK_NOPE + jnp.arange(QK_ROPE), jnp.arange(QK_ROPE)].set(1.0)
    w_uv_h = w_uv.transpose(1, 0, 2)
    zero_v = jnp.zeros((KV_LORA, V_HEAD), F32)
    w_uv_pair = jnp.stack([
        jnp.concatenate([jnp.concatenate([w_uv_h[2 * p], zero_v], axis=1),
                         jnp.concatenate([zero_v, w_uv_h[2 * p + 1]], axis=1)], axis=0)
        for p in range(N_PAIRS)]).astype(BF16)

    def block_diag(w):
        eye = jnp.eye(REC_BLOCKS, dtype=F32)
        return jnp.einsum('nde,nm->ndme', w, eye).reshape(REC_WIDTH, REC_WIDTH)

    bd_r, bd_i = block_diag(w_rg), block_diag(w_ig)
    half = REC_WIDTH // 2
    w_gate = jnp.stack([
        jnp.concatenate([bd_r[j * half:(j + 1) * half, j * half:(j + 1) * half],
                         bd_i[j * half:(j + 1) * half, j * half:(j + 1) * half]], axis=1)
        for j in range(2)]).astype(BF16)

    w_router_t = jnp.concatenate([w_group.T, jnp.zeros((SUBLANE - N_GROUPS, d), F32), w_expert.T], axis=0)
    row = lambda v: v.reshape(1, -1)
    return {
        "w_in": w_in_ext, "q_norm_g": row(q_norm_g), "w_uq": w_uq_ext, "kv_norm_g": row(kv_norm_g),
        "w_ukv": w_ukv, "w_uk_t": w_uk_t, "w_pe_sel": sel.astype(BF16), "w_uv_pair": w_uv_pair,
        "conv_w": conv_w, "conv_b": row(conv_b), "w_gate": w_gate, "b_rg": row(b_rg), "b_ig": row(b_ig),
        "lru_lambda": row(lru_lambda), "att_out_g": row(att_out_g), "rec_out_g": row(rec_out_g),
        "w_out": w_out.astype(BF16), "ln1_g": row(ln1_g), "ln1_b": row(ln1_b),
        "w_router_t": w_router_t, "b_group": b_group.reshape(-1, 1), "b_expert": b_expert.reshape(-1, 1),
        "w_gate_up": w_gate_up.astype(BF16).reshape(N_GROUPS, EXPERTS_PER_GROUP, d, 2 * D_EXPERT),
        "w_down": w_down.astype(BF16).reshape(N_GROUPS, EXPERTS_PER_GROUP, D_EXPERT, d),
        "ln2_g": row(ln2_g), "ln2_b": row(ln2_b),
    }


def _rope_tables(pos):
    half = QK_ROPE // 2
    inv = ROPE_THETA ** (-(jnp.arange(half, dtype=F32) * 2.0 / QK_ROPE))
    ang = pos.astype(F32)[:, None] * inv[None, :]
    cos, sin = jnp.cos(ang), jnp.sin(ang)
    t = pos.shape[0]
    cos_t = jnp.concatenate([jnp.ones((t, QK_NOPE), F32), cos, cos,
                             jnp.zeros((t, HEAD_PAD - QK_NOPE - QK_ROPE), F32)], axis=1)
    sin_t = jnp.concatenate([jnp.zeros((t, QK_NOPE), F32), -sin, sin,
                             jnp.zeros((t, HEAD_PAD - QK_NOPE - QK_ROPE), F32)], axis=1)
    return cos_t, sin_t


def kernel(x_prompt, x_sample, cache_kv_latent, cache_k_rope, state_conv, state_rec, page_table,
           w_in, q_norm_g, w_uq, kv_norm_g, w_uk, w_uv, conv_w, conv_b, w_rg, b_rg, w_ig, b_ig,
           lru_lambda, att_out_g, rec_out_g, w_out, ln1_g, ln1_b, w_group, b_group, w_expert,
           b_expert, w_gate_up, w_down, ln2_g, ln2_b):
    wts = _prep_weights(w_in, q_norm_g, w_uq, kv_norm_g, w_uk, w_uv, conv_w, conv_b, w_rg, b_rg, w_ig, b_ig,
                        lru_lambda, att_out_g, rec_out_g, w_out, ln1_g, ln1_b, w_group, b_group, w_expert,
                        b_expert, w_gate_up, w_down, ln2_g, ln2_b)
    bp, sp, d = x_prompt.shape
    bd, td, _ = x_sample.shape
    assert td == 1, "the sample path handles one new token per sequence"
    past_len = page_table.shape[1] * PAGE_SIZE
    ko = QK_NOPE

    tm_proj = min(TM_PROJ, sp)
    cos_p, sin_p = _rope_tables(jnp.arange(sp, dtype=I32))
    q, k, v, c_p, kpe_blk, rx, rg = _project(x_prompt, cos_p, sin_p, wts, tm_proj)
    att_p = _attention(q, k, v, min(TQ_ATT, sp), min(TK_ATT, sp))
    rec_p, h_p = _lru_prompt(rx, rg, wts, min(TM_LRU, sp))
    tm_moe = min(TM_MOE, bp * sp)
    y_p = _merge_and_ffn(x_prompt, att_p, rec_p, wts, min(TM_MERGE, sp, tm_moe), tm_moe, CH_MOE)
    kpe_p = kpe_blk[..., ko:ko + QK_ROPE]
    conv_p = rx[:, sp - (CONV_W - 1):, :]

    xs = x_sample.reshape(1, bd, d)
    cos_s, sin_s = _rope_tables(jnp.full((bd,), past_len, I32))
    q_s, _, _, c_s, kpe_s_blk, rx_s, rg_s = _project(xs, cos_s, sin_s, wts, bd)
    kpe_s = kpe_s_blk[0, :, ko:ko + QK_ROPE]
    qlat, qpe = _absorb(q_s[0], wts)
    o_lat = _sample_attention(page_table, qlat.transpose(1, 0, 2), qpe.transpose(1, 0, 2),
                              c_s.reshape(bd, 1, KV_LORA), kpe_s.reshape(bd, 1, QK_ROPE),
                              cache_kv_latent, cache_k_rope.transpose(0, 2, 1))
    att_s = _value_up(o_lat.reshape(bd, N_HEADS * KV_LORA), wts)
    rec_s, conv_s_t, h_s = _lru_step(rx_s[0], rg_s[0], state_conv.transpose(1, 0, 2), state_rec, wts)
    y_s = _merge_and_ffn(xs, att_s, rec_s.reshape(1, bd, REC_WIDTH), wts, bd, bd, min(CH_MOE, bd))

    return (y_p, y_s.reshape(bd, 1, d), c_p, kpe_p, conv_p, h_p.reshape(bp, REC_WIDTH),
            c_s.reshape(bd, 1, KV_LORA), kpe_s.reshape(bd, 1, QK_ROPE), conv_s_t.transpose(1, 0, 2), h_s)
```

```python
import functools
import math

import jax
import jax.numpy as jnp
from jax import lax
from jax.experimental import pallas as pl
from jax.experimental.pallas import tpu as pltpu

F32 = jnp.float32
BF16 = jnp.bfloat16
I32 = jnp.int32

N_HEADS = 8
QK_NOPE = 64
QK_ROPE = 32
V_HEAD = 64
Q_LORA = 384
KV_LORA = 256
ROPE_THETA = 10000.0
SM_SCALE = (QK_NOPE + QK_ROPE) ** -0.5
REC_WIDTH = 512
REC_BLOCKS = 8
REC_BLOCK_W = REC_WIDTH // REC_BLOCKS
CONV_W = 4
LRU_C = 8.0
N_GROUPS = 4
EXPERTS_PER_GROUP = 8
D_EXPERT = 256
DEPTH = 1
ALPHA = (2.0 * DEPTH) ** 0.25
LN_EPS = 1e-5
RMS_EPS = 1e-6
NEG_INF = -1e30
PAGE_SIZE = 128

LANE = 128
SUBLANE = 8
HEAD_PAD = LANE
N_PAIRS = N_HEADS * V_HEAD // LANE
VMEM_LIMIT = 56 * 1024 * 1024

T_ATT = 512
KEY_STRIP = 512
TM_LRU = 256
TM_MERGE = 512
TM_MOE = 1024
CH_MOE = 256
PAGES_PER_CHUNK = 32
CACHE_SLOTS = 3
EXP2_SCALE = SM_SCALE * math.log2(math.e)


def _dot(a, b):
    return jnp.dot(a, b, preferred_element_type=F32)


def _dot_nt(a, b, precision=None):
    return lax.dot_general(a, b, (((1,), (1,)), ((), ())), preferred_element_type=F32, precision=precision)


def _dot_tn(a, b):
    return lax.dot_general(a, b, (((0,), (0,)), ((), ())), preferred_element_type=F32)


def _cparams(semantics, flags=None):
    return pltpu.CompilerParams(dimension_semantics=semantics, vmem_limit_bytes=VMEM_LIMIT, flags=flags)


def _rmsnorm(x, g):
    return x * lax.rsqrt(jnp.mean(x * x, axis=-1, keepdims=True) + RMS_EPS) * g


def _layernorm(x, g, b):
    mu = jnp.mean(x, axis=-1, keepdims=True)
    xc = x - mu
    var = jnp.mean(xc * xc, axis=-1, keepdims=True)
    return xc * lax.rsqrt(var + LN_EPS) * g + b


def _gelu_tanh(x):
    return x * (0.5 * (1.0 + jnp.tanh(math.sqrt(2.0 / math.pi) * (x + 0.044715 * (x * x * x)))))


def _lru_coeffs(xc, pre_r, pre_i, b_r, b_i, lam):
    r = jax.nn.sigmoid(pre_r + b_r)
    i = jax.nn.sigmoid(pre_i + b_i)
    neg_lam = -lam
    softplus = jnp.maximum(neg_lam, 0.0) + jnp.log1p(jnp.exp(-jnp.abs(neg_lam)))
    log_a = (-LRU_C * softplus) * r
    a = jnp.exp(log_a)
    u = jnp.sqrt(-jnp.tanh(log_a) * (a * a + 1.0)) * (i * xc)
    return a, u


def _gate_preacts(xc, wg_ref):
    half = REC_WIDTH // 2
    g0 = _dot(xc[:, :half].astype(BF16), wg_ref[0])
    g1 = _dot(xc[:, half:].astype(BF16), wg_ref[1])
    pre_r = jnp.concatenate([g0[:, :half], g1[:, :half]], axis=1)
    pre_i = jnp.concatenate([g0[:, half:], g1[:, half:]], axis=1)
    return pre_r, pre_i


def _proj_kernel(x_ref, cos_ref, sin_ref, cost_ref, sint_ref, win_ref, qg_ref, wuqt_ref, kvg_ref, wuk_ref, wuvt_ref,
                 qt_ref, k_ref, vt_ref, ckv_ref, kpe_ref, rx_ref, rg_ref):
    x = x_ref[0].astype(BF16)
    z = _dot(x, win_ref[...])
    o_kv = Q_LORA
    o_ka = o_kv + KV_LORA
    o_kb = o_ka + LANE
    o_rx = o_kb + LANE
    o_rg = o_rx + REC_WIDTH
    qn = _rmsnorm(z[:, :o_kv], qg_ref[...]).astype(BF16)
    qq = _dot_nt(wuqt_ref[...], qn)
    cos_t = cost_ref[...]
    sin_t = sint_ref[...]
    sw = N_HEADS * HEAD_PAD
    for h in range(N_HEADS):
        lo = h * HEAD_PAD
        qt_ref[0, h, 0] = (qq[lo:lo + HEAD_PAD] * cos_t + qq[sw + lo:sw + lo + HEAD_PAD] * sin_t).astype(BF16)
    ckv = _rmsnorm(z[:, o_kv:o_ka], kvg_ref[...])
    ckv_ref[0] = ckv
    kpe = z[:, o_ka:o_kb] * cos_ref[...] + z[:, o_kb:o_rx] * sin_ref[...]
    kpe_ref[0] = kpe
    ckv_b = ckv.astype(BF16)
    kn = _dot(ckv_b, wuk_ref[...])
    for h in range(N_HEADS):
        lo = h * HEAD_PAD
        k_ref[0, h] = (kn[:, lo:lo + HEAD_PAD] + kpe).astype(BF16)
    vt = _dot_nt(wuvt_ref[...], ckv_b)
    for p in range(N_PAIRS):
        vt_ref[0, p, 0] = vt[p * LANE:(p + 1) * LANE].astype(BF16)
    rx_ref[0] = z[:, o_rx:o_rg]
    rg_ref[0] = z[:, o_rg:o_rg + REC_WIDTH]


def _project(x, tables, wts, tm):
    b, s, d = x.shape
    nt = s // tm
    in_w = wts["w_in"].shape[1]
    full = lambda shape: pl.BlockSpec(shape, lambda bi, si: (0,) * len(shape))
    cos_n, sin_n, cos_t, sin_t = tables
    return pl.pallas_call(
        _proj_kernel,
        grid=(b, nt),
        in_specs=[
            pl.BlockSpec((1, tm, d), lambda bi, si: (bi, si, 0)),
            pl.BlockSpec((tm, LANE), lambda bi, si: (si, 0)),
            pl.BlockSpec((tm, LANE), lambda bi, si: (si, 0)),
            pl.BlockSpec((HEAD_PAD, tm), lambda bi, si: (0, si)),
            pl.BlockSpec((HEAD_PAD, tm), lambda bi, si: (0, si)),
            full((d, in_w)),
            full((1, Q_LORA)),
            full((2 * N_HEADS * HEAD_PAD, Q_LORA)),
            full((1, KV_LORA)),
            full((KV_LORA, N_HEADS * HEAD_PAD)),
            full((N_PAIRS * LANE, KV_LORA)),
        ],
        out_specs=[
            pl.BlockSpec((1, N_HEADS, 1, HEAD_PAD, tm), lambda bi, si: (bi, 0, si, 0, 0)),
            pl.BlockSpec((1, N_HEADS, tm, HEAD_PAD), lambda bi, si: (bi, 0, si, 0)),
            pl.BlockSpec((1, N_PAIRS, 1, LANE, tm), lambda bi, si: (bi, 0, si, 0, 0)),
            pl.BlockSpec((1, tm, KV_LORA), lambda bi, si: (bi, si, 0)),
            pl.BlockSpec((1, tm, LANE), lambda bi, si: (bi, si, 0)),
            pl.BlockSpec((1, tm, REC_WIDTH), lambda bi, si: (bi, si, 0)),
            pl.BlockSpec((1, tm, REC_WIDTH), lambda bi, si: (bi, si, 0)),
        ],
        out_shape=[
            jax.ShapeDtypeStruct((b, N_HEADS, nt, HEAD_PAD, tm), BF16),
            jax.ShapeDtypeStruct((b, N_HEADS, s, HEAD_PAD), BF16),
            jax.ShapeDtypeStruct((b, N_PAIRS, nt, LANE, tm), BF16),
            jax.ShapeDtypeStruct((b, s, KV_LORA), F32),
            jax.ShapeDtypeStruct((b, s, LANE), F32),
            jax.ShapeDtypeStruct((b, s, REC_WIDTH), F32),
            jax.ShapeDtypeStruct((b, s, REC_WIDTH), F32),
        ],
        compiler_params=_cparams(("parallel", "parallel")),
        name="proj",
    )(x, cos_n, sin_n, cos_t, sin_t, wts["w_in"], wts["q_norm_g"], wts["w_uq_t"], wts["kv_norm_g"],
      wts["w_uk"], wts["w_uv_t"])


def _lru_prompt_kernel(rx_ref, rg_ref, cw_ref, cb_ref, wg_ref, br_ref, bi_ref, lam_ref,
                       y_ref, hlast_ref, xp_ref, h_ref):
    tm = rx_ref.shape[1]
    si = pl.program_id(1)

    @pl.when(si == 0)
    def _():
        xp_ref[0:SUBLANE, :] = jnp.zeros((SUBLANE, REC_WIDTH), F32)
        h_ref[...] = jnp.zeros_like(h_ref)

    x = rx_ref[0]
    xp_ref[SUBLANE:SUBLANE + tm, :] = x
    xc = cb_ref[...] + x * cw_ref[CONV_W - 1:CONV_W, :]
    for m in range(1, CONV_W):
        xc = xc + xp_ref[pl.ds(SUBLANE - m, tm), :] * cw_ref[CONV_W - 1 - m:CONV_W - m, :]
    xp_ref[0:SUBLANE, :] = x[tm - SUBLANE:, :]

    pre_r, pre_i = _gate_preacts(xc, wg_ref)
    a, u = _lru_coeffs(xc, pre_r, pre_i, br_ref[...], bi_ref[...], lam_ref[...])

    row = lax.broadcasted_iota(I32, (tm, REC_WIDTH), 0)
    d = 1
    while d < tm:
        keep = row >= d
        a_sh = jnp.where(keep, pltpu.roll(a, d, 0), 1.0)
        u_sh = jnp.where(keep, pltpu.roll(u, d, 0), 0.0)
        u = u + a * u_sh
        a = a * a_sh
        d *= 2
    h = a * h_ref[...] + u
    h_ref[...] = h[tm - 1:tm, :]
    hlast_ref[0] = h[tm - 1:tm, :]
    y_ref[0] = h * _gelu_tanh(rg_ref[0])


def _lru_prompt(rx, rg, wts, tm):
    b, s, w = rx.shape
    full = lambda shape: pl.BlockSpec(shape, lambda bi, si: (0,) * len(shape))
    return pl.pallas_call(
        _lru_prompt_kernel,
        grid=(b, s // tm),
        in_specs=[
            pl.BlockSpec((1, tm, w), lambda bi, si: (bi, si, 0)),
            pl.BlockSpec((1, tm, w), lambda bi, si: (bi, si, 0)),
            full((CONV_W, w)), full((1, w)), full((2, w // 2, w)), full((1, w)), full((1, w)), full((1, w)),
        ],
        out_specs=[
            pl.BlockSpec((1, tm, w), lambda bi, si: (bi, si, 0)),
            pl.BlockSpec((1, 1, w), lambda bi, si: (bi, 0, 0)),
        ],
        out_shape=[jax.ShapeDtypeStruct((b, s, w), F32), jax.ShapeDtypeStruct((b, 1, w), F32)],
        scratch_shapes=[pltpu.VMEM((tm + SUBLANE, w), F32), pltpu.VMEM((1, w), F32)],
        compiler_params=_cparams(("arbitrary", "arbitrary")),
        name="lru_prompt",
    )(rx, rg, wts["conv_w"], wts["conv_b"], wts["w_gate"], wts["b_rg"], wts["b_ig"], wts["lru_lambda"])


def _lru_step_kernel(rx_ref, rg_ref, conv_ref, h0_ref, cw_ref, cb_ref, wg_ref, br_ref, bi_ref, lam_ref,
                     y_ref, newconv_ref, h_ref):
    x = rx_ref[...]
    xc = cb_ref[...] + x * cw_ref[CONV_W - 1:CONV_W, :]
    for k in range(CONV_W - 1):
        xc = xc + conv_ref[k] * cw_ref[k:k + 1, :]
    pre_r, pre_i = _gate_preacts(xc, wg_ref)
    a, u = _lru_coeffs(xc, pre_r, pre_i, br_ref[...], bi_ref[...], lam_ref[...])
    h = a * h0_ref[...] + u
    h_ref[...] = h
    y_ref[...] = h * _gelu_tanh(rg_ref[...])
    for k in range(CONV_W - 2):
        newconv_ref[k] = conv_ref[k + 1]
    newconv_ref[CONV_W - 2] = x


def _lru_step(rx, rg, conv_t, h0, wts):
    n, w = rx.shape
    return pl.pallas_call(
        _lru_step_kernel,
        out_shape=[jax.ShapeDtypeStruct((n, w), F32),
                   jax.ShapeDtypeStruct((CONV_W - 1, n, w), F32),
                   jax.ShapeDtypeStruct((n, w), F32)],
        name="lru_step",
    )(rx, rg, conv_t, h0, wts["conv_w"], wts["conv_b"], wts["w_gate"], wts["b_rg"], wts["b_ig"], wts["lru_lambda"])


def _attn_kernel(qt_ref, k_ref, vt_ref, o_ref):
    t = qt_ref.shape[-1]
    qi = pl.program_id(2)
    ks = KEY_STRIP
    key_i = lax.broadcasted_iota(I32, (ks, t), 0)
    qry_i = lax.broadcasted_iota(I32, (ks, t), 1)
    qts = [qt_ref[0, e, 0] for e in range(2)]

    def scores(j, r, e):
        kb = k_ref[0, e, pl.ds(pl.multiple_of(j * t + r * ks, ks), ks), :]
        return _dot(kb, qts[e])

    def step(j, carry, diagonal):
        carry = list(carry)
        units = [(r, e) for r in range(t // ks) for e in range(2)]
        st_next = scores(j, *units[0])
        for u, (r, e) in enumerate(units):
            st = st_next
            if u + 1 < len(units):
                st_next = scores(j, *units[u + 1])
            m, l, acc = carry[e]
            vb = vt_ref[0, 0, j, e * V_HEAD:(e + 1) * V_HEAD, r * ks:(r + 1) * ks]
            if diagonal:
                st = jnp.where(key_i + r * ks <= qry_i, st, NEG_INF)
            m_new = jnp.maximum(m, jnp.max(st, axis=0, keepdims=True))
            alpha = jnp.exp2((m - m_new) * EXP2_SCALE)
            pt = jnp.exp2((st - m_new) * EXP2_SCALE)
            l = alpha * l + jnp.sum(pt, axis=0, keepdims=True)
            acc = alpha * acc + _dot(vb, pt.astype(BF16))
            carry[e] = (m_new, l, acc)
        return tuple(carry)

    init = (jnp.full((1, t), NEG_INF, F32), jnp.zeros((1, t), F32), jnp.zeros((V_HEAD, t), F32))
    carry = lax.fori_loop(0, qi, functools.partial(step, diagonal=False), (init, init))
    carry = step(qi, carry, diagonal=True)
    halves = [acc / l for (_, l, acc) in carry]
    o_ref[0, 0] = jnp.concatenate(halves, axis=0).T


def _attention(qt, k, vt):
    b, _, nt, _, t = qt.shape
    s = nt * t
    return pl.pallas_call(
        _attn_kernel,
        grid=(b, N_PAIRS, nt),
        in_specs=[
            pl.BlockSpec((1, 2, 1, HEAD_PAD, t), lambda bi, pi, qi: (bi, pi, qi, 0, 0)),
            pl.BlockSpec((1, 2, s, HEAD_PAD), lambda bi, pi, qi: (bi, pi, 0, 0)),
            pl.BlockSpec((1, 1, nt, LANE, t), lambda bi, pi, qi: (bi, pi, 0, 0, 0)),
        ],
        out_specs=pl.BlockSpec((1, 1, t, LANE), lambda bi, pi, qi: (bi, pi, qi, 0)),
        out_shape=jax.ShapeDtypeStruct((b, N_PAIRS, s, LANE), F32),
        compiler_params=_cparams(("parallel", "parallel", "parallel")),
        name="attn_prompt",
    )(qt, k, vt)


def _absorb_kernel(qt_ref, wlat_ref, wpe_ref, qlat_ref, qpe_ref):
    for h in range(N_HEADS):
        qt = qt_ref[h]
        qlat_ref[h] = _dot_tn(qt, wlat_ref[h])
        qpe_ref[h] = _dot_tn(qt, wpe_ref[...])


def _absorb(qt, wts):
    _, _, n = qt.shape
    return pl.pallas_call(
        _absorb_kernel,
        out_shape=[jax.ShapeDtypeStruct((N_HEADS, n, KV_LORA), F32),
                   jax.ShapeDtypeStruct((N_HEADS, n, QK_ROPE), F32)],
        name="absorb_q",
    )(qt, wts["w_uk_t"], wts["w_pe_sel"])


def _sample_attn_kernel(pt_ref, qlat_ref, qpe_ref, cnew_ref, knew_ref, ckv_hbm, kr_hbm, o_ref,
                        cbuf, kbuf, sem, *, n_chunks, total_chunks):
    b = pl.program_id(0)
    cp = PAGES_PER_CHUNK

    def copies(g, slot):
        out = []
        for p in range(cp):
            page = pt_ref[g * cp + p]
            rows = pl.ds(p * PAGE_SIZE, PAGE_SIZE)
            out.append(pltpu.make_async_copy(ckv_hbm.at[page], cbuf.at[slot, rows], sem.at[0, slot]))
            out.append(pltpu.make_async_copy(kr_hbm.at[page], kbuf.at[slot, :, rows], sem.at[1, slot]))
        return out

    def start(g, slot):
        for c in copies(g, slot):
            c.start()

    def wait(g, slot):
        for c in copies(g, slot):
            c.wait()

    ahead = CACHE_SLOTS - 1

    @pl.when(b == 0)
    def _():
        for g0 in range(min(ahead, total_chunks)):
            start(g0, g0)

    qlat = qlat_ref[0].astype(BF16)
    qpe = qpe_ref[0].astype(BF16)

    def chunk(c, carry):
        m, l, acc = carry
        g = b * n_chunks + c
        slot = lax.rem(g, CACHE_SLOTS)

        @pl.when(g + ahead < total_chunks)
        def _():
            start(g + ahead, lax.rem(g + ahead, CACHE_SLOTS))

        wait(g, slot)
        cb = cbuf[slot].astype(BF16)
        kb = kbuf[slot].astype(BF16)
        s = _dot_nt(qlat, cb) + _dot(qpe, kb)
        m_new = jnp.maximum(m, jnp.max(s, axis=-1, keepdims=True))
        alpha = jnp.exp2((m - m_new) * EXP2_SCALE)
        p = jnp.exp2((s - m_new) * EXP2_SCALE)
        l = alpha * l + jnp.sum(p, axis=-1, keepdims=True)
        acc = alpha * acc + _dot(p.astype(BF16), cb)
        return m_new, l, acc

    carry = (jnp.full((N_HEADS, 1), NEG_INF, F32), jnp.zeros((N_HEADS, 1), F32),
             jnp.zeros((N_HEADS, KV_LORA), F32))
    m, l, acc = lax.fori_loop(0, n_chunks, chunk, carry)

    cnew = cnew_ref[0]
    knew = knew_ref[0]
    s_new = (jnp.sum(qlat_ref[0] * cnew, axis=-1, keepdims=True)
             + jnp.sum(qpe_ref[0] * knew, axis=-1, keepdims=True))
    m_fin = jnp.maximum(m, s_new)
    alpha = jnp.exp2((m - m_fin) * EXP2_SCALE)
    p_new = jnp.exp2((s_new - m_fin) * EXP2_SCALE)
    l = alpha * l + p_new
    acc = alpha * acc + p_new * cnew
    o_ref[0] = acc / l


def _sample_attention(page_table, qlat, qpe, c_new, k_new, cache_kv, cache_kr):
    bd, n_pages = page_table.shape
    n_chunks = n_pages // PAGES_PER_CHUNK
    rows = PAGES_PER_CHUNK * PAGE_SIZE
    kern = functools.partial(_sample_attn_kernel, n_chunks=n_chunks, total_chunks=bd * n_chunks)
    grid_spec = pltpu.PrefetchScalarGridSpec(
        num_scalar_prefetch=1,
        grid=(bd,),
        in_specs=[
            pl.BlockSpec((1, N_HEADS, KV_LORA), lambda bi, pt: (bi, 0, 0)),
            pl.BlockSpec((1, N_HEADS, QK_ROPE), lambda bi, pt: (bi, 0, 0)),
            pl.BlockSpec((1, 1, KV_LORA), lambda bi, pt: (bi, 0, 0)),
            pl.BlockSpec((1, 1, QK_ROPE), lambda bi, pt: (bi, 0, 0)),
            pl.BlockSpec(memory_space=pl.ANY),
            pl.BlockSpec(memory_space=pl.ANY),
        ],
        out_specs=pl.BlockSpec((1, N_HEADS, KV_LORA), lambda bi, pt: (bi, 0, 0)),
        scratch_shapes=[
            pltpu.VMEM((CACHE_SLOTS, rows, KV_LORA), F32),
            pltpu.VMEM((CACHE_SLOTS, QK_ROPE, rows), F32),
            pltpu.SemaphoreType.DMA((2, CACHE_SLOTS)),
        ],
    )
    return pl.pallas_call(
        kern,
        grid_spec=grid_spec,
        out_shape=jax.ShapeDtypeStruct((bd, N_HEADS, KV_LORA), F32),
        compiler_params=_cparams(("arbitrary",)),
        name="attn_sample",
    )(page_table.reshape(-1), qlat, qpe, c_new, k_new, cache_kv, cache_kr)


def _value_up_kernel(olat_ref, wv_ref, o_ref):
    w = 2 * KV_LORA
    for p in range(N_PAIRS):
        o_ref[0, p] = _dot(olat_ref[:, p * w:(p + 1) * w].astype(BF16), wv_ref[p])


def _value_up(olat2d, wts):
    n = olat2d.shape[0]
    return pl.pallas_call(
        _value_up_kernel,
        out_shape=jax.ShapeDtypeStruct((1, N_PAIRS, n, LANE), F32),
        name="value_up",
    )(olat2d, wts["w_uv_pair"])


def _merge_kernel(x_ref, att_ref, rec_ref, ag_ref, rgn_ref, wout_ref, g1_ref, b1_ref, wr_ref, bg_ref, be_ref,
                  x1_ref, gsel_ref, gate_ref, cnt_ref):
    tm = x_ref.shape[1]
    att = [att_ref[0, p] for p in range(N_PAIRS)]
    ss = att[0] * att[0]
    for p in range(1, N_PAIRS):
        ss = ss + att[p] * att[p]
    inv = lax.rsqrt(jnp.sum(ss, axis=-1, keepdims=True) / (N_PAIRS * LANE) + RMS_EPS)
    parts = [(att[p] * inv * ag_ref[:, p * LANE:(p + 1) * LANE]).astype(BF16) for p in range(N_PAIRS)]
    parts.append(_rmsnorm(rec_ref[0], rgn_ref[...]).astype(BF16))
    mixed = jnp.concatenate(parts, axis=-1)
    mix = _dot(mixed, wout_ref[...])
    x1 = _layernorm(ALPHA * x_ref[0] + mix, g1_ref[...], b1_ref[...])
    x1_ref[0] = x1

    lt = _dot_nt(wr_ref[...], x1, precision=lax.Precision.HIGHEST)
    g = [lt[k:k + 1, :] for k in range(N_GROUPS)]
    gmax = functools.reduce(jnp.maximum, g)
    ex = [jnp.exp(gk - gmax) for gk in g]
    den = functools.reduce(lambda p, q: p + q, ex)
    best = g[0] + bg_ref[0:1, :]
    idx = jnp.zeros((1, tm), I32)
    for k in range(1, N_GROUPS):
        cand = g[k] + bg_ref[k:k + 1, :]
        upd = cand > best
        idx = jnp.where(upd, k, idx)
        best = jnp.where(upd, cand, best)
    gp = ex[0]
    e_sel = lt[SUBLANE:SUBLANE + EXPERTS_PER_GROUP, :]
    e_bias = jnp.broadcast_to(be_ref[0:EXPERTS_PER_GROUP, :], (EXPERTS_PER_GROUP, tm))
    for k in range(1, N_GROUPS):
        hit = idx == k
        lo = SUBLANE + k * EXPERTS_PER_GROUP
        gp = jnp.where(hit, ex[k], gp)
        e_sel = jnp.where(hit, lt[lo:lo + EXPERTS_PER_GROUP, :], e_sel)
        e_bias = jnp.where(hit, be_ref[k * EXPERTS_PER_GROUP:(k + 1) * EXPERTS_PER_GROUP, :], e_bias)
    g_prob = gp / den
    sc = e_sel + e_bias
    sub = lax.broadcasted_iota(I32, (EXPERTS_PER_GROUP, tm), 0)
    m1 = jnp.max(sc, axis=0, keepdims=True)
    i1 = jnp.min(jnp.where(sc == m1, sub, EXPERTS_PER_GROUP), axis=0, keepdims=True)
    mask1 = sub == i1
    sc2 = jnp.where(mask1, -jnp.inf, sc)
    m2 = jnp.max(sc2, axis=0, keepdims=True)
    i2 = jnp.min(jnp.where(sc2 == m2, sub, EXPERTS_PER_GROUP), axis=0, keepdims=True)
    mask2 = sub == i2
    v1 = jnp.sum(jnp.where(mask1, e_sel, 0.0), axis=0, keepdims=True)
    v2 = jnp.sum(jnp.where(mask2, e_sel, 0.0), axis=0, keepdims=True)
    vm = jnp.maximum(v1, v2)
    e1 = jnp.exp(v1 - vm)
    e2 = jnp.exp(v2 - vm)
    esum = e1 + e2
    gate = g_prob * (jnp.where(mask1, e1 / esum, 0.0) + jnp.where(mask2, e2 / esum, 0.0))
    gsel_ref[0] = idx
    gate_ref[0] = gate
    rowi = lax.broadcasted_iota(I32, (SUBLANE, LANE), 0)
    cnt = jnp.zeros((SUBLANE, LANE), F32)
    for k in range(N_GROUPS):
        ck = jnp.sum(jnp.where(idx == k, 1.0, 0.0), axis=-1, keepdims=True)
        cnt = jnp.where(rowi == k, ck, cnt)
    cnt_ref[0] = cnt


def _merge(x, att, rec, wts, tm):
    b, s, d = x.shape
    nt = s // tm
    mw = wts["w_out"].shape[0]
    n_r = wts["w_router_t"].shape[0]
    full = lambda shape: pl.BlockSpec(shape, lambda bi, si: (0,) * len(shape))
    return pl.pallas_call(
        _merge_kernel,
        grid=(b, nt),
        in_specs=[
            pl.BlockSpec((1, tm, d), lambda bi, si: (bi, si, 0)),
            pl.BlockSpec((1, N_PAIRS, tm, LANE), lambda bi, si: (bi, 0, si, 0)),
            pl.BlockSpec((1, tm, REC_WIDTH), lambda bi, si: (bi, si, 0)),
            full((1, N_PAIRS * LANE)), full((1, REC_WIDTH)), full((mw, d)), full((1, d)), full((1, d)),
            full((n_r, d)), full((N_GROUPS, 1)), full((N_GROUPS * EXPERTS_PER_GROUP, 1)),
        ],
        out_specs=[
            pl.BlockSpec((1, tm, d), lambda bi, si: (bi, si, 0)),
            pl.BlockSpec((1, 1, tm), lambda bi, si: (bi * nt + si, 0, 0)),
            pl.BlockSpec((1, EXPERTS_PER_GROUP, tm), lambda bi, si: (bi * nt + si, 0, 0)),
            pl.BlockSpec((1, SUBLANE, LANE), lambda bi, si: (bi * nt + si, 0, 0)),
        ],
        out_shape=[
            jax.ShapeDtypeStruct((b, s, d), F32),
            jax.ShapeDtypeStruct((b * nt, 1, tm), I32),
            jax.ShapeDtypeStruct((b * nt, EXPERTS_PER_GROUP, tm), F32),
            jax.ShapeDtypeStruct((b * nt, SUBLANE, LANE), F32),
        ],
        compiler_params=_cparams(("parallel", "parallel")),
        name="merge_router",
    )(x, att, rec, wts["att_out_g"], wts["rec_out_g"], wts["w_out"], wts["ln1_g"], wts["ln1_b"],
      wts["w_router_t"], wts["b_group"], wts["b_expert"])


def _moe_kernel(nch_ref, x1_ref, gsel_ref, gate_ref, wgu_ref, wd_ref, g2_ref, b2_ref, o_ref, xb_ref, tri_ref,
                *, ch):
    tm = x1_ref.shape[0]
    ti = pl.program_id(0)
    gi = pl.program_id(1)

    @pl.when(jnp.logical_and(ti == 0, gi == 0))
    def _():
        r = lax.broadcasted_iota(I32, (tm, tm), 0)
        c = lax.broadcasted_iota(I32, (tm, tm), 1)
        tri_ref[...] = jnp.where(r < c, 1.0, 0.0).astype(BF16)

    @pl.when(gi == 0)
    def _():
        xb_ref[...] = x1_ref[...].astype(BF16)
        o_ref[...] = jnp.zeros_like(o_ref)

    in_group = gsel_ref[0] == gi
    member = jnp.broadcast_to(jnp.where(in_group, 1.0, 0.0), (SUBLANE, tm)).astype(BF16)
    before = _dot(member, tri_ref[...])
    rank = jnp.where(in_group, before[0:1, :].astype(I32), -1)
    gate = gate_ref[0]
    g_hi = gate.astype(BF16).astype(F32)
    g_mid = (gate - g_hi).astype(BF16).astype(F32)
    g_lo = (gate - g_hi) - g_mid
    n_terms = 3
    gate_terms = jnp.concatenate(
        [g_hi, g_mid, g_lo, jnp.zeros((LANE - n_terms * EXPERTS_PER_GROUP, tm), F32)], axis=0).astype(BF16)

    def chunk(c, carry):
        slot_id = lax.broadcasted_iota(I32, (ch, tm), 0) + c * ch
        onehot_b = jnp.where(slot_id == rank, 1.0, 0.0).astype(BF16)
        xg = _dot(onehot_b, xb_ref[...]).astype(BF16)
        gt = _dot_nt(onehot_b, gate_terms)
        gc = gt
        for k in range(1, n_terms):
            gc = gc + pltpu.roll(gt, LANE - k * EXPERTS_PER_GROUP, 1)
        acc = jnp.zeros((ch, o_ref.shape[1]), F32)
        for j in range(EXPERTS_PER_GROUP):
            gu = _dot(xg, wgu_ref[0, j])
            hid = jax.nn.silu(gu[:, :D_EXPERT]) * gu[:, D_EXPERT:]
            acc = acc + gc[:, j:j + 1] * _dot(hid.astype(BF16), wd_ref[0, j])
        o_ref[...] += _dot_tn(onehot_b, acc.astype(BF16))
        return carry

    lax.fori_loop(0, nch_ref[ti * N_GROUPS + gi], chunk, 0)

    @pl.when(gi == N_GROUPS - 1)
    def _():
        o_ref[...] = _layernorm(ALPHA * x1_ref[...] + o_ref[...], g2_ref[...], b2_ref[...])


def _moe(x1, gsel, gate, nch, wts, tm, ch):
    n, d = x1.shape
    nt = n // tm
    e2 = 2 * D_EXPERT
    grid_spec = pltpu.PrefetchScalarGridSpec(
        num_scalar_prefetch=1,
        grid=(nt, N_GROUPS),
        in_specs=[
            pl.BlockSpec((tm, d), lambda ti, gi, nc: (ti, 0)),
            pl.BlockSpec((1, 1, tm), lambda ti, gi, nc: (ti, 0, 0)),
            pl.BlockSpec((1, EXPERTS_PER_GROUP, tm), lambda ti, gi, nc: (ti, 0, 0)),
            pl.BlockSpec((1, EXPERTS_PER_GROUP, d, e2), lambda ti, gi, nc: (gi, 0, 0, 0)),
            pl.BlockSpec((1, EXPERTS_PER_GROUP, D_EXPERT, d), lambda ti, gi, nc: (gi, 0, 0, 0)),
            pl.BlockSpec((1, d), lambda ti, gi, nc: (0, 0)),
            pl.BlockSpec((1, d), lambda ti, gi, nc: (0, 0)),
        ],
        out_specs=pl.BlockSpec((tm, d), lambda ti, gi, nc: (ti, 0)),
        scratch_shapes=[pltpu.VMEM((tm, d), BF16), pltpu.VMEM((tm, tm), BF16)],
    )
    return pl.pallas_call(
        functools.partial(_moe_kernel, ch=ch),
        grid_spec=grid_spec,
        out_shape=jax.ShapeDtypeStruct((n, d), F32),
        compiler_params=_cparams(("arbitrary", "arbitrary")),
        name="moe",
    )(nch, x1, gsel, gate, wts["w_gate_up"], wts["w_down"], wts["ln2_g"], wts["ln2_b"])


def _merge_and_ffn(x, att, rec, wts, tm_merge, tm_moe, ch):
    b, s, d = x.shape
    x1, gsel, gate, cnt = _merge(x, att, rec, wts, tm_merge)
    n = b * s
    nt = n // tm_moe
    f = tm_moe // tm_merge
    gsel = gsel.reshape(nt, 1, tm_moe)
    gate = gate.reshape(nt, f, EXPERTS_PER_GROUP, tm_merge).transpose(0, 2, 1, 3).reshape(nt, EXPERTS_PER_GROUP, tm_moe)
    counts = cnt[:, :N_GROUPS, 0].reshape(nt, f, N_GROUPS).sum(axis=1).astype(I32)
    nch = ((counts + (ch - 1)) // ch).reshape(-1)
    y = _moe(x1.reshape(n, d), gsel, gate, nch, wts, tm_moe, ch)
    return y.reshape(b, s, d)


def _swap_halves(w):
    half = QK_ROPE // 2
    return jnp.concatenate([w[..., half:], w[..., :half]], axis=-1)


def _prep_weights(w_in, q_norm_g, w_uq, kv_norm_g, w_uk, w_uv, conv_w, conv_b, w_rg, b_rg, w_ig, b_ig,
                  lru_lambda, att_out_g, rec_out_g, w_out, ln1_g, ln1_b, w_group, b_group, w_expert,
                  b_expert, w_gate_up, w_down, ln2_g, ln2_b):
    d = w_in.shape[0]
    o1, o2, o3, o4 = Q_LORA, Q_LORA + KV_LORA, Q_LORA + KV_LORA + QK_ROPE, Q_LORA + KV_LORA + QK_ROPE + REC_WIDTH
    w_kpe = w_in[:, o2:o3]
    pad_lo = jnp.zeros((d, QK_NOPE), F32)
    pad_hi = jnp.zeros((d, HEAD_PAD - QK_NOPE - QK_ROPE), F32)
    w_in_ext = jnp.concatenate([
        w_in[:, :o2],
        pad_lo, w_kpe, pad_hi,
        pad_lo, _swap_halves(w_kpe), pad_hi,
        w_in[:, o3:o4], w_in[:, o4:],
    ], axis=1).astype(BF16)

    nope, pe = w_uq[..., :QK_NOPE], w_uq[..., QK_NOPE:]
    zq = lambda n: jnp.zeros((Q_LORA, N_HEADS, n), F32)
    q_main = jnp.concatenate([nope, pe, zq(HEAD_PAD - QK_NOPE - QK_ROPE)], axis=-1)
    q_swap = jnp.concatenate([zq(QK_NOPE), _swap_halves(pe), zq(HEAD_PAD - QK_NOPE - QK_ROPE)], axis=-1)
    w_uq_t = jnp.concatenate([q_main.reshape(Q_LORA, -1), q_swap.reshape(Q_LORA, -1)], axis=1).T.astype(BF16)

    k_pad = jnp.concatenate([w_uk, jnp.zeros((KV_LORA, N_HEADS, HEAD_PAD - QK_NOPE), F32)], axis=-1)
    w_uk_pad = k_pad.reshape(KV_LORA, -1).astype(BF16)
    w_uv_t = w_uv.reshape(KV_LORA, -1).T.astype(BF16)

    w_uk_t = jnp.concatenate([w_uk.transpose(1, 2, 0),
                              jnp.zeros((N_HEADS, HEAD_PAD - QK_NOPE, KV_LORA), F32)], axis=1).astype(BF16)
    sel = jnp.zeros((HEAD_PAD, QK_ROPE), F32).at[QK_NOPE + jnp.arange(QK_ROPE), jnp.arange(QK_ROPE)].set(1.0)
    w_uv_h = w_uv.transpose(1, 0, 2)
    zero_v = jnp.zeros((KV_LORA, V_HEAD), F32)
    w_uv_pair = jnp.stack([
        jnp.concatenate([jnp.concatenate([w_uv_h[2 * p], zero_v], axis=1),
                         jnp.concatenate([zero_v, w_uv_h[2 * p + 1]], axis=1)], axis=0)
        for p in range(N_PAIRS)]).astype(BF16)

    def block_diag(w):
        eye = jnp.eye(REC_BLOCKS, dtype=F32)
        return jnp.einsum('nde,nm->ndme', w, eye).reshape(REC_WIDTH, REC_WIDTH)

    bd_r, bd_i = block_diag(w_rg), block_diag(w_ig)
    half = REC_WIDTH // 2
    w_gate = jnp.stack([
        jnp.concatenate([bd_r[j * half:(j + 1) * half, j * half:(j + 1) * half],
                         bd_i[j * half:(j + 1) * half, j * half:(j + 1) * half]], axis=1)
        for j in range(2)]).astype(BF16)

    w_router_t = jnp.concatenate([w_group.T, jnp.zeros((SUBLANE - N_GROUPS, d), F32), w_expert.T], axis=0)
    row = lambda v: v.reshape(1, -1)
    return {
        "w_in": w_in_ext, "q_norm_g": row(q_norm_g), "w_uq_t": w_uq_t, "kv_norm_g": row(kv_norm_g),
        "w_uk": w_uk_pad, "w_uv_t": w_uv_t, "w_uk_t": w_uk_t, "w_pe_sel": sel.astype(BF16), "w_uv_pair": w_uv_pair,
        "conv_w": conv_w, "conv_b": row(conv_b), "w_gate": w_gate, "b_rg": row(b_rg), "b_ig": row(b_ig),
        "lru_lambda": row(lru_lambda), "att_out_g": row(att_out_g), "rec_out_g": row(rec_out_g),
        "w_out": w_out.astype(BF16), "ln1_g": row(ln1_g), "ln1_b": row(ln1_b),
        "w_router_t": w_router_t, "b_group": b_group.reshape(-1, 1), "b_expert": b_expert.reshape(-1, 1),
        "w_gate_up": w_gate_up.astype(BF16).reshape(N_GROUPS, EXPERTS_PER_GROUP, d, 2 * D_EXPERT),
        "w_down": w_down.astype(BF16).reshape(N_GROUPS, EXPERTS_PER_GROUP, D_EXPERT, d),
        "ln2_g": row(ln2_g), "ln2_b": row(ln2_b),
    }


def _rope_tables(pos):
    half = QK_ROPE // 2
    inv = ROPE_THETA ** (-(jnp.arange(half, dtype=F32) * 2.0 / QK_ROPE))
    ang = pos.astype(F32)[:, None] * inv[None, :]
    cos, sin = jnp.cos(ang), jnp.sin(ang)
    t = pos.shape[0]
    cos_t = jnp.concatenate([jnp.ones((t, QK_NOPE), F32), cos, cos,
                             jnp.zeros((t, HEAD_PAD - QK_NOPE - QK_ROPE), F32)], axis=1)
    sin_t = jnp.concatenate([jnp.zeros((t, QK_NOPE), F32), -sin, sin,
                             jnp.zeros((t, HEAD_PAD - QK_NOPE - QK_ROPE), F32)], axis=1)
    return cos_t, sin_t, cos_t.T, sin_t.T


def kernel(x_prompt, x_sample, cache_kv_latent, cache_k_rope, state_conv, state_rec, page_table,
           w_in, q_norm_g, w_uq, kv_norm_g, w_uk, w_uv, conv_w, conv_b, w_rg, b_rg, w_ig, b_ig,
           lru_lambda, att_out_g, rec_out_g, w_out, ln1_g, ln1_b, w_group, b_group, w_expert,
           b_expert, w_gate_up, w_down, ln2_g, ln2_b):
    wts = _prep_weights(w_in, q_norm_g, w_uq, kv_norm_g, w_uk, w_uv, conv_w, conv_b, w_rg, b_rg, w_ig, b_ig,
                        lru_lambda, att_out_g, rec_out_g, w_out, ln1_g, ln1_b, w_group, b_group, w_expert,
                        b_expert, w_gate_up, w_down, ln2_g, ln2_b)
    bp, sp, d = x_prompt.shape
    bd, td, _ = x_sample.shape
    assert td == 1, "the sample path handles one new token per sequence"
    past_len = page_table.shape[1] * PAGE_SIZE
    ko = QK_NOPE

    qt, k, vt, c_p, kpe_blk, rx, rg = _project(x_prompt, _rope_tables(jnp.arange(sp, dtype=I32)), wts,
                                               min(T_ATT, sp))
    att_p = _attention(qt, k, vt)
    rec_p, h_p = _lru_prompt(rx, rg, wts, min(TM_LRU, sp))
    tm_moe = min(TM_MOE, bp * sp)
    y_p = _merge_and_ffn(x_prompt, att_p, rec_p, wts, min(TM_MERGE, sp, tm_moe), tm_moe, CH_MOE)
    kpe_p = kpe_blk[..., ko:ko + QK_ROPE]
    conv_p = rx[:, sp - (CONV_W - 1):, :]

    xs = x_sample.reshape(1, bd, d)
    qt_s, _, _, c_s, kpe_s_blk, rx_s, rg_s = _project(xs, _rope_tables(jnp.full((bd,), past_len, I32)), wts, bd)
    kpe_s = kpe_s_blk[0, :, ko:ko + QK_ROPE]
    qlat, qpe = _absorb(qt_s[0, :, 0], wts)
    o_lat = _sample_attention(page_table, qlat.transpose(1, 0, 2), qpe.transpose(1, 0, 2),
                              c_s.reshape(bd, 1, KV_LORA), kpe_s.reshape(bd, 1, QK_ROPE),
                              cache_kv_latent, cache_k_rope.transpose(0, 2, 1))
    att_s = _value_up(o_lat.reshape(bd, N_HEADS * KV_LORA), wts)
    rec_s, conv_s_t, h_s = _lru_step(rx_s[0], rg_s[0], state_conv.transpose(1, 0, 2), state_rec, wts)
    y_s = _merge_and_ffn(xs, att_s, rec_s.reshape(1, bd, REC_WIDTH), wts, bd, bd, min(CH_MOE, bd))

    return (y_p, y_s.reshape(bd, 1, d), c_p, kpe_p, conv_p, h_p.reshape(bp, REC_WIDTH),
            c_s.reshape(bd, 1, KV_LORA), kpe_s.reshape(bd, 1, QK_ROPE), conv_s_t.transpose(1, 0, 2), h_s)
```

```python
import functools
import math

import jax
import jax.numpy as jnp
from jax import lax
from jax.experimental import pallas as pl
from jax.experimental.pallas import tpu as pltpu

F32 = jnp.float32
BF16 = jnp.bfloat16
I32 = jnp.int32

N_HEADS = 8
QK_NOPE = 64
QK_ROPE = 32
V_HEAD = 64
Q_LORA = 384
KV_LORA = 256
ROPE_THETA = 10000.0
SM_SCALE = (QK_NOPE + QK_ROPE) ** -0.5
REC_WIDTH = 512
REC_BLOCKS = 8
REC_BLOCK_W = REC_WIDTH // REC_BLOCKS
CONV_W = 4
LRU_C = 8.0
N_GROUPS = 4
EXPERTS_PER_GROUP = 8
D_EXPERT = 256
DEPTH = 1
ALPHA = (2.0 * DEPTH) ** 0.25
LN_EPS = 1e-5
RMS_EPS = 1e-6
NEG_INF = -1e30
PAGE_SIZE = 128

LANE = 128
SUBLANE = 8
HEAD_PAD = LANE
N_PAIRS = N_HEADS * V_HEAD // LANE
VMEM_LIMIT = 56 * 1024 * 1024

T_ATT = 512
KEY_STRIP = 512
TM_LRU = 256
TM_MERGE = 512
TM_MOE = 1024
CH_MOE = 256
PAGES_PER_CHUNK = 32
CACHE_SLOTS = 3
EXP2_SCALE = SM_SCALE * math.log2(math.e)


def _dot(a, b):
    return jnp.dot(a, b, preferred_element_type=F32)


def _dot_nt(a, b, precision=None):
    return lax.dot_general(a, b, (((1,), (1,)), ((), ())), preferred_element_type=F32, precision=precision)


def _dot_tn(a, b):
    return lax.dot_general(a, b, (((0,), (0,)), ((), ())), preferred_element_type=F32)


def _cparams(semantics, flags=None):
    return pltpu.CompilerParams(dimension_semantics=semantics, vmem_limit_bytes=VMEM_LIMIT, flags=flags)


def _rmsnorm(x, g):
    return x * lax.rsqrt(jnp.mean(x * x, axis=-1, keepdims=True) + RMS_EPS) * g


def _layernorm(x, g, b):
    mu = jnp.mean(x, axis=-1, keepdims=True)
    xc = x - mu
    var = jnp.mean(xc * xc, axis=-1, keepdims=True)
    return xc * lax.rsqrt(var + LN_EPS) * g + b


def _gelu_tanh(x):
    return x * (0.5 * (1.0 + jnp.tanh(math.sqrt(2.0 / math.pi) * (x + 0.044715 * (x * x * x)))))


def _lru_coeffs(xc, pre_r, pre_i, b_r, b_i, lam):
    r = jax.nn.sigmoid(pre_r + b_r)
    i = jax.nn.sigmoid(pre_i + b_i)
    neg_lam = -lam
    softplus = jnp.maximum(neg_lam, 0.0) + jnp.log1p(jnp.exp(-jnp.abs(neg_lam)))
    log_a = (-LRU_C * softplus) * r
    a = jnp.exp(log_a)
    u = jnp.sqrt(-jnp.tanh(log_a) * (a * a + 1.0)) * (i * xc)
    return a, u


def _gate_preacts(xc, wg_ref):
    half = REC_WIDTH // 2
    g0 = _dot(xc[:, :half].astype(BF16), wg_ref[0])
    g1 = _dot(xc[:, half:].astype(BF16), wg_ref[1])
    pre_r = jnp.concatenate([g0[:, :half], g1[:, :half]], axis=1)
    pre_i = jnp.concatenate([g0[:, half:], g1[:, half:]], axis=1)
    return pre_r, pre_i


def _proj_kernel(x_ref, cos_ref, sin_ref, cost_ref, sint_ref, win_ref, qg_ref, wuqt_ref, kvg_ref, wuk_ref, wuvt_ref,
                 qt_ref, k_ref, vt_ref, ckv_ref, kpe_ref, rx_ref, rg_ref):
    x = x_ref[0].astype(BF16)
    z = _dot(x, win_ref[...])
    o_kv = Q_LORA
    o_ka = o_kv + KV_LORA
    o_kb = o_ka + LANE
    o_rx = o_kb + LANE
    o_rg = o_rx + REC_WIDTH
    qn = _rmsnorm(z[:, :o_kv], qg_ref[...]).astype(BF16)
    qq = _dot_nt(wuqt_ref[...], qn)
    cos_t = cost_ref[...]
    sin_t = sint_ref[...]
    sw = N_HEADS * HEAD_PAD
    for h in range(N_HEADS):
        lo = h * HEAD_PAD
        qt_ref[0, h, 0] = (qq[lo:lo + HEAD_PAD] * cos_t + qq[sw + lo:sw + lo + HEAD_PAD] * sin_t).astype(BF16)
    ckv = _rmsnorm(z[:, o_kv:o_ka], kvg_ref[...])
    ckv_ref[0] = ckv
    kpe = z[:, o_ka:o_kb] * cos_ref[...] + z[:, o_kb:o_rx] * sin_ref[...]
    kpe_ref[0] = kpe
    ckv_b = ckv.astype(BF16)
    kn = _dot(ckv_b, wuk_ref[...])
    for h in range(N_HEADS):
        lo = h * HEAD_PAD
        k_ref[0, h] = (kn[:, lo:lo + HEAD_PAD] + kpe).astype(BF16)
    vt = _dot_nt(wuvt_ref[...], ckv_b)
    for p in range(N_PAIRS):
        vt_ref[0, p, 0] = vt[p * LANE:(p + 1) * LANE].astype(BF16)
    rx_ref[0] = z[:, o_rx:o_rg]
    rg_ref[0] = z[:, o_rg:o_rg + REC_WIDTH]


def _project(x, tables, wts, tm):
    b, s, d = x.shape
    nt = s // tm
    in_w = wts["w_in"].shape[1]
    full = lambda shape: pl.BlockSpec(shape, lambda bi, si: (0,) * len(shape))
    cos_n, sin_n, cos_t, sin_t = tables
    return pl.pallas_call(
        _proj_kernel,
        grid=(b, nt),
        in_specs=[
            pl.BlockSpec((1, tm, d), lambda bi, si: (bi, si, 0)),
            pl.BlockSpec((tm, LANE), lambda bi, si: (si, 0)),
            pl.BlockSpec((tm, LANE), lambda bi, si: (si, 0)),
            pl.BlockSpec((HEAD_PAD, tm), lambda bi, si: (0, si)),
            pl.BlockSpec((HEAD_PAD, tm), lambda bi, si: (0, si)),
            full((d, in_w)),
            full((1, Q_LORA)),
            full((2 * N_HEADS * HEAD_PAD, Q_LORA)),
            full((1, KV_LORA)),
            full((KV_LORA, N_HEADS * HEAD_PAD)),
            full((N_PAIRS * LANE, KV_LORA)),
        ],
        out_specs=[
            pl.BlockSpec((1, N_HEADS, 1, HEAD_PAD, tm), lambda bi, si: (bi, 0, si, 0, 0)),
            pl.BlockSpec((1, N_HEADS, tm, HEAD_PAD), lambda bi, si: (bi, 0, si, 0)),
            pl.BlockSpec((1, N_PAIRS, 1, LANE, tm), lambda bi, si: (bi, 0, si, 0, 0)),
            pl.BlockSpec((1, tm, KV_LORA), lambda bi, si: (bi, si, 0)),
            pl.BlockSpec((1, tm, LANE), lambda bi, si: (bi, si, 0)),
            pl.BlockSpec((1, tm, REC_WIDTH), lambda bi, si: (bi, si, 0)),
            pl.BlockSpec((1, tm, REC_WIDTH), lambda bi, si: (bi, si, 0)),
        ],
        out_shape=[
            jax.ShapeDtypeStruct((b, N_HEADS, nt, HEAD_PAD, tm), BF16),
            jax.ShapeDtypeStruct((b, N_HEADS, s, HEAD_PAD), BF16),
            jax.ShapeDtypeStruct((b, N_PAIRS, nt, LANE, tm), BF16),
            jax.ShapeDtypeStruct((b, s, KV_LORA), F32),
            jax.ShapeDtypeStruct((b, s, LANE), F32),
            jax.ShapeDtypeStruct((b, s, REC_WIDTH), F32),
            jax.ShapeDtypeStruct((b, s, REC_WIDTH), F32),
        ],
        compiler_params=_cparams(("parallel", "parallel")),
        name="proj",
    )(x, cos_n, sin_n, cos_t, sin_t, wts["w_in"], wts["q_norm_g"], wts["w_uq_t"], wts["kv_norm_g"],
      wts["w_uk"], wts["w_uv_t"])


def _lru_prompt_kernel(rx_ref, rg_ref, cw_ref, cb_ref, wg_ref, br_ref, bi_ref, lam_ref,
                       y_ref, hlast_ref, xp_ref, h_ref):
    tm = rx_ref.shape[1]
    si = pl.program_id(1)

    @pl.when(si == 0)
    def _():
        xp_ref[0:SUBLANE, :] = jnp.zeros((SUBLANE, REC_WIDTH), F32)
        h_ref[...] = jnp.zeros_like(h_ref)

    x = rx_ref[0]
    xp_ref[SUBLANE:SUBLANE + tm, :] = x
    xc = cb_ref[...] + x * cw_ref[CONV_W - 1:CONV_W, :]
    for m in range(1, CONV_W):
        xc = xc + xp_ref[pl.ds(SUBLANE - m, tm), :] * cw_ref[CONV_W - 1 - m:CONV_W - m, :]
    xp_ref[0:SUBLANE, :] = x[tm - SUBLANE:, :]

    pre_r, pre_i = _gate_preacts(xc, wg_ref)
    a, u = _lru_coeffs(xc, pre_r, pre_i, br_ref[...], bi_ref[...], lam_ref[...])

    row = lax.broadcasted_iota(I32, (tm, REC_WIDTH), 0)
    d = 1
    while d < tm:
        keep = row >= d
        a_sh = jnp.where(keep, pltpu.roll(a, d, 0), 1.0)
        u_sh = jnp.where(keep, pltpu.roll(u, d, 0), 0.0)
        u = u + a * u_sh
        a = a * a_sh
        d *= 2
    h = a * h_ref[...] + u
    h_ref[...] = h[tm - 1:tm, :]
    hlast_ref[0] = h[tm - 1:tm, :]
    y_ref[0] = h * _gelu_tanh(rg_ref[0])


def _lru_prompt(rx, rg, wts, tm):
    b, s, w = rx.shape
    full = lambda shape: pl.BlockSpec(shape, lambda bi, si: (0,) * len(shape))
    return pl.pallas_call(
        _lru_prompt_kernel,
        grid=(b, s // tm),
        in_specs=[
            pl.BlockSpec((1, tm, w), lambda bi, si: (bi, si, 0)),
            pl.BlockSpec((1, tm, w), lambda bi, si: (bi, si, 0)),
            full((CONV_W, w)), full((1, w)), full((2, w // 2, w)), full((1, w)), full((1, w)), full((1, w)),
        ],
        out_specs=[
            pl.BlockSpec((1, tm, w), lambda bi, si: (bi, si, 0)),
            pl.BlockSpec((1, 1, w), lambda bi, si: (bi, 0, 0)),
        ],
        out_shape=[jax.ShapeDtypeStruct((b, s, w), F32), jax.ShapeDtypeStruct((b, 1, w), F32)],
        scratch_shapes=[pltpu.VMEM((tm + SUBLANE, w), F32), pltpu.VMEM((1, w), F32)],
        compiler_params=_cparams(("arbitrary", "arbitrary")),
        name="lru_prompt",
    )(rx, rg, wts["conv_w"], wts["conv_b"], wts["w_gate"], wts["b_rg"], wts["b_ig"], wts["lru_lambda"])


def _lru_step_kernel(rx_ref, rg_ref, conv_ref, h0_ref, cw_ref, cb_ref, wg_ref, br_ref, bi_ref, lam_ref,
                     y_ref, newconv_ref, h_ref):
    x = rx_ref[...]
    xc = cb_ref[...] + x * cw_ref[CONV_W - 1:CONV_W, :]
    for k in range(CONV_W - 1):
        xc = xc + conv_ref[k] * cw_ref[k:k + 1, :]
    pre_r, pre_i = _gate_preacts(xc, wg_ref)
    a, u = _lru_coeffs(xc, pre_r, pre_i, br_ref[...], bi_ref[...], lam_ref[...])
    h = a * h0_ref[...] + u
    h_ref[...] = h
    y_ref[...] = h * _gelu_tanh(rg_ref[...])
    for k in range(CONV_W - 2):
        newconv_ref[k] = conv_ref[k + 1]
    newconv_ref[CONV_W - 2] = x


def _lru_step(rx, rg, conv_t, h0, wts):
    n, w = rx.shape
    return pl.pallas_call(
        _lru_step_kernel,
        out_shape=[jax.ShapeDtypeStruct((n, w), F32),
                   jax.ShapeDtypeStruct((CONV_W - 1, n, w), F32),
                   jax.ShapeDtypeStruct((n, w), F32)],
        name="lru_step",
    )(rx, rg, conv_t, h0, wts["conv_w"], wts["conv_b"], wts["w_gate"], wts["b_rg"], wts["b_ig"], wts["lru_lambda"])


def _fused_attn_kernel(pt_ref, qt_ref, k_ref, vt_ref, qlat_ref, qpe_ref, cnew_ref, knew_ref, ckv_hbm, kr_hbm,
                       o_ref, olat_ref, cbuf, kbuf, sem, ms_ref, ls_ref, accs_ref,
                       *, n_batch, n_q, n_chunks, total_chunks):
    t = qt_ref.shape[-1]
    bi, pi, qi = pl.program_id(0), pl.program_id(1), pl.program_id(2)
    steps_per_group = n_q * (n_q + 1) // 2
    total_steps = n_batch * N_PAIRS * steps_per_group
    base = (bi * N_PAIRS + pi) * steps_per_group + (qi * (qi + 1)) // 2
    ks = KEY_STRIP
    key_i = lax.broadcasted_iota(I32, (ks, t), 0)
    qry_i = lax.broadcasted_iota(I32, (ks, t), 1)
    qts = [qt_ref[0, e, 0] for e in range(2)]

    cp = PAGES_PER_CHUNK
    ahead = CACHE_SLOTS - 1

    def copies(g, slot):
        out = []
        for p in range(cp):
            page = pt_ref[g * cp + p]
            rows = pl.ds(p * PAGE_SIZE, PAGE_SIZE)
            out.append(pltpu.make_async_copy(ckv_hbm.at[page], cbuf.at[slot, rows], sem.at[0, slot]))
            out.append(pltpu.make_async_copy(kr_hbm.at[page], kbuf.at[slot, :, rows], sem.at[1, slot]))
        return out

    def start(g, slot):
        for c in copies(g, slot):
            c.start()

    @pl.when(jnp.logical_and(jnp.logical_and(bi == 0, pi == 0), qi == 0))
    def _():
        for g0 in range(min(ahead, total_chunks)):
            start(g0, g0)

    def sample_chunk_stages(g):
        v = {}

        def fetch_and_score():
            @pl.when(g + ahead < total_chunks)
            def _():
                start(g + ahead, lax.rem(g + ahead, CACHE_SLOTS))

            slot = lax.rem(g, CACHE_SLOTS)
            for c in copies(g, slot):
                c.wait()
            v["b"] = lax.div(g, n_chunks)
            v["qlat"] = qlat_ref[v["b"]]
            v["qpe"] = qpe_ref[v["b"]]
            v["cb"] = cbuf[slot].astype(BF16)
            kb = kbuf[slot].astype(BF16)
            v["s"] = _dot_nt(v["qlat"].astype(BF16), v["cb"]) + _dot(v["qpe"].astype(BF16), kb)

        def softmax_and_values():
            first = lax.rem(g, n_chunks) == 0
            m = jnp.where(first, NEG_INF, ms_ref[...])
            l = jnp.where(first, 0.0, ls_ref[...])
            acc = jnp.where(first, 0.0, accs_ref[...])
            s = v["s"]
            m_new = jnp.maximum(m, jnp.max(s, axis=-1, keepdims=True))
            alpha = jnp.exp2((m - m_new) * EXP2_SCALE)
            p = jnp.exp2((s - m_new) * EXP2_SCALE)
            v["l"] = alpha * l + jnp.sum(p, axis=-1, keepdims=True)
            v["acc"] = alpha * acc + _dot(p.astype(BF16), v["cb"])
            v["m"] = m_new

        def finish():
            m_new, l, acc, b = v["m"], v["l"], v["acc"], v["b"]
            ms_ref[...] = m_new
            ls_ref[...] = l
            accs_ref[...] = acc
            cnew = cnew_ref[b]
            knew = knew_ref[b]
            s_new = (jnp.sum(v["qlat"] * cnew, axis=-1, keepdims=True)
                     + jnp.sum(v["qpe"] * knew, axis=-1, keepdims=True))
            m_fin = jnp.maximum(m_new, s_new)
            a_fin = jnp.exp2((m_new - m_fin) * EXP2_SCALE)
            p_new = jnp.exp2((s_new - m_fin) * EXP2_SCALE)
            olat_ref[b] = (a_fin * acc + p_new * cnew) / (a_fin * l + p_new)

        return [fetch_and_score, softmax_and_values, finish]

    def scores(j, r, e):
        kb = k_ref[0, e, pl.ds(pl.multiple_of(j * t + r * ks, ks), ks), :]
        return _dot(kb, qts[e])

    def step(j, carry, diagonal, with_chunk):
        stages = sample_chunk_stages(base + j) if with_chunk else []
        carry = list(carry)
        units = [(r, e) for r in range(t // ks) for e in range(2)]
        st_next = scores(j, *units[0])
        for u, (r, e) in enumerate(units):
            st = st_next
            if u + 1 < len(units):
                st_next = scores(j, *units[u + 1])
            if stages:
                stages.pop(0)()
            m, l, acc = carry[e]
            vb = vt_ref[0, 0, j, e * V_HEAD:(e + 1) * V_HEAD, r * ks:(r + 1) * ks]
            if diagonal:
                st = jnp.where(key_i + r * ks <= qry_i, st, NEG_INF)
            m_new = jnp.maximum(m, jnp.max(st, axis=0, keepdims=True))
            alpha = jnp.exp2((m - m_new) * EXP2_SCALE)
            pt = jnp.exp2((st - m_new) * EXP2_SCALE)
            l = alpha * l + jnp.sum(pt, axis=0, keepdims=True)
            acc = alpha * acc + _dot(vb, pt.astype(BF16))
            carry[e] = (m_new, l, acc)
        for stage in stages:
            stage()
        return tuple(carry)

    init = (jnp.full((1, t), NEG_INF, F32), jnp.zeros((1, t), F32), jnp.zeros((V_HEAD, t), F32))
    n_with = jnp.clip(total_chunks - base, 0, qi)
    carry = lax.fori_loop(0, n_with, functools.partial(step, diagonal=False, with_chunk=True), (init, init))
    carry = lax.fori_loop(n_with, qi, functools.partial(step, diagonal=False, with_chunk=False), carry)
    carry = lax.cond(base + qi < total_chunks,
                     functools.partial(step, qi, diagonal=True, with_chunk=True),
                     functools.partial(step, qi, diagonal=True, with_chunk=False), carry)
    halves = [acc / l for (_, l, acc) in carry]
    o_ref[0, 0] = jnp.concatenate(halves, axis=0).T

    if total_chunks > total_steps:
        @pl.when(jnp.logical_and(jnp.logical_and(bi == n_batch - 1, pi == N_PAIRS - 1), qi == n_q - 1))
        def _():
            def drain(g, c):
                for stage in sample_chunk_stages(g):
                    stage()
                return c
            lax.fori_loop(total_steps, total_chunks, drain, 0)


def _attention(qt, k, vt, page_table, qlat, qpe, c_new, k_new, cache_kv, cache_kr_t):
    b, _, nt, _, t = qt.shape
    s = nt * t
    bd, n_pages = page_table.shape
    n_chunks = n_pages // PAGES_PER_CHUNK
    rows = PAGES_PER_CHUNK * PAGE_SIZE
    kern = functools.partial(_fused_attn_kernel, n_batch=b, n_q=nt, n_chunks=n_chunks, total_chunks=bd * n_chunks)
    whole = lambda shape: pl.BlockSpec(shape, lambda bi, pi, qi, pt: (0,) * len(shape))
    grid_spec = pltpu.PrefetchScalarGridSpec(
        num_scalar_prefetch=1,
        grid=(b, N_PAIRS, nt),
        in_specs=[
            pl.BlockSpec((1, 2, 1, HEAD_PAD, t), lambda bi, pi, qi, pt: (bi, pi, qi, 0, 0)),
            pl.BlockSpec((1, 2, s, HEAD_PAD), lambda bi, pi, qi, pt: (bi, pi, 0, 0)),
            pl.BlockSpec((1, 1, nt, LANE, t), lambda bi, pi, qi, pt: (bi, pi, 0, 0, 0)),
            whole((bd, N_HEADS, KV_LORA)),
            whole((bd, N_HEADS, QK_ROPE)),
            whole((bd, 1, KV_LORA)),
            whole((bd, 1, QK_ROPE)),
            pl.BlockSpec(memory_space=pl.ANY),
            pl.BlockSpec(memory_space=pl.ANY),
        ],
        out_specs=[
            pl.BlockSpec((1, 1, t, LANE), lambda bi, pi, qi, pt: (bi, pi, qi, 0)),
            whole((bd, N_HEADS, KV_LORA)),
        ],
        scratch_shapes=[
            pltpu.VMEM((CACHE_SLOTS, rows, KV_LORA), F32),
            pltpu.VMEM((CACHE_SLOTS, QK_ROPE, rows), F32),
            pltpu.SemaphoreType.DMA((2, CACHE_SLOTS)),
            pltpu.VMEM((N_HEADS, 1), F32),
            pltpu.VMEM((N_HEADS, 1), F32),
            pltpu.VMEM((N_HEADS, KV_LORA), F32),
        ],
    )
    return pl.pallas_call(
        kern,
        grid_spec=grid_spec,
        out_shape=[jax.ShapeDtypeStruct((b, N_PAIRS, s, LANE), F32),
                   jax.ShapeDtypeStruct((bd, N_HEADS, KV_LORA), F32)],
        compiler_params=_cparams(("arbitrary", "arbitrary", "arbitrary")),
        name="attn_fused",
    )(page_table.reshape(-1), qt, k, vt, qlat, qpe, c_new, k_new, cache_kv, cache_kr_t)


def _absorb_kernel(qt_ref, wlat_ref, wpe_ref, qlat_ref, qpe_ref):
    for h in range(N_HEADS):
        qt = qt_ref[h]
        qlat_ref[h] = _dot_tn(qt, wlat_ref[h])
        qpe_ref[h] = _dot_tn(qt, wpe_ref[...])


def _absorb(qt, wts):
    _, _, n = qt.shape
    return pl.pallas_call(
        _absorb_kernel,
        out_shape=[jax.ShapeDtypeStruct((N_HEADS, n, KV_LORA), F32),
                   jax.ShapeDtypeStruct((N_HEADS, n, QK_ROPE), F32)],
        name="absorb_q",
    )(qt, wts["w_uk_t"], wts["w_pe_sel"])


def _value_up_kernel(olat_ref, wv_ref, o_ref):
    w = 2 * KV_LORA
    for p in range(N_PAIRS):
        o_ref[0, p] = _dot(olat_ref[:, p * w:(p + 1) * w].astype(BF16), wv_ref[p])


def _value_up(olat2d, wts):
    n = olat2d.shape[0]
    return pl.pallas_call(
        _value_up_kernel,
        out_shape=jax.ShapeDtypeStruct((1, N_PAIRS, n, LANE), F32),
        name="value_up",
    )(olat2d, wts["w_uv_pair"])


def _merge_kernel(x_ref, att_ref, rec_ref, ag_ref, rgn_ref, wout_ref, g1_ref, b1_ref, wr_ref, bg_ref, be_ref,
                  x1_ref, gsel_ref, gate_ref, cnt_ref):
    tm = x_ref.shape[1]
    att = [att_ref[0, p] for p in range(N_PAIRS)]
    ss = att[0] * att[0]
    for p in range(1, N_PAIRS):
        ss = ss + att[p] * att[p]
    inv = lax.rsqrt(jnp.sum(ss, axis=-1, keepdims=True) / (N_PAIRS * LANE) + RMS_EPS)
    parts = [(att[p] * inv * ag_ref[:, p * LANE:(p + 1) * LANE]).astype(BF16) for p in range(N_PAIRS)]
    parts.append(_rmsnorm(rec_ref[0], rgn_ref[...]).astype(BF16))
    mixed = jnp.concatenate(parts, axis=-1)
    mix = _dot(mixed, wout_ref[...])
    x1 = _layernorm(ALPHA * x_ref[0] + mix, g1_ref[...], b1_ref[...])
    x1_ref[0] = x1

    lt = _dot_nt(wr_ref[...], x1, precision=lax.Precision.HIGHEST)
    g = [lt[k:k + 1, :] for k in range(N_GROUPS)]
    gmax = functools.reduce(jnp.maximum, g)
    ex = [jnp.exp(gk - gmax) for gk in g]
    den = functools.reduce(lambda p, q: p + q, ex)
    best = g[0] + bg_ref[0:1, :]
    idx = jnp.zeros((1, tm), I32)
    for k in range(1, N_GROUPS):
        cand = g[k] + bg_ref[k:k + 1, :]
        upd = cand > best
        idx = jnp.where(upd, k, idx)
        best = jnp.where(upd, cand, best)
    gp = ex[0]
    e_sel = lt[SUBLANE:SUBLANE + EXPERTS_PER_GROUP, :]
    e_bias = jnp.broadcast_to(be_ref[0:EXPERTS_PER_GROUP, :], (EXPERTS_PER_GROUP, tm))
    for k in range(1, N_GROUPS):
        hit = idx == k
        lo = SUBLANE + k * EXPERTS_PER_GROUP
        gp = jnp.where(hit, ex[k], gp)
        e_sel = jnp.where(hit, lt[lo:lo + EXPERTS_PER_GROUP, :], e_sel)
        e_bias = jnp.where(hit, be_ref[k * EXPERTS_PER_GROUP:(k + 1) * EXPERTS_PER_GROUP, :], e_bias)
    g_prob = gp / den
    sc = e_sel + e_bias
    sub = lax.broadcasted_iota(I32, (EXPERTS_PER_GROUP, tm), 0)
    m1 = jnp.max(sc, axis=0, keepdims=True)
    i1 = jnp.min(jnp.where(sc == m1, sub, EXPERTS_PER_GROUP), axis=0, keepdims=True)
    mask1 = sub == i1
    sc2 = jnp.where(mask1, -jnp.inf, sc)
    m2 = jnp.max(sc2, axis=0, keepdims=True)
    i2 = jnp.min(jnp.where(sc2 == m2, sub, EXPERTS_PER_GROUP), axis=0, keepdims=True)
    mask2 = sub == i2
    v1 = jnp.sum(jnp.where(mask1, e_sel, 0.0), axis=0, keepdims=True)
    v2 = jnp.sum(jnp.where(mask2, e_sel, 0.0), axis=0, keepdims=True)
    vm = jnp.maximum(v1, v2)
    e1 = jnp.exp(v1 - vm)
    e2 = jnp.exp(v2 - vm)
    esum = e1 + e2
    gate = g_prob * (jnp.where(mask1, e1 / esum, 0.0) + jnp.where(mask2, e2 / esum, 0.0))
    gsel_ref[0] = idx
    gate_ref[0] = gate
    rowi = lax.broadcasted_iota(I32, (SUBLANE, LANE), 0)
    cnt = jnp.zeros((SUBLANE, LANE), F32)
    for k in range(N_GROUPS):
        ck = jnp.sum(jnp.where(idx == k, 1.0, 0.0), axis=-1, keepdims=True)
        cnt = jnp.where(rowi == k, ck, cnt)
    cnt_ref[0] = cnt


def _merge(x, att, rec, wts, tm):
    b, s, d = x.shape
    nt = s // tm
    mw = wts["w_out"].shape[0]
    n_r = wts["w_router_t"].shape[0]
    full = lambda shape: pl.BlockSpec(shape, lambda bi, si: (0,) * len(shape))
    return pl.pallas_call(
        _merge_kernel,
        grid=(b, nt),
        in_specs=[
            pl.BlockSpec((1, tm, d), lambda bi, si: (bi, si, 0)),
            pl.BlockSpec((1, N_PAIRS, tm, LANE), lambda bi, si: (bi, 0, si, 0)),
            pl.BlockSpec((1, tm, REC_WIDTH), lambda bi, si: (bi, si, 0)),
            full((1, N_PAIRS * LANE)), full((1, REC_WIDTH)), full((mw, d)), full((1, d)), full((1, d)),
            full((n_r, d)), full((N_GROUPS, 1)), full((N_GROUPS * EXPERTS_PER_GROUP, 1)),
        ],
        out_specs=[
            pl.BlockSpec((1, tm, d), lambda bi, si: (bi, si, 0)),
            pl.BlockSpec((1, 1, tm), lambda bi, si: (bi * nt + si, 0, 0)),
            pl.BlockSpec((1, EXPERTS_PER_GROUP, tm), lambda bi, si: (bi * nt + si, 0, 0)),
            pl.BlockSpec((1, SUBLANE, LANE), lambda bi, si: (bi * nt + si, 0, 0)),
        ],
        out_shape=[
            jax.ShapeDtypeStruct((b, s, d), F32),
            jax.ShapeDtypeStruct((b * nt, 1, tm), I32),
            jax.ShapeDtypeStruct((b * nt, EXPERTS_PER_GROUP, tm), F32),
            jax.ShapeDtypeStruct((b * nt, SUBLANE, LANE), F32),
        ],
        compiler_params=_cparams(("parallel", "parallel")),
        name="merge_router",
    )(x, att, rec, wts["att_out_g"], wts["rec_out_g"], wts["w_out"], wts["ln1_g"], wts["ln1_b"],
      wts["w_router_t"], wts["b_group"], wts["b_expert"])


def _moe_kernel(nch_ref, x1_ref, gsel_ref, gate_ref, wgu_ref, wd_ref, g2_ref, b2_ref, o_ref, xb_ref, tri_ref,
                *, ch):
    tm = x1_ref.shape[0]
    ti = pl.program_id(0)
    gi = pl.program_id(1)

    @pl.when(jnp.logical_and(ti == 0, gi == 0))
    def _():
        r = lax.broadcasted_iota(I32, (tm, tm), 0)
        c = lax.broadcasted_iota(I32, (tm, tm), 1)
        tri_ref[...] = jnp.where(r < c, 1.0, 0.0).astype(BF16)

    @pl.when(gi == 0)
    def _():
        xb_ref[...] = x1_ref[...].astype(BF16)
        o_ref[...] = jnp.zeros_like(o_ref)

    in_group = gsel_ref[0] == gi
    member = jnp.broadcast_to(jnp.where(in_group, 1.0, 0.0), (SUBLANE, tm)).astype(BF16)
    before = _dot(member, tri_ref[...])
    rank = jnp.where(in_group, before[0:1, :].astype(I32), -1)
    gate = gate_ref[0]
    g_hi = gate.astype(BF16).astype(F32)
    g_mid = (gate - g_hi).astype(BF16).astype(F32)
    g_lo = (gate - g_hi) - g_mid
    n_terms = 3
    gate_terms = jnp.concatenate(
        [g_hi, g_mid, g_lo, jnp.zeros((LANE - n_terms * EXPERTS_PER_GROUP, tm), F32)], axis=0).astype(BF16)

    def chunk(c, carry):
        slot_id = lax.broadcasted_iota(I32, (ch, tm), 0) + c * ch
        onehot_b = jnp.where(slot_id == rank, 1.0, 0.0).astype(BF16)
        xg = _dot(onehot_b, xb_ref[...]).astype(BF16)
        gt = _dot_nt(onehot_b, gate_terms)
        gc = gt
        for k in range(1, n_terms):
            gc = gc + pltpu.roll(gt, LANE - k * EXPERTS_PER_GROUP, 1)
        acc = jnp.zeros((ch, o_ref.shape[1]), F32)
        for j in range(EXPERTS_PER_GROUP):
            gu = _dot(xg, wgu_ref[0, j])
            hid = jax.nn.silu(gu[:, :D_EXPERT]) * gu[:, D_EXPERT:]
            acc = acc + gc[:, j:j + 1] * _dot(hid.astype(BF16), wd_ref[0, j])
        o_ref[...] += _dot_tn(onehot_b, acc.astype(BF16))
        return carry

    lax.fori_loop(0, nch_ref[ti * N_GROUPS + gi], chunk, 0)

    @pl.when(gi == N_GROUPS - 1)
    def _():
        o_ref[...] = _layernorm(ALPHA * x1_ref[...] + o_ref[...], g2_ref[...], b2_ref[...])


def _moe(x1, gsel, gate, nch, wts, tm, ch):
    n, d = x1.shape
    nt = n // tm
    e2 = 2 * D_EXPERT
    grid_spec = pltpu.PrefetchScalarGridSpec(
        num_scalar_prefetch=1,
        grid=(nt, N_GROUPS),
        in_specs=[
            pl.BlockSpec((tm, d), lambda ti, gi, nc: (ti, 0)),
            pl.BlockSpec((1, 1, tm), lambda ti, gi, nc: (ti, 0, 0)),
            pl.BlockSpec((1, EXPERTS_PER_GROUP, tm), lambda ti, gi, nc: (ti, 0, 0)),
            pl.BlockSpec((1, EXPERTS_PER_GROUP, d, e2), lambda ti, gi, nc: (gi, 0, 0, 0)),
            pl.BlockSpec((1, EXPERTS_PER_GROUP, D_EXPERT, d), lambda ti, gi, nc: (gi, 0, 0, 0)),
            pl.BlockSpec((1, d), lambda ti, gi, nc: (0, 0)),
            pl.BlockSpec((1, d), lambda ti, gi, nc: (0, 0)),
        ],
        out_specs=pl.BlockSpec((tm, d), lambda ti, gi, nc: (ti, 0)),
        scratch_shapes=[pltpu.VMEM((tm, d), BF16), pltpu.VMEM((tm, tm), BF16)],
    )
    return pl.pallas_call(
        functools.partial(_moe_kernel, ch=ch),
        grid_spec=grid_spec,
        out_shape=jax.ShapeDtypeStruct((n, d), F32),
        compiler_params=_cparams(("arbitrary", "arbitrary")),
        name="moe",
    )(nch, x1, gsel, gate, wts["w_gate_up"], wts["w_down"], wts["ln2_g"], wts["ln2_b"])


def _merge_and_ffn(x, att, rec, wts, tm_merge, tm_moe, ch):
    b, s, d = x.shape
    x1, gsel, gate, cnt = _merge(x, att, rec, wts, tm_merge)
    n = b * s
    nt = n // tm_moe
    f = tm_moe // tm_merge
    gsel = gsel.reshape(nt, 1, tm_moe)
    gate = gate.reshape(nt, f, EXPERTS_PER_GROUP, tm_merge).transpose(0, 2, 1, 3).reshape(nt, EXPERTS_PER_GROUP, tm_moe)
    counts = cnt[:, :N_GROUPS, 0].reshape(nt, f, N_GROUPS).sum(axis=1).astype(I32)
    nch = ((counts + (ch - 1)) // ch).reshape(-1)
    y = _moe(x1.reshape(n, d), gsel, gate, nch, wts, tm_moe, ch)
    return y.reshape(b, s, d)


def _swap_halves(w):
    half = QK_ROPE // 2
    return jnp.concatenate([w[..., half:], w[..., :half]], axis=-1)


def _prep_weights(w_in, q_norm_g, w_uq, kv_norm_g, w_uk, w_uv, conv_w, conv_b, w_rg, b_rg, w_ig, b_ig,
                  lru_lambda, att_out_g, rec_out_g, w_out, ln1_g, ln1_b, w_group, b_group, w_expert,
                  b_expert, w_gate_up, w_down, ln2_g, ln2_b):
    d = w_in.shape[0]
    o1, o2, o3, o4 = Q_LORA, Q_LORA + KV_LORA, Q_LORA + KV_LORA + QK_ROPE, Q_LORA + KV_LORA + QK_ROPE + REC_WIDTH
    w_kpe = w_in[:, o2:o3]
    pad_lo = jnp.zeros((d, QK_NOPE), F32)
    pad_hi = jnp.zeros((d, HEAD_PAD - QK_NOPE - QK_ROPE), F32)
    w_in_ext = jnp.concatenate([
        w_in[:, :o2],
        pad_lo, w_kpe, pad_hi,
        pad_lo, _swap_halves(w_kpe), pad_hi,
        w_in[:, o3:o4], w_in[:, o4:],
    ], axis=1).astype(BF16)

    nope, pe = w_uq[..., :QK_NOPE], w_uq[..., QK_NOPE:]
    zq = lambda n: jnp.zeros((Q_LORA, N_HEADS, n), F32)
    q_main = jnp.concatenate([nope, pe, zq(HEAD_PAD - QK_NOPE - QK_ROPE)], axis=-1)
    q_swap = jnp.concatenate([zq(QK_NOPE), _swap_halves(pe), zq(HEAD_PAD - QK_NOPE - QK_ROPE)], axis=-1)
    w_uq_t = jnp.concatenate([q_main.reshape(Q_LORA, -1), q_swap.reshape(Q_LORA, -1)], axis=1).T.astype(BF16)

    k_pad = jnp.concatenate([w_uk, jnp.zeros((KV_LORA, N_HEADS, HEAD_PAD - QK_NOPE), F32)], axis=-1)
    w_uk_pad = k_pad.reshape(KV_LORA, -1).astype(BF16)
    w_uv_t = w_uv.reshape(KV_LORA, -1).T.astype(BF16)

    w_uk_t = jnp.concatenate([w_uk.transpose(1, 2, 0),
                              jnp.zeros((N_HEADS, HEAD_PAD - QK_NOPE, KV_LORA), F32)], axis=1).astype(BF16)
    sel = jnp.zeros((HEAD_PAD, QK_ROPE), F32).at[QK_NOPE + jnp.arange(QK_ROPE), jnp.arange(QK_ROPE)].set(1.0)
    w_uv_h = w_uv.transpose(1, 0, 2)
    zero_v = jnp.zeros((KV_LORA, V_HEAD), F32)
    w_uv_pair = jnp.stack([
        jnp.concatenate([jnp.concatenate([w_uv_h[2 * p], zero_v], axis=1),
                         jnp.concatenate([zero_v, w_uv_h[2 * p + 1]], axis=1)], axis=0)
        for p in range(N_PAIRS)]).astype(BF16)

    def block_diag(w):
        eye = jnp.eye(REC_BLOCKS, dtype=F32)
        return jnp.einsum('nde,nm->ndme', w, eye).reshape(REC_WIDTH, REC_WIDTH)

    bd_r, bd_i = block_diag(w_rg), block_diag(w_ig)
    half = REC_WIDTH // 2
    w_gate = jnp.stack([
        jnp.concatenate([bd_r[j * half:(j + 1) * half, j * half:(j + 1) * half],
                         bd_i[j * half:(j + 1) * half, j * half:(j + 1) * half]], axis=1)
        for j in range(2)]).astype(BF16)

    w_router_t = jnp.concatenate([w_group.T, jnp.zeros((SUBLANE - N_GROUPS, d), F32), w_expert.T], axis=0)
    row = lambda v: v.reshape(1, -1)
    return {
        "w_in": w_in_ext, "q_norm_g": row(q_norm_g), "w_uq_t": w_uq_t, "kv_norm_g": row(kv_norm_g),
        "w_uk": w_uk_pad, "w_uv_t": w_uv_t, "w_uk_t": w_uk_t, "w_pe_sel": sel.astype(BF16), "w_uv_pair": w_uv_pair,
        "conv_w": conv_w, "conv_b": row(conv_b), "w_gate": w_gate, "b_rg": row(b_rg), "b_ig": row(b_ig),
        "lru_lambda": row(lru_lambda), "att_out_g": row(att_out_g), "rec_out_g": row(rec_out_g),
        "w_out": w_out.astype(BF16), "ln1_g": row(ln1_g), "ln1_b": row(ln1_b),
        "w_router_t": w_router_t, "b_group": b_group.reshape(-1, 1), "b_expert": b_expert.reshape(-1, 1),
        "w_gate_up": w_gate_up.astype(BF16).reshape(N_GROUPS, EXPERTS_PER_GROUP, d, 2 * D_EXPERT),
        "w_down": w_down.astype(BF16).reshape(N_GROUPS, EXPERTS_PER_GROUP, D_EXPERT, d),
        "ln2_g": row(ln2_g), "ln2_b": row(ln2_b),
    }


def _rope_tables(pos):
    half = QK_ROPE // 2
    inv = ROPE_THETA ** (-(jnp.arange(half, dtype=F32) * 2.0 / QK_ROPE))
    ang = pos.astype(F32)[:, None] * inv[None, :]
    cos, sin = jnp.cos(ang), jnp.sin(ang)
    t = pos.shape[0]
    cos_t = jnp.concatenate([jnp.ones((t, QK_NOPE), F32), cos, cos,
                             jnp.zeros((t, HEAD_PAD - QK_NOPE - QK_ROPE), F32)], axis=1)
    sin_t = jnp.concatenate([jnp.zeros((t, QK_NOPE), F32), -sin, sin,
                             jnp.zeros((t, HEAD_PAD - QK_NOPE - QK_ROPE), F32)], axis=1)
    return cos_t, sin_t, cos_t.T, sin_t.T


def kernel(x_prompt, x_sample, cache_kv_latent, cache_k_rope, state_conv, state_rec, page_table,
           w_in, q_norm_g, w_uq, kv_norm_g, w_uk, w_uv, conv_w, conv_b, w_rg, b_rg, w_ig, b_ig,
           lru_lambda, att_out_g, rec_out_g, w_out, ln1_g, ln1_b, w_group, b_group, w_expert,
           b_expert, w_gate_up, w_down, ln2_g, ln2_b):
    wts = _prep_weights(w_in, q_norm_g, w_uq, kv_norm_g, w_uk, w_uv, conv_w, conv_b, w_rg, b_rg, w_ig, b_ig,
                        lru_lambda, att_out_g, rec_out_g, w_out, ln1_g, ln1_b, w_group, b_group, w_expert,
                        b_expert, w_gate_up, w_down, ln2_g, ln2_b)
    bp, sp, d = x_prompt.shape
    bd, td, _ = x_sample.shape
    assert td == 1, "the sample path handles one new token per sequence"
    past_len = page_table.shape[1] * PAGE_SIZE
    ko = QK_NOPE

    qt, k, vt, c_p, kpe_blk, rx, rg = _project(x_prompt, _rope_tables(jnp.arange(sp, dtype=I32)), wts,
                                               min(T_ATT, sp))
    xs = x_sample.reshape(1, bd, d)
    qt_s, _, _, c_s, kpe_s_blk, rx_s, rg_s = _project(xs, _rope_tables(jnp.full((bd,), past_len, I32)), wts, bd)
    kpe_s = kpe_s_blk[0, :, ko:ko + QK_ROPE]
    qlat, qpe = _absorb(qt_s[0, :, 0], wts)

    att_p, o_lat = _attention(qt, k, vt, page_table, qlat.transpose(1, 0, 2), qpe.transpose(1, 0, 2),
                              c_s.reshape(bd, 1, KV_LORA), kpe_s.reshape(bd, 1, QK_ROPE),
                              cache_kv_latent, cache_k_rope.transpose(0, 2, 1))

    rec_p, h_p = _lru_prompt(rx, rg, wts, min(TM_LRU, sp))
    tm_moe = min(TM_MOE, bp * sp)
    y_p = _merge_and_ffn(x_prompt, att_p, rec_p, wts, min(TM_MERGE, sp, tm_moe), tm_moe, CH_MOE)
    kpe_p = kpe_blk[..., ko:ko + QK_ROPE]
    conv_p = rx[:, sp - (CONV_W - 1):, :]

    att_s = _value_up(o_lat.reshape(bd, N_HEADS * KV_LORA), wts)
    rec_s, conv_s_t, h_s = _lru_step(rx_s[0], rg_s[0], state_conv.transpose(1, 0, 2), state_rec, wts)
    y_s = _merge_and_ffn(xs, att_s, rec_s.reshape(1, bd, REC_WIDTH), wts, bd, bd, min(CH_MOE, bd))

    return (y_p, y_s.reshape(bd, 1, d), c_p, kpe_p, conv_p, h_p.reshape(bp, REC_WIDTH),
            c_s.reshape(bd, 1, KV_LORA), kpe_s.reshape(bd, 1, QK_ROPE), conv_s_t.transpose(1, 0, 2), h_s)
```

```python
import functools
import math

import jax
import jax.numpy as jnp
from jax import lax
from jax.experimental import pallas as pl
from jax.experimental.pallas import tpu as pltpu

F32 = jnp.float32
BF16 = jnp.bfloat16
I32 = jnp.int32

N_HEADS = 8
QK_NOPE = 64
QK_ROPE = 32
V_HEAD = 64
Q_LORA = 384
KV_LORA = 256
ROPE_THETA = 10000.0
SM_SCALE = (QK_NOPE + QK_ROPE) ** -0.5
REC_WIDTH = 512
REC_BLOCKS = 8
REC_BLOCK_W = REC_WIDTH // REC_BLOCKS
CONV_W = 4
LRU_C = 8.0
N_GROUPS = 4
EXPERTS_PER_GROUP = 8
N_EXPERTS = N_GROUPS * EXPERTS_PER_GROUP
TOP_K = 2
D_EXPERT = 256
DEPTH = 1
ALPHA = (2.0 * DEPTH) ** 0.25
LN_EPS = 1e-5
RMS_EPS = 1e-6
NEG_INF = -1e30
PAGE_SIZE = 128

LANE = 128
SUBLANE = 8
HEAD_PAD = LANE
N_PAIRS = N_HEADS * V_HEAD // LANE
VMEM_LIMIT = 56 * 1024 * 1024

T_ATT = 512
KEY_STRIP = 512
TM_LRU = 256
TM_MERGE = 512
TM_MOE = 1024
CH_MOE = 256
ROW_UNIT = 16
PAGES_PER_CHUNK = 32
CACHE_SLOTS = 3
EXP2_SCALE = SM_SCALE * math.log2(math.e)


def _dot(a, b):
    return jnp.dot(a, b, preferred_element_type=F32)


def _dot_nt(a, b, precision=None):
    return lax.dot_general(a, b, (((1,), (1,)), ((), ())), preferred_element_type=F32, precision=precision)


def _dot_tn(a, b):
    return lax.dot_general(a, b, (((0,), (0,)), ((), ())), preferred_element_type=F32)


def _cparams(semantics, flags=None):
    return pltpu.CompilerParams(dimension_semantics=semantics, vmem_limit_bytes=VMEM_LIMIT, flags=flags)


def _rmsnorm(x, g):
    return x * lax.rsqrt(jnp.mean(x * x, axis=-1, keepdims=True) + RMS_EPS) * g


def _layernorm(x, g, b):
    mu = jnp.mean(x, axis=-1, keepdims=True)
    xc = x - mu
    var = jnp.mean(xc * xc, axis=-1, keepdims=True)
    return xc * lax.rsqrt(var + LN_EPS) * g + b


def _gelu_tanh(x):
    return x * (0.5 * (1.0 + jnp.tanh(math.sqrt(2.0 / math.pi) * (x + 0.044715 * (x * x * x)))))


def _lru_coeffs(xc, pre_r, pre_i, b_r, b_i, lam):
    r = jax.nn.sigmoid(pre_r + b_r)
    i = jax.nn.sigmoid(pre_i + b_i)
    neg_lam = -lam
    softplus = jnp.maximum(neg_lam, 0.0) + jnp.log1p(jnp.exp(-jnp.abs(neg_lam)))
    log_a = (-LRU_C * softplus) * r
    a = jnp.exp(log_a)
    u = jnp.sqrt(-jnp.tanh(log_a) * (a * a + 1.0)) * (i * xc)
    return a, u


def _gate_preacts(xc, wg_ref):
    half = REC_WIDTH // 2
    g0 = _dot(xc[:, :half].astype(BF16), wg_ref[0])
    g1 = _dot(xc[:, half:].astype(BF16), wg_ref[1])
    pre_r = jnp.concatenate([g0[:, :half], g1[:, :half]], axis=1)
    pre_i = jnp.concatenate([g0[:, half:], g1[:, half:]], axis=1)
    return pre_r, pre_i


def _proj_kernel(x_ref, cos_ref, sin_ref, cost_ref, sint_ref, win_ref, qg_ref, wuqt_ref, kvg_ref, wuk_ref, wuvt_ref,
                 qt_ref, k_ref, vt_ref, ckv_ref, kpe_ref, rx_ref, rg_ref):
    x = x_ref[0].astype(BF16)
    z = _dot(x, win_ref[...])
    o_kv = Q_LORA
    o_ka = o_kv + KV_LORA
    o_kb = o_ka + LANE
    o_rx = o_kb + LANE
    o_rg = o_rx + REC_WIDTH
    qn = _rmsnorm(z[:, :o_kv], qg_ref[...]).astype(BF16)
    qq = _dot_nt(wuqt_ref[...], qn)
    cos_t = cost_ref[...]
    sin_t = sint_ref[...]
    sw = N_HEADS * HEAD_PAD
    for h in range(N_HEADS):
        lo = h * HEAD_PAD
        qt_ref[0, h, 0] = (qq[lo:lo + HEAD_PAD] * cos_t + qq[sw + lo:sw + lo + HEAD_PAD] * sin_t).astype(BF16)
    ckv = _rmsnorm(z[:, o_kv:o_ka], kvg_ref[...])
    ckv_ref[0] = ckv
    kpe = z[:, o_ka:o_kb] * cos_ref[...] + z[:, o_kb:o_rx] * sin_ref[...]
    kpe_ref[0] = kpe
    ckv_b = ckv.astype(BF16)
    kn = _dot(ckv_b, wuk_ref[...])
    for h in range(N_HEADS):
        lo = h * HEAD_PAD
        k_ref[0, h] = (kn[:, lo:lo + HEAD_PAD] + kpe).astype(BF16)
    vt = _dot_nt(wuvt_ref[...], ckv_b)
    for p in range(N_PAIRS):
        vt_ref[0, p, 0] = vt[p * LANE:(p + 1) * LANE].astype(BF16)
    rx_ref[0] = z[:, o_rx:o_rg]
    rg_ref[0] = z[:, o_rg:o_rg + REC_WIDTH]


def _project(x, tables, wts, tm):
    b, s, d = x.shape
    nt = s // tm
    in_w = wts["w_in"].shape[1]
    full = lambda shape: pl.BlockSpec(shape, lambda bi, si: (0,) * len(shape))
    cos_n, sin_n, cos_t, sin_t = tables
    return pl.pallas_call(
        _proj_kernel,
        grid=(b, nt),
        in_specs=[
            pl.BlockSpec((1, tm, d), lambda bi, si: (bi, si, 0)),
            pl.BlockSpec((tm, LANE), lambda bi, si: (si, 0)),
            pl.BlockSpec((tm, LANE), lambda bi, si: (si, 0)),
            pl.BlockSpec((HEAD_PAD, tm), lambda bi, si: (0, si)),
            pl.BlockSpec((HEAD_PAD, tm), lambda bi, si: (0, si)),
            full((d, in_w)),
            full((1, Q_LORA)),
            full((2 * N_HEADS * HEAD_PAD, Q_LORA)),
            full((1, KV_LORA)),
            full((KV_LORA, N_HEADS * HEAD_PAD)),
            full((N_PAIRS * LANE, KV_LORA)),
        ],
        out_specs=[
            pl.BlockSpec((1, N_HEADS, 1, HEAD_PAD, tm), lambda bi, si: (bi, 0, si, 0, 0)),
            pl.BlockSpec((1, N_HEADS, tm, HEAD_PAD), lambda bi, si: (bi, 0, si, 0)),
            pl.BlockSpec((1, N_PAIRS, 1, LANE, tm), lambda bi, si: (bi, 0, si, 0, 0)),
            pl.BlockSpec((1, tm, KV_LORA), lambda bi, si: (bi, si, 0)),
            pl.BlockSpec((1, tm, LANE), lambda bi, si: (bi, si, 0)),
            pl.BlockSpec((1, tm, REC_WIDTH), lambda bi, si: (bi, si, 0)),
            pl.BlockSpec((1, tm, REC_WIDTH), lambda bi, si: (bi, si, 0)),
        ],
        out_shape=[
            jax.ShapeDtypeStruct((b, N_HEADS, nt, HEAD_PAD, tm), BF16),
            jax.ShapeDtypeStruct((b, N_HEADS, s, HEAD_PAD), BF16),
            jax.ShapeDtypeStruct((b, N_PAIRS, nt, LANE, tm), BF16),
            jax.ShapeDtypeStruct((b, s, KV_LORA), F32),
            jax.ShapeDtypeStruct((b, s, LANE), F32),
            jax.ShapeDtypeStruct((b, s, REC_WIDTH), F32),
            jax.ShapeDtypeStruct((b, s, REC_WIDTH), F32),
        ],
        compiler_params=_cparams(("parallel", "parallel")),
        name="proj",
    )(x, cos_n, sin_n, cos_t, sin_t, wts["w_in"], wts["q_norm_g"], wts["w_uq_t"], wts["kv_norm_g"],
      wts["w_uk"], wts["w_uv_t"])


def _lru_prompt_kernel(rx_ref, rg_ref, cw_ref, cb_ref, wg_ref, br_ref, bi_ref, lam_ref,
                       y_ref, hlast_ref, xp_ref, h_ref):
    tm = rx_ref.shape[1]
    si = pl.program_id(1)

    @pl.when(si == 0)
    def _():
        xp_ref[0:SUBLANE, :] = jnp.zeros((SUBLANE, REC_WIDTH), F32)
        h_ref[...] = jnp.zeros_like(h_ref)

    x = rx_ref[0]
    xp_ref[SUBLANE:SUBLANE + tm, :] = x
    xc = cb_ref[...] + x * cw_ref[CONV_W - 1:CONV_W, :]
    for m in range(1, CONV_W):
        xc = xc + xp_ref[pl.ds(SUBLANE - m, tm), :] * cw_ref[CONV_W - 1 - m:CONV_W - m, :]
    xp_ref[0:SUBLANE, :] = x[tm - SUBLANE:, :]

    pre_r, pre_i = _gate_preacts(xc, wg_ref)
    a, u = _lru_coeffs(xc, pre_r, pre_i, br_ref[...], bi_ref[...], lam_ref[...])

    row = lax.broadcasted_iota(I32, (tm, REC_WIDTH), 0)
    d = 1
    while d < tm:
        keep = row >= d
        a_sh = jnp.where(keep, pltpu.roll(a, d, 0), 1.0)
        u_sh = jnp.where(keep, pltpu.roll(u, d, 0), 0.0)
        u = u + a * u_sh
        a = a * a_sh
        d *= 2
    h = a * h_ref[...] + u
    h_ref[...] = h[tm - 1:tm, :]
    hlast_ref[0] = h[tm - 1:tm, :]
    y_ref[0] = h * _gelu_tanh(rg_ref[0])


def _lru_prompt(rx, rg, wts, tm):
    b, s, w = rx.shape
    full = lambda shape: pl.BlockSpec(shape, lambda bi, si: (0,) * len(shape))
    return pl.pallas_call(
        _lru_prompt_kernel,
        grid=(b, s // tm),
        in_specs=[
            pl.BlockSpec((1, tm, w), lambda bi, si: (bi, si, 0)),
            pl.BlockSpec((1, tm, w), lambda bi, si: (bi, si, 0)),
            full((CONV_W, w)), full((1, w)), full((2, w // 2, w)), full((1, w)), full((1, w)), full((1, w)),
        ],
        out_specs=[
            pl.BlockSpec((1, tm, w), lambda bi, si: (bi, si, 0)),
            pl.BlockSpec((1, 1, w), lambda bi, si: (bi, 0, 0)),
        ],
        out_shape=[jax.ShapeDtypeStruct((b, s, w), F32), jax.ShapeDtypeStruct((b, 1, w), F32)],
        scratch_shapes=[pltpu.VMEM((tm + SUBLANE, w), F32), pltpu.VMEM((1, w), F32)],
        compiler_params=_cparams(("arbitrary", "arbitrary")),
        name="lru_prompt",
    )(rx, rg, wts["conv_w"], wts["conv_b"], wts["w_gate"], wts["b_rg"], wts["b_ig"], wts["lru_lambda"])


def _lru_step_kernel(rx_ref, rg_ref, conv_ref, h0_ref, cw_ref, cb_ref, wg_ref, br_ref, bi_ref, lam_ref,
                     y_ref, newconv_ref, h_ref):
    x = rx_ref[...]
    xc = cb_ref[...] + x * cw_ref[CONV_W - 1:CONV_W, :]
    for k in range(CONV_W - 1):
        xc = xc + conv_ref[k] * cw_ref[k:k + 1, :]
    pre_r, pre_i = _gate_preacts(xc, wg_ref)
    a, u = _lru_coeffs(xc, pre_r, pre_i, br_ref[...], bi_ref[...], lam_ref[...])
    h = a * h0_ref[...] + u
    h_ref[...] = h
    y_ref[...] = h * _gelu_tanh(rg_ref[...])
    for k in range(CONV_W - 2):
        newconv_ref[k] = conv_ref[k + 1]
    newconv_ref[CONV_W - 2] = x


def _lru_step(rx, rg, conv_t, h0, wts):
    n, w = rx.shape
    return pl.pallas_call(
        _lru_step_kernel,
        out_shape=[jax.ShapeDtypeStruct((n, w), F32),
                   jax.ShapeDtypeStruct((CONV_W - 1, n, w), F32),
                   jax.ShapeDtypeStruct((n, w), F32)],
        name="lru_step",
    )(rx, rg, conv_t, h0, wts["conv_w"], wts["conv_b"], wts["w_gate"], wts["b_rg"], wts["b_ig"], wts["lru_lambda"])


def _fused_attn_kernel(pt_ref, qt_ref, k_ref, vt_ref, qlat_ref, qpe_ref, cnew_ref, knew_ref, ckv_hbm, kr_hbm,
                       o_ref, olat_ref, cbuf, kbuf, sem, ms_ref, ls_ref, accs_ref,
                       *, n_batch, n_q, n_chunks, total_chunks):
    t = qt_ref.shape[-1]
    bi, pi, qi = pl.program_id(0), pl.program_id(1), pl.program_id(2)
    steps_per_group = n_q * (n_q + 1) // 2
    total_steps = n_batch * N_PAIRS * steps_per_group
    base = (bi * N_PAIRS + pi) * steps_per_group + (qi * (qi + 1)) // 2
    ks = KEY_STRIP
    key_i = lax.broadcasted_iota(I32, (ks, t), 0)
    qry_i = lax.broadcasted_iota(I32, (ks, t), 1)
    qts = [qt_ref[0, e, 0] for e in range(2)]

    cp = PAGES_PER_CHUNK
    ahead = CACHE_SLOTS - 1

    def copies(g, slot):
        out = []
        for p in range(cp):
            page = pt_ref[g * cp + p]
            rows = pl.ds(p * PAGE_SIZE, PAGE_SIZE)
            out.append(pltpu.make_async_copy(ckv_hbm.at[page], cbuf.at[slot, rows], sem.at[0, slot]))
            out.append(pltpu.make_async_copy(kr_hbm.at[page], kbuf.at[slot, :, rows], sem.at[1, slot]))
        return out

    def start(g, slot):
        for c in copies(g, slot):
            c.start()

    @pl.when(jnp.logical_and(jnp.logical_and(bi == 0, pi == 0), qi == 0))
    def _():
        for g0 in range(min(ahead, total_chunks)):
            start(g0, g0)

    def sample_chunk_stages(g):
        v = {}

        def fetch_and_score():
            @pl.when(g + ahead < total_chunks)
            def _():
                start(g + ahead, lax.rem(g + ahead, CACHE_SLOTS))

            slot = lax.rem(g, CACHE_SLOTS)
            for c in copies(g, slot):
                c.wait()
            v["b"] = lax.div(g, n_chunks)
            v["qlat"] = qlat_ref[v["b"]]
            v["qpe"] = qpe_ref[v["b"]]
            v["cb"] = cbuf[slot].astype(BF16)
            kb = kbuf[slot].astype(BF16)
            v["s"] = _dot_nt(v["qlat"].astype(BF16), v["cb"]) + _dot(v["qpe"].astype(BF16), kb)

        def softmax_and_values():
            first = lax.rem(g, n_chunks) == 0
            m = jnp.where(first, NEG_INF, ms_ref[...])
            l = jnp.where(first, 0.0, ls_ref[...])
            acc = jnp.where(first, 0.0, accs_ref[...])
            s = v["s"]
            m_new = jnp.maximum(m, jnp.max(s, axis=-1, keepdims=True))
            alpha = jnp.exp2((m - m_new) * EXP2_SCALE)
            p = jnp.exp2((s - m_new) * EXP2_SCALE)
            v["l"] = alpha * l + jnp.sum(p, axis=-1, keepdims=True)
            v["acc"] = alpha * acc + _dot(p.astype(BF16), v["cb"])
            v["m"] = m_new

        def finish():
            m_new, l, acc, b = v["m"], v["l"], v["acc"], v["b"]
            ms_ref[...] = m_new
            ls_ref[...] = l
            accs_ref[...] = acc
            cnew = cnew_ref[b]
            knew = knew_ref[b]
            s_new = (jnp.sum(v["qlat"] * cnew, axis=-1, keepdims=True)
                     + jnp.sum(v["qpe"] * knew, axis=-1, keepdims=True))
            m_fin = jnp.maximum(m_new, s_new)
            a_fin = jnp.exp2((m_new - m_fin) * EXP2_SCALE)
            p_new = jnp.exp2((s_new - m_fin) * EXP2_SCALE)
            olat_ref[b] = (a_fin * acc + p_new * cnew) / (a_fin * l + p_new)

        return [fetch_and_score, softmax_and_values, finish]

    def scores(j, r, e):
        kb = k_ref[0, e, pl.ds(pl.multiple_of(j * t + r * ks, ks), ks), :]
        return _dot(kb, qts[e])

    def step(j, carry, diagonal, with_chunk):
        stages = sample_chunk_stages(base + j) if with_chunk else []
        carry = list(carry)
        units = [(r, e) for r in range(t // ks) for e in range(2)]
        st_next = scores(j, *units[0])
        for u, (r, e) in enumerate(units):
            st = st_next
            if u + 1 < len(units):
                st_next = scores(j, *units[u + 1])
            if stages:
                stages.pop(0)()
            m, l, acc = carry[e]
            vb = vt_ref[0, 0, j, e * V_HEAD:(e + 1) * V_HEAD, r * ks:(r + 1) * ks]
            if diagonal:
                st = jnp.where(key_i + r * ks <= qry_i, st, NEG_INF)
            m_new = jnp.maximum(m, jnp.max(st, axis=0, keepdims=True))
            alpha = jnp.exp2((m - m_new) * EXP2_SCALE)
            pt = jnp.exp2((st - m_new) * EXP2_SCALE)
            l = alpha * l + jnp.sum(pt, axis=0, keepdims=True)
            acc = alpha * acc + _dot(vb, pt.astype(BF16))
            carry[e] = (m_new, l, acc)
        for stage in stages:
            stage()
        return tuple(carry)

    init = (jnp.full((1, t), NEG_INF, F32), jnp.zeros((1, t), F32), jnp.zeros((V_HEAD, t), F32))
    n_with = jnp.clip(total_chunks - base, 0, qi)
    carry = lax.fori_loop(0, n_with, functools.partial(step, diagonal=False, with_chunk=True), (init, init))
    carry = lax.fori_loop(n_with, qi, functools.partial(step, diagonal=False, with_chunk=False), carry)
    carry = lax.cond(base + qi < total_chunks,
                     functools.partial(step, qi, diagonal=True, with_chunk=True),
                     functools.partial(step, qi, diagonal=True, with_chunk=False), carry)
    halves = [acc / l for (_, l, acc) in carry]
    o_ref[0, 0] = jnp.concatenate(halves, axis=0).T

    if total_chunks > total_steps:
        @pl.when(jnp.logical_and(jnp.logical_and(bi == n_batch - 1, pi == N_PAIRS - 1), qi == n_q - 1))
        def _():
            def drain(g, c):
                for stage in sample_chunk_stages(g):
                    stage()
                return c
            lax.fori_loop(total_steps, total_chunks, drain, 0)


def _attention(qt, k, vt, page_table, qlat, qpe, c_new, k_new, cache_kv, cache_kr_t):
    b, _, nt, _, t = qt.shape
    s = nt * t
    bd, n_pages = page_table.shape
    n_chunks = n_pages // PAGES_PER_CHUNK
    rows = PAGES_PER_CHUNK * PAGE_SIZE
    kern = functools.partial(_fused_attn_kernel, n_batch=b, n_q=nt, n_chunks=n_chunks, total_chunks=bd * n_chunks)
    whole = lambda shape: pl.BlockSpec(shape, lambda bi, pi, qi, pt: (0,) * len(shape))
    grid_spec = pltpu.PrefetchScalarGridSpec(
        num_scalar_prefetch=1,
        grid=(b, N_PAIRS, nt),
        in_specs=[
            pl.BlockSpec((1, 2, 1, HEAD_PAD, t), lambda bi, pi, qi, pt: (bi, pi, qi, 0, 0)),
            pl.BlockSpec((1, 2, s, HEAD_PAD), lambda bi, pi, qi, pt: (bi, pi, 0, 0)),
            pl.BlockSpec((1, 1, nt, LANE, t), lambda bi, pi, qi, pt: (bi, pi, 0, 0, 0)),
            whole((bd, N_HEADS, KV_LORA)),
            whole((bd, N_HEADS, QK_ROPE)),
            whole((bd, 1, KV_LORA)),
            whole((bd, 1, QK_ROPE)),
            pl.BlockSpec(memory_space=pl.ANY),
            pl.BlockSpec(memory_space=pl.ANY),
        ],
        out_specs=[
            pl.BlockSpec((1, 1, t, LANE), lambda bi, pi, qi, pt: (bi, pi, qi, 0)),
            whole((bd, N_HEADS, KV_LORA)),
        ],
        scratch_shapes=[
            pltpu.VMEM((CACHE_SLOTS, rows, KV_LORA), F32),
            pltpu.VMEM((CACHE_SLOTS, QK_ROPE, rows), F32),
            pltpu.SemaphoreType.DMA((2, CACHE_SLOTS)),
            pltpu.VMEM((N_HEADS, 1), F32),
            pltpu.VMEM((N_HEADS, 1), F32),
            pltpu.VMEM((N_HEADS, KV_LORA), F32),
        ],
    )
    return pl.pallas_call(
        kern,
        grid_spec=grid_spec,
        out_shape=[jax.ShapeDtypeStruct((b, N_PAIRS, s, LANE), F32),
                   jax.ShapeDtypeStruct((bd, N_HEADS, KV_LORA), F32)],
        compiler_params=_cparams(("arbitrary", "arbitrary", "arbitrary")),
        name="attn_fused",
    )(page_table.reshape(-1), qt, k, vt, qlat, qpe, c_new, k_new, cache_kv, cache_kr_t)


def _absorb_kernel(qt_ref, wlat_ref, wpe_ref, qlat_ref, qpe_ref):
    for h in range(N_HEADS):
        qt = qt_ref[h]
        qlat_ref[h] = _dot_tn(qt, wlat_ref[h])
        qpe_ref[h] = _dot_tn(qt, wpe_ref[...])


def _absorb(qt, wts):
    _, _, n = qt.shape
    return pl.pallas_call(
        _absorb_kernel,
        out_shape=[jax.ShapeDtypeStruct((N_HEADS, n, KV_LORA), F32),
                   jax.ShapeDtypeStruct((N_HEADS, n, QK_ROPE), F32)],
        name="absorb_q",
    )(qt, wts["w_uk_t"], wts["w_pe_sel"])


def _value_up_kernel(olat_ref, wv_ref, o_ref):
    w = 2 * KV_LORA
    for p in range(N_PAIRS):
        o_ref[0, p] = _dot(olat_ref[:, p * w:(p + 1) * w].astype(BF16), wv_ref[p])


def _value_up(olat2d, wts):
    n = olat2d.shape[0]
    return pl.pallas_call(
        _value_up_kernel,
        out_shape=jax.ShapeDtypeStruct((1, N_PAIRS, n, LANE), F32),
        name="value_up",
    )(olat2d, wts["w_uv_pair"])


def _merge_kernel(x_ref, att_ref, rec_ref, ag_ref, rgn_ref, wout_ref, g1_ref, b1_ref, wr_ref, bg_ref, be_ref,
                  x1_ref, gsel_ref, gate_ref, sel_ref, cnt_ref):
    tm = x_ref.shape[1]
    att = [att_ref[0, p] for p in range(N_PAIRS)]
    ss = att[0] * att[0]
    for p in range(1, N_PAIRS):
        ss = ss + att[p] * att[p]
    inv = lax.rsqrt(jnp.sum(ss, axis=-1, keepdims=True) / (N_PAIRS * LANE) + RMS_EPS)
    parts = [(att[p] * inv * ag_ref[:, p * LANE:(p + 1) * LANE]).astype(BF16) for p in range(N_PAIRS)]
    parts.append(_rmsnorm(rec_ref[0], rgn_ref[...]).astype(BF16))
    mixed = jnp.concatenate(parts, axis=-1)
    mix = _dot(mixed, wout_ref[...])
    x1 = _layernorm(ALPHA * x_ref[0] + mix, g1_ref[...], b1_ref[...])
    x1_ref[0] = x1

    lt = _dot_nt(wr_ref[...], x1, precision=lax.Precision.HIGHEST)
    g = [lt[k:k + 1, :] for k in range(N_GROUPS)]
    gmax = functools.reduce(jnp.maximum, g)
    ex = [jnp.exp(gk - gmax) for gk in g]
    den = functools.reduce(lambda p, q: p + q, ex)
    best = g[0] + bg_ref[0:1, :]
    idx = jnp.zeros((1, tm), I32)
    for k in range(1, N_GROUPS):
        cand = g[k] + bg_ref[k:k + 1, :]
        upd = cand > best
        idx = jnp.where(upd, k, idx)
        best = jnp.where(upd, cand, best)
    gp = ex[0]
    e_sel = lt[SUBLANE:SUBLANE + EXPERTS_PER_GROUP, :]
    e_bias = jnp.broadcast_to(be_ref[0:EXPERTS_PER_GROUP, :], (EXPERTS_PER_GROUP, tm))
    for k in range(1, N_GROUPS):
        hit = idx == k
        lo = SUBLANE + k * EXPERTS_PER_GROUP
        gp = jnp.where(hit, ex[k], gp)
        e_sel = jnp.where(hit, lt[lo:lo + EXPERTS_PER_GROUP, :], e_sel)
        e_bias = jnp.where(hit, be_ref[k * EXPERTS_PER_GROUP:(k + 1) * EXPERTS_PER_GROUP, :], e_bias)
    g_prob = gp / den
    sc = e_sel + e_bias
    sub = lax.broadcasted_iota(I32, (EXPERTS_PER_GROUP, tm), 0)
    m1 = jnp.max(sc, axis=0, keepdims=True)
    i1 = jnp.min(jnp.where(sc == m1, sub, EXPERTS_PER_GROUP), axis=0, keepdims=True)
    mask1 = sub == i1
    sc2 = jnp.where(mask1, -jnp.inf, sc)
    m2 = jnp.max(sc2, axis=0, keepdims=True)
    i2 = jnp.min(jnp.where(sc2 == m2, sub, EXPERTS_PER_GROUP), axis=0, keepdims=True)
    mask2 = sub == i2
    v1 = jnp.sum(jnp.where(mask1, e_sel, 0.0), axis=0, keepdims=True)
    v2 = jnp.sum(jnp.where(mask2, e_sel, 0.0), axis=0, keepdims=True)
    vm = jnp.maximum(v1, v2)
    e1 = jnp.exp(v1 - vm)
    e2 = jnp.exp(v2 - vm)
    esum = e1 + e2
    gate = g_prob * (jnp.where(mask1, e1 / esum, 0.0) + jnp.where(mask2, e2 / esum, 0.0))
    gsel_ref[0] = idx
    gate_ref[0] = gate
    sel = jnp.where(jnp.logical_or(mask1, mask2), 1.0, 0.0)
    sel_ref[0] = sel
    for k in range(N_GROUPS):
        ck = jnp.sum(jnp.where(idx == k, sel, 0.0), axis=-1, keepdims=True)
        cnt_ref[0, k * EXPERTS_PER_GROUP:(k + 1) * EXPERTS_PER_GROUP, :] = jnp.broadcast_to(
            ck, (EXPERTS_PER_GROUP, LANE))


def _merge(x, att, rec, wts, tm):
    b, s, d = x.shape
    nt = s // tm
    mw = wts["w_out"].shape[0]
    n_r = wts["w_router_t"].shape[0]
    full = lambda shape: pl.BlockSpec(shape, lambda bi, si: (0,) * len(shape))
    return pl.pallas_call(
        _merge_kernel,
        grid=(b, nt),
        in_specs=[
            pl.BlockSpec((1, tm, d), lambda bi, si: (bi, si, 0)),
            pl.BlockSpec((1, N_PAIRS, tm, LANE), lambda bi, si: (bi, 0, si, 0)),
            pl.BlockSpec((1, tm, REC_WIDTH), lambda bi, si: (bi, si, 0)),
            full((1, N_PAIRS * LANE)), full((1, REC_WIDTH)), full((mw, d)), full((1, d)), full((1, d)),
            full((n_r, d)), full((N_GROUPS, 1)), full((N_GROUPS * EXPERTS_PER_GROUP, 1)),
        ],
        out_specs=[
            pl.BlockSpec((1, tm, d), lambda bi, si: (bi, si, 0)),
            pl.BlockSpec((1, 1, tm), lambda bi, si: (bi * nt + si, 0, 0)),
            pl.BlockSpec((1, EXPERTS_PER_GROUP, tm), lambda bi, si: (bi * nt + si, 0, 0)),
            pl.BlockSpec((1, EXPERTS_PER_GROUP, tm), lambda bi, si: (bi * nt + si, 0, 0)),
            pl.BlockSpec((1, N_EXPERTS, LANE), lambda bi, si: (bi * nt + si, 0, 0)),
        ],
        out_shape=[
            jax.ShapeDtypeStruct((b, s, d), F32),
            jax.ShapeDtypeStruct((b * nt, 1, tm), I32),
            jax.ShapeDtypeStruct((b * nt, EXPERTS_PER_GROUP, tm), F32),
            jax.ShapeDtypeStruct((b * nt, EXPERTS_PER_GROUP, tm), F32),
            jax.ShapeDtypeStruct((b * nt, N_EXPERTS, LANE), F32),
        ],
        compiler_params=_cparams(("parallel", "parallel")),
        name="merge_router",
    )(x, att, rec, wts["att_out_g"], wts["rec_out_g"], wts["w_out"], wts["ln1_g"], wts["ln1_b"],
      wts["w_router_t"], wts["b_group"], wts["b_expert"])


def _moe_kernel(nch_ref, x1_ref, gsel_ref, gate_ref, wgu_ref, wd_ref, g2_ref, b2_ref, o_ref, xb_ref, tri_ref,
                *, ch):
    tm = x1_ref.shape[0]
    ti = pl.program_id(0)
    gi = pl.program_id(1)

    @pl.when(jnp.logical_and(ti == 0, gi == 0))
    def _():
        r = lax.broadcasted_iota(I32, (tm, tm), 0)
        c = lax.broadcasted_iota(I32, (tm, tm), 1)
        tri_ref[...] = jnp.where(r < c, 1.0, 0.0).astype(BF16)

    @pl.when(gi == 0)
    def _():
        xb_ref[...] = x1_ref[...].astype(BF16)
        o_ref[...] = jnp.zeros_like(o_ref)

    in_group = gsel_ref[0] == gi
    member = jnp.broadcast_to(jnp.where(in_group, 1.0, 0.0), (SUBLANE, tm)).astype(BF16)
    before = _dot(member, tri_ref[...])
    rank = jnp.where(in_group, before[0:1, :].astype(I32), -1)
    gate = gate_ref[0]
    g_hi = gate.astype(BF16).astype(F32)
    g_mid = (gate - g_hi).astype(BF16).astype(F32)
    g_lo = (gate - g_hi) - g_mid
    n_terms = 3
    gate_terms = jnp.concatenate(
        [g_hi, g_mid, g_lo, jnp.zeros((LANE - n_terms * EXPERTS_PER_GROUP, tm), F32)], axis=0).astype(BF16)

    def chunk(c, carry):
        slot_id = lax.broadcasted_iota(I32, (ch, tm), 0) + c * ch
        onehot_b = jnp.where(slot_id == rank, 1.0, 0.0).astype(BF16)
        xg = _dot(onehot_b, xb_ref[...]).astype(BF16)
        gt = _dot_nt(onehot_b, gate_terms)
        gc = gt
        for k in range(1, n_terms):
            gc = gc + pltpu.roll(gt, LANE - k * EXPERTS_PER_GROUP, 1)
        acc = jnp.zeros((ch, o_ref.shape[1]), F32)
        for j in range(EXPERTS_PER_GROUP):
            gu = _dot(xg, wgu_ref[0, j])
            hid = jax.nn.silu(gu[:, :D_EXPERT]) * gu[:, D_EXPERT:]
            acc = acc + gc[:, j:j + 1] * _dot(hid.astype(BF16), wd_ref[0, j])
        o_ref[...] += _dot_tn(onehot_b, acc.astype(BF16))
        return carry

    lax.fori_loop(0, nch_ref[ti * N_GROUPS + gi], chunk, 0)

    @pl.when(gi == N_GROUPS - 1)
    def _():
        o_ref[...] = _layernorm(ALPHA * x1_ref[...] + o_ref[...], g2_ref[...], b2_ref[...])


def _moe(x1, gsel, gate, nch, wts, tm, ch):
    n, d = x1.shape
    nt = n // tm
    e2 = 2 * D_EXPERT
    grid_spec = pltpu.PrefetchScalarGridSpec(
        num_scalar_prefetch=1,
        grid=(nt, N_GROUPS),
        in_specs=[
            pl.BlockSpec((tm, d), lambda ti, gi, nc: (ti, 0)),
            pl.BlockSpec((1, 1, tm), lambda ti, gi, nc: (ti, 0, 0)),
            pl.BlockSpec((1, EXPERTS_PER_GROUP, tm), lambda ti, gi, nc: (ti, 0, 0)),
            pl.BlockSpec((1, EXPERTS_PER_GROUP, d, e2), lambda ti, gi, nc: (gi, 0, 0, 0)),
            pl.BlockSpec((1, EXPERTS_PER_GROUP, D_EXPERT, d), lambda ti, gi, nc: (gi, 0, 0, 0)),
            pl.BlockSpec((1, d), lambda ti, gi, nc: (0, 0)),
            pl.BlockSpec((1, d), lambda ti, gi, nc: (0, 0)),
        ],
        out_specs=pl.BlockSpec((tm, d), lambda ti, gi, nc: (ti, 0)),
        scratch_shapes=[pltpu.VMEM((tm, d), BF16), pltpu.VMEM((tm, tm), BF16)],
    )
    return pl.pallas_call(
        functools.partial(_moe_kernel, ch=ch),
        grid_spec=grid_spec,
        out_shape=jax.ShapeDtypeStruct((n, d), F32),
        compiler_params=_cparams(("arbitrary", "arbitrary")),
        name="moe",
    )(nch, x1, gsel, gate, wts["w_gate_up"], wts["w_down"], wts["ln2_g"], wts["ln2_b"])


def _build_tri(tri_ref):
    tm = tri_ref.shape[0]
    r = lax.broadcasted_iota(I32, (tm, tm), 0)
    c = lax.broadcasted_iota(I32, (tm, tm), 1)
    tri_ref[...] = jnp.where(r < c, 1.0, 0.0).astype(BF16)


def _sorted_positions(gsel, sel, start_col, tri_ref):
    member = jnp.concatenate([jnp.where(gsel == g, sel, 0.0) for g in range(N_GROUPS)], axis=0)
    before = _dot(member.astype(BF16), tri_ref[...])
    routed = member > 0.5
    pos = jnp.where(routed, start_col + before, -1.0)
    pos_a = jnp.max(pos, axis=0, keepdims=True)
    pos_b = jnp.sum(jnp.where(routed, start_col + before, 0.0), axis=0, keepdims=True) - pos_a
    return pos, pos_a.astype(I32), pos_b.astype(I32)


def _exact_terms(v):
    hi = v.astype(BF16).astype(F32)
    mid = (v - hi).astype(BF16).astype(F32)
    lo = (v - hi) - mid
    row = lax.broadcasted_iota(I32, (LANE, v.shape[1]), 0)
    return jnp.where(row == 0, hi, jnp.where(row == 1, mid, jnp.where(row == 2, lo, 0.0))).astype(BF16)


def _dispatch_kernel(udst_ref, nun_ref, nch_ref, x1_ref, gsel_ref, sel_ref, gate_ref, start_ref, xs_in, gs_in,
                     xs_hbm, gs_hbm, xb_ref, tri_ref, xsrt, gsrt, sem, *, rch, umax):
    del xs_in, gs_in
    i = pl.program_id(0)
    tm = x1_ref.shape[0]

    @pl.when(i == 0)
    def _():
        _build_tri(tri_ref)

    xb_ref[...] = x1_ref[...].astype(BF16)
    gsel = gsel_ref[0]
    pos, pos_a, pos_b = _sorted_positions(gsel, sel_ref[0], start_ref[0], tri_ref)
    gate32 = jnp.concatenate([jnp.where(gsel == g, gate_ref[0], 0.0) for g in range(N_GROUPS)], axis=0)
    gate_a = jnp.sum(jnp.where(pos == pos_a.astype(F32), gate32, 0.0), axis=0, keepdims=True)
    gate_b = jnp.sum(jnp.where(pos == pos_b.astype(F32), gate32, 0.0), axis=0, keepdims=True)
    terms_a = _exact_terms(gate_a)
    terms_b = _exact_terms(gate_b)

    def chunk(c, carry):
        r0 = pl.multiple_of(c * rch, rch)
        rid = lax.broadcasted_iota(I32, (rch, tm), 0) + r0
        hit_a = rid == pos_a
        hit_b = rid == pos_b
        onehot = jnp.where(jnp.logical_or(hit_a, hit_b), 1.0, 0.0).astype(BF16)
        xsrt[pl.ds(r0, rch), :] = _dot(onehot, xb_ref[...]).astype(BF16)
        gt = (_dot_nt(jnp.where(hit_a, 1.0, 0.0).astype(BF16), terms_a)
              + _dot_nt(jnp.where(hit_b, 1.0, 0.0).astype(BF16), terms_b))
        gsrt[pl.ds(r0, rch), :] = (gt + pltpu.roll(gt, LANE - 1, 1)) + pltpu.roll(gt, LANE - 2, 1)
        return carry

    lax.fori_loop(0, nch_ref[i], chunk, 0)

    def unit_copies(u):
        src = pl.ds(pl.multiple_of(u * ROW_UNIT, ROW_UNIT), ROW_UNIT)
        dst = pl.ds(pl.multiple_of(udst_ref[i * umax + u], ROW_UNIT), ROW_UNIT)
        return (pltpu.make_async_copy(xsrt.at[src], xs_hbm.at[dst], sem.at[0]),
                pltpu.make_async_copy(gsrt.at[src], gs_hbm.at[dst], sem.at[1]))

    def start_unit(u, carry):
        for cp in unit_copies(u):
            cp.start()
        return carry

    def wait_unit(u, carry):
        for cp in unit_copies(u):
            cp.wait()
        return carry

    lax.fori_loop(0, nun_ref[i], start_unit, 0)
    lax.fori_loop(0, nun_ref[i], wait_unit, 0)


def _dispatch(x1, gsel, sel, gate, start_col, tables, total_rows, tm, rch, umax):
    n, d = x1.shape
    nt = n // tm
    rows_max = umax * ROW_UNIT
    unit_dst, n_units, n_chunks = tables
    xs0 = jnp.zeros((total_rows, d), BF16)
    gs0 = jnp.zeros((total_rows, LANE), F32)
    grid_spec = pltpu.PrefetchScalarGridSpec(
        num_scalar_prefetch=3,
        grid=(nt,),
        in_specs=[
            pl.BlockSpec((tm, d), lambda i, *_: (i, 0)),
            pl.BlockSpec((1, 1, tm), lambda i, *_: (i, 0, 0)),
            pl.BlockSpec((1, EXPERTS_PER_GROUP, tm), lambda i, *_: (i, 0, 0)),
            pl.BlockSpec((1, EXPERTS_PER_GROUP, tm), lambda i, *_: (i, 0, 0)),
            pl.BlockSpec((1, N_EXPERTS, 1), lambda i, *_: (i, 0, 0)),
            pl.BlockSpec(memory_space=pl.ANY),
            pl.BlockSpec(memory_space=pl.ANY),
        ],
        out_specs=[pl.BlockSpec(memory_space=pl.ANY), pl.BlockSpec(memory_space=pl.ANY)],
        scratch_shapes=[
            pltpu.VMEM((tm, d), BF16),
            pltpu.VMEM((tm, tm), BF16),
            pltpu.VMEM((rows_max, d), BF16),
            pltpu.VMEM((rows_max, LANE), F32),
            pltpu.SemaphoreType.DMA((2,)),
        ],
    )
    return pl.pallas_call(
        functools.partial(_dispatch_kernel, rch=rch, umax=umax),
        grid_spec=grid_spec,
        out_shape=[jax.ShapeDtypeStruct((total_rows, d), BF16), jax.ShapeDtypeStruct((total_rows, LANE), F32)],
        input_output_aliases={8: 0, 9: 1},
        compiler_params=_cparams(("arbitrary",)),
        name="moe_dispatch",
    )(unit_dst, n_units, n_chunks, x1, gsel, sel, gate, start_col, xs0, gs0)


def _expert_kernel(ce_ref, x_ref, g_ref, wgu_ref, wd_ref, y_ref):
    c = pl.program_id(0)

    @pl.when(ce_ref[c] >= 0)
    def _():
        gu = _dot(x_ref[...], wgu_ref[0])
        hid = jax.nn.silu(gu[:, :D_EXPERT]) * gu[:, D_EXPERT:]
        y_ref[...] = (g_ref[:, 0:1] * _dot(hid.astype(BF16), wd_ref[0])).astype(BF16)

    @pl.when(ce_ref[c] < 0)
    def _():
        y_ref[...] = jnp.zeros_like(y_ref)


def _experts(xs, gs, chunk_expert, wts, rch):
    total_rows, d = xs.shape
    e2 = 2 * D_EXPERT
    w_gu = wts["w_gate_up"].reshape(N_EXPERTS, d, e2)
    w_dn = wts["w_down"].reshape(N_EXPERTS, D_EXPERT, d)
    grid_spec = pltpu.PrefetchScalarGridSpec(
        num_scalar_prefetch=1,
        grid=(total_rows // rch,),
        in_specs=[
            pl.BlockSpec((rch, d), lambda c, ce: (c, 0)),
            pl.BlockSpec((rch, LANE), lambda c, ce: (c, 0)),
            pl.BlockSpec((1, d, e2), lambda c, ce: (jnp.maximum(ce[c], 0), 0, 0)),
            pl.BlockSpec((1, D_EXPERT, d), lambda c, ce: (jnp.maximum(ce[c], 0), 0, 0)),
        ],
        out_specs=pl.BlockSpec((rch, d), lambda c, ce: (c, 0)),
    )
    return pl.pallas_call(
        _expert_kernel,
        grid_spec=grid_spec,
        out_shape=jax.ShapeDtypeStruct((total_rows, d), BF16),
        compiler_params=_cparams(("arbitrary",)),
        name="moe_experts",
    )(chunk_expert, xs, gs, w_gu, w_dn)


def _combine_kernel(udst_ref, nun_ref, nch_ref, x1_ref, gsel_ref, sel_ref, start_ref, ys_hbm, g2_ref, b2_ref,
                    o_ref, tri_ref, ysrt, sem, *, rch, umax):
    i = pl.program_id(0)
    tm = x1_ref.shape[0]

    @pl.when(i == 0)
    def _():
        _build_tri(tri_ref)
        ysrt[...] = jnp.zeros_like(ysrt)

    def unit_copy(u):
        src = pl.ds(pl.multiple_of(udst_ref[i * umax + u], ROW_UNIT), ROW_UNIT)
        dst = pl.ds(pl.multiple_of(u * ROW_UNIT, ROW_UNIT), ROW_UNIT)
        return pltpu.make_async_copy(ys_hbm.at[src], ysrt.at[dst], sem.at[0])

    def start_unit(u, carry):
        unit_copy(u).start()
        return carry

    def wait_unit(u, carry):
        unit_copy(u).wait()
        return carry

    lax.fori_loop(0, nun_ref[i], start_unit, 0)
    _, pos_a, pos_b = _sorted_positions(gsel_ref[0], sel_ref[0], start_ref[0], tri_ref)
    o_ref[...] = jnp.zeros_like(o_ref)
    lax.fori_loop(0, nun_ref[i], wait_unit, 0)

    def chunk(c, carry):
        r0 = pl.multiple_of(c * rch, rch)
        rid = lax.broadcasted_iota(I32, (rch, tm), 0) + r0
        onehot = jnp.where(jnp.logical_or(rid == pos_a, rid == pos_b), 1.0, 0.0).astype(BF16)
        o_ref[...] += _dot_tn(onehot, ysrt[pl.ds(r0, rch), :])
        return carry

    lax.fori_loop(0, nch_ref[i], chunk, 0)
    o_ref[...] = _layernorm(ALPHA * x1_ref[...] + o_ref[...], g2_ref[...], b2_ref[...])


def _combine(x1, gsel, sel, start_col, ys, tables, wts, tm, rch, umax):
    n, d = x1.shape
    nt = n // tm
    unit_dst, n_units, n_chunks = tables
    grid_spec = pltpu.PrefetchScalarGridSpec(
        num_scalar_prefetch=3,
        grid=(nt,),
        in_specs=[
            pl.BlockSpec((tm, d), lambda i, *_: (i, 0)),
            pl.BlockSpec((1, 1, tm), lambda i, *_: (i, 0, 0)),
            pl.BlockSpec((1, EXPERTS_PER_GROUP, tm), lambda i, *_: (i, 0, 0)),
            pl.BlockSpec((1, N_EXPERTS, 1), lambda i, *_: (i, 0, 0)),
            pl.BlockSpec(memory_space=pl.ANY),
            pl.BlockSpec((1, d), lambda i, *_: (0, 0)),
            pl.BlockSpec((1, d), lambda i, *_: (0, 0)),
        ],
        out_specs=pl.BlockSpec((tm, d), lambda i, *_: (i, 0)),
        scratch_shapes=[
            pltpu.VMEM((tm, tm), BF16),
            pltpu.VMEM((umax * ROW_UNIT, d), BF16),
            pltpu.SemaphoreType.DMA((1,)),
        ],
    )
    return pl.pallas_call(
        functools.partial(_combine_kernel, rch=rch, umax=umax),
        grid_spec=grid_spec,
        out_shape=jax.ShapeDtypeStruct((n, d), F32),
        compiler_params=_cparams(("arbitrary",)),
        name="moe_combine",
    )(unit_dst, n_units, n_chunks, x1, gsel, sel, start_col, ys, wts["ln2_g"], wts["ln2_b"])


def _routing_tables(cnt, tm, rch):
    nt = cnt.shape[0]
    seg = (cnt + (ROW_UNIT - 1)) // ROW_UNIT * ROW_UNIT
    start = jnp.cumsum(seg, axis=1) - seg
    rows = seg.sum(axis=1)
    per_expert = seg.sum(axis=0)
    region = (per_expert + (rch - 1)) // rch * rch
    region_start = jnp.cumsum(region) - region
    seg_dst = region_start[None, :] + jnp.cumsum(seg, axis=0) - seg
    rows_max = -(-(TOP_K * tm + N_EXPERTS * (ROW_UNIT - 1)) // rch) * rch
    umax = rows_max // ROW_UNIT
    total_rows = -(-(TOP_K * tm * nt + nt * N_EXPERTS * (ROW_UNIT - 1) + N_EXPERTS * (rch - 1)) // rch) * rch
    u_row = jnp.arange(umax, dtype=I32) * ROW_UNIT
    seg_of_unit = jnp.minimum(jnp.sum(u_row[None, :, None] >= (start + seg)[:, None, :], axis=-1), N_EXPERTS - 1)
    unit_dst = (jnp.take_along_axis(seg_dst, seg_of_unit, axis=1)
                + u_row[None, :] - jnp.take_along_axis(start, seg_of_unit, axis=1))
    c_row = jnp.arange(total_rows // rch, dtype=I32) * rch
    e_of_chunk = jnp.sum(c_row[:, None] >= (region_start + region)[None, :], axis=-1)
    e_clamped = jnp.minimum(e_of_chunk, N_EXPERTS - 1)
    used = jnp.logical_and(e_of_chunk < N_EXPERTS, c_row < region_start[e_clamped] + per_expert[e_clamped])
    chunk_expert = jnp.where(used, e_clamped, -1).astype(I32)
    tables = (unit_dst.reshape(-1).astype(I32), (rows // ROW_UNIT).astype(I32),
              ((rows + (rch - 1)) // rch).astype(I32))
    return tables, chunk_expert, start.astype(F32).reshape(nt, N_EXPERTS, 1), total_rows, umax


def _merge_and_routed_ffn(x, att, rec, wts, tm_merge, tm, rch):
    b, s, d = x.shape
    x1, gsel, gate, sel, cnt = _merge(x, att, rec, wts, tm_merge)
    n = b * s
    nt = n // tm
    f = tm // tm_merge
    regroup = lambda a: a.reshape(nt, f, EXPERTS_PER_GROUP, tm_merge).transpose(0, 2, 1, 3).reshape(
        nt, EXPERTS_PER_GROUP, tm)
    gsel, gate, sel = gsel.reshape(nt, 1, tm), regroup(gate), regroup(sel)
    counts = cnt[:, :, 0].reshape(nt, f, N_EXPERTS).sum(axis=1).astype(I32)
    tables, chunk_expert, start_col, total_rows, umax = _routing_tables(counts, tm, rch)
    x1 = x1.reshape(n, d)
    xs, gs = _dispatch(x1, gsel, sel, gate, start_col, tables, total_rows, tm, rch, umax)
    ys = _experts(xs, gs, chunk_expert, wts, rch)
    y = _combine(x1, gsel, sel, start_col, ys, tables, wts, tm, rch, umax)
    return y.reshape(b, s, d)


def _merge_and_ffn(x, att, rec, wts, tm_merge, tm_moe, ch):
    b, s, d = x.shape
    x1, gsel, gate, _, cnt = _merge(x, att, rec, wts, tm_merge)
    n = b * s
    nt = n // tm_moe
    f = tm_moe // tm_merge
    gsel = gsel.reshape(nt, 1, tm_moe)
    gate = gate.reshape(nt, f, EXPERTS_PER_GROUP, tm_merge).transpose(0, 2, 1, 3).reshape(nt, EXPERTS_PER_GROUP, tm_moe)
    per_expert = cnt[:, :, 0].reshape(nt, f, N_GROUPS, EXPERTS_PER_GROUP)
    counts = (per_expert.sum(axis=(1, 3)) / TOP_K).astype(I32)
    nch = ((counts + (ch - 1)) // ch).reshape(-1)
    y = _moe(x1.reshape(n, d), gsel, gate, nch, wts, tm_moe, ch)
    return y.reshape(b, s, d)


def _swap_halves(w):
    half = QK_ROPE // 2
    return jnp.concatenate([w[..., half:], w[..., :half]], axis=-1)


def _prep_weights(w_in, q_norm_g, w_uq, kv_norm_g, w_uk, w_uv, conv_w, conv_b, w_rg, b_rg, w_ig, b_ig,
                  lru_lambda, att_out_g, rec_out_g, w_out, ln1_g, ln1_b, w_group, b_group, w_expert,
                  b_expert, w_gate_up, w_down, ln2_g, ln2_b):
    d = w_in.shape[0]
    o1, o2, o3, o4 = Q_LORA, Q_LORA + KV_LORA, Q_LORA + KV_LORA + QK_ROPE, Q_LORA + KV_LORA + QK_ROPE + REC_WIDTH
    w_kpe = w_in[:, o2:o3]
    pad_lo = jnp.zeros((d, QK_NOPE), F32)
    pad_hi = jnp.zeros((d, HEAD_PAD - QK_NOPE - QK_ROPE), F32)
    w_in_ext = jnp.concatenate([
        w_in[:, :o2],
        pad_lo, w_kpe, pad_hi,
        pad_lo, _swap_halves(w_kpe), pad_hi,
        w_in[:, o3:o4], w_in[:, o4:],
    ], axis=1).astype(BF16)

    nope, pe = w_uq[..., :QK_NOPE], w_uq[..., QK_NOPE:]
    zq = lambda n: jnp.zeros((Q_LORA, N_HEADS, n), F32)
    q_main = jnp.concatenate([nope, pe, zq(HEAD_PAD - QK_NOPE - QK_ROPE)], axis=-1)
    q_swap = jnp.concatenate([zq(QK_NOPE), _swap_halves(pe), zq(HEAD_PAD - QK_NOPE - QK_ROPE)], axis=-1)
    w_uq_t = jnp.concatenate([q_main.reshape(Q_LORA, -1), q_swap.reshape(Q_LORA, -1)], axis=1).T.astype(BF16)

    k_pad = jnp.concatenate([w_uk, jnp.zeros((KV_LORA, N_HEADS, HEAD_PAD - QK_NOPE), F32)], axis=-1)
    w_uk_pad = k_pad.reshape(KV_LORA, -1).astype(BF16)
    w_uv_t = w_uv.reshape(KV_LORA, -1).T.astype(BF16)

    w_uk_t = jnp.concatenate([w_uk.transpose(1, 2, 0),
                              jnp.zeros((N_HEADS, HEAD_PAD - QK_NOPE, KV_LORA), F32)], axis=1).astype(BF16)
    sel = jnp.zeros((HEAD_PAD, QK_ROPE), F32).at[QK_NOPE + jnp.arange(QK_ROPE), jnp.arange(QK_ROPE)].set(1.0)
    w_uv_h = w_uv.transpose(1, 0, 2)
    zero_v = jnp.zeros((KV_LORA, V_HEAD), F32)
    w_uv_pair = jnp.stack([
        jnp.concatenate([jnp.concatenate([w_uv_h[2 * p], zero_v], axis=1),
                         jnp.concatenate([zero_v, w_uv_h[2 * p + 1]], axis=1)], axis=0)
        for p in range(N_PAIRS)]).astype(BF16)

    def block_diag(w):
        eye = jnp.eye(REC_BLOCKS, dtype=F32)
        return jnp.einsum('nde,nm->ndme', w, eye).reshape(REC_WIDTH, REC_WIDTH)

    bd_r, bd_i = block_diag(w_rg), block_diag(w_ig)
    half = REC_WIDTH // 2
    w_gate = jnp.stack([
        jnp.concatenate([bd_r[j * half:(j + 1) * half, j * half:(j + 1) * half],
                         bd_i[j * half:(j + 1) * half, j * half:(j + 1) * half]], axis=1)
        for j in range(2)]).astype(BF16)

    w_router_t = jnp.concatenate([w_group.T, jnp.zeros((SUBLANE - N_GROUPS, d), F32), w_expert.T], axis=0)
    row = lambda v: v.reshape(1, -1)
    return {
        "w_in": w_in_ext, "q_norm_g": row(q_norm_g), "w_uq_t": w_uq_t, "kv_norm_g": row(kv_norm_g),
        "w_uk": w_uk_pad, "w_uv_t": w_uv_t, "w_uk_t": w_uk_t, "w_pe_sel": sel.astype(BF16), "w_uv_pair": w_uv_pair,
        "conv_w": conv_w, "conv_b": row(conv_b), "w_gate": w_gate, "b_rg": row(b_rg), "b_ig": row(b_ig),
        "lru_lambda": row(lru_lambda), "att_out_g": row(att_out_g), "rec_out_g": row(rec_out_g),
        "w_out": w_out.astype(BF16), "ln1_g": row(ln1_g), "ln1_b": row(ln1_b),
        "w_router_t": w_router_t, "b_group": b_group.reshape(-1, 1), "b_expert": b_expert.reshape(-1, 1),
        "w_gate_up": w_gate_up.astype(BF16).reshape(N_GROUPS, EXPERTS_PER_GROUP, d, 2 * D_EXPERT),
        "w_down": w_down.astype(BF16).reshape(N_GROUPS, EXPERTS_PER_GROUP, D_EXPERT, d),
        "ln2_g": row(ln2_g), "ln2_b": row(ln2_b),
    }


def _rope_tables(pos):
    half = QK_ROPE // 2
    inv = ROPE_THETA ** (-(jnp.arange(half, dtype=F32) * 2.0 / QK_ROPE))
    ang = pos.astype(F32)[:, None] * inv[None, :]
    cos, sin = jnp.cos(ang), jnp.sin(ang)
    t = pos.shape[0]
    cos_t = jnp.concatenate([jnp.ones((t, QK_NOPE), F32), cos, cos,
                             jnp.zeros((t, HEAD_PAD - QK_NOPE - QK_ROPE), F32)], axis=1)
    sin_t = jnp.concatenate([jnp.zeros((t, QK_NOPE), F32), -sin, sin,
                             jnp.zeros((t, HEAD_PAD - QK_NOPE - QK_ROPE), F32)], axis=1)
    return cos_t, sin_t, cos_t.T, sin_t.T


def kernel(x_prompt, x_sample, cache_kv_latent, cache_k_rope, state_conv, state_rec, page_table,
           w_in, q_norm_g, w_uq, kv_norm_g, w_uk, w_uv, conv_w, conv_b, w_rg, b_rg, w_ig, b_ig,
           lru_lambda, att_out_g, rec_out_g, w_out, ln1_g, ln1_b, w_group, b_group, w_expert,
           b_expert, w_gate_up, w_down, ln2_g, ln2_b):
    wts = _prep_weights(w_in, q_norm_g, w_uq, kv_norm_g, w_uk, w_uv, conv_w, conv_b, w_rg, b_rg, w_ig, b_ig,
                        lru_lambda, att_out_g, rec_out_g, w_out, ln1_g, ln1_b, w_group, b_group, w_expert,
                        b_expert, w_gate_up, w_down, ln2_g, ln2_b)
    bp, sp, d = x_prompt.shape
    bd, td, _ = x_sample.shape
    assert td == 1, "the sample path handles one new token per sequence"
    past_len = page_table.shape[1] * PAGE_SIZE
    ko = QK_NOPE

    qt, k, vt, c_p, kpe_blk, rx, rg = _project(x_prompt, _rope_tables(jnp.arange(sp, dtype=I32)), wts,
                                               min(T_ATT, sp))
    xs = x_sample.reshape(1, bd, d)
    qt_s, _, _, c_s, kpe_s_blk, rx_s, rg_s = _project(xs, _rope_tables(jnp.full((bd,), past_len, I32)), wts, bd)
    kpe_s = kpe_s_blk[0, :, ko:ko + QK_ROPE]
    qlat, qpe = _absorb(qt_s[0, :, 0], wts)

    att_p, o_lat = _attention(qt, k, vt, page_table, qlat.transpose(1, 0, 2), qpe.transpose(1, 0, 2),
                              c_s.reshape(bd, 1, KV_LORA), kpe_s.reshape(bd, 1, QK_ROPE),
                              cache_kv_latent, cache_k_rope.transpose(0, 2, 1))

    rec_p, h_p = _lru_prompt(rx, rg, wts, min(TM_LRU, sp))
    tm_moe = min(TM_MOE, bp * sp)
    y_p = _merge_and_routed_ffn(x_prompt, att_p, rec_p, wts, min(TM_MERGE, sp, tm_moe), tm_moe, CH_MOE)
    kpe_p = kpe_blk[..., ko:ko + QK_ROPE]
    conv_p = rx[:, sp - (CONV_W - 1):, :]

    att_s = _value_up(o_lat.reshape(bd, N_HEADS * KV_LORA), wts)
    rec_s, conv_s_t, h_s = _lru_step(rx_s[0], rg_s[0], state_conv.transpose(1, 0, 2), state_rec, wts)
    y_s = _merge_and_ffn(xs, att_s, rec_s.reshape(1, bd, REC_WIDTH), wts, bd, bd, min(CH_MOE, bd))

    return (y_p, y_s.reshape(bd, 1, d), c_p, kpe_p, conv_p, h_p.reshape(bp, REC_WIDTH),
            c_s.reshape(bd, 1, KV_LORA), kpe_s.reshape(bd, 1, QK_ROPE), conv_s_t.transpose(1, 0, 2), h_s)
```

```python
import functools
import math

import jax
import jax.numpy as jnp
from jax import lax
from jax.experimental import pallas as pl
from jax.experimental.pallas import tpu as pltpu

F32 = jnp.float32
BF16 = jnp.bfloat16
I32 = jnp.int32

N_HEADS = 8
QK_NOPE = 64
QK_ROPE = 32
V_HEAD = 64
Q_LORA = 384
KV_LORA = 256
ROPE_THETA = 10000.0
SM_SCALE = (QK_NOPE + QK_ROPE) ** -0.5
REC_WIDTH = 512
REC_BLOCKS = 8
REC_BLOCK_W = REC_WIDTH // REC_BLOCKS
CONV_W = 4
LRU_C = 8.0
N_GROUPS = 4
EXPERTS_PER_GROUP = 8
N_EXPERTS = N_GROUPS * EXPERTS_PER_GROUP
TOP_K = 2
D_EXPERT = 256
DEPTH = 1
ALPHA = (2.0 * DEPTH) ** 0.25
LN_EPS = 1e-5
RMS_EPS = 1e-6
NEG_INF = -1e30
PAGE_SIZE = 128

LANE = 128
SUBLANE = 8
HEAD_PAD = LANE
N_PAIRS = N_HEADS * V_HEAD // LANE
VMEM_LIMIT = 56 * 1024 * 1024

T_ATT = 512
KEY_STRIP = 512
TM_LRU = 256
TM_MERGE = 512
TM_MOE = 1024
TM_ROUTED = 512
CH_EXPERT = 512
CH_MOE = 256
ROW_UNIT = 16
PAGES_PER_CHUNK = 32
CACHE_SLOTS = 3
EXP2_SCALE = SM_SCALE * math.log2(math.e)


def _dot(a, b):
    return jnp.dot(a, b, preferred_element_type=F32)


def _dot_nt(a, b, precision=None):
    return lax.dot_general(a, b, (((1,), (1,)), ((), ())), preferred_element_type=F32, precision=precision)


def _dot_tn(a, b):
    return lax.dot_general(a, b, (((0,), (0,)), ((), ())), preferred_element_type=F32)


def _cparams(semantics, flags=None):
    return pltpu.CompilerParams(dimension_semantics=semantics, vmem_limit_bytes=VMEM_LIMIT, flags=flags)


def _rmsnorm(x, g):
    return x * lax.rsqrt(jnp.mean(x * x, axis=-1, keepdims=True) + RMS_EPS) * g


def _layernorm(x, g, b):
    mu = jnp.mean(x, axis=-1, keepdims=True)
    xc = x - mu
    var = jnp.mean(xc * xc, axis=-1, keepdims=True)
    return xc * lax.rsqrt(var + LN_EPS) * g + b


def _gelu_tanh(x):
    return x * (0.5 * (1.0 + jnp.tanh(math.sqrt(2.0 / math.pi) * (x + 0.044715 * (x * x * x)))))


def _lru_coeffs(xc, pre_r, pre_i, b_r, b_i, lam):
    r = jax.nn.sigmoid(pre_r + b_r)
    i = jax.nn.sigmoid(pre_i + b_i)
    neg_lam = -lam
    softplus = jnp.maximum(neg_lam, 0.0) + jnp.log1p(jnp.exp(-jnp.abs(neg_lam)))
    log_a = (-LRU_C * softplus) * r
    a = jnp.exp(log_a)
    u = jnp.sqrt(-jnp.tanh(log_a) * (a * a + 1.0)) * (i * xc)
    return a, u


def _gate_preacts(xc, wg_ref):
    half = REC_WIDTH // 2
    g0 = _dot(xc[:, :half].astype(BF16), wg_ref[0])
    g1 = _dot(xc[:, half:].astype(BF16), wg_ref[1])
    pre_r = jnp.concatenate([g0[:, :half], g1[:, :half]], axis=1)
    pre_i = jnp.concatenate([g0[:, half:], g1[:, half:]], axis=1)
    return pre_r, pre_i


def _proj_kernel(x_ref, cos_ref, sin_ref, cost_ref, sint_ref, win_ref, qg_ref, wuqt_ref, kvg_ref, wuk_ref, wuvt_ref,
                 qt_ref, k_ref, vt_ref, ckv_ref, kpe_ref, rx_ref, rg_ref):
    x = x_ref[0].astype(BF16)
    z = _dot(x, win_ref[...])
    o_kv = Q_LORA
    o_ka = o_kv + KV_LORA
    o_kb = o_ka + LANE
    o_rx = o_kb + LANE
    o_rg = o_rx + REC_WIDTH
    qn = _rmsnorm(z[:, :o_kv], qg_ref[...]).astype(BF16)
    qq = _dot_nt(wuqt_ref[...], qn)
    cos_t = cost_ref[...]
    sin_t = sint_ref[...]
    sw = N_HEADS * HEAD_PAD
    for h in range(N_HEADS):
        lo = h * HEAD_PAD
        qt_ref[0, h, 0] = (qq[lo:lo + HEAD_PAD] * cos_t + qq[sw + lo:sw + lo + HEAD_PAD] * sin_t).astype(BF16)
    ckv = _rmsnorm(z[:, o_kv:o_ka], kvg_ref[...])
    ckv_ref[0] = ckv
    kpe = z[:, o_ka:o_kb] * cos_ref[...] + z[:, o_kb:o_rx] * sin_ref[...]
    kpe_ref[0] = kpe
    ckv_b = ckv.astype(BF16)
    kn = _dot(ckv_b, wuk_ref[...])
    for h in range(N_HEADS):
        lo = h * HEAD_PAD
        k_ref[0, h] = (kn[:, lo:lo + HEAD_PAD] + kpe).astype(BF16)
    vt = _dot_nt(wuvt_ref[...], ckv_b)
    for p in range(N_PAIRS):
        vt_ref[0, p, 0] = vt[p * LANE:(p + 1) * LANE].astype(BF16)
    rx_ref[0] = z[:, o_rx:o_rg]
    rg_ref[0] = z[:, o_rg:o_rg + REC_WIDTH]


def _project(x, tables, wts, tm):
    b, s, d = x.shape
    nt = s // tm
    in_w = wts["w_in"].shape[1]
    full = lambda shape: pl.BlockSpec(shape, lambda bi, si: (0,) * len(shape))
    cos_n, sin_n, cos_t, sin_t = tables
    return pl.pallas_call(
        _proj_kernel,
        grid=(b, nt),
        in_specs=[
            pl.BlockSpec((1, tm, d), lambda bi, si: (bi, si, 0)),
            pl.BlockSpec((tm, LANE), lambda bi, si: (si, 0)),
            pl.BlockSpec((tm, LANE), lambda bi, si: (si, 0)),
            pl.BlockSpec((HEAD_PAD, tm), lambda bi, si: (0, si)),
            pl.BlockSpec((HEAD_PAD, tm), lambda bi, si: (0, si)),
            full((d, in_w)),
            full((1, Q_LORA)),
            full((2 * N_HEADS * HEAD_PAD, Q_LORA)),
            full((1, KV_LORA)),
            full((KV_LORA, N_HEADS * HEAD_PAD)),
            full((N_PAIRS * LANE, KV_LORA)),
        ],
        out_specs=[
            pl.BlockSpec((1, N_HEADS, 1, HEAD_PAD, tm), lambda bi, si: (bi, 0, si, 0, 0)),
            pl.BlockSpec((1, N_HEADS, tm, HEAD_PAD), lambda bi, si: (bi, 0, si, 0)),
            pl.BlockSpec((1, N_PAIRS, 1, LANE, tm), lambda bi, si: (bi, 0, si, 0, 0)),
            pl.BlockSpec((1, tm, KV_LORA), lambda bi, si: (bi, si, 0)),
            pl.BlockSpec((1, tm, LANE), lambda bi, si: (bi, si, 0)),
            pl.BlockSpec((1, tm, REC_WIDTH), lambda bi, si: (bi, si, 0)),
            pl.BlockSpec((1, tm, REC_WIDTH), lambda bi, si: (bi, si, 0)),
        ],
        out_shape=[
            jax.ShapeDtypeStruct((b, N_HEADS, nt, HEAD_PAD, tm), BF16),
            jax.ShapeDtypeStruct((b, N_HEADS, s, HEAD_PAD), BF16),
            jax.ShapeDtypeStruct((b, N_PAIRS, nt, LANE, tm), BF16),
            jax.ShapeDtypeStruct((b, s, KV_LORA), F32),
            jax.ShapeDtypeStruct((b, s, LANE), F32),
            jax.ShapeDtypeStruct((b, s, REC_WIDTH), F32),
            jax.ShapeDtypeStruct((b, s, REC_WIDTH), F32),
        ],
        compiler_params=_cparams(("parallel", "parallel")),
        name="proj",
    )(x, cos_n, sin_n, cos_t, sin_t, wts["w_in"], wts["q_norm_g"], wts["w_uq_t"], wts["kv_norm_g"],
      wts["w_uk"], wts["w_uv_t"])


def _lru_prompt_kernel(rx_ref, rg_ref, cw_ref, cb_ref, wg_ref, br_ref, bi_ref, lam_ref,
                       y_ref, hlast_ref, xp_ref, h_ref):
    tm = rx_ref.shape[1]
    si = pl.program_id(1)

    @pl.when(si == 0)
    def _():
        xp_ref[0:SUBLANE, :] = jnp.zeros((SUBLANE, REC_WIDTH), F32)
        h_ref[...] = jnp.zeros_like(h_ref)

    x = rx_ref[0]
    xp_ref[SUBLANE:SUBLANE + tm, :] = x
    xc = cb_ref[...] + x * cw_ref[CONV_W - 1:CONV_W, :]
    for m in range(1, CONV_W):
        xc = xc + xp_ref[pl.ds(SUBLANE - m, tm), :] * cw_ref[CONV_W - 1 - m:CONV_W - m, :]
    xp_ref[0:SUBLANE, :] = x[tm - SUBLANE:, :]

    pre_r, pre_i = _gate_preacts(xc, wg_ref)
    a, u = _lru_coeffs(xc, pre_r, pre_i, br_ref[...], bi_ref[...], lam_ref[...])

    row = lax.broadcasted_iota(I32, (tm, REC_WIDTH), 0)
    d = 1
    while d < tm:
        keep = row >= d
        a_sh = jnp.where(keep, pltpu.roll(a, d, 0), 1.0)
        u_sh = jnp.where(keep, pltpu.roll(u, d, 0), 0.0)
        u = u + a * u_sh
        a = a * a_sh
        d *= 2
    h = a * h_ref[...] + u
    h_ref[...] = h[tm - 1:tm, :]
    hlast_ref[0] = h[tm - 1:tm, :]
    y_ref[0] = h * _gelu_tanh(rg_ref[0])


def _lru_prompt(rx, rg, wts, tm):
    b, s, w = rx.shape
    full = lambda shape: pl.BlockSpec(shape, lambda bi, si: (0,) * len(shape))
    return pl.pallas_call(
        _lru_prompt_kernel,
        grid=(b, s // tm),
        in_specs=[
            pl.BlockSpec((1, tm, w), lambda bi, si: (bi, si, 0)),
            pl.BlockSpec((1, tm, w), lambda bi, si: (bi, si, 0)),
            full((CONV_W, w)), full((1, w)), full((2, w // 2, w)), full((1, w)), full((1, w)), full((1, w)),
        ],
        out_specs=[
            pl.BlockSpec((1, tm, w), lambda bi, si: (bi, si, 0)),
            pl.BlockSpec((1, 1, w), lambda bi, si: (bi, 0, 0)),
        ],
        out_shape=[jax.ShapeDtypeStruct((b, s, w), F32), jax.ShapeDtypeStruct((b, 1, w), F32)],
        scratch_shapes=[pltpu.VMEM((tm + SUBLANE, w), F32), pltpu.VMEM((1, w), F32)],
        compiler_params=_cparams(("arbitrary", "arbitrary")),
        name="lru_prompt",
    )(rx, rg, wts["conv_w"], wts["conv_b"], wts["w_gate"], wts["b_rg"], wts["b_ig"], wts["lru_lambda"])


def _lru_step_kernel(rx_ref, rg_ref, conv_ref, h0_ref, cw_ref, cb_ref, wg_ref, br_ref, bi_ref, lam_ref,
                     y_ref, newconv_ref, h_ref):
    x = rx_ref[...]
    xc = cb_ref[...] + x * cw_ref[CONV_W - 1:CONV_W, :]
    for k in range(CONV_W - 1):
        xc = xc + conv_ref[k] * cw_ref[k:k + 1, :]
    pre_r, pre_i = _gate_preacts(xc, wg_ref)
    a, u = _lru_coeffs(xc, pre_r, pre_i, br_ref[...], bi_ref[...], lam_ref[...])
    h = a * h0_ref[...] + u
    h_ref[...] = h
    y_ref[...] = h * _gelu_tanh(rg_ref[...])
    for k in range(CONV_W - 2):
        newconv_ref[k] = conv_ref[k + 1]
    newconv_ref[CONV_W - 2] = x


def _lru_step(rx, rg, conv_t, h0, wts):
    n, w = rx.shape
    return pl.pallas_call(
        _lru_step_kernel,
        out_shape=[jax.ShapeDtypeStruct((n, w), F32),
                   jax.ShapeDtypeStruct((CONV_W - 1, n, w), F32),
                   jax.ShapeDtypeStruct((n, w), F32)],
        name="lru_step",
    )(rx, rg, conv_t, h0, wts["conv_w"], wts["conv_b"], wts["w_gate"], wts["b_rg"], wts["b_ig"], wts["lru_lambda"])


def _fused_attn_kernel(pt_ref, qt_ref, k_ref, vt_ref, qlat_ref, qpe_ref, cnew_ref, knew_ref, ckv_hbm, kr_hbm,
                       o_ref, olat_ref, cbuf, kbuf, sem, ms_ref, ls_ref, accs_ref,
                       *, n_batch, n_q, n_chunks, total_chunks):
    t = qt_ref.shape[-1]
    bi, pi, qi = pl.program_id(0), pl.program_id(1), pl.program_id(2)
    steps_per_group = n_q * (n_q + 1) // 2
    total_steps = n_batch * N_PAIRS * steps_per_group
    base = (bi * N_PAIRS + pi) * steps_per_group + (qi * (qi + 1)) // 2
    ks = KEY_STRIP
    key_i = lax.broadcasted_iota(I32, (ks, t), 0)
    qry_i = lax.broadcasted_iota(I32, (ks, t), 1)
    qts = [qt_ref[0, e, 0] for e in range(2)]

    cp = PAGES_PER_CHUNK
    ahead = CACHE_SLOTS - 1

    def copies(g, slot):
        out = []
        for p in range(cp):
            page = pt_ref[g * cp + p]
            rows = pl.ds(p * PAGE_SIZE, PAGE_SIZE)
            out.append(pltpu.make_async_copy(ckv_hbm.at[page], cbuf.at[slot, rows], sem.at[0, slot]))
            out.append(pltpu.make_async_copy(kr_hbm.at[page], kbuf.at[slot, :, rows], sem.at[1, slot]))
        return out

    def start(g, slot):
        for c in copies(g, slot):
            c.start()

    @pl.when(jnp.logical_and(jnp.logical_and(bi == 0, pi == 0), qi == 0))
    def _():
        for g0 in range(min(ahead, total_chunks)):
            start(g0, g0)

    def sample_chunk_stages(g):
        v = {}

        def fetch_and_score():
            @pl.when(g + ahead < total_chunks)
            def _():
                start(g + ahead, lax.rem(g + ahead, CACHE_SLOTS))

            slot = lax.rem(g, CACHE_SLOTS)
            for c in copies(g, slot):
                c.wait()
            v["b"] = lax.div(g, n_chunks)
            v["qlat"] = qlat_ref[v["b"]]
            v["qpe"] = qpe_ref[v["b"]]
            v["cb"] = cbuf[slot].astype(BF16)
            kb = kbuf[slot].astype(BF16)
            v["s"] = _dot_nt(v["qlat"].astype(BF16), v["cb"]) + _dot(v["qpe"].astype(BF16), kb)

        def softmax_and_values():
            first = lax.rem(g, n_chunks) == 0
            m = jnp.where(first, NEG_INF, ms_ref[...])
            l = jnp.where(first, 0.0, ls_ref[...])
            acc = jnp.where(first, 0.0, accs_ref[...])
            s = v["s"]
            m_new = jnp.maximum(m, jnp.max(s, axis=-1, keepdims=True))
            alpha = jnp.exp2((m - m_new) * EXP2_SCALE)
            p = jnp.exp2((s - m_new) * EXP2_SCALE)
            v["l"] = alpha * l + jnp.sum(p, axis=-1, keepdims=True)
            v["acc"] = alpha * acc + _dot(p.astype(BF16), v["cb"])
            v["m"] = m_new

        def finish():
            m_new, l, acc, b = v["m"], v["l"], v["acc"], v["b"]
            ms_ref[...] = m_new
            ls_ref[...] = l
            accs_ref[...] = acc
            cnew = cnew_ref[b]
            knew = knew_ref[b]
            s_new = (jnp.sum(v["qlat"] * cnew, axis=-1, keepdims=True)
                     + jnp.sum(v["qpe"] * knew, axis=-1, keepdims=True))
            m_fin = jnp.maximum(m_new, s_new)
            a_fin = jnp.exp2((m_new - m_fin) * EXP2_SCALE)
            p_new = jnp.exp2((s_new - m_fin) * EXP2_SCALE)
            olat_ref[b] = (a_fin * acc + p_new * cnew) / (a_fin * l + p_new)

        return [fetch_and_score, softmax_and_values, finish]

    def scores(j, r, e):
        kb = k_ref[0, e, pl.ds(pl.multiple_of(j * t + r * ks, ks), ks), :]
        return _dot(kb, qts[e])

    def step(j, carry, diagonal, with_chunk):
        stages = sample_chunk_stages(base + j) if with_chunk else []
        carry = list(carry)
        units = [(r, e) for r in range(t // ks) for e in range(2)]
        st_next = scores(j, *units[0])
        for u, (r, e) in enumerate(units):
            st = st_next
            if u + 1 < len(units):
                st_next = scores(j, *units[u + 1])
            if stages:
                stages.pop(0)()
            m, l, acc = carry[e]
            vb = vt_ref[0, 0, j, e * V_HEAD:(e + 1) * V_HEAD, r * ks:(r + 1) * ks]
            if diagonal:
                st = jnp.where(key_i + r * ks <= qry_i, st, NEG_INF)
            m_new = jnp.maximum(m, jnp.max(st, axis=0, keepdims=True))
            alpha = jnp.exp2((m - m_new) * EXP2_SCALE)
            pt = jnp.exp2((st - m_new) * EXP2_SCALE)
            l = alpha * l + jnp.sum(pt, axis=0, keepdims=True)
            acc = alpha * acc + _dot(vb, pt.astype(BF16))
            carry[e] = (m_new, l, acc)
        for stage in stages:
            stage()
        return tuple(carry)

    init = (jnp.full((1, t), NEG_INF, F32), jnp.zeros((1, t), F32), jnp.zeros((V_HEAD, t), F32))
    n_with = jnp.clip(total_chunks - base, 0, qi)
    carry = lax.fori_loop(0, n_with, functools.partial(step, diagonal=False, with_chunk=True), (init, init))
    carry = lax.fori_loop(n_with, qi, functools.partial(step, diagonal=False, with_chunk=False), carry)
    carry = lax.cond(base + qi < total_chunks,
                     functools.partial(step, qi, diagonal=True, with_chunk=True),
                     functools.partial(step, qi, diagonal=True, with_chunk=False), carry)
    halves = [acc / l for (_, l, acc) in carry]
    o_ref[0, 0] = jnp.concatenate(halves, axis=0).T

    if total_chunks > total_steps:
        @pl.when(jnp.logical_and(jnp.logical_and(bi == n_batch - 1, pi == N_PAIRS - 1), qi == n_q - 1))
        def _():
            def drain(g, c):
                for stage in sample_chunk_stages(g):
                    stage()
                return c
            lax.fori_loop(total_steps, total_chunks, drain, 0)


def _attention(qt, k, vt, page_table, qlat, qpe, c_new, k_new, cache_kv, cache_kr_t):
    b, _, nt, _, t = qt.shape
    s = nt * t
    bd, n_pages = page_table.shape
    n_chunks = n_pages // PAGES_PER_CHUNK
    rows = PAGES_PER_CHUNK * PAGE_SIZE
    kern = functools.partial(_fused_attn_kernel, n_batch=b, n_q=nt, n_chunks=n_chunks, total_chunks=bd * n_chunks)
    whole = lambda shape: pl.BlockSpec(shape, lambda bi, pi, qi, pt: (0,) * len(shape))
    grid_spec = pltpu.PrefetchScalarGridSpec(
        num_scalar_prefetch=1,
        grid=(b, N_PAIRS, nt),
        in_specs=[
            pl.BlockSpec((1, 2, 1, HEAD_PAD, t), lambda bi, pi, qi, pt: (bi, pi, qi, 0, 0)),
            pl.BlockSpec((1, 2, s, HEAD_PAD), lambda bi, pi, qi, pt: (bi, pi, 0, 0)),
            pl.BlockSpec((1, 1, nt, LANE, t), lambda bi, pi, qi, pt: (bi, pi, 0, 0, 0)),
            whole((bd, N_HEADS, KV_LORA)),
            whole((bd, N_HEADS, QK_ROPE)),
            whole((bd, 1, KV_LORA)),
            whole((bd, 1, QK_ROPE)),
            pl.BlockSpec(memory_space=pl.ANY),
            pl.BlockSpec(memory_space=pl.ANY),
        ],
        out_specs=[
            pl.BlockSpec((1, 1, t, LANE), lambda bi, pi, qi, pt: (bi, pi, qi, 0)),
            whole((bd, N_HEADS, KV_LORA)),
        ],
        scratch_shapes=[
            pltpu.VMEM((CACHE_SLOTS, rows, KV_LORA), F32),
            pltpu.VMEM((CACHE_SLOTS, QK_ROPE, rows), F32),
            pltpu.SemaphoreType.DMA((2, CACHE_SLOTS)),
            pltpu.VMEM((N_HEADS, 1), F32),
            pltpu.VMEM((N_HEADS, 1), F32),
            pltpu.VMEM((N_HEADS, KV_LORA), F32),
        ],
    )
    return pl.pallas_call(
        kern,
        grid_spec=grid_spec,
        out_shape=[jax.ShapeDtypeStruct((b, N_PAIRS, s, LANE), F32),
                   jax.ShapeDtypeStruct((bd, N_HEADS, KV_LORA), F32)],
        compiler_params=_cparams(("arbitrary", "arbitrary", "arbitrary")),
        name="attn_fused",
    )(page_table.reshape(-1), qt, k, vt, qlat, qpe, c_new, k_new, cache_kv, cache_kr_t)


def _absorb_kernel(qt_ref, wlat_ref, wpe_ref, qlat_ref, qpe_ref):
    for h in range(N_HEADS):
        qt = qt_ref[h]
        qlat_ref[h] = _dot_tn(qt, wlat_ref[h])
        qpe_ref[h] = _dot_tn(qt, wpe_ref[...])


def _absorb(qt, wts):
    _, _, n = qt.shape
    return pl.pallas_call(
        _absorb_kernel,
        out_shape=[jax.ShapeDtypeStruct((N_HEADS, n, KV_LORA), F32),
                   jax.ShapeDtypeStruct((N_HEADS, n, QK_ROPE), F32)],
        name="absorb_q",
    )(qt, wts["w_uk_t"], wts["w_pe_sel"])


def _value_up_kernel(olat_ref, wv_ref, o_ref):
    w = 2 * KV_LORA
    for p in range(N_PAIRS):
        o_ref[0, p] = _dot(olat_ref[:, p * w:(p + 1) * w].astype(BF16), wv_ref[p])


def _value_up(olat2d, wts):
    n = olat2d.shape[0]
    return pl.pallas_call(
        _value_up_kernel,
        out_shape=jax.ShapeDtypeStruct((1, N_PAIRS, n, LANE), F32),
        name="value_up",
    )(olat2d, wts["w_uv_pair"])


def _merge_kernel(x_ref, att_ref, rec_ref, ag_ref, rgn_ref, wout_ref, g1_ref, b1_ref, wr_ref, bg_ref, be_ref,
                  x1_ref, gsel_ref, gate_ref, sel_ref, cnt_ref):
    tm = x_ref.shape[1]
    att = [att_ref[0, p] for p in range(N_PAIRS)]
    ss = att[0] * att[0]
    for p in range(1, N_PAIRS):
        ss = ss + att[p] * att[p]
    inv = lax.rsqrt(jnp.sum(ss, axis=-1, keepdims=True) / (N_PAIRS * LANE) + RMS_EPS)
    parts = [(att[p] * inv * ag_ref[:, p * LANE:(p + 1) * LANE]).astype(BF16) for p in range(N_PAIRS)]
    parts.append(_rmsnorm(rec_ref[0], rgn_ref[...]).astype(BF16))
    mixed = jnp.concatenate(parts, axis=-1)
    mix = _dot(mixed, wout_ref[...])
    x1 = _layernorm(ALPHA * x_ref[0] + mix, g1_ref[...], b1_ref[...])
    x1_ref[0] = x1

    lt = _dot_nt(wr_ref[...], x1, precision=lax.Precision.HIGHEST)
    g = [lt[k:k + 1, :] for k in range(N_GROUPS)]
    gmax = functools.reduce(jnp.maximum, g)
    ex = [jnp.exp(gk - gmax) for gk in g]
    den = functools.reduce(lambda p, q: p + q, ex)
    best = g[0] + bg_ref[0:1, :]
    idx = jnp.zeros((1, tm), I32)
    for k in range(1, N_GROUPS):
        cand = g[k] + bg_ref[k:k + 1, :]
        upd = cand > best
        idx = jnp.where(upd, k, idx)
        best = jnp.where(upd, cand, best)
    gp = ex[0]
    e_sel = lt[SUBLANE:SUBLANE + EXPERTS_PER_GROUP, :]
    e_bias = jnp.broadcast_to(be_ref[0:EXPERTS_PER_GROUP, :], (EXPERTS_PER_GROUP, tm))
    for k in range(1, N_GROUPS):
        hit = idx == k
        lo = SUBLANE + k * EXPERTS_PER_GROUP
        gp = jnp.where(hit, ex[k], gp)
        e_sel = jnp.where(hit, lt[lo:lo + EXPERTS_PER_GROUP, :], e_sel)
        e_bias = jnp.where(hit, be_ref[k * EXPERTS_PER_GROUP:(k + 1) * EXPERTS_PER_GROUP, :], e_bias)
    g_prob = gp / den
    sc = e_sel + e_bias
    sub = lax.broadcasted_iota(I32, (EXPERTS_PER_GROUP, tm), 0)
    m1 = jnp.max(sc, axis=0, keepdims=True)
    i1 = jnp.min(jnp.where(sc == m1, sub, EXPERTS_PER_GROUP), axis=0, keepdims=True)
    mask1 = sub == i1
    sc2 = jnp.where(mask1, -jnp.inf, sc)
    m2 = jnp.max(sc2, axis=0, keepdims=True)
    i2 = jnp.min(jnp.where(sc2 == m2, sub, EXPERTS_PER_GROUP), axis=0, keepdims=True)
    mask2 = sub == i2
    v1 = jnp.sum(jnp.where(mask1, e_sel, 0.0), axis=0, keepdims=True)
    v2 = jnp.sum(jnp.where(mask2, e_sel, 0.0), axis=0, keepdims=True)
    vm = jnp.maximum(v1, v2)
    e1 = jnp.exp(v1 - vm)
    e2 = jnp.exp(v2 - vm)
    esum = e1 + e2
    gate = g_prob * (jnp.where(mask1, e1 / esum, 0.0) + jnp.where(mask2, e2 / esum, 0.0))
    gsel_ref[0] = idx
    gate_ref[0] = gate
    sel = jnp.where(jnp.logical_or(mask1, mask2), 1.0, 0.0)
    sel_ref[0] = sel
    for k in range(N_GROUPS):
        ck = jnp.sum(jnp.where(idx == k, sel, 0.0), axis=-1, keepdims=True)
        cnt_ref[0, k * EXPERTS_PER_GROUP:(k + 1) * EXPERTS_PER_GROUP, :] = jnp.broadcast_to(
            ck, (EXPERTS_PER_GROUP, LANE))


def _merge(x, att, rec, wts, tm):
    b, s, d = x.shape
    nt = s // tm
    mw = wts["w_out"].shape[0]
    n_r = wts["w_router_t"].shape[0]
    full = lambda shape: pl.BlockSpec(shape, lambda bi, si: (0,) * len(shape))
    return pl.pallas_call(
        _merge_kernel,
        grid=(b, nt),
        in_specs=[
            pl.BlockSpec((1, tm, d), lambda bi, si: (bi, si, 0)),
            pl.BlockSpec((1, N_PAIRS, tm, LANE), lambda bi, si: (bi, 0, si, 0)),
            pl.BlockSpec((1, tm, REC_WIDTH), lambda bi, si: (bi, si, 0)),
            full((1, N_PAIRS * LANE)), full((1, REC_WIDTH)), full((mw, d)), full((1, d)), full((1, d)),
            full((n_r, d)), full((N_GROUPS, 1)), full((N_GROUPS * EXPERTS_PER_GROUP, 1)),
        ],
        out_specs=[
            pl.BlockSpec((1, tm, d), lambda bi, si: (bi, si, 0)),
            pl.BlockSpec((1, 1, tm), lambda bi, si: (bi * nt + si, 0, 0)),
            pl.BlockSpec((1, EXPERTS_PER_GROUP, tm), lambda bi, si: (bi * nt + si, 0, 0)),
            pl.BlockSpec((1, EXPERTS_PER_GROUP, tm), lambda bi, si: (bi * nt + si, 0, 0)),
            pl.BlockSpec((1, N_EXPERTS, LANE), lambda bi, si: (bi * nt + si, 0, 0)),
        ],
        out_shape=[
            jax.ShapeDtypeStruct((b, s, d), F32),
            jax.ShapeDtypeStruct((b * nt, 1, tm), I32),
            jax.ShapeDtypeStruct((b * nt, EXPERTS_PER_GROUP, tm), F32),
            jax.ShapeDtypeStruct((b * nt, EXPERTS_PER_GROUP, tm), F32),
            jax.ShapeDtypeStruct((b * nt, N_EXPERTS, LANE), F32),
        ],
        compiler_params=_cparams(("parallel", "parallel")),
        name="merge_router",
    )(x, att, rec, wts["att_out_g"], wts["rec_out_g"], wts["w_out"], wts["ln1_g"], wts["ln1_b"],
      wts["w_router_t"], wts["b_group"], wts["b_expert"])


def _moe_kernel(nch_ref, x1_ref, gsel_ref, gate_ref, wgu_ref, wd_ref, g2_ref, b2_ref, o_ref, xb_ref, tri_ref,
                *, ch):
    tm = x1_ref.shape[0]
    ti = pl.program_id(0)
    gi = pl.program_id(1)

    @pl.when(jnp.logical_and(ti == 0, gi == 0))
    def _():
        r = lax.broadcasted_iota(I32, (tm, tm), 0)
        c = lax.broadcasted_iota(I32, (tm, tm), 1)
        tri_ref[...] = jnp.where(r < c, 1.0, 0.0).astype(BF16)

    @pl.when(gi == 0)
    def _():
        xb_ref[...] = x1_ref[...].astype(BF16)
        o_ref[...] = jnp.zeros_like(o_ref)

    in_group = gsel_ref[0] == gi
    member = jnp.broadcast_to(jnp.where(in_group, 1.0, 0.0), (SUBLANE, tm)).astype(BF16)
    before = _dot(member, tri_ref[...])
    rank = jnp.where(in_group, before[0:1, :].astype(I32), -1)
    gate = gate_ref[0]
    g_hi = gate.astype(BF16).astype(F32)
    g_mid = (gate - g_hi).astype(BF16).astype(F32)
    g_lo = (gate - g_hi) - g_mid
    n_terms = 3
    gate_terms = jnp.concatenate(
        [g_hi, g_mid, g_lo, jnp.zeros((LANE - n_terms * EXPERTS_PER_GROUP, tm), F32)], axis=0).astype(BF16)

    def chunk(c, carry):
        slot_id = lax.broadcasted_iota(I32, (ch, tm), 0) + c * ch
        onehot_b = jnp.where(slot_id == rank, 1.0, 0.0).astype(BF16)
        xg = _dot(onehot_b, xb_ref[...]).astype(BF16)
        gt = _dot_nt(onehot_b, gate_terms)
        gc = gt
        for k in range(1, n_terms):
            gc = gc + pltpu.roll(gt, LANE - k * EXPERTS_PER_GROUP, 1)
        acc = jnp.zeros((ch, o_ref.shape[1]), F32)
        for j in range(EXPERTS_PER_GROUP):
            gu = _dot(xg, wgu_ref[0, j])
            hid = jax.nn.silu(gu[:, :D_EXPERT]) * gu[:, D_EXPERT:]
            acc = acc + gc[:, j:j + 1] * _dot(hid.astype(BF16), wd_ref[0, j])
        o_ref[...] += _dot_tn(onehot_b, acc.astype(BF16))
        return carry

    lax.fori_loop(0, nch_ref[ti * N_GROUPS + gi], chunk, 0)

    @pl.when(gi == N_GROUPS - 1)
    def _():
        o_ref[...] = _layernorm(ALPHA * x1_ref[...] + o_ref[...], g2_ref[...], b2_ref[...])


def _moe(x1, gsel, gate, nch, wts, tm, ch):
    n, d = x1.shape
    nt = n // tm
    e2 = 2 * D_EXPERT
    grid_spec = pltpu.PrefetchScalarGridSpec(
        num_scalar_prefetch=1,
        grid=(nt, N_GROUPS),
        in_specs=[
            pl.BlockSpec((tm, d), lambda ti, gi, nc: (ti, 0)),
            pl.BlockSpec((1, 1, tm), lambda ti, gi, nc: (ti, 0, 0)),
            pl.BlockSpec((1, EXPERTS_PER_GROUP, tm), lambda ti, gi, nc: (ti, 0, 0)),
            pl.BlockSpec((1, EXPERTS_PER_GROUP, d, e2), lambda ti, gi, nc: (gi, 0, 0, 0)),
            pl.BlockSpec((1, EXPERTS_PER_GROUP, D_EXPERT, d), lambda ti, gi, nc: (gi, 0, 0, 0)),
            pl.BlockSpec((1, d), lambda ti, gi, nc: (0, 0)),
            pl.BlockSpec((1, d), lambda ti, gi, nc: (0, 0)),
        ],
        out_specs=pl.BlockSpec((tm, d), lambda ti, gi, nc: (ti, 0)),
        scratch_shapes=[pltpu.VMEM((tm, d), BF16), pltpu.VMEM((tm, tm), BF16)],
    )
    return pl.pallas_call(
        functools.partial(_moe_kernel, ch=ch),
        grid_spec=grid_spec,
        out_shape=jax.ShapeDtypeStruct((n, d), F32),
        compiler_params=_cparams(("arbitrary", "arbitrary")),
        name="moe",
    )(nch, x1, gsel, gate, wts["w_gate_up"], wts["w_down"], wts["ln2_g"], wts["ln2_b"])


def _build_tri(tri_ref):
    tm = tri_ref.shape[0]
    r = lax.broadcasted_iota(I32, (tm, tm), 0)
    c = lax.broadcasted_iota(I32, (tm, tm), 1)
    tri_ref[...] = jnp.where(r < c, 1.0, 0.0).astype(BF16)


def _sorted_positions(gsel, sel, start_col, tri_ref):
    member = jnp.concatenate([jnp.where(gsel == g, sel, 0.0) for g in range(N_GROUPS)], axis=0)
    before = _dot(member.astype(BF16), tri_ref[...])
    routed = member > 0.5
    pos = jnp.where(routed, start_col + before, -1.0)
    pos_a = jnp.max(pos, axis=0, keepdims=True)
    pos_b = jnp.sum(jnp.where(routed, start_col + before, 0.0), axis=0, keepdims=True) - pos_a
    return pos, pos_a.astype(I32), pos_b.astype(I32)


def _exact_terms(v):
    hi = v.astype(BF16).astype(F32)
    mid = (v - hi).astype(BF16).astype(F32)
    lo = (v - hi) - mid
    row = lax.broadcasted_iota(I32, (LANE, v.shape[1]), 0)
    return jnp.where(row == 0, hi, jnp.where(row == 1, mid, jnp.where(row == 2, lo, 0.0))).astype(BF16)


def _dispatch_kernel(udst_ref, nun_ref, nch_ref, tstart_ref, tunits_ref, rest_ref, x1_ref, gsel_ref, sel_ref,
                     gate_ref, start_ref, xs_hbm, gs_hbm, xb_ref, tri_ref, xsrt, gsrt, sem, *, rch, ech, umax):
    i = pl.program_id(0)
    tm = x1_ref.shape[0]

    @pl.when(i == 0)
    def _():
        _build_tri(tri_ref)
        unit = pl.ds(0, ROW_UNIT)
        xsrt[unit, :] = jnp.zeros((ROW_UNIT, xsrt.shape[1]), BF16)
        gsrt[unit, :] = jnp.zeros((ROW_UNIT, LANE), F32)

        def tail_copies(e, k):
            dst = pl.ds(pl.multiple_of(tstart_ref[e] + k * ROW_UNIT, ROW_UNIT), ROW_UNIT)
            return (pltpu.make_async_copy(xsrt.at[unit], xs_hbm.at[dst], sem.at[0]),
                    pltpu.make_async_copy(gsrt.at[unit], gs_hbm.at[dst], sem.at[1]))

        def per_expert(fn):
            def over_experts(e, carry):
                def over_units(k, c):
                    for cp in tail_copies(e, k):
                        fn(cp)
                    return c
                return lax.fori_loop(0, tunits_ref[e], over_units, carry)
            lax.fori_loop(0, N_EXPERTS, over_experts, 0)

        per_expert(lambda cp: cp.start())
        per_expert(lambda cp: cp.wait())

        blk = pl.ds(0, ech)
        xsrt[blk, :] = jnp.zeros((ech, xsrt.shape[1]), BF16)
        gsrt[blk, :] = jnp.zeros((ech, LANE), F32)

        def rest_copies(k):
            dst = pl.ds(pl.multiple_of(rest_ref[0] + k * ech, ech), ech)
            return (pltpu.make_async_copy(xsrt.at[blk], xs_hbm.at[dst], sem.at[0]),
                    pltpu.make_async_copy(gsrt.at[blk], gs_hbm.at[dst], sem.at[1]))

        def start_rest(k, c):
            for cp in rest_copies(k):
                cp.start()
            return c

        def wait_rest(k, c):
            for cp in rest_copies(k):
                cp.wait()
            return c

        lax.fori_loop(0, rest_ref[1], start_rest, 0)
        lax.fori_loop(0, rest_ref[1], wait_rest, 0)

    xb_ref[...] = x1_ref[...].astype(BF16)
    gsel = gsel_ref[0]
    pos, pos_a, pos_b = _sorted_positions(gsel, sel_ref[0], start_ref[0], tri_ref)
    gate32 = jnp.concatenate([jnp.where(gsel == g, gate_ref[0], 0.0) for g in range(N_GROUPS)], axis=0)
    gate_a = jnp.sum(jnp.where(pos == pos_a.astype(F32), gate32, 0.0), axis=0, keepdims=True)
    gate_b = jnp.sum(jnp.where(pos == pos_b.astype(F32), gate32, 0.0), axis=0, keepdims=True)
    terms_a = _exact_terms(gate_a)
    terms_b = _exact_terms(gate_b)

    def chunk(c, carry):
        r0 = pl.multiple_of(c * rch, rch)
        rid = lax.broadcasted_iota(I32, (rch, tm), 0) + r0
        hit_a = rid == pos_a
        hit_b = rid == pos_b
        onehot = jnp.where(jnp.logical_or(hit_a, hit_b), 1.0, 0.0).astype(BF16)
        xsrt[pl.ds(r0, rch), :] = _dot(onehot, xb_ref[...]).astype(BF16)
        gt = (_dot_nt(jnp.where(hit_a, 1.0, 0.0).astype(BF16), terms_a)
              + _dot_nt(jnp.where(hit_b, 1.0, 0.0).astype(BF16), terms_b))
        gsrt[pl.ds(r0, rch), :] = (gt + pltpu.roll(gt, LANE - 1, 1)) + pltpu.roll(gt, LANE - 2, 1)
        return carry

    lax.fori_loop(0, nch_ref[i], chunk, 0)

    def unit_copies(u):
        src = pl.ds(pl.multiple_of(u * ROW_UNIT, ROW_UNIT), ROW_UNIT)
        dst = pl.ds(pl.multiple_of(udst_ref[i * umax + u], ROW_UNIT), ROW_UNIT)
        return (pltpu.make_async_copy(xsrt.at[src], xs_hbm.at[dst], sem.at[0]),
                pltpu.make_async_copy(gsrt.at[src], gs_hbm.at[dst], sem.at[1]))

    def start_unit(u, carry):
        for cp in unit_copies(u):
            cp.start()
        return carry

    def wait_unit(u, carry):
        for cp in unit_copies(u):
            cp.wait()
        return carry

    lax.fori_loop(0, nun_ref[i], start_unit, 0)
    lax.fori_loop(0, nun_ref[i], wait_unit, 0)


def _dispatch(x1, gsel, sel, gate, start_col, tables, tails, total_rows, tm, rch, ech, umax):
    n, d = x1.shape
    nt = n // tm
    rows_max = umax * ROW_UNIT
    assert rows_max >= ech, "the sorted-tile buffer doubles as the zero source of one expert chunk"
    unit_dst, n_units, n_chunks = tables
    tail_start, tail_units, rest = tails
    grid_spec = pltpu.PrefetchScalarGridSpec(
        num_scalar_prefetch=6,
        grid=(nt,),
        in_specs=[
            pl.BlockSpec((tm, d), lambda i, *_: (i, 0)),
            pl.BlockSpec((1, 1, tm), lambda i, *_: (i, 0, 0)),
            pl.BlockSpec((1, EXPERTS_PER_GROUP, tm), lambda i, *_: (i, 0, 0)),
            pl.BlockSpec((1, EXPERTS_PER_GROUP, tm), lambda i, *_: (i, 0, 0)),
            pl.BlockSpec((1, N_EXPERTS, 1), lambda i, *_: (i, 0, 0)),
        ],
        out_specs=[pl.BlockSpec(memory_space=pl.ANY), pl.BlockSpec(memory_space=pl.ANY)],
        scratch_shapes=[
            pltpu.VMEM((tm, d), BF16),
            pltpu.VMEM((tm, tm), BF16),
            pltpu.VMEM((rows_max, d), BF16),
            pltpu.VMEM((rows_max, LANE), F32),
            pltpu.SemaphoreType.DMA((2,)),
        ],
    )
    return pl.pallas_call(
        functools.partial(_dispatch_kernel, rch=rch, ech=ech, umax=umax),
        grid_spec=grid_spec,
        out_shape=[jax.ShapeDtypeStruct((total_rows, d), BF16), jax.ShapeDtypeStruct((total_rows, LANE), F32)],
        compiler_params=_cparams(("arbitrary",)),
        name="moe_dispatch",
    )(unit_dst, n_units, n_chunks, tail_start, tail_units, rest, x1, gsel, sel, gate, start_col)


def _expert_kernel(ce_ref, x_ref, g_ref, wgu_ref, wd_ref, y_ref):
    c = pl.program_id(0)

    @pl.when(ce_ref[c] >= 0)
    def _():
        half = x_ref.shape[0] // 2
        for h in range(2):
            rows = pl.ds(h * half, half)
            gu = _dot(x_ref[rows, :], wgu_ref[0])
            hid = jax.nn.silu(gu[:, :D_EXPERT]) * gu[:, D_EXPERT:]
            y_ref[rows, :] = (g_ref[rows, 0:1] * _dot(hid.astype(BF16), wd_ref[0])).astype(BF16)

    @pl.when(ce_ref[c] < 0)
    def _():
        y_ref[...] = jnp.zeros_like(y_ref)


def _experts(xs, gs, chunk_expert, wts, rch):
    total_rows, d = xs.shape
    e2 = 2 * D_EXPERT
    w_gu = wts["w_gate_up"].reshape(N_EXPERTS, d, e2)
    w_dn = wts["w_down"].reshape(N_EXPERTS, D_EXPERT, d)
    grid_spec = pltpu.PrefetchScalarGridSpec(
        num_scalar_prefetch=1,
        grid=(total_rows // rch,),
        in_specs=[
            pl.BlockSpec((rch, d), lambda c, ce: (jnp.where(ce[c] >= 0, c, 0), 0)),
            pl.BlockSpec((rch, LANE), lambda c, ce: (jnp.where(ce[c] >= 0, c, 0), 0)),
            pl.BlockSpec((1, d, e2), lambda c, ce: (jnp.maximum(ce[c], 0), 0, 0)),
            pl.BlockSpec((1, D_EXPERT, d), lambda c, ce: (jnp.maximum(ce[c], 0), 0, 0)),
        ],
        out_specs=pl.BlockSpec((rch, d), lambda c, ce: (c, 0)),
    )
    return pl.pallas_call(
        _expert_kernel,
        grid_spec=grid_spec,
        out_shape=jax.ShapeDtypeStruct((total_rows, d), BF16),
        compiler_params=_cparams(("arbitrary",)),
        name="moe_experts",
    )(chunk_expert, xs, gs, w_gu, w_dn)


def _combine_kernel(udst_ref, nun_ref, nch_ref, x1_ref, gsel_ref, sel_ref, start_ref, ys_hbm, g2_ref, b2_ref,
                    o_ref, tri_ref, ysrt, sem, *, rch, umax):
    i = pl.program_id(0)
    tm = x1_ref.shape[0]

    @pl.when(i == 0)
    def _():
        _build_tri(tri_ref)
        ysrt[...] = jnp.zeros_like(ysrt)

    def unit_copy(u):
        src = pl.ds(pl.multiple_of(udst_ref[i * umax + u], ROW_UNIT), ROW_UNIT)
        dst = pl.ds(pl.multiple_of(u * ROW_UNIT, ROW_UNIT), ROW_UNIT)
        return pltpu.make_async_copy(ys_hbm.at[src], ysrt.at[dst], sem.at[0])

    def start_unit(u, carry):
        unit_copy(u).start()
        return carry

    def wait_unit(u, carry):
        unit_copy(u).wait()
        return carry

    lax.fori_loop(0, nun_ref[i], start_unit, 0)
    _, pos_a, pos_b = _sorted_positions(gsel_ref[0], sel_ref[0], start_ref[0], tri_ref)
    o_ref[...] = jnp.zeros_like(o_ref)
    lax.fori_loop(0, nun_ref[i], wait_unit, 0)

    def chunk(c, carry):
        r0 = pl.multiple_of(c * rch, rch)
        rid = lax.broadcasted_iota(I32, (rch, tm), 0) + r0
        onehot = jnp.where(jnp.logical_or(rid == pos_a, rid == pos_b), 1.0, 0.0).astype(BF16)
        o_ref[...] += _dot_tn(onehot, ysrt[pl.ds(r0, rch), :])
        return carry

    lax.fori_loop(0, nch_ref[i], chunk, 0)
    o_ref[...] = _layernorm(ALPHA * x1_ref[...] + o_ref[...], g2_ref[...], b2_ref[...])


def _combine(x1, gsel, sel, start_col, ys, tables, wts, tm, rch, umax):
    n, d = x1.shape
    nt = n // tm
    unit_dst, n_units, n_chunks = tables
    grid_spec = pltpu.PrefetchScalarGridSpec(
        num_scalar_prefetch=3,
        grid=(nt,),
        in_specs=[
            pl.BlockSpec((tm, d), lambda i, *_: (i, 0)),
            pl.BlockSpec((1, 1, tm), lambda i, *_: (i, 0, 0)),
            pl.BlockSpec((1, EXPERTS_PER_GROUP, tm), lambda i, *_: (i, 0, 0)),
            pl.BlockSpec((1, N_EXPERTS, 1), lambda i, *_: (i, 0, 0)),
            pl.BlockSpec(memory_space=pl.ANY),
            pl.BlockSpec((1, d), lambda i, *_: (0, 0)),
            pl.BlockSpec((1, d), lambda i, *_: (0, 0)),
        ],
        out_specs=pl.BlockSpec((tm, d), lambda i, *_: (i, 0)),
        scratch_shapes=[
            pltpu.VMEM((tm, tm), BF16),
            pltpu.VMEM((umax * ROW_UNIT, d), BF16),
            pltpu.SemaphoreType.DMA((1,)),
        ],
    )
    return pl.pallas_call(
        functools.partial(_combine_kernel, rch=rch, umax=umax),
        grid_spec=grid_spec,
        out_shape=jax.ShapeDtypeStruct((n, d), F32),
        compiler_params=_cparams(("arbitrary",)),
        name="moe_combine",
    )(unit_dst, n_units, n_chunks, x1, gsel, sel, start_col, ys, wts["ln2_g"], wts["ln2_b"])


def _routing_tables(cnt, tm, rch, ech):
    nt = cnt.shape[0]
    seg = (cnt + (ROW_UNIT - 1)) // ROW_UNIT * ROW_UNIT
    start = jnp.cumsum(seg, axis=1) - seg
    rows = seg.sum(axis=1)
    per_expert = seg.sum(axis=0)
    region = (per_expert + (ech - 1)) // ech * ech
    region_start = jnp.cumsum(region) - region
    seg_dst = region_start[None, :] + jnp.cumsum(seg, axis=0) - seg
    rows_max = -(-(TOP_K * tm + N_EXPERTS * (ROW_UNIT - 1)) // rch) * rch
    umax = rows_max // ROW_UNIT
    total_rows = -(-(TOP_K * tm * nt + nt * N_EXPERTS * (ROW_UNIT - 1) + N_EXPERTS * (ech - 1)) // ech) * ech
    u_row = (jnp.arange(umax, dtype=I32) * ROW_UNIT)[None, :, None]
    in_seg = jnp.logical_and(u_row >= start[:, None, :], u_row < (start + seg)[:, None, :])
    unit_dst = jnp.sum(jnp.where(in_seg, (seg_dst - start)[:, None, :] + u_row, 0), axis=-1)
    c_row = (jnp.arange(total_rows // ech, dtype=I32) * ech)[:, None]
    in_region = jnp.logical_and(c_row >= region_start[None, :], c_row < (region_start + per_expert)[None, :])
    chunk_expert = jnp.sum(jnp.where(in_region, jnp.arange(N_EXPERTS, dtype=I32)[None, :] + 1, 0), axis=-1) - 1
    tables = (unit_dst.reshape(-1).astype(I32), (rows // ROW_UNIT).astype(I32),
              ((rows + (rch - 1)) // rch).astype(I32))
    used_rows = region.sum()
    tails = ((region_start + per_expert).astype(I32), ((region - per_expert) // ROW_UNIT).astype(I32),
             jnp.stack([used_rows, (total_rows - used_rows) // ech]).astype(I32))
    return tables, tails, chunk_expert.astype(I32), start.astype(F32).reshape(nt, N_EXPERTS, 1), total_rows, umax


def _merge_and_routed_ffn(x, att, rec, wts, tm_merge, tm, rch):
    b, s, d = x.shape
    x1, gsel, gate, sel, cnt = _merge(x, att, rec, wts, tm_merge)
    n = b * s
    nt = n // tm
    f = tm // tm_merge
    regroup = lambda a: a.reshape(nt, f, EXPERTS_PER_GROUP, tm_merge).transpose(0, 2, 1, 3).reshape(
        nt, EXPERTS_PER_GROUP, tm)
    gsel, gate, sel = gsel.reshape(nt, 1, tm), regroup(gate), regroup(sel)
    counts = cnt[:, :, 0].reshape(nt, f, N_EXPERTS).sum(axis=1).astype(I32)
    ech = CH_EXPERT
    tables, tails, chunk_expert, start_col, total_rows, umax = _routing_tables(counts, tm, rch, ech)
    x1 = x1.reshape(n, d)
    xs, gs = _dispatch(x1, gsel, sel, gate, start_col, tables, tails, total_rows, tm, rch, ech, umax)
    ys = _experts(xs, gs, chunk_expert, wts, ech)
    y = _combine(x1, gsel, sel, start_col, ys, tables, wts, tm, rch, umax)
    return y.reshape(b, s, d)


def _merge_and_ffn(x, att, rec, wts, tm_merge, tm_moe, ch):
    b, s, d = x.shape
    x1, gsel, gate, _, cnt = _merge(x, att, rec, wts, tm_merge)
    n = b * s
    nt = n // tm_moe
    f = tm_moe // tm_merge
    gsel = gsel.reshape(nt, 1, tm_moe)
    gate = gate.reshape(nt, f, EXPERTS_PER_GROUP, tm_merge).transpose(0, 2, 1, 3).reshape(nt, EXPERTS_PER_GROUP, tm_moe)
    per_expert = cnt[:, :, 0].reshape(nt, f, N_GROUPS, EXPERTS_PER_GROUP)
    counts = (per_expert.sum(axis=(1, 3)) / TOP_K).astype(I32)
    nch = ((counts + (ch - 1)) // ch).reshape(-1)
    y = _moe(x1.reshape(n, d), gsel, gate, nch, wts, tm_moe, ch)
    return y.reshape(b, s, d)


def _swap_halves(w):
    half = QK_ROPE // 2
    return jnp.concatenate([w[..., half:], w[..., :half]], axis=-1)


def _prep_weights(w_in, q_norm_g, w_uq, kv_norm_g, w_uk, w_uv, conv_w, conv_b, w_rg, b_rg, w_ig, b_ig,
                  lru_lambda, att_out_g, rec_out_g, w_out, ln1_g, ln1_b, w_group, b_group, w_expert,
                  b_expert, w_gate_up, w_down, ln2_g, ln2_b):
    d = w_in.shape[0]
    o1, o2, o3, o4 = Q_LORA, Q_LORA + KV_LORA, Q_LORA + KV_LORA + QK_ROPE, Q_LORA + KV_LORA + QK_ROPE + REC_WIDTH
    w_kpe = w_in[:, o2:o3]
    pad_lo = jnp.zeros((d, QK_NOPE), F32)
    pad_hi = jnp.zeros((d, HEAD_PAD - QK_NOPE - QK_ROPE), F32)
    w_in_ext = jnp.concatenate([
        w_in[:, :o2],
        pad_lo, w_kpe, pad_hi,
        pad_lo, _swap_halves(w_kpe), pad_hi,
        w_in[:, o3:o4], w_in[:, o4:],
    ], axis=1).astype(BF16)

    nope, pe = w_uq[..., :QK_NOPE], w_uq[..., QK_NOPE:]
    zq = lambda n: jnp.zeros((Q_LORA, N_HEADS, n), F32)
    q_main = jnp.concatenate([nope, pe, zq(HEAD_PAD - QK_NOPE - QK_ROPE)], axis=-1)
    q_swap = jnp.concatenate([zq(QK_NOPE), _swap_halves(pe), zq(HEAD_PAD - QK_NOPE - QK_ROPE)], axis=-1)
    w_uq_t = jnp.concatenate([q_main.reshape(Q_LORA, -1), q_swap.reshape(Q_LORA, -1)], axis=1).T.astype(BF16)

    k_pad = jnp.concatenate([w_uk, jnp.zeros((KV_LORA, N_HEADS, HEAD_PAD - QK_NOPE), F32)], axis=-1)
    w_uk_pad = k_pad.reshape(KV_LORA, -1).astype(BF16)
    w_uv_t = w_uv.reshape(KV_LORA, -1).T.astype(BF16)

    w_uk_t = jnp.concatenate([w_uk.transpose(1, 2, 0),
                              jnp.zeros((N_HEADS, HEAD_PAD - QK_NOPE, KV_LORA), F32)], axis=1).astype(BF16)
    sel = jnp.zeros((HEAD_PAD, QK_ROPE), F32).at[QK_NOPE + jnp.arange(QK_ROPE), jnp.arange(QK_ROPE)].set(1.0)
    w_uv_h = w_uv.transpose(1, 0, 2)
    zero_v = jnp.zeros((KV_LORA, V_HEAD), F32)
    w_uv_pair = jnp.stack([
        jnp.concatenate([jnp.concatenate([w_uv_h[2 * p], zero_v], axis=1),
                         jnp.concatenate([zero_v, w_uv_h[2 * p + 1]], axis=1)], axis=0)
        for p in range(N_PAIRS)]).astype(BF16)

    def block_diag(w):
        eye = jnp.eye(REC_BLOCKS, dtype=F32)
        return jnp.einsum('nde,nm->ndme', w, eye).reshape(REC_WIDTH, REC_WIDTH)

    bd_r, bd_i = block_diag(w_rg), block_diag(w_ig)
    half = REC_WIDTH // 2
    w_gate = jnp.stack([
        jnp.concatenate([bd_r[j * half:(j + 1) * half, j * half:(j + 1) * half],
                         bd_i[j * half:(j + 1) * half, j * half:(j + 1) * half]], axis=1)
        for j in range(2)]).astype(BF16)

    w_router_t = jnp.concatenate([w_group.T, jnp.zeros((SUBLANE - N_GROUPS, d), F32), w_expert.T], axis=0)
    row = lambda v: v.reshape(1, -1)
    return {
        "w_in": w_in_ext, "q_norm_g": row(q_norm_g), "w_uq_t": w_uq_t, "kv_norm_g": row(kv_norm_g),
        "w_uk": w_uk_pad, "w_uv_t": w_uv_t, "w_uk_t": w_uk_t, "w_pe_sel": sel.astype(BF16), "w_uv_pair": w_uv_pair,
        "conv_w": conv_w, "conv_b": row(conv_b), "w_gate": w_gate, "b_rg": row(b_rg), "b_ig": row(b_ig),
        "lru_lambda": row(lru_lambda), "att_out_g": row(att_out_g), "rec_out_g": row(rec_out_g),
        "w_out": w_out.astype(BF16), "ln1_g": row(ln1_g), "ln1_b": row(ln1_b),
        "w_router_t": w_router_t, "b_group": b_group.reshape(-1, 1), "b_expert": b_expert.reshape(-1, 1),
        "w_gate_up": w_gate_up.astype(BF16).reshape(N_GROUPS, EXPERTS_PER_GROUP, d, 2 * D_EXPERT),
        "w_down": w_down.astype(BF16).reshape(N_GROUPS, EXPERTS_PER_GROUP, D_EXPERT, d),
        "ln2_g": row(ln2_g), "ln2_b": row(ln2_b),
    }


def _rope_tables(pos):
    half = QK_ROPE // 2
    inv = ROPE_THETA ** (-(jnp.arange(half, dtype=F32) * 2.0 / QK_ROPE))
    ang = pos.astype(F32)[:, None] * inv[None, :]
    cos, sin = jnp.cos(ang), jnp.sin(ang)
    t = pos.shape[0]
    cos_t = jnp.concatenate([jnp.ones((t, QK_NOPE), F32), cos, cos,
                             jnp.zeros((t, HEAD_PAD - QK_NOPE - QK_ROPE), F32)], axis=1)
    sin_t = jnp.concatenate([jnp.zeros((t, QK_NOPE), F32), -sin, sin,
                             jnp.zeros((t, HEAD_PAD - QK_NOPE - QK_ROPE), F32)], axis=1)
    return cos_t, sin_t, cos_t.T, sin_t.T


def kernel(x_prompt, x_sample, cache_kv_latent, cache_k_rope, state_conv, state_rec, page_table,
           w_in, q_norm_g, w_uq, kv_norm_g, w_uk, w_uv, conv_w, conv_b, w_rg, b_rg, w_ig, b_ig,
           lru_lambda, att_out_g, rec_out_g, w_out, ln1_g, ln1_b, w_group, b_group, w_expert,
           b_expert, w_gate_up, w_down, ln2_g, ln2_b):
    wts = _prep_weights(w_in, q_norm_g, w_uq, kv_norm_g, w_uk, w_uv, conv_w, conv_b, w_rg, b_rg, w_ig, b_ig,
                        lru_lambda, att_out_g, rec_out_g, w_out, ln1_g, ln1_b, w_group, b_group, w_expert,
                        b_expert, w_gate_up, w_down, ln2_g, ln2_b)
    bp, sp, d = x_prompt.shape
    bd, td, _ = x_sample.shape
    assert td == 1, "the sample path handles one new token per sequence"
    past_len = page_table.shape[1] * PAGE_SIZE
    ko = QK_NOPE

    qt, k, vt, c_p, kpe_blk, rx, rg = _project(x_prompt, _rope_tables(jnp.arange(sp, dtype=I32)), wts,
                                               min(T_ATT, sp))
    xs = x_sample.reshape(1, bd, d)
    qt_s, _, _, c_s, kpe_s_blk, rx_s, rg_s = _project(xs, _rope_tables(jnp.full((bd,), past_len, I32)), wts, bd)
    kpe_s = kpe_s_blk[0, :, ko:ko + QK_ROPE]
    qlat, qpe = _absorb(qt_s[0, :, 0], wts)

    att_p, o_lat = _attention(qt, k, vt, page_table, qlat.transpose(1, 0, 2), qpe.transpose(1, 0, 2),
                              c_s.reshape(bd, 1, KV_LORA), kpe_s.reshape(bd, 1, QK_ROPE),
                              cache_kv_latent, cache_k_rope.transpose(0, 2, 1))

    rec_p, h_p = _lru_prompt(rx, rg, wts, min(TM_LRU, sp))
    tm_moe = min(TM_MOE, bp * sp)
    y_p = _merge_and_routed_ffn(x_prompt, att_p, rec_p, wts, min(TM_MERGE, sp, tm_moe), min(TM_ROUTED, tm_moe),
                                CH_MOE)
    kpe_p = kpe_blk[..., ko:ko + QK_ROPE]
    conv_p = rx[:, sp - (CONV_W - 1):, :]

    att_s = _value_up(o_lat.reshape(bd, N_HEADS * KV_LORA), wts)
    rec_s, conv_s_t, h_s = _lru_step(rx_s[0], rg_s[0], state_conv.transpose(1, 0, 2), state_rec, wts)
    y_s = _merge_and_ffn(xs, att_s, rec_s.reshape(1, bd, REC_WIDTH), wts, bd, bd, min(CH_MOE, bd))

    return (y_p, y_s.reshape(bd, 1, d), c_p, kpe_p, conv_p, h_p.reshape(bp, REC_WIDTH),
            c_s.reshape(bd, 1, KV_LORA), kpe_s.reshape(bd, 1, QK_ROPE), conv_s_t.transpose(1, 0, 2), h_s)
```

```python
import functools
import math

import jax
import jax.numpy as jnp
from jax import lax
from jax.experimental import pallas as pl
from jax.experimental.pallas import tpu as pltpu

F32 = jnp.float32
BF16 = jnp.bfloat16
I32 = jnp.int32

N_HEADS = 8
QK_NOPE = 64
QK_ROPE = 32
V_HEAD = 64
Q_LORA = 384
KV_LORA = 256
ROPE_THETA = 10000.0
SM_SCALE = (QK_NOPE + QK_ROPE) ** -0.5
REC_WIDTH = 512
REC_BLOCKS = 8
REC_BLOCK_W = REC_WIDTH // REC_BLOCKS
CONV_W = 4
LRU_C = 8.0
N_GROUPS = 4
EXPERTS_PER_GROUP = 8
N_EXPERTS = N_GROUPS * EXPERTS_PER_GROUP
TOP_K = 2
D_EXPERT = 256
DEPTH = 1
ALPHA = (2.0 * DEPTH) ** 0.25
LN_EPS = 1e-5
RMS_EPS = 1e-6
NEG_INF = -1e30
PAGE_SIZE = 128

LANE = 128
SUBLANE = 8
HEAD_PAD = LANE
N_PAIRS = N_HEADS * V_HEAD // LANE
VMEM_LIMIT = 56 * 1024 * 1024

T_ATT = 512
KEY_STRIP = 512
TM_LRU = 256
TM_MERGE = 512
TM_MOE = 1024
TM_ROUTED = 512
CH_EXPERT = 512
CH_MOE = 256
ROW_UNIT = 16
PAGES_PER_CHUNK = 32
CACHE_SLOTS = 3
EXP2_SCALE = SM_SCALE * math.log2(math.e)


def _dot(a, b):
    return jnp.dot(a, b, preferred_element_type=F32)


def _dot_nt(a, b, precision=None):
    return lax.dot_general(a, b, (((1,), (1,)), ((), ())), preferred_element_type=F32, precision=precision)


def _dot_tn(a, b):
    return lax.dot_general(a, b, (((0,), (0,)), ((), ())), preferred_element_type=F32)


def _cparams(semantics, flags=None):
    return pltpu.CompilerParams(dimension_semantics=semantics, vmem_limit_bytes=VMEM_LIMIT, flags=flags)


def _rmsnorm(x, g):
    return x * lax.rsqrt(jnp.mean(x * x, axis=-1, keepdims=True) + RMS_EPS) * g


def _layernorm(x, g, b):
    mu = jnp.mean(x, axis=-1, keepdims=True)
    xc = x - mu
    var = jnp.mean(xc * xc, axis=-1, keepdims=True)
    return xc * lax.rsqrt(var + LN_EPS) * g + b


def _gelu_tanh(x):
    return x * (0.5 * (1.0 + jnp.tanh(math.sqrt(2.0 / math.pi) * (x + 0.044715 * (x * x * x)))))


def _lru_coeffs(xc, pre_r, pre_i, b_r, b_i, lam):
    r = jax.nn.sigmoid(pre_r + b_r)
    i = jax.nn.sigmoid(pre_i + b_i)
    neg_lam = -lam
    softplus = jnp.maximum(neg_lam, 0.0) + jnp.log1p(jnp.exp(-jnp.abs(neg_lam)))
    log_a = (-LRU_C * softplus) * r
    a = jnp.exp(log_a)
    u = jnp.sqrt(-jnp.tanh(log_a) * (a * a + 1.0)) * (i * xc)
    return a, u


def _gate_preacts(xc, wg_ref):
    half = REC_WIDTH // 2
    g0 = _dot(xc[:, :half].astype(BF16), wg_ref[0])
    g1 = _dot(xc[:, half:].astype(BF16), wg_ref[1])
    pre_r = jnp.concatenate([g0[:, :half], g1[:, :half]], axis=1)
    pre_i = jnp.concatenate([g0[:, half:], g1[:, half:]], axis=1)
    return pre_r, pre_i


def _proj_kernel(x_ref, cos_ref, sin_ref, cost_ref, sint_ref, win_ref, qg_ref, wuqt_ref, kvg_ref, wuk_ref, wuvt_ref,
                 qt_ref, k_ref, vt_ref, ckv_ref, kpe_ref, rx_ref, rg_ref):
    x = x_ref[0].astype(BF16)
    z = _dot(x, win_ref[...])
    o_kv = Q_LORA
    o_ka = o_kv + KV_LORA
    o_kb = o_ka + LANE
    o_rx = o_kb + LANE
    o_rg = o_rx + REC_WIDTH
    qn = _rmsnorm(z[:, :o_kv], qg_ref[...]).astype(BF16)
    qq = _dot_nt(wuqt_ref[...], qn)
    cos_t = cost_ref[...]
    sin_t = sint_ref[...]
    sw = N_HEADS * HEAD_PAD
    for h in range(N_HEADS):
        lo = h * HEAD_PAD
        qt_ref[0, h, 0] = (qq[lo:lo + HEAD_PAD] * cos_t + qq[sw + lo:sw + lo + HEAD_PAD] * sin_t).astype(BF16)
    ckv = _rmsnorm(z[:, o_kv:o_ka], kvg_ref[...])
    ckv_ref[0] = ckv
    kpe = z[:, o_ka:o_kb] * cos_ref[...] + z[:, o_kb:o_rx] * sin_ref[...]
    kpe_ref[0] = kpe
    ckv_b = ckv.astype(BF16)
    kn = _dot(ckv_b, wuk_ref[...])
    for h in range(N_HEADS):
        lo = h * HEAD_PAD
        k_ref[0, h] = (kn[:, lo:lo + HEAD_PAD] + kpe).astype(BF16)
    vt = _dot_nt(wuvt_ref[...], ckv_b)
    for p in range(N_PAIRS):
        vt_ref[0, p, 0] = vt[p * LANE:(p + 1) * LANE].astype(BF16)
    rx_ref[0] = z[:, o_rx:o_rg]
    rg_ref[0] = z[:, o_rg:o_rg + REC_WIDTH]


def _project(x, tables, wts, tm):
    b, s, d = x.shape
    nt = s // tm
    in_w = wts["w_in"].shape[1]
    full = lambda shape: pl.BlockSpec(shape, lambda bi, si: (0,) * len(shape))
    cos_n, sin_n, cos_t, sin_t = tables
    return pl.pallas_call(
        _proj_kernel,
        grid=(b, nt),
        in_specs=[
            pl.BlockSpec((1, tm, d), lambda bi, si: (bi, si, 0)),
            pl.BlockSpec((tm, LANE), lambda bi, si: (si, 0)),
            pl.BlockSpec((tm, LANE), lambda bi, si: (si, 0)),
            pl.BlockSpec((HEAD_PAD, tm), lambda bi, si: (0, si)),
            pl.BlockSpec((HEAD_PAD, tm), lambda bi, si: (0, si)),
            full((d, in_w)),
            full((1, Q_LORA)),
            full((2 * N_HEADS * HEAD_PAD, Q_LORA)),
            full((1, KV_LORA)),
            full((KV_LORA, N_HEADS * HEAD_PAD)),
            full((N_PAIRS * LANE, KV_LORA)),
        ],
        out_specs=[
            pl.BlockSpec((1, N_HEADS, 1, HEAD_PAD, tm), lambda bi, si: (bi, 0, si, 0, 0)),
            pl.BlockSpec((1, N_HEADS, tm, HEAD_PAD), lambda bi, si: (bi, 0, si, 0)),
            pl.BlockSpec((1, N_PAIRS, 1, LANE, tm), lambda bi, si: (bi, 0, si, 0, 0)),
            pl.BlockSpec((1, tm, KV_LORA), lambda bi, si: (bi, si, 0)),
            pl.BlockSpec((1, tm, LANE), lambda bi, si: (bi, si, 0)),
            pl.BlockSpec((1, tm, REC_WIDTH), lambda bi, si: (bi, si, 0)),
            pl.BlockSpec((1, tm, REC_WIDTH), lambda bi, si: (bi, si, 0)),
        ],
        out_shape=[
            jax.ShapeDtypeStruct((b, N_HEADS, nt, HEAD_PAD, tm), BF16),
            jax.ShapeDtypeStruct((b, N_HEADS, s, HEAD_PAD), BF16),
            jax.ShapeDtypeStruct((b, N_PAIRS, nt, LANE, tm), BF16),
            jax.ShapeDtypeStruct((b, s, KV_LORA), F32),
            jax.ShapeDtypeStruct((b, s, LANE), F32),
            jax.ShapeDtypeStruct((b, s, REC_WIDTH), F32),
            jax.ShapeDtypeStruct((b, s, REC_WIDTH), F32),
        ],
        compiler_params=_cparams(("parallel", "parallel")),
        name="proj",
    )(x, cos_n, sin_n, cos_t, sin_t, wts["w_in"], wts["q_norm_g"], wts["w_uq_t"], wts["kv_norm_g"],
      wts["w_uk"], wts["w_uv_t"])


def _lru_prompt_kernel(rx_ref, rg_ref, cw_ref, cb_ref, wg_ref, br_ref, bi_ref, lam_ref,
                       y_ref, hlast_ref, xp_ref, h_ref):
    tm = rx_ref.shape[1]
    si = pl.program_id(1)

    @pl.when(si == 0)
    def _():
        xp_ref[0:SUBLANE, :] = jnp.zeros((SUBLANE, REC_WIDTH), F32)
        h_ref[...] = jnp.zeros_like(h_ref)

    x = rx_ref[0]
    xp_ref[SUBLANE:SUBLANE + tm, :] = x
    xc = cb_ref[...] + x * cw_ref[CONV_W - 1:CONV_W, :]
    for m in range(1, CONV_W):
        xc = xc + xp_ref[pl.ds(SUBLANE - m, tm), :] * cw_ref[CONV_W - 1 - m:CONV_W - m, :]
    xp_ref[0:SUBLANE, :] = x[tm - SUBLANE:, :]

    pre_r, pre_i = _gate_preacts(xc, wg_ref)
    a, u = _lru_coeffs(xc, pre_r, pre_i, br_ref[...], bi_ref[...], lam_ref[...])

    row = lax.broadcasted_iota(I32, (tm, REC_WIDTH), 0)
    d = 1
    while d < tm:
        keep = row >= d
        a_sh = jnp.where(keep, pltpu.roll(a, d, 0), 1.0)
        u_sh = jnp.where(keep, pltpu.roll(u, d, 0), 0.0)
        u = u + a * u_sh
        a = a * a_sh
        d *= 2
    h = a * h_ref[...] + u
    h_ref[...] = h[tm - 1:tm, :]
    hlast_ref[0] = h[tm - 1:tm, :]
    y_ref[0] = h * _gelu_tanh(rg_ref[0])


def _lru_prompt(rx, rg, wts, tm):
    b, s, w = rx.shape
    full = lambda shape: pl.BlockSpec(shape, lambda bi, si: (0,) * len(shape))
    return pl.pallas_call(
        _lru_prompt_kernel,
        grid=(b, s // tm),
        in_specs=[
            pl.BlockSpec((1, tm, w), lambda bi, si: (bi, si, 0)),
            pl.BlockSpec((1, tm, w), lambda bi, si: (bi, si, 0)),
            full((CONV_W, w)), full((1, w)), full((2, w // 2, w)), full((1, w)), full((1, w)), full((1, w)),
        ],
        out_specs=[
            pl.BlockSpec((1, tm, w), lambda bi, si: (bi, si, 0)),
            pl.BlockSpec((1, 1, w), lambda bi, si: (bi, 0, 0)),
        ],
        out_shape=[jax.ShapeDtypeStruct((b, s, w), F32), jax.ShapeDtypeStruct((b, 1, w), F32)],
        scratch_shapes=[pltpu.VMEM((tm + SUBLANE, w), F32), pltpu.VMEM((1, w), F32)],
        compiler_params=_cparams(("arbitrary", "arbitrary")),
        name="lru_prompt",
    )(rx, rg, wts["conv_w"], wts["conv_b"], wts["w_gate"], wts["b_rg"], wts["b_ig"], wts["lru_lambda"])


def _lru_step_kernel(rx_ref, rg_ref, conv_ref, h0_ref, cw_ref, cb_ref, wg_ref, br_ref, bi_ref, lam_ref,
                     y_ref, newconv_ref, h_ref):
    x = rx_ref[...]
    xc = cb_ref[...] + x * cw_ref[CONV_W - 1:CONV_W, :]
    for k in range(CONV_W - 1):
        xc = xc + conv_ref[k] * cw_ref[k:k + 1, :]
    pre_r, pre_i = _gate_preacts(xc, wg_ref)
    a, u = _lru_coeffs(xc, pre_r, pre_i, br_ref[...], bi_ref[...], lam_ref[...])
    h = a * h0_ref[...] + u
    h_ref[...] = h
    y_ref[...] = h * _gelu_tanh(rg_ref[...])
    for k in range(CONV_W - 2):
        newconv_ref[k] = conv_ref[k + 1]
    newconv_ref[CONV_W - 2] = x


def _lru_step(rx, rg, conv_t, h0, wts):
    n, w = rx.shape
    return pl.pallas_call(
        _lru_step_kernel,
        out_shape=[jax.ShapeDtypeStruct((n, w), F32),
                   jax.ShapeDtypeStruct((CONV_W - 1, n, w), F32),
                   jax.ShapeDtypeStruct((n, w), F32)],
        name="lru_step",
    )(rx, rg, conv_t, h0, wts["conv_w"], wts["conv_b"], wts["w_gate"], wts["b_rg"], wts["b_ig"], wts["lru_lambda"])


def _fused_attn_kernel(pt_ref, qt_ref, k_ref, vt_ref, qlat_ref, qpe_ref, cnew_ref, knew_ref, ckv_hbm, kr_hbm,
                       o_ref, olat_ref, cbuf, kbuf, sem, ms_ref, ls_ref, accs_ref,
                       *, n_batch, n_q, n_chunks, total_chunks):
    t = qt_ref.shape[-1]
    bi, pi, qi = pl.program_id(0), pl.program_id(1), pl.program_id(2)
    steps_per_group = n_q * (n_q + 1) // 2
    total_steps = n_batch * N_PAIRS * steps_per_group
    base = (bi * N_PAIRS + pi) * steps_per_group + (qi * (qi + 1)) // 2
    ks = KEY_STRIP
    key_i = lax.broadcasted_iota(I32, (ks, t), 0)
    qry_i = lax.broadcasted_iota(I32, (ks, t), 1)
    qts = [qt_ref[0, e, 0] for e in range(2)]

    cp = PAGES_PER_CHUNK
    ahead = CACHE_SLOTS - 1

    def copies(g, slot):
        out = []
        for p in range(cp):
            page = pt_ref[g * cp + p]
            rows = pl.ds(p * PAGE_SIZE, PAGE_SIZE)
            out.append(pltpu.make_async_copy(ckv_hbm.at[page], cbuf.at[slot, rows], sem.at[0, slot]))
            out.append(pltpu.make_async_copy(kr_hbm.at[page], kbuf.at[slot, :, rows], sem.at[1, slot]))
        return out

    def start(g, slot):
        for c in copies(g, slot):
            c.start()

    @pl.when(jnp.logical_and(jnp.logical_and(bi == 0, pi == 0), qi == 0))
    def _():
        for g0 in range(min(ahead, total_chunks)):
            start(g0, g0)

    def sample_chunk_stages(g):
        v = {}

        def fetch_and_score():
            @pl.when(g + ahead < total_chunks)
            def _():
                start(g + ahead, lax.rem(g + ahead, CACHE_SLOTS))

            slot = lax.rem(g, CACHE_SLOTS)
            for c in copies(g, slot):
                c.wait()
            v["b"] = lax.div(g, n_chunks)
            v["qlat"] = qlat_ref[v["b"]]
            v["qpe"] = qpe_ref[v["b"]]
            v["cb"] = cbuf[slot].astype(BF16)
            kb = kbuf[slot].astype(BF16)
            v["s"] = _dot_nt(v["qlat"].astype(BF16), v["cb"]) + _dot(v["qpe"].astype(BF16), kb)

        def softmax_and_values():
            first = lax.rem(g, n_chunks) == 0
            m = jnp.where(first, NEG_INF, ms_ref[...])
            l = jnp.where(first, 0.0, ls_ref[...])
            acc = jnp.where(first, 0.0, accs_ref[...])
            s = v["s"]
            m_new = jnp.maximum(m, jnp.max(s, axis=-1, keepdims=True))
            alpha = jnp.exp2((m - m_new) * EXP2_SCALE)
            p = jnp.exp2((s - m_new) * EXP2_SCALE)
            v["l"] = alpha * l + jnp.sum(p, axis=-1, keepdims=True)
            v["acc"] = alpha * acc + _dot(p.astype(BF16), v["cb"])
            v["m"] = m_new

        def finish():
            m_new, l, acc, b = v["m"], v["l"], v["acc"], v["b"]
            ms_ref[...] = m_new
            ls_ref[...] = l
            accs_ref[...] = acc
            cnew = cnew_ref[b]
            knew = knew_ref[b]
            s_new = (jnp.sum(v["qlat"] * cnew, axis=-1, keepdims=True)
                     + jnp.sum(v["qpe"] * knew, axis=-1, keepdims=True))
            m_fin = jnp.maximum(m_new, s_new)
            a_fin = jnp.exp2((m_new - m_fin) * EXP2_SCALE)
            p_new = jnp.exp2((s_new - m_fin) * EXP2_SCALE)
            olat_ref[b] = (a_fin * acc + p_new * cnew) / (a_fin * l + p_new)

        return [fetch_and_score, softmax_and_values, finish]

    def scores(j, r, e):
        kb = k_ref[0, e, pl.ds(pl.multiple_of(j * t + r * ks, ks), ks), :]
        return _dot(kb, qts[e])

    def step(j, carry, diagonal, with_chunk):
        stages = sample_chunk_stages(base + j) if with_chunk else []
        carry = list(carry)
        units = [(r, e) for r in range(t // ks) for e in range(2)]
        st_next = scores(j, *units[0])
        for u, (r, e) in enumerate(units):
            st = st_next
            if u + 1 < len(units):
                st_next = scores(j, *units[u + 1])
            if stages:
                stages.pop(0)()
            m, l, acc = carry[e]
            vb = vt_ref[0, 0, j, e * V_HEAD:(e + 1) * V_HEAD, r * ks:(r + 1) * ks]
            if diagonal:
                st = jnp.where(key_i + r * ks <= qry_i, st, NEG_INF)
            m_new = jnp.maximum(m, jnp.max(st, axis=0, keepdims=True))
            alpha = jnp.exp2((m - m_new) * EXP2_SCALE)
            pt = jnp.exp2((st - m_new) * EXP2_SCALE)
            l = alpha * l + jnp.sum(pt, axis=0, keepdims=True)
            acc = alpha * acc + _dot(vb, pt.astype(BF16))
            carry[e] = (m_new, l, acc)
        for stage in stages:
            stage()
        return tuple(carry)

    init = (jnp.full((1, t), NEG_INF, F32), jnp.zeros((1, t), F32), jnp.zeros((V_HEAD, t), F32))
    n_with = jnp.clip(total_chunks - base, 0, qi)
    carry = lax.fori_loop(0, n_with, functools.partial(step, diagonal=False, with_chunk=True), (init, init))
    carry = lax.fori_loop(n_with, qi, functools.partial(step, diagonal=False, with_chunk=False), carry)
    carry = lax.cond(base + qi < total_chunks,
                     functools.partial(step, qi, diagonal=True, with_chunk=True),
                     functools.partial(step, qi, diagonal=True, with_chunk=False), carry)
    halves = [acc / l for (_, l, acc) in carry]
    o_ref[0, 0] = jnp.concatenate(halves, axis=0).T

    if total_chunks > total_steps:
        @pl.when(jnp.logical_and(jnp.logical_and(bi == n_batch - 1, pi == N_PAIRS - 1), qi == n_q - 1))
        def _():
            def drain(g, c):
                for stage in sample_chunk_stages(g):
                    stage()
                return c
            lax.fori_loop(total_steps, total_chunks, drain, 0)


def _attention(qt, k, vt, page_table, qlat, qpe, c_new, k_new, cache_kv, cache_kr_t):
    b, _, nt, _, t = qt.shape
    s = nt * t
    bd, n_pages = page_table.shape
    n_chunks = n_pages // PAGES_PER_CHUNK
    rows = PAGES_PER_CHUNK * PAGE_SIZE
    kern = functools.partial(_fused_attn_kernel, n_batch=b, n_q=nt, n_chunks=n_chunks, total_chunks=bd * n_chunks)
    whole = lambda shape: pl.BlockSpec(shape, lambda bi, pi, qi, pt: (0,) * len(shape))
    grid_spec = pltpu.PrefetchScalarGridSpec(
        num_scalar_prefetch=1,
        grid=(b, N_PAIRS, nt),
        in_specs=[
            pl.BlockSpec((1, 2, 1, HEAD_PAD, t), lambda bi, pi, qi, pt: (bi, pi, qi, 0, 0)),
            pl.BlockSpec((1, 2, s, HEAD_PAD), lambda bi, pi, qi, pt: (bi, pi, 0, 0)),
            pl.BlockSpec((1, 1, nt, LANE, t), lambda bi, pi, qi, pt: (bi, pi, 0, 0, 0)),
            whole((bd, N_HEADS, KV_LORA)),
            whole((bd, N_HEADS, QK_ROPE)),
            whole((bd, 1, KV_LORA)),
            whole((bd, 1, QK_ROPE)),
            pl.BlockSpec(memory_space=pl.ANY),
            pl.BlockSpec(memory_space=pl.ANY),
        ],
        out_specs=[
            pl.BlockSpec((1, 1, t, LANE), lambda bi, pi, qi, pt: (bi, pi, qi, 0)),
            whole((bd, N_HEADS, KV_LORA)),
        ],
        scratch_shapes=[
            pltpu.VMEM((CACHE_SLOTS, rows, KV_LORA), F32),
            pltpu.VMEM((CACHE_SLOTS, QK_ROPE, rows), F32),
            pltpu.SemaphoreType.DMA((2, CACHE_SLOTS)),
            pltpu.VMEM((N_HEADS, 1), F32),
            pltpu.VMEM((N_HEADS, 1), F32),
            pltpu.VMEM((N_HEADS, KV_LORA), F32),
        ],
    )
    return pl.pallas_call(
        kern,
        grid_spec=grid_spec,
        out_shape=[jax.ShapeDtypeStruct((b, N_PAIRS, s, LANE), F32),
                   jax.ShapeDtypeStruct((bd, N_HEADS, KV_LORA), F32)],
        compiler_params=_cparams(("arbitrary", "arbitrary", "arbitrary")),
        name="attn_fused",
    )(page_table.reshape(-1), qt, k, vt, qlat, qpe, c_new, k_new, cache_kv, cache_kr_t)


def _absorb_kernel(qt_ref, wlat_ref, wpe_ref, qlat_ref, qpe_ref):
    for h in range(N_HEADS):
        qt = qt_ref[h]
        qlat_ref[h] = _dot_tn(qt, wlat_ref[h])
        qpe_ref[h] = _dot_tn(qt, wpe_ref[...])


def _absorb(qt, wts):
    _, _, n = qt.shape
    return pl.pallas_call(
        _absorb_kernel,
        out_shape=[jax.ShapeDtypeStruct((N_HEADS, n, KV_LORA), F32),
                   jax.ShapeDtypeStruct((N_HEADS, n, QK_ROPE), F32)],
        name="absorb_q",
    )(qt, wts["w_uk_t"], wts["w_pe_sel"])


def _value_up_kernel(olat_ref, wv_ref, o_ref):
    w = 2 * KV_LORA
    for p in range(N_PAIRS):
        o_ref[0, p] = _dot(olat_ref[:, p * w:(p + 1) * w].astype(BF16), wv_ref[p])


def _value_up(olat2d, wts):
    n = olat2d.shape[0]
    return pl.pallas_call(
        _value_up_kernel,
        out_shape=jax.ShapeDtypeStruct((1, N_PAIRS, n, LANE), F32),
        name="value_up",
    )(olat2d, wts["w_uv_pair"])


def _merge_kernel(x_ref, att_ref, rec_ref, ag_ref, rgn_ref, wout_ref, g1_ref, b1_ref, wr_ref, bg_ref, be_ref,
                  x1_ref, gsel_ref, gate_ref, sel_ref, cnt_ref):
    tm = x_ref.shape[1]
    att = [att_ref[0, p] for p in range(N_PAIRS)]
    ss = att[0] * att[0]
    for p in range(1, N_PAIRS):
        ss = ss + att[p] * att[p]
    inv = lax.rsqrt(jnp.sum(ss, axis=-1, keepdims=True) / (N_PAIRS * LANE) + RMS_EPS)
    parts = [(att[p] * inv * ag_ref[:, p * LANE:(p + 1) * LANE]).astype(BF16) for p in range(N_PAIRS)]
    parts.append(_rmsnorm(rec_ref[0], rgn_ref[...]).astype(BF16))
    mixed = jnp.concatenate(parts, axis=-1)
    mix = _dot(mixed, wout_ref[...])
    x1 = _layernorm(ALPHA * x_ref[0] + mix, g1_ref[...], b1_ref[...])
    x1_ref[0] = x1

    n_r = wr_ref.shape[0] // 2
    x_hi = x1.astype(BF16)
    x_lo = (x1 - x_hi.astype(F32)).astype(BF16)
    both = _dot_nt(wr_ref[...], x_hi)
    lt = (both[:n_r] + both[n_r:]) + _dot_nt(wr_ref[0:n_r, :], x_lo)
    g = [lt[k:k + 1, :] for k in range(N_GROUPS)]
    gmax = functools.reduce(jnp.maximum, g)
    ex = [jnp.exp(gk - gmax) for gk in g]
    den = functools.reduce(lambda p, q: p + q, ex)
    best = g[0] + bg_ref[0:1, :]
    idx = jnp.zeros((1, tm), I32)
    for k in range(1, N_GROUPS):
        cand = g[k] + bg_ref[k:k + 1, :]
        upd = cand > best
        idx = jnp.where(upd, k, idx)
        best = jnp.where(upd, cand, best)
    gp = ex[0]
    e_sel = lt[SUBLANE:SUBLANE + EXPERTS_PER_GROUP, :]
    e_bias = jnp.broadcast_to(be_ref[0:EXPERTS_PER_GROUP, :], (EXPERTS_PER_GROUP, tm))
    for k in range(1, N_GROUPS):
        hit = idx == k
        lo = SUBLANE + k * EXPERTS_PER_GROUP
        gp = jnp.where(hit, ex[k], gp)
        e_sel = jnp.where(hit, lt[lo:lo + EXPERTS_PER_GROUP, :], e_sel)
        e_bias = jnp.where(hit, be_ref[k * EXPERTS_PER_GROUP:(k + 1) * EXPERTS_PER_GROUP, :], e_bias)
    g_prob = gp / den
    sc = e_sel + e_bias
    sub = lax.broadcasted_iota(I32, (EXPERTS_PER_GROUP, tm), 0)
    m1 = jnp.max(sc, axis=0, keepdims=True)
    i1 = jnp.min(jnp.where(sc == m1, sub, EXPERTS_PER_GROUP), axis=0, keepdims=True)
    mask1 = sub == i1
    sc2 = jnp.where(mask1, -jnp.inf, sc)
    m2 = jnp.max(sc2, axis=0, keepdims=True)
    i2 = jnp.min(jnp.where(sc2 == m2, sub, EXPERTS_PER_GROUP), axis=0, keepdims=True)
    mask2 = sub == i2
    v1 = jnp.sum(jnp.where(mask1, e_sel, 0.0), axis=0, keepdims=True)
    v2 = jnp.sum(jnp.where(mask2, e_sel, 0.0), axis=0, keepdims=True)
    vm = jnp.maximum(v1, v2)
    e1 = jnp.exp(v1 - vm)
    e2 = jnp.exp(v2 - vm)
    esum = e1 + e2
    gate = g_prob * (jnp.where(mask1, e1 / esum, 0.0) + jnp.where(mask2, e2 / esum, 0.0))
    gsel_ref[0] = idx
    gate_ref[0] = gate
    sel = jnp.where(jnp.logical_or(mask1, mask2), 1.0, 0.0)
    sel_ref[0] = sel
    for k in range(N_GROUPS):
        ck = jnp.sum(jnp.where(idx == k, sel, 0.0), axis=-1, keepdims=True)
        cnt_ref[0, k * EXPERTS_PER_GROUP:(k + 1) * EXPERTS_PER_GROUP, :] = jnp.broadcast_to(
            ck, (EXPERTS_PER_GROUP, LANE))


def _merge(x, att, rec, wts, tm):
    b, s, d = x.shape
    nt = s // tm
    mw = wts["w_out"].shape[0]
    n_r = wts["w_router_t"].shape[0]
    full = lambda shape: pl.BlockSpec(shape, lambda bi, si: (0,) * len(shape))
    return pl.pallas_call(
        _merge_kernel,
        grid=(b, nt),
        in_specs=[
            pl.BlockSpec((1, tm, d), lambda bi, si: (bi, si, 0)),
            pl.BlockSpec((1, N_PAIRS, tm, LANE), lambda bi, si: (bi, 0, si, 0)),
            pl.BlockSpec((1, tm, REC_WIDTH), lambda bi, si: (bi, si, 0)),
            full((1, N_PAIRS * LANE)), full((1, REC_WIDTH)), full((mw, d)), full((1, d)), full((1, d)),
            full((n_r, d)), full((N_GROUPS, 1)), full((N_GROUPS * EXPERTS_PER_GROUP, 1)),
        ],
        out_specs=[
            pl.BlockSpec((1, tm, d), lambda bi, si: (bi, si, 0)),
            pl.BlockSpec((1, 1, tm), lambda bi, si: (bi * nt + si, 0, 0)),
            pl.BlockSpec((1, EXPERTS_PER_GROUP, tm), lambda bi, si: (bi * nt + si, 0, 0)),
            pl.BlockSpec((1, EXPERTS_PER_GROUP, tm), lambda bi, si: (bi * nt + si, 0, 0)),
            pl.BlockSpec((1, N_EXPERTS, LANE), lambda bi, si: (bi * nt + si, 0, 0)),
        ],
        out_shape=[
            jax.ShapeDtypeStruct((b, s, d), F32),
            jax.ShapeDtypeStruct((b * nt, 1, tm), I32),
            jax.ShapeDtypeStruct((b * nt, EXPERTS_PER_GROUP, tm), F32),
            jax.ShapeDtypeStruct((b * nt, EXPERTS_PER_GROUP, tm), F32),
            jax.ShapeDtypeStruct((b * nt, N_EXPERTS, LANE), F32),
        ],
        compiler_params=_cparams(("parallel", "parallel")),
        name="merge_router",
    )(x, att, rec, wts["att_out_g"], wts["rec_out_g"], wts["w_out"], wts["ln1_g"], wts["ln1_b"],
      wts["w_router_t"], wts["b_group"], wts["b_expert"])


def _moe_kernel(nch_ref, x1_ref, gsel_ref, gate_ref, wgu_ref, wd_ref, g2_ref, b2_ref, o_ref, xb_ref, tri_ref,
                *, ch):
    tm = x1_ref.shape[0]
    ti = pl.program_id(0)
    gi = pl.program_id(1)

    @pl.when(jnp.logical_and(ti == 0, gi == 0))
    def _():
        r = lax.broadcasted_iota(I32, (tm, tm), 0)
        c = lax.broadcasted_iota(I32, (tm, tm), 1)
        tri_ref[...] = jnp.where(r < c, 1.0, 0.0).astype(BF16)

    @pl.when(gi == 0)
    def _():
        xb_ref[...] = x1_ref[...].astype(BF16)
        o_ref[...] = jnp.zeros_like(o_ref)

    in_group = gsel_ref[0] == gi
    member = jnp.broadcast_to(jnp.where(in_group, 1.0, 0.0), (SUBLANE, tm)).astype(BF16)
    before = _dot(member, tri_ref[...])
    rank = jnp.where(in_group, before[0:1, :].astype(I32), -1)
    gate = gate_ref[0]
    g_hi = gate.astype(BF16).astype(F32)
    g_mid = (gate - g_hi).astype(BF16).astype(F32)
    g_lo = (gate - g_hi) - g_mid
    n_terms = 3
    gate_terms = jnp.concatenate(
        [g_hi, g_mid, g_lo, jnp.zeros((LANE - n_terms * EXPERTS_PER_GROUP, tm), F32)], axis=0).astype(BF16)

    def chunk(c, carry):
        slot_id = lax.broadcasted_iota(I32, (ch, tm), 0) + c * ch
        onehot_b = jnp.where(slot_id == rank, 1.0, 0.0).astype(BF16)
        xg = _dot(onehot_b, xb_ref[...]).astype(BF16)
        gt = _dot_nt(onehot_b, gate_terms)
        gc = gt
        for k in range(1, n_terms):
            gc = gc + pltpu.roll(gt, LANE - k * EXPERTS_PER_GROUP, 1)
        acc = jnp.zeros((ch, o_ref.shape[1]), F32)
        for j in range(EXPERTS_PER_GROUP):
            gu = _dot(xg, wgu_ref[0, j])
            hid = jax.nn.silu(gu[:, :D_EXPERT]) * gu[:, D_EXPERT:]
            acc = acc + gc[:, j:j + 1] * _dot(hid.astype(BF16), wd_ref[0, j])
        o_ref[...] += _dot_tn(onehot_b, acc.astype(BF16))
        return carry

    lax.fori_loop(0, nch_ref[ti * N_GROUPS + gi], chunk, 0)

    @pl.when(gi == N_GROUPS - 1)
    def _():
        o_ref[...] = _layernorm(ALPHA * x1_ref[...] + o_ref[...], g2_ref[...], b2_ref[...])


def _moe(x1, gsel, gate, nch, wts, tm, ch):
    n, d = x1.shape
    nt = n // tm
    e2 = 2 * D_EXPERT
    grid_spec = pltpu.PrefetchScalarGridSpec(
        num_scalar_prefetch=1,
        grid=(nt, N_GROUPS),
        in_specs=[
            pl.BlockSpec((tm, d), lambda ti, gi, nc: (ti, 0)),
            pl.BlockSpec((1, 1, tm), lambda ti, gi, nc: (ti, 0, 0)),
            pl.BlockSpec((1, EXPERTS_PER_GROUP, tm), lambda ti, gi, nc: (ti, 0, 0)),
            pl.BlockSpec((1, EXPERTS_PER_GROUP, d, e2), lambda ti, gi, nc: (gi, 0, 0, 0)),
            pl.BlockSpec((1, EXPERTS_PER_GROUP, D_EXPERT, d), lambda ti, gi, nc: (gi, 0, 0, 0)),
            pl.BlockSpec((1, d), lambda ti, gi, nc: (0, 0)),
            pl.BlockSpec((1, d), lambda ti, gi, nc: (0, 0)),
        ],
        out_specs=pl.BlockSpec((tm, d), lambda ti, gi, nc: (ti, 0)),
        scratch_shapes=[pltpu.VMEM((tm, d), BF16), pltpu.VMEM((tm, tm), BF16)],
    )
    return pl.pallas_call(
        functools.partial(_moe_kernel, ch=ch),
        grid_spec=grid_spec,
        out_shape=jax.ShapeDtypeStruct((n, d), F32),
        compiler_params=_cparams(("arbitrary", "arbitrary")),
        name="moe",
    )(nch, x1, gsel, gate, wts["w_gate_up"], wts["w_down"], wts["ln2_g"], wts["ln2_b"])


def _build_tri(tri_ref):
    tm = tri_ref.shape[0]
    r = lax.broadcasted_iota(I32, (tm, tm), 0)
    c = lax.broadcasted_iota(I32, (tm, tm), 1)
    tri_ref[...] = jnp.where(r < c, 1.0, 0.0).astype(BF16)


def _sorted_positions(gsel, sel, start_col, tri_ref):
    member = jnp.concatenate([jnp.where(gsel == g, sel, 0.0) for g in range(N_GROUPS)], axis=0)
    before = _dot(member.astype(BF16), tri_ref[...])
    routed = member > 0.5
    pos = jnp.where(routed, start_col + before, -1.0)
    pos_a = jnp.max(pos, axis=0, keepdims=True)
    pos_b = jnp.sum(jnp.where(routed, start_col + before, 0.0), axis=0, keepdims=True) - pos_a
    return pos, pos_a.astype(I32), pos_b.astype(I32)


def _exact_terms(v):
    hi = v.astype(BF16).astype(F32)
    mid = (v - hi).astype(BF16).astype(F32)
    lo = (v - hi) - mid
    row = lax.broadcasted_iota(I32, (LANE, v.shape[1]), 0)
    return jnp.where(row == 0, hi, jnp.where(row == 1, mid, jnp.where(row == 2, lo, 0.0))).astype(BF16)


def _dispatch_kernel(udst_ref, nun_ref, nch_ref, tstart_ref, tunits_ref, rest_ref, x1_ref, gsel_ref, sel_ref,
                     gate_ref, start_ref, xs_hbm, gs_hbm, xb_ref, tri_ref, xsrt, gsrt, sem, *, rch, ech, umax):
    i = pl.program_id(0)
    tm = x1_ref.shape[0]

    @pl.when(i == 0)
    def _():
        _build_tri(tri_ref)
        unit = pl.ds(0, ROW_UNIT)
        xsrt[0, unit, :] = jnp.zeros((ROW_UNIT, xsrt.shape[2]), BF16)
        gsrt[0, unit, :] = jnp.zeros((ROW_UNIT, LANE), F32)

        def tail_copies(e, k):
            dst = pl.ds(pl.multiple_of(tstart_ref[e] + k * ROW_UNIT, ROW_UNIT), ROW_UNIT)
            return (pltpu.make_async_copy(xsrt.at[0, unit], xs_hbm.at[dst], sem.at[0, 0]),
                    pltpu.make_async_copy(gsrt.at[0, unit], gs_hbm.at[dst], sem.at[1, 0]))

        def per_expert(fn):
            def over_experts(e, carry):
                def over_units(k, c):
                    for cp in tail_copies(e, k):
                        fn(cp)
                    return c
                return lax.fori_loop(0, tunits_ref[e], over_units, carry)
            lax.fori_loop(0, N_EXPERTS, over_experts, 0)

        per_expert(lambda cp: cp.start())
        per_expert(lambda cp: cp.wait())

        blk = pl.ds(0, ech)
        xsrt[0, blk, :] = jnp.zeros((ech, xsrt.shape[2]), BF16)
        gsrt[0, blk, :] = jnp.zeros((ech, LANE), F32)

        def rest_copies(k):
            dst = pl.ds(pl.multiple_of(rest_ref[0] + k * ech, ech), ech)
            return (pltpu.make_async_copy(xsrt.at[0, blk], xs_hbm.at[dst], sem.at[0, 0]),
                    pltpu.make_async_copy(gsrt.at[0, blk], gs_hbm.at[dst], sem.at[1, 0]))

        def start_rest(k, c):
            for cp in rest_copies(k):
                cp.start()
            return c

        def wait_rest(k, c):
            for cp in rest_copies(k):
                cp.wait()
            return c

        lax.fori_loop(0, rest_ref[1], start_rest, 0)
        lax.fori_loop(0, rest_ref[1], wait_rest, 0)

    xb_ref[...] = x1_ref[...].astype(BF16)
    gsel = gsel_ref[0]
    pos, pos_a, pos_b = _sorted_positions(gsel, sel_ref[0], start_ref[0], tri_ref)
    gate32 = jnp.concatenate([jnp.where(gsel == g, gate_ref[0], 0.0) for g in range(N_GROUPS)], axis=0)
    gate_a = jnp.sum(jnp.where(pos == pos_a.astype(F32), gate32, 0.0), axis=0, keepdims=True)
    gate_b = jnp.sum(jnp.where(pos == pos_b.astype(F32), gate32, 0.0), axis=0, keepdims=True)
    terms_a = _exact_terms(gate_a)
    terms_b = _exact_terms(gate_b)
    slot = lax.rem(i, 2)

    def chunk(c, carry):
        r0 = pl.multiple_of(c * rch, rch)
        rid = lax.broadcasted_iota(I32, (rch, tm), 0) + r0
        hit_a = rid == pos_a
        hit_b = rid == pos_b
        onehot = jnp.where(jnp.logical_or(hit_a, hit_b), 1.0, 0.0).astype(BF16)
        xsrt[slot, pl.ds(r0, rch), :] = _dot(onehot, xb_ref[...]).astype(BF16)
        gt = (_dot_nt(jnp.where(hit_a, 1.0, 0.0).astype(BF16), terms_a)
              + _dot_nt(jnp.where(hit_b, 1.0, 0.0).astype(BF16), terms_b))
        gsrt[slot, pl.ds(r0, rch), :] = (gt + pltpu.roll(gt, LANE - 1, 1)) + pltpu.roll(gt, LANE - 2, 1)
        return carry

    lax.fori_loop(0, nch_ref[i], chunk, 0)

    def for_units(step, buf, fn):
        def body(u, carry):
            src = pl.ds(pl.multiple_of(u * ROW_UNIT, ROW_UNIT), ROW_UNIT)
            dst = pl.ds(pl.multiple_of(udst_ref[step * umax + u], ROW_UNIT), ROW_UNIT)
            fn(pltpu.make_async_copy(xsrt.at[buf, src], xs_hbm.at[dst], sem.at[0, buf]))
            fn(pltpu.make_async_copy(gsrt.at[buf, src], gs_hbm.at[dst], sem.at[1, buf]))
            return carry
        lax.fori_loop(0, nun_ref[step], body, 0)

    @pl.when(i > 0)
    def _():
        for_units(i - 1, 1 - slot, lambda cp: cp.wait())

    for_units(i, slot, lambda cp: cp.start())

    @pl.when(i == pl.num_programs(0) - 1)
    def _():
        for_units(i, slot, lambda cp: cp.wait())


def _dispatch(x1, gsel, sel, gate, start_col, tables, tails, total_rows, tm, rch, ech, umax):
    n, d = x1.shape
    nt = n // tm
    rows_max = umax * ROW_UNIT
    assert rows_max >= ech, "the sorted-tile buffer doubles as the zero source of one expert chunk"
    unit_dst, n_units, n_chunks = tables
    tail_start, tail_units, rest = tails
    grid_spec = pltpu.PrefetchScalarGridSpec(
        num_scalar_prefetch=6,
        grid=(nt,),
        in_specs=[
            pl.BlockSpec((tm, d), lambda i, *_: (i, 0)),
            pl.BlockSpec((1, 1, tm), lambda i, *_: (i, 0, 0)),
            pl.BlockSpec((1, EXPERTS_PER_GROUP, tm), lambda i, *_: (i, 0, 0)),
            pl.BlockSpec((1, EXPERTS_PER_GROUP, tm), lambda i, *_: (i, 0, 0)),
            pl.BlockSpec((1, N_EXPERTS, 1), lambda i, *_: (i, 0, 0)),
        ],
        out_specs=[pl.BlockSpec(memory_space=pl.ANY), pl.BlockSpec(memory_space=pl.ANY)],
        scratch_shapes=[
            pltpu.VMEM((tm, d), BF16),
            pltpu.VMEM((tm, tm), BF16),
            pltpu.VMEM((2, rows_max, d), BF16),
            pltpu.VMEM((2, rows_max, LANE), F32),
            pltpu.SemaphoreType.DMA((2, 2)),
        ],
    )
    return pl.pallas_call(
        functools.partial(_dispatch_kernel, rch=rch, ech=ech, umax=umax),
        grid_spec=grid_spec,
        out_shape=[jax.ShapeDtypeStruct((total_rows, d), BF16), jax.ShapeDtypeStruct((total_rows, LANE), F32)],
        compiler_params=_cparams(("arbitrary",)),
        name="moe_dispatch",
    )(unit_dst, n_units, n_chunks, tail_start, tail_units, rest, x1, gsel, sel, gate, start_col)


def _expert_kernel(ce_ref, x_ref, g_ref, wgu_ref, wd_ref, y_ref):
    c = pl.program_id(0)

    @pl.when(ce_ref[c] >= 0)
    def _():
        half = x_ref.shape[0] // 2
        for h in range(2):
            rows = pl.ds(h * half, half)
            gu = _dot(x_ref[rows, :], wgu_ref[0])
            hid = jax.nn.silu(gu[:, :D_EXPERT]) * gu[:, D_EXPERT:]
            y_ref[rows, :] = (g_ref[rows, 0:1] * _dot(hid.astype(BF16), wd_ref[0])).astype(BF16)

    @pl.when(ce_ref[c] < 0)
    def _():
        y_ref[...] = jnp.zeros_like(y_ref)


def _experts(xs, gs, chunk_expert, wts, rch):
    total_rows, d = xs.shape
    e2 = 2 * D_EXPERT
    w_gu = wts["w_gate_up"].reshape(N_EXPERTS, d, e2)
    w_dn = wts["w_down"].reshape(N_EXPERTS, D_EXPERT, d)
    grid_spec = pltpu.PrefetchScalarGridSpec(
        num_scalar_prefetch=1,
        grid=(total_rows // rch,),
        in_specs=[
            pl.BlockSpec((rch, d), lambda c, ce: (jnp.where(ce[c] >= 0, c, 0), 0)),
            pl.BlockSpec((rch, LANE), lambda c, ce: (jnp.where(ce[c] >= 0, c, 0), 0)),
            pl.BlockSpec((1, d, e2), lambda c, ce: (jnp.maximum(ce[c], 0), 0, 0)),
            pl.BlockSpec((1, D_EXPERT, d), lambda c, ce: (jnp.maximum(ce[c], 0), 0, 0)),
        ],
        out_specs=pl.BlockSpec((rch, d), lambda c, ce: (c, 0)),
    )
    return pl.pallas_call(
        _expert_kernel,
        grid_spec=grid_spec,
        out_shape=jax.ShapeDtypeStruct((total_rows, d), BF16),
        compiler_params=_cparams(("arbitrary",)),
        name="moe_experts",
    )(chunk_expert, xs, gs, w_gu, w_dn)


def _combine_kernel(udst_ref, nun_ref, nch_ref, x1_ref, gsel_ref, sel_ref, start_ref, ys_hbm, g2_ref, b2_ref,
                    o_ref, tri_ref, ysrt, sem, *, rch, umax):
    i = pl.program_id(0)
    tm = x1_ref.shape[0]

    @pl.when(i == 0)
    def _():
        _build_tri(tri_ref)
        ysrt[...] = jnp.zeros_like(ysrt)

    def for_units(step, buf, fn):
        def body(u, carry):
            src = pl.ds(pl.multiple_of(udst_ref[step * umax + u], ROW_UNIT), ROW_UNIT)
            dst = pl.ds(pl.multiple_of(u * ROW_UNIT, ROW_UNIT), ROW_UNIT)
            fn(pltpu.make_async_copy(ys_hbm.at[src], ysrt.at[buf, dst], sem.at[buf]))
            return carry
        lax.fori_loop(0, nun_ref[step], body, 0)

    slot = lax.rem(i, 2)

    @pl.when(i == 0)
    def _():
        for_units(0, 0, lambda cp: cp.start())

    @pl.when(i + 1 < pl.num_programs(0))
    def _():
        for_units(i + 1, 1 - slot, lambda cp: cp.start())

    _, pos_a, pos_b = _sorted_positions(gsel_ref[0], sel_ref[0], start_ref[0], tri_ref)
    o_ref[...] = jnp.zeros_like(o_ref)
    for_units(i, slot, lambda cp: cp.wait())

    def chunk(c, carry):
        r0 = pl.multiple_of(c * rch, rch)
        rid = lax.broadcasted_iota(I32, (rch, tm), 0) + r0
        onehot = jnp.where(jnp.logical_or(rid == pos_a, rid == pos_b), 1.0, 0.0).astype(BF16)
        o_ref[...] += _dot_tn(onehot, ysrt[slot, pl.ds(r0, rch), :])
        return carry

    lax.fori_loop(0, nch_ref[i], chunk, 0)
    o_ref[...] = _layernorm(ALPHA * x1_ref[...] + o_ref[...], g2_ref[...], b2_ref[...])


def _combine(x1, gsel, sel, start_col, ys, tables, wts, tm, rch, umax):
    n, d = x1.shape
    nt = n // tm
    unit_dst, n_units, n_chunks = tables
    grid_spec = pltpu.PrefetchScalarGridSpec(
        num_scalar_prefetch=3,
        grid=(nt,),
        in_specs=[
            pl.BlockSpec((tm, d), lambda i, *_: (i, 0)),
            pl.BlockSpec((1, 1, tm), lambda i, *_: (i, 0, 0)),
            pl.BlockSpec((1, EXPERTS_PER_GROUP, tm), lambda i, *_: (i, 0, 0)),
            pl.BlockSpec((1, N_EXPERTS, 1), lambda i, *_: (i, 0, 0)),
            pl.BlockSpec(memory_space=pl.ANY),
            pl.BlockSpec((1, d), lambda i, *_: (0, 0)),
            pl.BlockSpec((1, d), lambda i, *_: (0, 0)),
        ],
        out_specs=pl.BlockSpec((tm, d), lambda i, *_: (i, 0)),
        scratch_shapes=[
            pltpu.VMEM((tm, tm), BF16),
            pltpu.VMEM((2, umax * ROW_UNIT, d), BF16),
            pltpu.SemaphoreType.DMA((2,)),
        ],
    )
    return pl.pallas_call(
        functools.partial(_combine_kernel, rch=rch, umax=umax),
        grid_spec=grid_spec,
        out_shape=jax.ShapeDtypeStruct((n, d), F32),
        compiler_params=_cparams(("arbitrary",)),
        name="moe_combine",
    )(unit_dst, n_units, n_chunks, x1, gsel, sel, start_col, ys, wts["ln2_g"], wts["ln2_b"])


def _routing_tables(cnt, tm, rch, ech):
    nt = cnt.shape[0]
    seg = (cnt + (ROW_UNIT - 1)) // ROW_UNIT * ROW_UNIT
    start = jnp.cumsum(seg, axis=1) - seg
    rows = seg.sum(axis=1)
    per_expert = seg.sum(axis=0)
    region = (per_expert + (ech - 1)) // ech * ech
    region_start = jnp.cumsum(region) - region
    seg_dst = region_start[None, :] + jnp.cumsum(seg, axis=0) - seg
    rows_max = -(-(TOP_K * tm + N_EXPERTS * (ROW_UNIT - 1)) // rch) * rch
    umax = rows_max // ROW_UNIT
    total_rows = -(-(TOP_K * tm * nt + nt * N_EXPERTS * (ROW_UNIT - 1) + N_EXPERTS * (ech - 1)) // ech) * ech
    u_row = (jnp.arange(umax, dtype=I32) * ROW_UNIT)[None, :, None]
    in_seg = jnp.logical_and(u_row >= start[:, None, :], u_row < (start + seg)[:, None, :])
    unit_dst = jnp.sum(jnp.where(in_seg, (seg_dst - start)[:, None, :] + u_row, 0), axis=-1)
    c_row = (jnp.arange(total_rows // ech, dtype=I32) * ech)[:, None]
    in_region = jnp.logical_and(c_row >= region_start[None, :], c_row < (region_start + per_expert)[None, :])
    chunk_expert = jnp.sum(jnp.where(in_region, jnp.arange(N_EXPERTS, dtype=I32)[None, :] + 1, 0), axis=-1) - 1
    tables = (unit_dst.reshape(-1).astype(I32), (rows // ROW_UNIT).astype(I32),
              ((rows + (rch - 1)) // rch).astype(I32))
    used_rows = region.sum()
    tails = ((region_start + per_expert).astype(I32), ((region - per_expert) // ROW_UNIT).astype(I32),
             jnp.stack([used_rows, (total_rows - used_rows) // ech]).astype(I32))
    return tables, tails, chunk_expert.astype(I32), start.astype(F32).reshape(nt, N_EXPERTS, 1), total_rows, umax


def _merge_and_routed_ffn(x, att, rec, wts, tm_merge, tm, rch):
    b, s, d = x.shape
    x1, gsel, gate, sel, cnt = _merge(x, att, rec, wts, tm_merge)
    n = b * s
    nt = n // tm
    f = tm // tm_merge
    regroup = lambda a: a.reshape(nt, f, EXPERTS_PER_GROUP, tm_merge).transpose(0, 2, 1, 3).reshape(
        nt, EXPERTS_PER_GROUP, tm)
    gsel, gate, sel = gsel.reshape(nt, 1, tm), regroup(gate), regroup(sel)
    counts = cnt[:, :, 0].reshape(nt, f, N_EXPERTS).sum(axis=1).astype(I32)
    ech = CH_EXPERT
    tables, tails, chunk_expert, start_col, total_rows, umax = _routing_tables(counts, tm, rch, ech)
    x1 = x1.reshape(n, d)
    xs, gs = _dispatch(x1, gsel, sel, gate, start_col, tables, tails, total_rows, tm, rch, ech, umax)
    ys = _experts(xs, gs, chunk_expert, wts, ech)
    y = _combine(x1, gsel, sel, start_col, ys, tables, wts, tm, rch, umax)
    return y.reshape(b, s, d)


def _merge_and_ffn(x, att, rec, wts, tm_merge, tm_moe, ch):
    b, s, d = x.shape
    x1, gsel, gate, _, cnt = _merge(x, att, rec, wts, tm_merge)
    n = b * s
    nt = n // tm_moe
    f = tm_moe // tm_merge
    gsel = gsel.reshape(nt, 1, tm_moe)
    gate = gate.reshape(nt, f, EXPERTS_PER_GROUP, tm_merge).transpose(0, 2, 1, 3).reshape(nt, EXPERTS_PER_GROUP, tm_moe)
    per_expert = cnt[:, :, 0].reshape(nt, f, N_GROUPS, EXPERTS_PER_GROUP)
    counts = (per_expert.sum(axis=(1, 3)) / TOP_K).astype(I32)
    nch = ((counts + (ch - 1)) // ch).reshape(-1)
    y = _moe(x1.reshape(n, d), gsel, gate, nch, wts, tm_moe, ch)
    return y.reshape(b, s, d)


def _swap_halves(w):
    half = QK_ROPE // 2
    return jnp.concatenate([w[..., half:], w[..., :half]], axis=-1)


def _prep_weights(w_in, q_norm_g, w_uq, kv_norm_g, w_uk, w_uv, conv_w, conv_b, w_rg, b_rg, w_ig, b_ig,
                  lru_lambda, att_out_g, rec_out_g, w_out, ln1_g, ln1_b, w_group, b_group, w_expert,
                  b_expert, w_gate_up, w_down, ln2_g, ln2_b):
    d = w_in.shape[0]
    o1, o2, o3, o4 = Q_LORA, Q_LORA + KV_LORA, Q_LORA + KV_LORA + QK_ROPE, Q_LORA + KV_LORA + QK_ROPE + REC_WIDTH
    w_kpe = w_in[:, o2:o3]
    pad_lo = jnp.zeros((d, QK_NOPE), F32)
    pad_hi = jnp.zeros((d, HEAD_PAD - QK_NOPE - QK_ROPE), F32)
    w_in_ext = jnp.concatenate([
        w_in[:, :o2],
        pad_lo, w_kpe, pad_hi,
        pad_lo, _swap_halves(w_kpe), pad_hi,
        w_in[:, o3:o4], w_in[:, o4:],
    ], axis=1).astype(BF16)

    nope, pe = w_uq[..., :QK_NOPE], w_uq[..., QK_NOPE:]
    zq = lambda n: jnp.zeros((Q_LORA, N_HEADS, n), F32)
    q_main = jnp.concatenate([nope, pe, zq(HEAD_PAD - QK_NOPE - QK_ROPE)], axis=-1)
    q_swap = jnp.concatenate([zq(QK_NOPE), _swap_halves(pe), zq(HEAD_PAD - QK_NOPE - QK_ROPE)], axis=-1)
    w_uq_t = jnp.concatenate([q_main.reshape(Q_LORA, -1), q_swap.reshape(Q_LORA, -1)], axis=1).T.astype(BF16)

    k_pad = jnp.concatenate([w_uk, jnp.zeros((KV_LORA, N_HEADS, HEAD_PAD - QK_NOPE), F32)], axis=-1)
    w_uk_pad = k_pad.reshape(KV_LORA, -1).astype(BF16)
    w_uv_t = w_uv.reshape(KV_LORA, -1).T.astype(BF16)

    w_uk_t = jnp.concatenate([w_uk.transpose(1, 2, 0),
                              jnp.zeros((N_HEADS, HEAD_PAD - QK_NOPE, KV_LORA), F32)], axis=1).astype(BF16)
    sel = jnp.zeros((HEAD_PAD, QK_ROPE), F32).at[QK_NOPE + jnp.arange(QK_ROPE), jnp.arange(QK_ROPE)].set(1.0)
    w_uv_h = w_uv.transpose(1, 0, 2)
    zero_v = jnp.zeros((KV_LORA, V_HEAD), F32)
    w_uv_pair = jnp.stack([
        jnp.concatenate([jnp.concatenate([w_uv_h[2 * p], zero_v], axis=1),
                         jnp.concatenate([zero_v, w_uv_h[2 * p + 1]], axis=1)], axis=0)
        for p in range(N_PAIRS)]).astype(BF16)

    def block_diag(w):
        eye = jnp.eye(REC_BLOCKS, dtype=F32)
        return jnp.einsum('nde,nm->ndme', w, eye).reshape(REC_WIDTH, REC_WIDTH)

    bd_r, bd_i = block_diag(w_rg), block_diag(w_ig)
    half = REC_WIDTH // 2
    w_gate = jnp.stack([
        jnp.concatenate([bd_r[j * half:(j + 1) * half, j * half:(j + 1) * half],
                         bd_i[j * half:(j + 1) * half, j * half:(j + 1) * half]], axis=1)
        for j in range(2)]).astype(BF16)

    w_router_t = jnp.concatenate([w_group.T, jnp.zeros((SUBLANE - N_GROUPS, d), F32), w_expert.T], axis=0)
    w_router_hi = w_router_t.astype(BF16)
    w_router_lo = (w_router_t - w_router_hi.astype(F32)).astype(BF16)
    w_router_t = jnp.concatenate([w_router_hi, w_router_lo], axis=0)
    row = lambda v: v.reshape(1, -1)
    return {
        "w_in": w_in_ext, "q_norm_g": row(q_norm_g), "w_uq_t": w_uq_t, "kv_norm_g": row(kv_norm_g),
        "w_uk": w_uk_pad, "w_uv_t": w_uv_t, "w_uk_t": w_uk_t, "w_pe_sel": sel.astype(BF16), "w_uv_pair": w_uv_pair,
        "conv_w": conv_w, "conv_b": row(conv_b), "w_gate": w_gate, "b_rg": row(b_rg), "b_ig": row(b_ig),
        "lru_lambda": row(lru_lambda), "att_out_g": row(att_out_g), "rec_out_g": row(rec_out_g),
        "w_out": w_out.astype(BF16), "ln1_g": row(ln1_g), "ln1_b": row(ln1_b),
        "w_router_t": w_router_t, "b_group": b_group.reshape(-1, 1), "b_expert": b_expert.reshape(-1, 1),
        "w_gate_up": w_gate_up.astype(BF16).reshape(N_GROUPS, EXPERTS_PER_GROUP, d, 2 * D_EXPERT),
        "w_down": w_down.astype(BF16).reshape(N_GROUPS, EXPERTS_PER_GROUP, D_EXPERT, d),
        "ln2_g": row(ln2_g), "ln2_b": row(ln2_b),
    }


def _rope_tables(pos):
    half = QK_ROPE // 2
    inv = ROPE_THETA ** (-(jnp.arange(half, dtype=F32) * 2.0 / QK_ROPE))
    ang = pos.astype(F32)[:, None] * inv[None, :]
    cos, sin = jnp.cos(ang), jnp.sin(ang)
    t = pos.shape[0]
    cos_t = jnp.concatenate([jnp.ones((t, QK_NOPE), F32), cos, cos,
                             jnp.zeros((t, HEAD_PAD - QK_NOPE - QK_ROPE), F32)], axis=1)
    sin_t = jnp.concatenate([jnp.zeros((t, QK_NOPE), F32), -sin, sin,
                             jnp.zeros((t, HEAD_PAD - QK_NOPE - QK_ROPE), F32)], axis=1)
    return cos_t, sin_t, cos_t.T, sin_t.T


def kernel(x_prompt, x_sample, cache_kv_latent, cache_k_rope, state_conv, state_rec, page_table,
           w_in, q_norm_g, w_uq, kv_norm_g, w_uk, w_uv, conv_w, conv_b, w_rg, b_rg, w_ig, b_ig,
           lru_lambda, att_out_g, rec_out_g, w_out, ln1_g, ln1_b, w_group, b_group, w_expert,
           b_expert, w_gate_up, w_down, ln2_g, ln2_b):
    wts = _prep_weights(w_in, q_norm_g, w_uq, kv_norm_g, w_uk, w_uv, conv_w, conv_b, w_rg, b_rg, w_ig, b_ig,
                        lru_lambda, att_out_g, rec_out_g, w_out, ln1_g, ln1_b, w_group, b_group, w_expert,
                        b_expert, w_gate_up, w_down, ln2_g, ln2_b)
    bp, sp, d = x_prompt.shape
    bd, td, _ = x_sample.shape
    assert td == 1, "the sample path handles one new token per sequence"
    past_len = page_table.shape[1] * PAGE_SIZE
    ko = QK_NOPE

    qt, k, vt, c_p, kpe_blk, rx, rg = _project(x_prompt, _rope_tables(jnp.arange(sp, dtype=I32)), wts,
                                               min(T_ATT, sp))
    xs = x_sample.reshape(1, bd, d)
    qt_s, _, _, c_s, kpe_s_blk, rx_s, rg_s = _project(xs, _rope_tables(jnp.full((bd,), past_len, I32)), wts, bd)
    kpe_s = kpe_s_blk[0, :, ko:ko + QK_ROPE]
    qlat, qpe = _absorb(qt_s[0, :, 0], wts)

    att_p, o_lat = _attention(qt, k, vt, page_table, qlat.transpose(1, 0, 2), qpe.transpose(1, 0, 2),
                              c_s.reshape(bd, 1, KV_LORA), kpe_s.reshape(bd, 1, QK_ROPE),
                              cache_kv_latent, cache_k_rope.transpose(0, 2, 1))

    rec_p, h_p = _lru_prompt(rx, rg, wts, min(TM_LRU, sp))
    tm_moe = min(TM_MOE, bp * sp)
    y_p = _merge_and_routed_ffn(x_prompt, att_p, rec_p, wts, min(TM_MERGE, sp, tm_moe), min(TM_ROUTED, tm_moe),
                                CH_MOE)
    kpe_p = kpe_blk[..., ko:ko + QK_ROPE]
    conv_p = rx[:, sp - (CONV_W - 1):, :]

    att_s = _value_up(o_lat.reshape(bd, N_HEADS * KV_LORA), wts)
    rec_s, conv_s_t, h_s = _lru_step(rx_s[0], rg_s[0], state_conv.transpose(1, 0, 2), state_rec, wts)
    y_s = _merge_and_ffn(xs, att_s, rec_s.reshape(1, bd, REC_WIDTH), wts, bd, bd, min(CH_MOE, bd))

    return (y_p, y_s.reshape(bd, 1, d), c_p, kpe_p, conv_p, h_p.reshape(bp, REC_WIDTH),
            c_s.reshape(bd, 1, KV_LORA), kpe_s.reshape(bd, 1, QK_ROPE), conv_s_t.transpose(1, 0, 2), h_s)
```

```python
import functools
import math

import jax
import jax.numpy as jnp
from jax import lax
from jax.experimental import pallas as pl
from jax.experimental.pallas import tpu as pltpu

F32 = jnp.float32
BF16 = jnp.bfloat16
I32 = jnp.int32

N_HEADS = 8
QK_NOPE = 64
QK_ROPE = 32
V_HEAD = 64
Q_LORA = 384
KV_LORA = 256
ROPE_THETA = 10000.0
SM_SCALE = (QK_NOPE + QK_ROPE) ** -0.5
REC_WIDTH = 512
REC_BLOCKS = 8
REC_BLOCK_W = REC_WIDTH // REC_BLOCKS
CONV_W = 4
LRU_C = 8.0
N_GROUPS = 4
EXPERTS_PER_GROUP = 8
N_EXPERTS = N_GROUPS * EXPERTS_PER_GROUP
TOP_K = 2
D_EXPERT = 256
DEPTH = 1
ALPHA = (2.0 * DEPTH) ** 0.25
LN_EPS = 1e-5
RMS_EPS = 1e-6
NEG_INF = -1e30
PAGE_SIZE = 128

LANE = 128
SUBLANE = 8
HEAD_PAD = LANE
N_PAIRS = N_HEADS * V_HEAD // LANE
VMEM_LIMIT = 56 * 1024 * 1024

T_ATT = 512
KEY_STRIP = 512
TM_LRU = 256
TM_MERGE = 512
TM_MOE = 1024
TM_ROUTED = 512
CH_EXPERT = 512
CH_MOE = 256
ROW_UNIT = 16
PAGES_PER_CHUNK = 32
CACHE_SLOTS = 3
EXP2_SCALE = SM_SCALE * math.log2(math.e)


def _dot(a, b):
    return jnp.dot(a, b, preferred_element_type=F32)


def _dot_nt(a, b, precision=None):
    return lax.dot_general(a, b, (((1,), (1,)), ((), ())), preferred_element_type=F32, precision=precision)


def _dot_tn(a, b):
    return lax.dot_general(a, b, (((0,), (0,)), ((), ())), preferred_element_type=F32)


def _cparams(semantics, flags=None):
    return pltpu.CompilerParams(dimension_semantics=semantics, vmem_limit_bytes=VMEM_LIMIT, flags=flags)


def _rmsnorm(x, g):
    return x * lax.rsqrt(jnp.mean(x * x, axis=-1, keepdims=True) + RMS_EPS) * g


def _layernorm(x, g, b):
    mu = jnp.mean(x, axis=-1, keepdims=True)
    xc = x - mu
    var = jnp.mean(xc * xc, axis=-1, keepdims=True)
    return xc * lax.rsqrt(var + LN_EPS) * g + b


def _gelu_tanh(x):
    return x * (0.5 * (1.0 + jnp.tanh(math.sqrt(2.0 / math.pi) * (x + 0.044715 * (x * x * x)))))


def _lru_coeffs(xc, pre_r, pre_i, b_r, b_i, lam):
    r = jax.nn.sigmoid(pre_r + b_r)
    i = jax.nn.sigmoid(pre_i + b_i)
    neg_lam = -lam
    softplus = jnp.maximum(neg_lam, 0.0) + jnp.log1p(jnp.exp(-jnp.abs(neg_lam)))
    log_a = (-LRU_C * softplus) * r
    a = jnp.exp(log_a)
    u = jnp.sqrt(-jnp.tanh(log_a) * (a * a + 1.0)) * (i * xc)
    return a, u


def _gate_preacts(xc, wg_ref):
    half = REC_WIDTH // 2
    g0 = _dot(xc[:, :half].astype(BF16), wg_ref[0])
    g1 = _dot(xc[:, half:].astype(BF16), wg_ref[1])
    pre_r = jnp.concatenate([g0[:, :half], g1[:, :half]], axis=1)
    pre_i = jnp.concatenate([g0[:, half:], g1[:, half:]], axis=1)
    return pre_r, pre_i


def _proj_kernel(x_ref, cos_ref, sin_ref, cost_ref, sint_ref, win_ref, qg_ref, wuqt_ref, kvg_ref, wuk_ref, wuvt_ref,
                 qt_ref, k_ref, vt_ref, ckv_ref, kpe_ref, rx_ref, rg_ref):
    x = x_ref[0].astype(BF16)
    z = _dot(x, win_ref[...])
    o_kv = Q_LORA
    o_ka = o_kv + KV_LORA
    o_kb = o_ka + LANE
    o_rx = o_kb + LANE
    o_rg = o_rx + REC_WIDTH
    qn = _rmsnorm(z[:, :o_kv], qg_ref[...]).astype(BF16)
    qq = _dot_nt(wuqt_ref[...], qn)
    cos_t = cost_ref[...]
    sin_t = sint_ref[...]
    sw = N_HEADS * HEAD_PAD
    for h in range(N_HEADS):
        lo = h * HEAD_PAD
        qt_ref[0, h, 0] = (qq[lo:lo + HEAD_PAD] * cos_t + qq[sw + lo:sw + lo + HEAD_PAD] * sin_t).astype(BF16)
    ckv = _rmsnorm(z[:, o_kv:o_ka], kvg_ref[...])
    ckv_ref[0] = ckv
    kpe = z[:, o_ka:o_kb] * cos_ref[...] + z[:, o_kb:o_rx] * sin_ref[...]
    kpe_ref[0] = kpe
    ckv_b = ckv.astype(BF16)
    kn = _dot(ckv_b, wuk_ref[...])
    for h in range(N_HEADS):
        lo = h * HEAD_PAD
        k_ref[0, h] = (kn[:, lo:lo + HEAD_PAD] + kpe).astype(BF16)
    vt = _dot_nt(wuvt_ref[...], ckv_b)
    for p in range(N_PAIRS):
        vt_ref[0, p, 0] = vt[p * LANE:(p + 1) * LANE].astype(BF16)
    rx_ref[0] = z[:, o_rx:o_rg]
    rg_ref[0] = z[:, o_rg:o_rg + REC_WIDTH]


def _project(x, tables, wts, tm):
    b, s, d = x.shape
    nt = s // tm
    in_w = wts["w_in"].shape[1]
    full = lambda shape: pl.BlockSpec(shape, lambda bi, si: (0,) * len(shape))
    cos_n, sin_n, cos_t, sin_t = tables
    return pl.pallas_call(
        _proj_kernel,
        grid=(b, nt),
        in_specs=[
            pl.BlockSpec((1, tm, d), lambda bi, si: (bi, si, 0)),
            pl.BlockSpec((tm, LANE), lambda bi, si: (si, 0)),
            pl.BlockSpec((tm, LANE), lambda bi, si: (si, 0)),
            pl.BlockSpec((HEAD_PAD, tm), lambda bi, si: (0, si)),
            pl.BlockSpec((HEAD_PAD, tm), lambda bi, si: (0, si)),
            full((d, in_w)),
            full((1, Q_LORA)),
            full((2 * N_HEADS * HEAD_PAD, Q_LORA)),
            full((1, KV_LORA)),
            full((KV_LORA, N_HEADS * HEAD_PAD)),
            full((N_PAIRS * LANE, KV_LORA)),
        ],
        out_specs=[
            pl.BlockSpec((1, N_HEADS, 1, HEAD_PAD, tm), lambda bi, si: (bi, 0, si, 0, 0)),
            pl.BlockSpec((1, N_HEADS, tm, HEAD_PAD), lambda bi, si: (bi, 0, si, 0)),
            pl.BlockSpec((1, N_PAIRS, 1, LANE, tm), lambda bi, si: (bi, 0, si, 0, 0)),
            pl.BlockSpec((1, tm, KV_LORA), lambda bi, si: (bi, si, 0)),
            pl.BlockSpec((1, tm, LANE), lambda bi, si: (bi, si, 0)),
            pl.BlockSpec((1, tm, REC_WIDTH), lambda bi, si: (bi, si, 0)),
            pl.BlockSpec((1, tm, REC_WIDTH), lambda bi, si: (bi, si, 0)),
        ],
        out_shape=[
            jax.ShapeDtypeStruct((b, N_HEADS, nt, HEAD_PAD, tm), BF16),
            jax.ShapeDtypeStruct((b, N_HEADS, s, HEAD_PAD), BF16),
            jax.ShapeDtypeStruct((b, N_PAIRS, nt, LANE, tm), BF16),
            jax.ShapeDtypeStruct((b, s, KV_LORA), F32),
            jax.ShapeDtypeStruct((b, s, LANE), F32),
            jax.ShapeDtypeStruct((b, s, REC_WIDTH), F32),
            jax.ShapeDtypeStruct((b, s, REC_WIDTH), F32),
        ],
        compiler_params=_cparams(("parallel", "parallel")),
        name="proj",
    )(x, cos_n, sin_n, cos_t, sin_t, wts["w_in"], wts["q_norm_g"], wts["w_uq_t"], wts["kv_norm_g"],
      wts["w_uk"], wts["w_uv_t"])


def _lru_prompt_kernel(rx_ref, rg_ref, cw_ref, cb_ref, wg_ref, br_ref, bi_ref, lam_ref,
                       y_ref, hlast_ref, xp_ref, h_ref):
    tm = rx_ref.shape[1]
    si = pl.program_id(1)

    @pl.when(si == 0)
    def _():
        xp_ref[0:SUBLANE, :] = jnp.zeros((SUBLANE, REC_WIDTH), F32)
        h_ref[...] = jnp.zeros_like(h_ref)

    x = rx_ref[0]
    xp_ref[SUBLANE:SUBLANE + tm, :] = x
    xc = cb_ref[...] + x * cw_ref[CONV_W - 1:CONV_W, :]
    for m in range(1, CONV_W):
        xc = xc + xp_ref[pl.ds(SUBLANE - m, tm), :] * cw_ref[CONV_W - 1 - m:CONV_W - m, :]
    xp_ref[0:SUBLANE, :] = x[tm - SUBLANE:, :]

    pre_r, pre_i = _gate_preacts(xc, wg_ref)
    a, u = _lru_coeffs(xc, pre_r, pre_i, br_ref[...], bi_ref[...], lam_ref[...])

    row_in_group = lax.broadcasted_iota(I32, (tm, REC_WIDTH), 0) % SUBLANE
    d = 1
    while d < SUBLANE:
        keep = row_in_group >= d
        a_sh = jnp.where(keep, pltpu.roll(a, d, 0), 1.0)
        u_sh = jnp.where(keep, pltpu.roll(u, d, 0), 0.0)
        u = u + a * u_sh
        a = a * a_sh
        d *= 2
    carry = h_ref[...]
    groups = []
    for k in range(tm // SUBLANE):
        rows = slice(k * SUBLANE, (k + 1) * SUBLANE)
        hk = a[rows] * carry + u[rows]
        groups.append(hk)
        carry = hk[SUBLANE - 1:SUBLANE, :]
    h = jnp.concatenate(groups, axis=0)
    h_ref[...] = carry
    hlast_ref[0] = carry
    y_ref[0] = h * _gelu_tanh(rg_ref[0])


def _lru_prompt(rx, rg, wts, tm):
    b, s, w = rx.shape
    full = lambda shape: pl.BlockSpec(shape, lambda bi, si: (0,) * len(shape))
    return pl.pallas_call(
        _lru_prompt_kernel,
        grid=(b, s // tm),
        in_specs=[
            pl.BlockSpec((1, tm, w), lambda bi, si: (bi, si, 0)),
            pl.BlockSpec((1, tm, w), lambda bi, si: (bi, si, 0)),
            full((CONV_W, w)), full((1, w)), full((2, w // 2, w)), full((1, w)), full((1, w)), full((1, w)),
        ],
        out_specs=[
            pl.BlockSpec((1, tm, w), lambda bi, si: (bi, si, 0)),
            pl.BlockSpec((1, 1, w), lambda bi, si: (bi, 0, 0)),
        ],
        out_shape=[jax.ShapeDtypeStruct((b, s, w), F32), jax.ShapeDtypeStruct((b, 1, w), F32)],
        scratch_shapes=[pltpu.VMEM((tm + SUBLANE, w), F32), pltpu.VMEM((1, w), F32)],
        compiler_params=_cparams(("arbitrary", "arbitrary")),
        name="lru_prompt",
    )(rx, rg, wts["conv_w"], wts["conv_b"], wts["w_gate"], wts["b_rg"], wts["b_ig"], wts["lru_lambda"])


def _lru_step_kernel(rx_ref, rg_ref, conv_ref, h0_ref, cw_ref, cb_ref, wg_ref, br_ref, bi_ref, lam_ref,
                     y_ref, newconv_ref, h_ref):
    x = rx_ref[...]
    xc = cb_ref[...] + x * cw_ref[CONV_W - 1:CONV_W, :]
    for k in range(CONV_W - 1):
        xc = xc + conv_ref[k] * cw_ref[k:k + 1, :]
    pre_r, pre_i = _gate_preacts(xc, wg_ref)
    a, u = _lru_coeffs(xc, pre_r, pre_i, br_ref[...], bi_ref[...], lam_ref[...])
    h = a * h0_ref[...] + u
    h_ref[...] = h
    y_ref[...] = h * _gelu_tanh(rg_ref[...])
    for k in range(CONV_W - 2):
        newconv_ref[k] = conv_ref[k + 1]
    newconv_ref[CONV_W - 2] = x


def _lru_step(rx, rg, conv_t, h0, wts):
    n, w = rx.shape
    return pl.pallas_call(
        _lru_step_kernel,
        out_shape=[jax.ShapeDtypeStruct((n, w), F32),
                   jax.ShapeDtypeStruct((CONV_W - 1, n, w), F32),
                   jax.ShapeDtypeStruct((n, w), F32)],
        name="lru_step",
    )(rx, rg, conv_t, h0, wts["conv_w"], wts["conv_b"], wts["w_gate"], wts["b_rg"], wts["b_ig"], wts["lru_lambda"])


def _fused_attn_kernel(pt_ref, qt_ref, k_ref, vt_ref, qlat_ref, qpe_ref, cnew_ref, knew_ref, ckv_hbm, kr_hbm,
                       o_ref, olat_ref, cbuf, kbuf, sem, ms_ref, ls_ref, accs_ref,
                       *, n_batch, n_q, n_chunks, total_chunks):
    t = qt_ref.shape[-1]
    bi, pi, qi = pl.program_id(0), pl.program_id(1), pl.program_id(2)
    steps_per_group = n_q * (n_q + 1) // 2
    total_steps = n_batch * N_PAIRS * steps_per_group
    base = (bi * N_PAIRS + pi) * steps_per_group + (qi * (qi + 1)) // 2
    ks = KEY_STRIP
    key_i = lax.broadcasted_iota(I32, (ks, t), 0)
    qry_i = lax.broadcasted_iota(I32, (ks, t), 1)
    qts = [qt_ref[0, e, 0] for e in range(2)]

    cp = PAGES_PER_CHUNK
    ahead = CACHE_SLOTS - 1

    def copies(g, slot):
        out = []
        for p in range(cp):
            page = pt_ref[g * cp + p]
            rows = pl.ds(p * PAGE_SIZE, PAGE_SIZE)
            out.append(pltpu.make_async_copy(ckv_hbm.at[page], cbuf.at[slot, rows], sem.at[0, slot]))
            out.append(pltpu.make_async_copy(kr_hbm.at[page], kbuf.at[slot, :, rows], sem.at[1, slot]))
        return out

    def start(g, slot):
        for c in copies(g, slot):
            c.start()

    @pl.when(jnp.logical_and(jnp.logical_and(bi == 0, pi == 0), qi == 0))
    def _():
        for g0 in range(min(ahead, total_chunks)):
            start(g0, g0)

    def sample_chunk_stages(g):
        v = {}

        def fetch_and_score():
            @pl.when(g + ahead < total_chunks)
            def _():
                start(g + ahead, lax.rem(g + ahead, CACHE_SLOTS))

            slot = lax.rem(g, CACHE_SLOTS)
            for c in copies(g, slot):
                c.wait()
            v["b"] = lax.div(g, n_chunks)
            v["qlat"] = qlat_ref[v["b"]]
            v["qpe"] = qpe_ref[v["b"]]
            v["cb"] = cbuf[slot].astype(BF16)
            kb = kbuf[slot].astype(BF16)
            v["s"] = _dot_nt(v["qlat"].astype(BF16), v["cb"]) + _dot(v["qpe"].astype(BF16), kb)

        def softmax_and_values():
            first = lax.rem(g, n_chunks) == 0
            m = jnp.where(first, NEG_INF, ms_ref[...])
            l = jnp.where(first, 0.0, ls_ref[...])
            acc = jnp.where(first, 0.0, accs_ref[...])
            s = v["s"]
            m_new = jnp.maximum(m, jnp.max(s, axis=-1, keepdims=True))
            alpha = jnp.exp2((m - m_new) * EXP2_SCALE)
            p = jnp.exp2((s - m_new) * EXP2_SCALE)
            v["l"] = alpha * l + jnp.sum(p, axis=-1, keepdims=True)
            v["acc"] = alpha * acc + _dot(p.astype(BF16), v["cb"])
            v["m"] = m_new

        def finish():
            m_new, l, acc, b = v["m"], v["l"], v["acc"], v["b"]
            ms_ref[...] = m_new
            ls_ref[...] = l
            accs_ref[...] = acc
            cnew = cnew_ref[b]
            knew = knew_ref[b]
            s_new = (jnp.sum(v["qlat"] * cnew, axis=-1, keepdims=True)
                     + jnp.sum(v["qpe"] * knew, axis=-1, keepdims=True))
            m_fin = jnp.maximum(m_new, s_new)
            a_fin = jnp.exp2((m_new - m_fin) * EXP2_SCALE)
            p_new = jnp.exp2((s_new - m_fin) * EXP2_SCALE)
            olat_ref[b] = (a_fin * acc + p_new * cnew) / (a_fin * l + p_new)

        return [fetch_and_score, softmax_and_values, finish]

    def scores(j, r, e):
        kb = k_ref[0, e, pl.ds(pl.multiple_of(j * t + r * ks, ks), ks), :]
        return _dot(kb, qts[e])

    def step(j, carry, diagonal, with_chunk):
        stages = sample_chunk_stages(base + j) if with_chunk else []
        carry = list(carry)
        units = [(r, e) for r in range(t // ks) for e in range(2)]
        st_next = scores(j, *units[0])
        for u, (r, e) in enumerate(units):
            st = st_next
            if u + 1 < len(units):
                st_next = scores(j, *units[u + 1])
            if stages:
                stages.pop(0)()
            m, l, acc = carry[e]
            vb = vt_ref[0, 0, j, e * V_HEAD:(e + 1) * V_HEAD, r * ks:(r + 1) * ks]
            if diagonal:
                st = jnp.where(key_i + r * ks <= qry_i, st, NEG_INF)
            m_new = jnp.maximum(m, jnp.max(st, axis=0, keepdims=True))
            alpha = jnp.exp2((m - m_new) * EXP2_SCALE)
            pt = jnp.exp2((st - m_new) * EXP2_SCALE)
            l = alpha * l + jnp.sum(pt, axis=0, keepdims=True)
            acc = alpha * acc + _dot(vb, pt.astype(BF16))
            carry[e] = (m_new, l, acc)
        for stage in stages:
            stage()
        return tuple(carry)

    init = (jnp.full((1, t), NEG_INF, F32), jnp.zeros((1, t), F32), jnp.zeros((V_HEAD, t), F32))
    n_with = jnp.clip(total_chunks - base, 0, qi)
    carry = lax.fori_loop(0, n_with, functools.partial(step, diagonal=False, with_chunk=True), (init, init))
    carry = lax.fori_loop(n_with, qi, functools.partial(step, diagonal=False, with_chunk=False), carry)
    carry = lax.cond(base + qi < total_chunks,
                     functools.partial(step, qi, diagonal=True, with_chunk=True),
                     functools.partial(step, qi, diagonal=True, with_chunk=False), carry)
    halves = [acc / l for (_, l, acc) in carry]
    o_ref[0, 0] = jnp.concatenate(halves, axis=0).T

    if total_chunks > total_steps:
        @pl.when(jnp.logical_and(jnp.logical_and(bi == n_batch - 1, pi == N_PAIRS - 1), qi == n_q - 1))
        def _():
            def drain(g, c):
                for stage in sample_chunk_stages(g):
                    stage()
                return c
            lax.fori_loop(total_steps, total_chunks, drain, 0)


def _attention(qt, k, vt, page_table, qlat, qpe, c_new, k_new, cache_kv, cache_kr_t):
    b, _, nt, _, t = qt.shape
    s = nt * t
    bd, n_pages = page_table.shape
    n_chunks = n_pages // PAGES_PER_CHUNK
    rows = PAGES_PER_CHUNK * PAGE_SIZE
    kern = functools.partial(_fused_attn_kernel, n_batch=b, n_q=nt, n_chunks=n_chunks, total_chunks=bd * n_chunks)
    whole = lambda shape: pl.BlockSpec(shape, lambda bi, pi, qi, pt: (0,) * len(shape))
    grid_spec = pltpu.PrefetchScalarGridSpec(
        num_scalar_prefetch=1,
        grid=(b, N_PAIRS, nt),
        in_specs=[
            pl.BlockSpec((1, 2, 1, HEAD_PAD, t), lambda bi, pi, qi, pt: (bi, pi, qi, 0, 0)),
            pl.BlockSpec((1, 2, s, HEAD_PAD), lambda bi, pi, qi, pt: (bi, pi, 0, 0)),
            pl.BlockSpec((1, 1, nt, LANE, t), lambda bi, pi, qi, pt: (bi, pi, 0, 0, 0)),
            whole((bd, N_HEADS, KV_LORA)),
            whole((bd, N_HEADS, QK_ROPE)),
            whole((bd, 1, KV_LORA)),
            whole((bd, 1, QK_ROPE)),
            pl.BlockSpec(memory_space=pl.ANY),
            pl.BlockSpec(memory_space=pl.ANY),
        ],
        out_specs=[
            pl.BlockSpec((1, 1, t, LANE), lambda bi, pi, qi, pt: (bi, pi, qi, 0)),
            whole((bd, N_HEADS, KV_LORA)),
        ],
        scratch_shapes=[
            pltpu.VMEM((CACHE_SLOTS, rows, KV_LORA), F32),
            pltpu.VMEM((CACHE_SLOTS, QK_ROPE, rows), F32),
            pltpu.SemaphoreType.DMA((2, CACHE_SLOTS)),
            pltpu.VMEM((N_HEADS, 1), F32),
            pltpu.VMEM((N_HEADS, 1), F32),
            pltpu.VMEM((N_HEADS, KV_LORA), F32),
        ],
    )
    return pl.pallas_call(
        kern,
        grid_spec=grid_spec,
        out_shape=[jax.ShapeDtypeStruct((b, N_PAIRS, s, LANE), F32),
                   jax.ShapeDtypeStruct((bd, N_HEADS, KV_LORA), F32)],
        compiler_params=_cparams(("arbitrary", "arbitrary", "arbitrary")),
        name="attn_fused",
    )(page_table.reshape(-1), qt, k, vt, qlat, qpe, c_new, k_new, cache_kv, cache_kr_t)


def _absorb_kernel(qt_ref, wlat_ref, wpe_ref, qlat_ref, qpe_ref):
    for h in range(N_HEADS):
        qt = qt_ref[h]
        qlat_ref[h] = _dot_tn(qt, wlat_ref[h])
        qpe_ref[h] = _dot_tn(qt, wpe_ref[...])


def _absorb(qt, wts):
    _, _, n = qt.shape
    return pl.pallas_call(
        _absorb_kernel,
        out_shape=[jax.ShapeDtypeStruct((N_HEADS, n, KV_LORA), F32),
                   jax.ShapeDtypeStruct((N_HEADS, n, QK_ROPE), F32)],
        name="absorb_q",
    )(qt, wts["w_uk_t"], wts["w_pe_sel"])


def _value_up_kernel(olat_ref, wv_ref, o_ref):
    w = 2 * KV_LORA
    for p in range(N_PAIRS):
        o_ref[0, p] = _dot(olat_ref[:, p * w:(p + 1) * w].astype(BF16), wv_ref[p])


def _value_up(olat2d, wts):
    n = olat2d.shape[0]
    return pl.pallas_call(
        _value_up_kernel,
        out_shape=jax.ShapeDtypeStruct((1, N_PAIRS, n, LANE), F32),
        name="value_up",
    )(olat2d, wts["w_uv_pair"])


def _merge_kernel(x_ref, att_ref, rec_ref, ag_ref, rgn_ref, wout_ref, g1_ref, b1_ref, wr_ref, bg_ref, be_ref,
                  x1_ref, gsel_ref, gate_ref, sel_ref, cnt_ref):
    tm = x_ref.shape[1]
    att = [att_ref[0, p] for p in range(N_PAIRS)]
    ss = att[0] * att[0]
    for p in range(1, N_PAIRS):
        ss = ss + att[p] * att[p]
    inv = lax.rsqrt(jnp.sum(ss, axis=-1, keepdims=True) / (N_PAIRS * LANE) + RMS_EPS)
    parts = [(att[p] * inv * ag_ref[:, p * LANE:(p + 1) * LANE]).astype(BF16) for p in range(N_PAIRS)]
    parts.append(_rmsnorm(rec_ref[0], rgn_ref[...]).astype(BF16))
    mixed = jnp.concatenate(parts, axis=-1)
    mix = _dot(mixed, wout_ref[...])
    x1 = _layernorm(ALPHA * x_ref[0] + mix, g1_ref[...], b1_ref[...])
    x1_ref[0] = x1

    n_r = wr_ref.shape[0] // 2
    x_hi = x1.astype(BF16)
    x_lo = (x1 - x_hi.astype(F32)).astype(BF16)
    both = _dot_nt(wr_ref[...], x_hi)
    lt = (both[:n_r] + both[n_r:]) + _dot_nt(wr_ref[0:n_r, :], x_lo)
    g = [lt[k:k + 1, :] for k in range(N_GROUPS)]
    gmax = functools.reduce(jnp.maximum, g)
    ex = [jnp.exp(gk - gmax) for gk in g]
    den = functools.reduce(lambda p, q: p + q, ex)
    best = g[0] + bg_ref[0:1, :]
    idx = jnp.zeros((1, tm), I32)
    for k in range(1, N_GROUPS):
        cand = g[k] + bg_ref[k:k + 1, :]
        upd = cand > best
        idx = jnp.where(upd, k, idx)
        best = jnp.where(upd, cand, best)
    gp = ex[0]
    e_sel = lt[SUBLANE:SUBLANE + EXPERTS_PER_GROUP, :]
    e_bias = jnp.broadcast_to(be_ref[0:EXPERTS_PER_GROUP, :], (EXPERTS_PER_GROUP, tm))
    for k in range(1, N_GROUPS):
        hit = idx == k
        lo = SUBLANE + k * EXPERTS_PER_GROUP
        gp = jnp.where(hit, ex[k], gp)
        e_sel = jnp.where(hit, lt[lo:lo + EXPERTS_PER_GROUP, :], e_sel)
        e_bias = jnp.where(hit, be_ref[k * EXPERTS_PER_GROUP:(k + 1) * EXPERTS_PER_GROUP, :], e_bias)
    g_prob = gp / den
    sc = e_sel + e_bias
    sub = lax.broadcasted_iota(I32, (EXPERTS_PER_GROUP, tm), 0)
    m1 = jnp.max(sc, axis=0, keepdims=True)
    i1 = jnp.min(jnp.where(sc == m1, sub, EXPERTS_PER_GROUP), axis=0, keepdims=True)
    mask1 = sub == i1
    sc2 = jnp.where(mask1, -jnp.inf, sc)
    m2 = jnp.max(sc2, axis=0, keepdims=True)
    i2 = jnp.min(jnp.where(sc2 == m2, sub, EXPERTS_PER_GROUP), axis=0, keepdims=True)
    mask2 = sub == i2
    v1 = jnp.sum(jnp.where(mask1, e_sel, 0.0), axis=0, keepdims=True)
    v2 = jnp.sum(jnp.where(mask2, e_sel, 0.0), axis=0, keepdims=True)
    vm = jnp.maximum(v1, v2)
    e1 = jnp.exp(v1 - vm)
    e2 = jnp.exp(v2 - vm)
    esum = e1 + e2
    gate = g_prob * (jnp.where(mask1, e1 / esum, 0.0) + jnp.where(mask2, e2 / esum, 0.0))
    gsel_ref[0] = idx
    gate_ref[0] = gate
    sel = jnp.where(jnp.logical_or(mask1, mask2), 1.0, 0.0)
    sel_ref[0] = sel
    for k in range(N_GROUPS):
        ck = jnp.sum(jnp.where(idx == k, sel, 0.0), axis=-1, keepdims=True)
        cnt_ref[0, k * EXPERTS_PER_GROUP:(k + 1) * EXPERTS_PER_GROUP, :] = jnp.broadcast_to(
            ck, (EXPERTS_PER_GROUP, LANE))


def _merge(x, att, rec, wts, tm):
    b, s, d = x.shape
    nt = s // tm
    mw = wts["w_out"].shape[0]
    n_r = wts["w_router_t"].shape[0]
    full = lambda shape: pl.BlockSpec(shape, lambda bi, si: (0,) * len(shape))
    return pl.pallas_call(
        _merge_kernel,
        grid=(b, nt),
        in_specs=[
            pl.BlockSpec((1, tm, d), lambda bi, si: (bi, si, 0)),
            pl.BlockSpec((1, N_PAIRS, tm, LANE), lambda bi, si: (bi, 0, si, 0)),
            pl.BlockSpec((1, tm, REC_WIDTH), lambda bi, si: (bi, si, 0)),
            full((1, N_PAIRS * LANE)), full((1, REC_WIDTH)), full((mw, d)), full((1, d)), full((1, d)),
            full((n_r, d)), full((N_GROUPS, 1)), full((N_GROUPS * EXPERTS_PER_GROUP, 1)),
        ],
        out_specs=[
            pl.BlockSpec((1, tm, d), lambda bi, si: (bi, si, 0)),
            pl.BlockSpec((1, 1, tm), lambda bi, si: (bi * nt + si, 0, 0)),
            pl.BlockSpec((1, EXPERTS_PER_GROUP, tm), lambda bi, si: (bi * nt + si, 0, 0)),
            pl.BlockSpec((1, EXPERTS_PER_GROUP, tm), lambda bi, si: (bi * nt + si, 0, 0)),
            pl.BlockSpec((1, N_EXPERTS, LANE), lambda bi, si: (bi * nt + si, 0, 0)),
        ],
        out_shape=[
            jax.ShapeDtypeStruct((b, s, d), F32),
            jax.ShapeDtypeStruct((b * nt, 1, tm), I32),
            jax.ShapeDtypeStruct((b * nt, EXPERTS_PER_GROUP, tm), F32),
            jax.ShapeDtypeStruct((b * nt, EXPERTS_PER_GROUP, tm), F32),
            jax.ShapeDtypeStruct((b * nt, N_EXPERTS, LANE), F32),
        ],
        compiler_params=_cparams(("parallel", "parallel")),
        name="merge_router",
    )(x, att, rec, wts["att_out_g"], wts["rec_out_g"], wts["w_out"], wts["ln1_g"], wts["ln1_b"],
      wts["w_router_t"], wts["b_group"], wts["b_expert"])


def _moe_kernel(nch_ref, x1_ref, gsel_ref, gate_ref, wgu_ref, wd_ref, g2_ref, b2_ref, o_ref, xb_ref, tri_ref,
                *, ch):
    tm = x1_ref.shape[0]
    ti = pl.program_id(0)
    gi = pl.program_id(1)
    hi = pl.program_id(2)
    n_half = wgu_ref.shape[2]

    @pl.when(jnp.logical_and(jnp.logical_and(ti == 0, gi == 0), hi == 0))
    def _():
        _build_tri(tri_ref)

    @pl.when(jnp.logical_and(gi == 0, hi == 0))
    def _():
        xb_ref[...] = x1_ref[...].astype(BF16)
        o_ref[...] = jnp.zeros_like(o_ref)

    in_group = gsel_ref[0] == gi
    member = jnp.broadcast_to(jnp.where(in_group, 1.0, 0.0), (SUBLANE, tm)).astype(BF16)
    before = _dot(member, tri_ref[...])
    rank = jnp.where(in_group, before[0:1, :].astype(I32), -1)
    gate = gate_ref[0]
    g_hi = gate.astype(BF16).astype(F32)
    g_mid = (gate - g_hi).astype(BF16).astype(F32)
    g_lo = (gate - g_hi) - g_mid
    n_terms = 3
    gate_terms = jnp.concatenate(
        [g_hi, g_mid, g_lo, jnp.zeros((LANE - n_terms * EXPERTS_PER_GROUP, tm), F32)], axis=0).astype(BF16)

    def chunk(c, carry, first_expert):
        slot_id = lax.broadcasted_iota(I32, (ch, tm), 0) + c * ch
        onehot_b = jnp.where(slot_id == rank, 1.0, 0.0).astype(BF16)
        xg = _dot(onehot_b, xb_ref[...]).astype(BF16)
        gt = _dot_nt(onehot_b, gate_terms)
        gc = gt
        for k in range(1, n_terms):
            gc = gc + pltpu.roll(gt, LANE - k * EXPERTS_PER_GROUP, 1)
        acc = jnp.zeros((ch, o_ref.shape[1]), F32)
        for jj in range(n_half):
            j = first_expert + jj
            gu = _dot(xg, wgu_ref[0, 0, jj].astype(BF16))
            hid = jax.nn.silu(gu[:, :D_EXPERT]) * gu[:, D_EXPERT:]
            acc = acc + gc[:, j:j + 1] * _dot(hid.astype(BF16), wd_ref[0, 0, jj].astype(BF16))
        o_ref[...] += _dot_tn(onehot_b, acc.astype(BF16))
        return carry

    for half in range(EXPERTS_PER_GROUP // n_half):
        @pl.when(hi == half)
        def _(half=half):
            lax.fori_loop(0, nch_ref[ti * N_GROUPS + gi], functools.partial(chunk, first_expert=half * n_half), 0)

    @pl.when(jnp.logical_and(gi == N_GROUPS - 1, hi == pl.num_programs(2) - 1))
    def _():
        o_ref[...] = _layernorm(ALPHA * x1_ref[...] + o_ref[...], g2_ref[...], b2_ref[...])


def _moe(x1, gsel, gate, nch, wts, tm, ch):
    n, d = x1.shape
    nt = n // tm
    e2 = 2 * D_EXPERT
    halves = 2
    per_half = EXPERTS_PER_GROUP // halves
    w_gu = wts["w_gate_up"].reshape(N_GROUPS, halves, per_half, d, e2)
    w_dn = wts["w_down"].reshape(N_GROUPS, halves, per_half, D_EXPERT, d)
    grid_spec = pltpu.PrefetchScalarGridSpec(
        num_scalar_prefetch=1,
        grid=(nt, N_GROUPS, halves),
        in_specs=[
            pl.BlockSpec((tm, d), lambda ti, gi, hi, nc: (ti, 0)),
            pl.BlockSpec((1, 1, tm), lambda ti, gi, hi, nc: (ti, 0, 0)),
            pl.BlockSpec((1, EXPERTS_PER_GROUP, tm), lambda ti, gi, hi, nc: (ti, 0, 0)),
            pl.BlockSpec((1, 1, per_half, d, e2), lambda ti, gi, hi, nc: (gi, hi, 0, 0, 0)),
            pl.BlockSpec((1, 1, per_half, D_EXPERT, d), lambda ti, gi, hi, nc: (gi, hi, 0, 0, 0)),
            pl.BlockSpec((1, d), lambda ti, gi, hi, nc: (0, 0)),
            pl.BlockSpec((1, d), lambda ti, gi, hi, nc: (0, 0)),
        ],
        out_specs=pl.BlockSpec((tm, d), lambda ti, gi, hi, nc: (ti, 0)),
        scratch_shapes=[pltpu.VMEM((tm, d), BF16), pltpu.VMEM((tm, tm), BF16)],
    )
    return pl.pallas_call(
        functools.partial(_moe_kernel, ch=ch),
        grid_spec=grid_spec,
        out_shape=jax.ShapeDtypeStruct((n, d), F32),
        compiler_params=_cparams(("arbitrary", "arbitrary", "arbitrary")),
        name="moe",
    )(nch, x1, gsel, gate, w_gu, w_dn, wts["ln2_g"], wts["ln2_b"])


def _build_tri(tri_ref):
    tm = tri_ref.shape[0]
    r = lax.broadcasted_iota(I32, (tm, tm), 0)
    c = lax.broadcasted_iota(I32, (tm, tm), 1)
    tri_ref[...] = jnp.where(r < c, 1.0, 0.0).astype(BF16)


def _sorted_positions(gsel, sel, start_col, tri_ref):
    member = jnp.concatenate([jnp.where(gsel == g, sel, 0.0) for g in range(N_GROUPS)], axis=0)
    before = _dot(member.astype(BF16), tri_ref[...])
    routed = member > 0.5
    pos = jnp.where(routed, start_col + before, -1.0)
    pos_a = jnp.max(pos, axis=0, keepdims=True)
    pos_b = jnp.sum(jnp.where(routed, start_col + before, 0.0), axis=0, keepdims=True) - pos_a
    return pos, pos_a.astype(I32), pos_b.astype(I32)


def _exact_terms(v):
    hi = v.astype(BF16).astype(F32)
    mid = (v - hi).astype(BF16).astype(F32)
    lo = (v - hi) - mid
    row = lax.broadcasted_iota(I32, (LANE, v.shape[1]), 0)
    return jnp.where(row == 0, hi, jnp.where(row == 1, mid, jnp.where(row == 2, lo, 0.0))).astype(BF16)


def _dispatch_kernel(udst_ref, nun_ref, nch_ref, tstart_ref, tunits_ref, rest_ref, x1_ref, gsel_ref, sel_ref,
                     gate_ref, start_ref, xs_hbm, gs_hbm, xb_ref, tri_ref, xsrt, gsrt, sem, *, rch, ech, umax):
    i = pl.program_id(0)
    tm = x1_ref.shape[0]

    @pl.when(i == 0)
    def _():
        _build_tri(tri_ref)
        unit = pl.ds(0, ROW_UNIT)
        xsrt[0, unit, :] = jnp.zeros((ROW_UNIT, xsrt.shape[2]), BF16)
        gsrt[0, unit, :] = jnp.zeros((ROW_UNIT, LANE), F32)

        def tail_copies(e, k):
            dst = pl.ds(pl.multiple_of(tstart_ref[e] + k * ROW_UNIT, ROW_UNIT), ROW_UNIT)
            return (pltpu.make_async_copy(xsrt.at[0, unit], xs_hbm.at[dst], sem.at[0, 0]),
                    pltpu.make_async_copy(gsrt.at[0, unit], gs_hbm.at[dst], sem.at[1, 0]))

        def per_expert(fn):
            def over_experts(e, carry):
                def over_units(k, c):
                    for cp in tail_copies(e, k):
                        fn(cp)
                    return c
                return lax.fori_loop(0, tunits_ref[e], over_units, carry)
            lax.fori_loop(0, N_EXPERTS, over_experts, 0)

        per_expert(lambda cp: cp.start())
        per_expert(lambda cp: cp.wait())

        blk = pl.ds(0, ech)
        xsrt[0, blk, :] = jnp.zeros((ech, xsrt.shape[2]), BF16)
        gsrt[0, blk, :] = jnp.zeros((ech, LANE), F32)

        def rest_copies(k):
            dst = pl.ds(pl.multiple_of(rest_ref[0] + k * ech, ech), ech)
            return (pltpu.make_async_copy(xsrt.at[0, blk], xs_hbm.at[dst], sem.at[0, 0]),
                    pltpu.make_async_copy(gsrt.at[0, blk], gs_hbm.at[dst], sem.at[1, 0]))

        def start_rest(k, c):
            for cp in rest_copies(k):
                cp.start()
            return c

        def wait_rest(k, c):
            for cp in rest_copies(k):
                cp.wait()
            return c

        lax.fori_loop(0, rest_ref[1], start_rest, 0)
        lax.fori_loop(0, rest_ref[1], wait_rest, 0)

    xb_ref[...] = x1_ref[...].astype(BF16)
    gsel = gsel_ref[0]
    pos, pos_a, pos_b = _sorted_positions(gsel, sel_ref[0], start_ref[0], tri_ref)
    gate32 = jnp.concatenate([jnp.where(gsel == g, gate_ref[0], 0.0) for g in range(N_GROUPS)], axis=0)
    gate_a = jnp.sum(jnp.where(pos == pos_a.astype(F32), gate32, 0.0), axis=0, keepdims=True)
    gate_b = jnp.sum(jnp.where(pos == pos_b.astype(F32), gate32, 0.0), axis=0, keepdims=True)
    terms_a = _exact_terms(gate_a)
    terms_b = _exact_terms(gate_b)
    slot = lax.rem(i, 2)

    def chunk(c, carry):
        r0 = pl.multiple_of(c * rch, rch)
        rid = lax.broadcasted_iota(I32, (rch, tm), 0) + r0
        hit_a = rid == pos_a
        hit_b = rid == pos_b
        onehot = jnp.where(jnp.logical_or(hit_a, hit_b), 1.0, 0.0).astype(BF16)
        xsrt[slot, pl.ds(r0, rch), :] = _dot(onehot, xb_ref[...]).astype(BF16)
        gt = (_dot_nt(jnp.where(hit_a, 1.0, 0.0).astype(BF16), terms_a)
              + _dot_nt(jnp.where(hit_b, 1.0, 0.0).astype(BF16), terms_b))
        gsrt[slot, pl.ds(r0, rch), :] = (gt + pltpu.roll(gt, LANE - 1, 1)) + pltpu.roll(gt, LANE - 2, 1)
        return carry

    lax.fori_loop(0, nch_ref[i], chunk, 0)

    def for_units(step, buf, fn):
        def body(u, carry):
            src = pl.ds(pl.multiple_of(u * ROW_UNIT, ROW_UNIT), ROW_UNIT)
            dst = pl.ds(pl.multiple_of(udst_ref[step * umax + u], ROW_UNIT), ROW_UNIT)
            fn(pltpu.make_async_copy(xsrt.at[buf, src], xs_hbm.at[dst], sem.at[0, buf]))
            fn(pltpu.make_async_copy(gsrt.at[buf, src], gs_hbm.at[dst], sem.at[1, buf]))
            return carry
        lax.fori_loop(0, nun_ref[step], body, 0)

    @pl.when(i > 0)
    def _():
        for_units(i - 1, 1 - slot, lambda cp: cp.wait())

    for_units(i, slot, lambda cp: cp.start())

    @pl.when(i == pl.num_programs(0) - 1)
    def _():
        for_units(i, slot, lambda cp: cp.wait())


def _dispatch(x1, gsel, sel, gate, start_col, tables, tails, total_rows, tm, rch, ech, umax):
    n, d = x1.shape
    nt = n // tm
    rows_max = umax * ROW_UNIT
    assert rows_max >= ech, "the sorted-tile buffer doubles as the zero source of one expert chunk"
    unit_dst, n_units, n_chunks = tables
    tail_start, tail_units, rest = tails
    grid_spec = pltpu.PrefetchScalarGridSpec(
        num_scalar_prefetch=6,
        grid=(nt,),
        in_specs=[
            pl.BlockSpec((tm, d), lambda i, *_: (i, 0)),
            pl.BlockSpec((1, 1, tm), lambda i, *_: (i, 0, 0)),
            pl.BlockSpec((1, EXPERTS_PER_GROUP, tm), lambda i, *_: (i, 0, 0)),
            pl.BlockSpec((1, EXPERTS_PER_GROUP, tm), lambda i, *_: (i, 0, 0)),
            pl.BlockSpec((1, N_EXPERTS, 1), lambda i, *_: (i, 0, 0)),
        ],
        out_specs=[pl.BlockSpec(memory_space=pl.ANY), pl.BlockSpec(memory_space=pl.ANY)],
        scratch_shapes=[
            pltpu.VMEM((tm, d), BF16),
            pltpu.VMEM((tm, tm), BF16),
            pltpu.VMEM((2, rows_max, d), BF16),
            pltpu.VMEM((2, rows_max, LANE), F32),
            pltpu.SemaphoreType.DMA((2, 2)),
        ],
    )
    return pl.pallas_call(
        functools.partial(_dispatch_kernel, rch=rch, ech=ech, umax=umax),
        grid_spec=grid_spec,
        out_shape=[jax.ShapeDtypeStruct((total_rows, d), BF16), jax.ShapeDtypeStruct((total_rows, LANE), F32)],
        compiler_params=_cparams(("arbitrary",)),
        name="moe_dispatch",
    )(unit_dst, n_units, n_chunks, tail_start, tail_units, rest, x1, gsel, sel, gate, start_col)


def _expert_kernel(ce_ref, x_ref, g_ref, wgu_ref, wd_ref, y_ref):
    c = pl.program_id(0)

    @pl.when(ce_ref[c] >= 0)
    def _():
        half = x_ref.shape[0] // 2
        w_gu = wgu_ref[0].astype(BF16)
        w_dn = wd_ref[0].astype(BF16)
        for h in range(2):
            rows = pl.ds(h * half, half)
            gu = _dot(x_ref[rows, :], w_gu)
            hid = jax.nn.silu(gu[:, :D_EXPERT]) * gu[:, D_EXPERT:]
            y_ref[rows, :] = (g_ref[rows, 0:1] * _dot(hid.astype(BF16), w_dn)).astype(BF16)

    @pl.when(ce_ref[c] < 0)
    def _():
        y_ref[...] = jnp.zeros_like(y_ref)


def _experts(xs, gs, chunk_expert, wts, rch):
    total_rows, d = xs.shape
    e2 = 2 * D_EXPERT
    w_gu = wts["w_gate_up"].reshape(N_EXPERTS, d, e2)
    w_dn = wts["w_down"].reshape(N_EXPERTS, D_EXPERT, d)
    grid_spec = pltpu.PrefetchScalarGridSpec(
        num_scalar_prefetch=1,
        grid=(total_rows // rch,),
        in_specs=[
            pl.BlockSpec((rch, d), lambda c, ce: (jnp.where(ce[c] >= 0, c, 0), 0)),
            pl.BlockSpec((rch, LANE), lambda c, ce: (jnp.where(ce[c] >= 0, c, 0), 0)),
            pl.BlockSpec((1, d, e2), lambda c, ce: (jnp.maximum(ce[c], 0), 0, 0)),
            pl.BlockSpec((1, D_EXPERT, d), lambda c, ce: (jnp.maximum(ce[c], 0), 0, 0)),
        ],
        out_specs=pl.BlockSpec((rch, d), lambda c, ce: (c, 0)),
    )
    return pl.pallas_call(
        _expert_kernel,
        grid_spec=grid_spec,
        out_shape=jax.ShapeDtypeStruct((total_rows, d), BF16),
        compiler_params=_cparams(("arbitrary",)),
        name="moe_experts",
    )(chunk_expert, xs, gs, w_gu, w_dn)


def _combine_kernel(udst_ref, nun_ref, nch_ref, x1_ref, gsel_ref, sel_ref, start_ref, ys_hbm, g2_ref, b2_ref,
                    o_ref, tri_ref, ysrt, sem, *, rch, umax):
    i = pl.program_id(0)
    tm = x1_ref.shape[0]

    @pl.when(i == 0)
    def _():
        _build_tri(tri_ref)
        ysrt[...] = jnp.zeros_like(ysrt)

    def for_units(step, buf, fn):
        def body(u, carry):
            src = pl.ds(pl.multiple_of(udst_ref[step * umax + u], ROW_UNIT), ROW_UNIT)
            dst = pl.ds(pl.multiple_of(u * ROW_UNIT, ROW_UNIT), ROW_UNIT)
            fn(pltpu.make_async_copy(ys_hbm.at[src], ysrt.at[buf, dst], sem.at[buf]))
            return carry
        lax.fori_loop(0, nun_ref[step], body, 0)

    slot = lax.rem(i, 2)

    @pl.when(i == 0)
    def _():
        for_units(0, 0, lambda cp: cp.start())

    @pl.when(i + 1 < pl.num_programs(0))
    def _():
        for_units(i + 1, 1 - slot, lambda cp: cp.start())

    _, pos_a, pos_b = _sorted_positions(gsel_ref[0], sel_ref[0], start_ref[0], tri_ref)
    o_ref[...] = jnp.zeros_like(o_ref)
    for_units(i, slot, lambda cp: cp.wait())

    def chunk(c, carry):
        r0 = pl.multiple_of(c * rch, rch)
        rid = lax.broadcasted_iota(I32, (rch, tm), 0) + r0
        onehot = jnp.where(jnp.logical_or(rid == pos_a, rid == pos_b), 1.0, 0.0).astype(BF16)
        o_ref[...] += _dot_tn(onehot, ysrt[slot, pl.ds(r0, rch), :])
        return carry

    lax.fori_loop(0, nch_ref[i], chunk, 0)
    o_ref[...] = _layernorm(ALPHA * x1_ref[...] + o_ref[...], g2_ref[...], b2_ref[...])


def _combine(x1, gsel, sel, start_col, ys, tables, wts, tm, rch, umax):
    n, d = x1.shape
    nt = n // tm
    unit_dst, n_units, n_chunks = tables
    grid_spec = pltpu.PrefetchScalarGridSpec(
        num_scalar_prefetch=3,
        grid=(nt,),
        in_specs=[
            pl.BlockSpec((tm, d), lambda i, *_: (i, 0)),
            pl.BlockSpec((1, 1, tm), lambda i, *_: (i, 0, 0)),
            pl.BlockSpec((1, EXPERTS_PER_GROUP, tm), lambda i, *_: (i, 0, 0)),
            pl.BlockSpec((1, N_EXPERTS, 1), lambda i, *_: (i, 0, 0)),
            pl.BlockSpec(memory_space=pl.ANY),
            pl.BlockSpec((1, d), lambda i, *_: (0, 0)),
            pl.BlockSpec((1, d), lambda i, *_: (0, 0)),
        ],
        out_specs=pl.BlockSpec((tm, d), lambda i, *_: (i, 0)),
        scratch_shapes=[
            pltpu.VMEM((tm, tm), BF16),
            pltpu.VMEM((2, umax * ROW_UNIT, d), BF16),
            pltpu.SemaphoreType.DMA((2,)),
        ],
    )
    return pl.pallas_call(
        functools.partial(_combine_kernel, rch=rch, umax=umax),
        grid_spec=grid_spec,
        out_shape=jax.ShapeDtypeStruct((n, d), F32),
        compiler_params=_cparams(("arbitrary",)),
        name="moe_combine",
    )(unit_dst, n_units, n_chunks, x1, gsel, sel, start_col, ys, wts["ln2_g"], wts["ln2_b"])


def _routing_tables(cnt, tm, rch, ech):
    nt = cnt.shape[0]
    seg = (cnt + (ROW_UNIT - 1)) // ROW_UNIT * ROW_UNIT
    start = jnp.cumsum(seg, axis=1) - seg
    rows = seg.sum(axis=1)
    per_expert = seg.sum(axis=0)
    region = (per_expert + (ech - 1)) // ech * ech
    region_start = jnp.cumsum(region) - region
    seg_dst = region_start[None, :] + jnp.cumsum(seg, axis=0) - seg
    rows_max = -(-(TOP_K * tm + N_EXPERTS * (ROW_UNIT - 1)) // rch) * rch
    umax = rows_max // ROW_UNIT
    total_rows = -(-(TOP_K * tm * nt + nt * N_EXPERTS * (ROW_UNIT - 1) + N_EXPERTS * (ech - 1)) // ech) * ech
    u_row = (jnp.arange(umax, dtype=I32) * ROW_UNIT)[None, :, None]
    in_seg = jnp.logical_and(u_row >= start[:, None, :], u_row < (start + seg)[:, None, :])
    unit_dst = jnp.sum(jnp.where(in_seg, (seg_dst - start)[:, None, :] + u_row, 0), axis=-1)
    c_row = (jnp.arange(total_rows // ech, dtype=I32) * ech)[:, None]
    in_region = jnp.logical_and(c_row >= region_start[None, :], c_row < (region_start + per_expert)[None, :])
    chunk_expert = jnp.sum(jnp.where(in_region, jnp.arange(N_EXPERTS, dtype=I32)[None, :] + 1, 0), axis=-1) - 1
    tables = (unit_dst.reshape(-1).astype(I32), (rows // ROW_UNIT).astype(I32),
              ((rows + (rch - 1)) // rch).astype(I32))
    used_rows = region.sum()
    tails = ((region_start + per_expert).astype(I32), ((region - per_expert) // ROW_UNIT).astype(I32),
             jnp.stack([used_rows, (total_rows - used_rows) // ech]).astype(I32))
    return tables, tails, chunk_expert.astype(I32), start.astype(F32).reshape(nt, N_EXPERTS, 1), total_rows, umax


def _merge_and_routed_ffn(x, att, rec, wts, tm_merge, tm, rch):
    b, s, d = x.shape
    x1, gsel, gate, sel, cnt = _merge(x, att, rec, wts, tm_merge)
    n = b * s
    nt = n // tm
    f = tm // tm_merge
    regroup = lambda a: a.reshape(nt, f, EXPERTS_PER_GROUP, tm_merge).transpose(0, 2, 1, 3).reshape(
        nt, EXPERTS_PER_GROUP, tm)
    gsel, gate, sel = gsel.reshape(nt, 1, tm), regroup(gate), regroup(sel)
    counts = cnt[:, :, 0].reshape(nt, f, N_EXPERTS).sum(axis=1).astype(I32)
    ech = CH_EXPERT
    tables, tails, chunk_expert, start_col, total_rows, umax = _routing_tables(counts, tm, rch, ech)
    x1 = x1.reshape(n, d)
    xs, gs = _dispatch(x1, gsel, sel, gate, start_col, tables, tails, total_rows, tm, rch, ech, umax)
    ys = _experts(xs, gs, chunk_expert, wts, ech)
    y = _combine(x1, gsel, sel, start_col, ys, tables, wts, tm, rch, umax)
    return y.reshape(b, s, d)


def _merge_and_ffn(x, att, rec, wts, tm_merge, tm_moe, ch):
    b, s, d = x.shape
    x1, gsel, gate, _, cnt = _merge(x, att, rec, wts, tm_merge)
    n = b * s
    nt = n // tm_moe
    f = tm_moe // tm_merge
    gsel = gsel.reshape(nt, 1, tm_moe)
    gate = gate.reshape(nt, f, EXPERTS_PER_GROUP, tm_merge).transpose(0, 2, 1, 3).reshape(nt, EXPERTS_PER_GROUP, tm_moe)
    per_expert = cnt[:, :, 0].reshape(nt, f, N_GROUPS, EXPERTS_PER_GROUP)
    counts = (per_expert.sum(axis=(1, 3)) / TOP_K).astype(I32)
    nch = ((counts + (ch - 1)) // ch).reshape(-1)
    y = _moe(x1.reshape(n, d), gsel, gate, nch, wts, tm_moe, ch)
    return y.reshape(b, s, d)


def _swap_halves(w):
    half = QK_ROPE // 2
    return jnp.concatenate([w[..., half:], w[..., :half]], axis=-1)


def _prep_weights(w_in, q_norm_g, w_uq, kv_norm_g, w_uk, w_uv, conv_w, conv_b, w_rg, b_rg, w_ig, b_ig,
                  lru_lambda, att_out_g, rec_out_g, w_out, ln1_g, ln1_b, w_group, b_group, w_expert,
                  b_expert, w_gate_up, w_down, ln2_g, ln2_b):
    d = w_in.shape[0]
    o1, o2, o3, o4 = Q_LORA, Q_LORA + KV_LORA, Q_LORA + KV_LORA + QK_ROPE, Q_LORA + KV_LORA + QK_ROPE + REC_WIDTH
    w_kpe = w_in[:, o2:o3]
    pad_lo = jnp.zeros((d, QK_NOPE), F32)
    pad_hi = jnp.zeros((d, HEAD_PAD - QK_NOPE - QK_ROPE), F32)
    w_in_ext = jnp.concatenate([
        w_in[:, :o2],
        pad_lo, w_kpe, pad_hi,
        pad_lo, _swap_halves(w_kpe), pad_hi,
        w_in[:, o3:o4], w_in[:, o4:],
    ], axis=1).astype(BF16)

    nope, pe = w_uq[..., :QK_NOPE], w_uq[..., QK_NOPE:]
    zq = lambda n: jnp.zeros((Q_LORA, N_HEADS, n), F32)
    q_main = jnp.concatenate([nope, pe, zq(HEAD_PAD - QK_NOPE - QK_ROPE)], axis=-1)
    q_swap = jnp.concatenate([zq(QK_NOPE), _swap_halves(pe), zq(HEAD_PAD - QK_NOPE - QK_ROPE)], axis=-1)
    w_uq_t = jnp.concatenate([q_main.reshape(Q_LORA, -1), q_swap.reshape(Q_LORA, -1)], axis=1).T.astype(BF16)

    k_pad = jnp.concatenate([w_uk, jnp.zeros((KV_LORA, N_HEADS, HEAD_PAD - QK_NOPE), F32)], axis=-1)
    w_uk_pad = k_pad.reshape(KV_LORA, -1).astype(BF16)
    w_uv_t = w_uv.reshape(KV_LORA, -1).T.astype(BF16)

    w_uk_t = jnp.concatenate([w_uk.transpose(1, 2, 0),
                              jnp.zeros((N_HEADS, HEAD_PAD - QK_NOPE, KV_LORA), F32)], axis=1).astype(BF16)
    sel = jnp.zeros((HEAD_PAD, QK_ROPE), F32).at[QK_NOPE + jnp.arange(QK_ROPE), jnp.arange(QK_ROPE)].set(1.0)
    w_uv_h = w_uv.transpose(1, 0, 2)
    zero_v = jnp.zeros((KV_LORA, V_HEAD), F32)
    w_uv_pair = jnp.stack([
        jnp.concatenate([jnp.concatenate([w_uv_h[2 * p], zero_v], axis=1),
                         jnp.concatenate([zero_v, w_uv_h[2 * p + 1]], axis=1)], axis=0)
        for p in range(N_PAIRS)]).astype(BF16)

    def block_diag(w):
        eye = jnp.eye(REC_BLOCKS, dtype=F32)
        return jnp.einsum('nde,nm->ndme', w, eye).reshape(REC_WIDTH, REC_WIDTH)

    bd_r, bd_i = block_diag(w_rg), block_diag(w_ig)
    half = REC_WIDTH // 2
    w_gate = jnp.stack([
        jnp.concatenate([bd_r[j * half:(j + 1) * half, j * half:(j + 1) * half],
                         bd_i[j * half:(j + 1) * half, j * half:(j + 1) * half]], axis=1)
        for j in range(2)]).astype(BF16)

    w_router_t = jnp.concatenate([w_group.T, jnp.zeros((SUBLANE - N_GROUPS, d), F32), w_expert.T], axis=0)
    w_router_hi = w_router_t.astype(BF16)
    w_router_lo = (w_router_t - w_router_hi.astype(F32)).astype(BF16)
    w_router_t = jnp.concatenate([w_router_hi, w_router_lo], axis=0)
    row = lambda v: v.reshape(1, -1)
    return {
        "w_in": w_in_ext, "q_norm_g": row(q_norm_g), "w_uq_t": w_uq_t, "kv_norm_g": row(kv_norm_g),
        "w_uk": w_uk_pad, "w_uv_t": w_uv_t, "w_uk_t": w_uk_t, "w_pe_sel": sel.astype(BF16), "w_uv_pair": w_uv_pair,
        "conv_w": conv_w, "conv_b": row(conv_b), "w_gate": w_gate, "b_rg": row(b_rg), "b_ig": row(b_ig),
        "lru_lambda": row(lru_lambda), "att_out_g": row(att_out_g), "rec_out_g": row(rec_out_g),
        "w_out": w_out.astype(BF16), "ln1_g": row(ln1_g), "ln1_b": row(ln1_b),
        "w_router_t": w_router_t, "b_group": b_group.reshape(-1, 1), "b_expert": b_expert.reshape(-1, 1),
        "w_gate_up": w_gate_up, "w_down": w_down,
        "ln2_g": row(ln2_g), "ln2_b": row(ln2_b),
    }


def _rope_tables(pos):
    half = QK_ROPE // 2
    inv = ROPE_THETA ** (-(jnp.arange(half, dtype=F32) * 2.0 / QK_ROPE))
    ang = pos.astype(F32)[:, None] * inv[None, :]
    cos, sin = jnp.cos(ang), jnp.sin(ang)
    t = pos.shape[0]
    cos_t = jnp.concatenate([jnp.ones((t, QK_NOPE), F32), cos, cos,
                             jnp.zeros((t, HEAD_PAD - QK_NOPE - QK_ROPE), F32)], axis=1)
    sin_t = jnp.concatenate([jnp.zeros((t, QK_NOPE), F32), -sin, sin,
                             jnp.zeros((t, HEAD_PAD - QK_NOPE - QK_ROPE), F32)], axis=1)
    return cos_t, sin_t, cos_t.T, sin_t.T


def kernel(x_prompt, x_sample, cache_kv_latent, cache_k_rope, state_conv, state_rec, page_table,
           w_in, q_norm_g, w_uq, kv_norm_g, w_uk, w_uv, conv_w, conv_b, w_rg, b_rg, w_ig, b_ig,
           lru_lambda, att_out_g, rec_out_g, w_out, ln1_g, ln1_b, w_group, b_group, w_expert,
           b_expert, w_gate_up, w_down, ln2_g, ln2_b):
    wts = _prep_weights(w_in, q_norm_g, w_uq, kv_norm_g, w_uk, w_uv, conv_w, conv_b, w_rg, b_rg, w_ig, b_ig,
                        lru_lambda, att_out_g, rec_out_g, w_out, ln1_g, ln1_b, w_group, b_group, w_expert,
                        b_expert, w_gate_up, w_down, ln2_g, ln2_b)
    bp, sp, d = x_prompt.shape
    bd, td, _ = x_sample.shape
    assert td == 1, "the sample path handles one new token per sequence"
    past_len = page_table.shape[1] * PAGE_SIZE
    ko = QK_NOPE

    qt, k, vt, c_p, kpe_blk, rx, rg = _project(x_prompt, _rope_tables(jnp.arange(sp, dtype=I32)), wts,
                                               min(T_ATT, sp))
    xs = x_sample.reshape(1, bd, d)
    qt_s, _, _, c_s, kpe_s_blk, rx_s, rg_s = _project(xs, _rope_tables(jnp.full((bd,), past_len, I32)), wts, bd)
    kpe_s = kpe_s_blk[0, :, ko:ko + QK_ROPE]
    qlat, qpe = _absorb(qt_s[0, :, 0], wts)

    att_p, o_lat = _attention(qt, k, vt, page_table, qlat.transpose(1, 0, 2), qpe.transpose(1, 0, 2),
                              c_s.reshape(bd, 1, KV_LORA), kpe_s.reshape(bd, 1, QK_ROPE),
                              cache_kv_latent, cache_k_rope.transpose(0, 2, 1))

    rec_p, h_p = _lru_prompt(rx, rg, wts, min(TM_LRU, sp))
    tm_moe = min(TM_MOE, bp * sp)
    y_p = _merge_and_routed_ffn(x_prompt, att_p, rec_p, wts, min(TM_MERGE, sp, tm_moe), min(TM_ROUTED, tm_moe),
                                CH_MOE)
    kpe_p = kpe_blk[..., ko:ko + QK_ROPE]
    conv_p = rx[:, sp - (CONV_W - 1):, :]

    att_s = _value_up(o_lat.reshape(bd, N_HEADS * KV_LORA), wts)
    rec_s, conv_s_t, h_s = _lru_step(rx_s[0], rg_s[0], state_conv.transpose(1, 0, 2), state_rec, wts)
    y_s = _merge_and_ffn(xs, att_s, rec_s.reshape(1, bd, REC_WIDTH), wts, bd, bd, min(CH_MOE, bd))

    return (y_p, y_s.reshape(bd, 1, d), c_p, kpe_p, conv_p, h_p.reshape(bp, REC_WIDTH),
            c_s.reshape(bd, 1, KV_LORA), kpe_s.reshape(bd, 1, QK_ROPE), conv_s_t.transpose(1, 0, 2), h_s)
```

```python
import functools
import math

import jax
import jax.numpy as jnp
from jax import lax
from jax.experimental import pallas as pl
from jax.experimental.pallas import tpu as pltpu

F32 = jnp.float32
BF16 = jnp.bfloat16
I32 = jnp.int32

N_HEADS = 8
QK_NOPE = 64
QK_ROPE = 32
V_HEAD = 64
Q_LORA = 384
KV_LORA = 256
ROPE_THETA = 10000.0
SM_SCALE = (QK_NOPE + QK_ROPE) ** -0.5
REC_WIDTH = 512
REC_BLOCKS = 8
REC_BLOCK_W = REC_WIDTH // REC_BLOCKS
CONV_W = 4
LRU_C = 8.0
N_GROUPS = 4
EXPERTS_PER_GROUP = 8
N_EXPERTS = N_GROUPS * EXPERTS_PER_GROUP
TOP_K = 2
D_EXPERT = 256
DEPTH = 1
ALPHA = (2.0 * DEPTH) ** 0.25
LN_EPS = 1e-5
RMS_EPS = 1e-6
NEG_INF = -1e30
PAGE_SIZE = 128

LANE = 128
SUBLANE = 8
HEAD_PAD = LANE
N_PAIRS = N_HEADS * V_HEAD // LANE
VMEM_LIMIT = 56 * 1024 * 1024

T_ATT = 512
KEY_STRIP = 512
TM_LRU = 256
TM_MERGE = 512
TM_MOE = 1024
TM_ROUTED = 512
CH_EXPERT = 512
CH_MOE = 256
ROW_UNIT = 16
PAGES_PER_CHUNK = 32
CACHE_SLOTS = 3
EXP2_SCALE = SM_SCALE * math.log2(math.e)


def _dot(a, b):
    return jnp.dot(a, b, preferred_element_type=F32)


def _dot_nt(a, b, precision=None):
    return lax.dot_general(a, b, (((1,), (1,)), ((), ())), preferred_element_type=F32, precision=precision)


def _dot_tn(a, b):
    return lax.dot_general(a, b, (((0,), (0,)), ((), ())), preferred_element_type=F32)


def _cparams(semantics, flags=None):
    return pltpu.CompilerParams(dimension_semantics=semantics, vmem_limit_bytes=VMEM_LIMIT, flags=flags)


def _rmsnorm(x, g):
    return x * lax.rsqrt(jnp.mean(x * x, axis=-1, keepdims=True) + RMS_EPS) * g


def _layernorm(x, g, b):
    mu = jnp.mean(x, axis=-1, keepdims=True)
    xc = x - mu
    var = jnp.mean(xc * xc, axis=-1, keepdims=True)
    return xc * lax.rsqrt(var + LN_EPS) * g + b


def _gelu_tanh(x):
    return x * (0.5 * (1.0 + jnp.tanh(math.sqrt(2.0 / math.pi) * (x + 0.044715 * (x * x * x)))))


def _lru_coeffs(xc, pre_r, pre_i, b_r, b_i, lam):
    r = jax.nn.sigmoid(pre_r + b_r)
    i = jax.nn.sigmoid(pre_i + b_i)
    neg_lam = -lam
    softplus = jnp.maximum(neg_lam, 0.0) + jnp.log1p(jnp.exp(-jnp.abs(neg_lam)))
    log_a = (-LRU_C * softplus) * r
    a = jnp.exp(log_a)
    u = jnp.sqrt(-jnp.tanh(log_a) * (a * a + 1.0)) * (i * xc)
    return a, u


def _gate_preacts(xc, wg_ref):
    half = REC_WIDTH // 2
    g0 = _dot(xc[:, :half].astype(BF16), wg_ref[0])
    g1 = _dot(xc[:, half:].astype(BF16), wg_ref[1])
    pre_r = jnp.concatenate([g0[:, :half], g1[:, :half]], axis=1)
    pre_i = jnp.concatenate([g0[:, half:], g1[:, half:]], axis=1)
    return pre_r, pre_i


def _proj_kernel(x_ref, cos_ref, sin_ref, cost_ref, sint_ref, win_ref, qg_ref, wuqt_ref, kvg_ref, wuk_ref, wuvt_ref,
                 qt_ref, k_ref, vt_ref, ckv_ref, kpe_ref, rx_ref, rg_ref):
    x = x_ref[0].astype(BF16)
    z = _dot(x, win_ref[...])
    o_kv = Q_LORA
    o_ka = o_kv + KV_LORA
    o_kb = o_ka + LANE
    o_rx = o_kb + LANE
    o_rg = o_rx + REC_WIDTH
    qn = _rmsnorm(z[:, :o_kv], qg_ref[...]).astype(BF16)
    qq = _dot_nt(wuqt_ref[...], qn)
    cos_t = cost_ref[...]
    sin_t = sint_ref[...]
    sw = N_HEADS * HEAD_PAD
    for h in range(N_HEADS):
        lo = h * HEAD_PAD
        q_rot = qq[lo:lo + HEAD_PAD] * cos_t + qq[sw + lo:sw + lo + HEAD_PAD] * sin_t
        qt_ref[0, h, 0] = (q_rot * EXP2_SCALE).astype(BF16)
    ckv = _rmsnorm(z[:, o_kv:o_ka], kvg_ref[...])
    ckv_ref[0] = ckv
    kpe = z[:, o_ka:o_kb] * cos_ref[...] + z[:, o_kb:o_rx] * sin_ref[...]
    kpe_ref[0] = kpe
    ckv_b = ckv.astype(BF16)
    kn = _dot(ckv_b, wuk_ref[...])
    for h in range(N_HEADS):
        lo = h * HEAD_PAD
        k_ref[0, h] = (kn[:, lo:lo + HEAD_PAD] + kpe).astype(BF16)
    vt = _dot_nt(wuvt_ref[...], ckv_b)
    for p in range(N_PAIRS):
        vt_ref[0, p, 0] = vt[p * LANE:(p + 1) * LANE].astype(BF16)
    rx_ref[0] = z[:, o_rx:o_rg]
    rg_ref[0] = z[:, o_rg:o_rg + REC_WIDTH]


def _project(x, tables, wts, tm):
    b, s, d = x.shape
    nt = s // tm
    in_w = wts["w_in"].shape[1]
    full = lambda shape: pl.BlockSpec(shape, lambda bi, si: (0,) * len(shape))
    cos_n, sin_n, cos_t, sin_t = tables
    return pl.pallas_call(
        _proj_kernel,
        grid=(b, nt),
        in_specs=[
            pl.BlockSpec((1, tm, d), lambda bi, si: (bi, si, 0)),
            pl.BlockSpec((tm, LANE), lambda bi, si: (si, 0)),
            pl.BlockSpec((tm, LANE), lambda bi, si: (si, 0)),
            pl.BlockSpec((HEAD_PAD, tm), lambda bi, si: (0, si)),
            pl.BlockSpec((HEAD_PAD, tm), lambda bi, si: (0, si)),
            full((d, in_w)),
            full((1, Q_LORA)),
            full((2 * N_HEADS * HEAD_PAD, Q_LORA)),
            full((1, KV_LORA)),
            full((KV_LORA, N_HEADS * HEAD_PAD)),
            full((N_PAIRS * LANE, KV_LORA)),
        ],
        out_specs=[
            pl.BlockSpec((1, N_HEADS, 1, HEAD_PAD, tm), lambda bi, si: (bi, 0, si, 0, 0)),
            pl.BlockSpec((1, N_HEADS, tm, HEAD_PAD), lambda bi, si: (bi, 0, si, 0)),
            pl.BlockSpec((1, N_PAIRS, 1, LANE, tm), lambda bi, si: (bi, 0, si, 0, 0)),
            pl.BlockSpec((1, tm, KV_LORA), lambda bi, si: (bi, si, 0)),
            pl.BlockSpec((1, tm, LANE), lambda bi, si: (bi, si, 0)),
            pl.BlockSpec((1, tm, REC_WIDTH), lambda bi, si: (bi, si, 0)),
            pl.BlockSpec((1, tm, REC_WIDTH), lambda bi, si: (bi, si, 0)),
        ],
        out_shape=[
            jax.ShapeDtypeStruct((b, N_HEADS, nt, HEAD_PAD, tm), BF16),
            jax.ShapeDtypeStruct((b, N_HEADS, s, HEAD_PAD), BF16),
            jax.ShapeDtypeStruct((b, N_PAIRS, nt, LANE, tm), BF16),
            jax.ShapeDtypeStruct((b, s, KV_LORA), F32),
            jax.ShapeDtypeStruct((b, s, LANE), F32),
            jax.ShapeDtypeStruct((b, s, REC_WIDTH), F32),
            jax.ShapeDtypeStruct((b, s, REC_WIDTH), F32),
        ],
        compiler_params=_cparams(("parallel", "parallel")),
        name="proj",
    )(x, cos_n, sin_n, cos_t, sin_t, wts["w_in"], wts["q_norm_g"], wts["w_uq_t"], wts["kv_norm_g"],
      wts["w_uk"], wts["w_uv_t"])


def _lru_prompt_kernel(rx_ref, rg_ref, cw_ref, cb_ref, wg_ref, br_ref, bi_ref, lam_ref,
                       y_ref, hlast_ref, xp_ref, h_ref):
    tm = rx_ref.shape[1]
    si = pl.program_id(1)

    @pl.when(si == 0)
    def _():
        xp_ref[0:SUBLANE, :] = jnp.zeros((SUBLANE, REC_WIDTH), F32)
        h_ref[...] = jnp.zeros_like(h_ref)

    x = rx_ref[0]
    xp_ref[SUBLANE:SUBLANE + tm, :] = x
    xc = cb_ref[...] + x * cw_ref[CONV_W - 1:CONV_W, :]
    for m in range(1, CONV_W):
        xc = xc + xp_ref[pl.ds(SUBLANE - m, tm), :] * cw_ref[CONV_W - 1 - m:CONV_W - m, :]
    xp_ref[0:SUBLANE, :] = x[tm - SUBLANE:, :]

    pre_r, pre_i = _gate_preacts(xc, wg_ref)
    a, u = _lru_coeffs(xc, pre_r, pre_i, br_ref[...], bi_ref[...], lam_ref[...])

    row_in_group = lax.broadcasted_iota(I32, (tm, REC_WIDTH), 0) % SUBLANE
    d = 1
    while d < SUBLANE:
        keep = row_in_group >= d
        a_sh = jnp.where(keep, pltpu.roll(a, d, 0), 1.0)
        u_sh = jnp.where(keep, pltpu.roll(u, d, 0), 0.0)
        u = u + a * u_sh
        a = a * a_sh
        d *= 2
    carry = h_ref[...]
    groups = []
    for k in range(tm // SUBLANE):
        rows = slice(k * SUBLANE, (k + 1) * SUBLANE)
        hk = a[rows] * carry + u[rows]
        groups.append(hk)
        carry = hk[SUBLANE - 1:SUBLANE, :]
    h = jnp.concatenate(groups, axis=0)
    h_ref[...] = carry
    hlast_ref[0] = carry
    y_ref[0] = h * _gelu_tanh(rg_ref[0])


def _lru_prompt(rx, rg, wts, tm):
    b, s, w = rx.shape
    full = lambda shape: pl.BlockSpec(shape, lambda bi, si: (0,) * len(shape))
    return pl.pallas_call(
        _lru_prompt_kernel,
        grid=(b, s // tm),
        in_specs=[
            pl.BlockSpec((1, tm, w), lambda bi, si: (bi, si, 0)),
            pl.BlockSpec((1, tm, w), lambda bi, si: (bi, si, 0)),
            full((CONV_W, w)), full((1, w)), full((2, w // 2, w)), full((1, w)), full((1, w)), full((1, w)),
        ],
        out_specs=[
            pl.BlockSpec((1, tm, w), lambda bi, si: (bi, si, 0)),
            pl.BlockSpec((1, 1, w), lambda bi, si: (bi, 0, 0)),
        ],
        out_shape=[jax.ShapeDtypeStruct((b, s, w), F32), jax.ShapeDtypeStruct((b, 1, w), F32)],
        scratch_shapes=[pltpu.VMEM((tm + SUBLANE, w), F32), pltpu.VMEM((1, w), F32)],
        compiler_params=_cparams(("arbitrary", "arbitrary")),
        name="lru_prompt",
    )(rx, rg, wts["conv_w"], wts["conv_b"], wts["w_gate"], wts["b_rg"], wts["b_ig"], wts["lru_lambda"])


def _lru_step_kernel(rx_ref, rg_ref, conv_ref, h0_ref, cw_ref, cb_ref, wg_ref, br_ref, bi_ref, lam_ref,
                     y_ref, newconv_ref, h_ref):
    x = rx_ref[...]
    xc = cb_ref[...] + x * cw_ref[CONV_W - 1:CONV_W, :]
    for k in range(CONV_W - 1):
        xc = xc + conv_ref[k] * cw_ref[k:k + 1, :]
    pre_r, pre_i = _gate_preacts(xc, wg_ref)
    a, u = _lru_coeffs(xc, pre_r, pre_i, br_ref[...], bi_ref[...], lam_ref[...])
    h = a * h0_ref[...] + u
    h_ref[...] = h
    y_ref[...] = h * _gelu_tanh(rg_ref[...])
    for k in range(CONV_W - 2):
        newconv_ref[k] = conv_ref[k + 1]
    newconv_ref[CONV_W - 2] = x


def _lru_step(rx, rg, conv_t, h0, wts):
    n, w = rx.shape
    return pl.pallas_call(
        _lru_step_kernel,
        out_shape=[jax.ShapeDtypeStruct((n, w), F32),
                   jax.ShapeDtypeStruct((CONV_W - 1, n, w), F32),
                   jax.ShapeDtypeStruct((n, w), F32)],
        name="lru_step",
    )(rx, rg, conv_t, h0, wts["conv_w"], wts["conv_b"], wts["w_gate"], wts["b_rg"], wts["b_ig"], wts["lru_lambda"])


def _fused_attn_kernel(pt_ref, qt_ref, k_ref, vt_ref, qlat_ref, qpe_ref, cnew_ref, knew_ref, ckv_hbm, kr_hbm,
                       o_ref, olat_ref, cbuf, kbuf, sem, ms_ref, ls_ref, accs_ref,
                       *, n_batch, n_q, n_chunks, total_chunks):
    t = qt_ref.shape[-1]
    bi, pi, qi = pl.program_id(0), pl.program_id(1), pl.program_id(2)
    steps_per_group = n_q * (n_q + 1) // 2
    total_steps = n_batch * N_PAIRS * steps_per_group
    base = (bi * N_PAIRS + pi) * steps_per_group + (qi * (qi + 1)) // 2
    ks = KEY_STRIP
    key_i = lax.broadcasted_iota(I32, (ks, t), 0)
    qry_i = lax.broadcasted_iota(I32, (ks, t), 1)
    qts = [qt_ref[0, e, 0] for e in range(2)]

    cp = PAGES_PER_CHUNK
    ahead = CACHE_SLOTS - 1

    def copies(g, slot):
        out = []
        for p in range(cp):
            page = pt_ref[g * cp + p]
            rows = pl.ds(p * PAGE_SIZE, PAGE_SIZE)
            out.append(pltpu.make_async_copy(ckv_hbm.at[page], cbuf.at[slot, rows], sem.at[0, slot]))
            out.append(pltpu.make_async_copy(kr_hbm.at[page], kbuf.at[slot, :, rows], sem.at[1, slot]))
        return out

    def start(g, slot):
        for c in copies(g, slot):
            c.start()

    @pl.when(jnp.logical_and(jnp.logical_and(bi == 0, pi == 0), qi == 0))
    def _():
        for g0 in range(min(ahead, total_chunks)):
            start(g0, g0)

    def sample_chunk_stages(g):
        v = {}

        def fetch_and_score():
            @pl.when(g + ahead < total_chunks)
            def _():
                start(g + ahead, lax.rem(g + ahead, CACHE_SLOTS))

            slot = lax.rem(g, CACHE_SLOTS)
            for c in copies(g, slot):
                c.wait()
            v["b"] = lax.div(g, n_chunks)
            v["qlat"] = qlat_ref[v["b"]]
            v["qpe"] = qpe_ref[v["b"]]
            v["cb"] = cbuf[slot].astype(BF16)
            kb = kbuf[slot].astype(BF16)
            v["s"] = _dot_nt(v["qlat"].astype(BF16), v["cb"]) + _dot(v["qpe"].astype(BF16), kb)

        def softmax_and_values():
            first = lax.rem(g, n_chunks) == 0
            m = jnp.where(first, NEG_INF, ms_ref[...])
            l = jnp.where(first, 0.0, ls_ref[...])
            acc = jnp.where(first, 0.0, accs_ref[...])
            s = v["s"]
            m_new = jnp.maximum(m, jnp.max(s, axis=-1, keepdims=True))
            alpha = jnp.exp2((m - m_new))
            p = jnp.exp2((s - m_new))
            v["l"] = alpha * l + jnp.sum(p, axis=-1, keepdims=True)
            v["acc"] = alpha * acc + _dot(p.astype(BF16), v["cb"])
            v["m"] = m_new

        def finish():
            m_new, l, acc, b = v["m"], v["l"], v["acc"], v["b"]
            ms_ref[...] = m_new
            ls_ref[...] = l
            accs_ref[...] = acc
            cnew = cnew_ref[b]
            knew = knew_ref[b]
            s_new = (jnp.sum(v["qlat"] * cnew, axis=-1, keepdims=True)
                     + jnp.sum(v["qpe"] * knew, axis=-1, keepdims=True))
            m_fin = jnp.maximum(m_new, s_new)
            a_fin = jnp.exp2((m_new - m_fin))
            p_new = jnp.exp2((s_new - m_fin))
            olat_ref[b] = (a_fin * acc + p_new * cnew) / (a_fin * l + p_new)

        return [fetch_and_score, softmax_and_values, finish]

    def scores(j, r, e):
        kb = k_ref[0, e, pl.ds(pl.multiple_of(j * t + r * ks, ks), ks), :]
        return _dot(kb, qts[e])

    def step(j, carry, diagonal, with_chunk):
        stages = sample_chunk_stages(base + j) if with_chunk else []
        carry = list(carry)
        units = [(r, e) for r in range(t // ks) for e in range(2)]
        st_next = scores(j, *units[0])
        for u, (r, e) in enumerate(units):
            st = st_next
            if u + 1 < len(units):
                st_next = scores(j, *units[u + 1])
            if stages:
                stages.pop(0)()
            m, l, acc = carry[e]
            vb = vt_ref[0, 0, j, e * V_HEAD:(e + 1) * V_HEAD, r * ks:(r + 1) * ks]
            if diagonal:
                st = jnp.where(key_i + r * ks <= qry_i, st, NEG_INF)
            m_new = jnp.maximum(m, jnp.max(st, axis=0, keepdims=True))
            alpha = jnp.exp2((m - m_new))
            pt = jnp.exp2((st - m_new))
            l = alpha * l + jnp.sum(pt, axis=0, keepdims=True)
            acc = alpha * acc + _dot(vb, pt.astype(BF16))
            carry[e] = (m_new, l, acc)
        for stage in stages:
            stage()
        return tuple(carry)

    init = (jnp.full((1, t), NEG_INF, F32), jnp.zeros((1, t), F32), jnp.zeros((V_HEAD, t), F32))
    n_with = jnp.clip(total_chunks - base, 0, qi)
    carry = lax.fori_loop(0, n_with, functools.partial(step, diagonal=False, with_chunk=True), (init, init))
    carry = lax.fori_loop(n_with, qi, functools.partial(step, diagonal=False, with_chunk=False), carry)
    carry = lax.cond(base + qi < total_chunks,
                     functools.partial(step, qi, diagonal=True, with_chunk=True),
                     functools.partial(step, qi, diagonal=True, with_chunk=False), carry)
    halves = [acc / l for (_, l, acc) in carry]
    o_ref[0, 0] = jnp.concatenate(halves, axis=0).T

    if total_chunks > total_steps:
        @pl.when(jnp.logical_and(jnp.logical_and(bi == n_batch - 1, pi == N_PAIRS - 1), qi == n_q - 1))
        def _():
            def drain(g, c):
                for stage in sample_chunk_stages(g):
                    stage()
                return c
            lax.fori_loop(total_steps, total_chunks, drain, 0)


def _attention(qt, k, vt, page_table, qlat, qpe, c_new, k_new, cache_kv, cache_kr_t):
    b, _, nt, _, t = qt.shape
    s = nt * t
    bd, n_pages = page_table.shape
    n_chunks = n_pages // PAGES_PER_CHUNK
    rows = PAGES_PER_CHUNK * PAGE_SIZE
    kern = functools.partial(_fused_attn_kernel, n_batch=b, n_q=nt, n_chunks=n_chunks, total_chunks=bd * n_chunks)
    whole = lambda shape: pl.BlockSpec(shape, lambda bi, pi, qi, pt: (0,) * len(shape))
    grid_spec = pltpu.PrefetchScalarGridSpec(
        num_scalar_prefetch=1,
        grid=(b, N_PAIRS, nt),
        in_specs=[
            pl.BlockSpec((1, 2, 1, HEAD_PAD, t), lambda bi, pi, qi, pt: (bi, pi, qi, 0, 0)),
            pl.BlockSpec((1, 2, s, HEAD_PAD), lambda bi, pi, qi, pt: (bi, pi, 0, 0)),
            pl.BlockSpec((1, 1, nt, LANE, t), lambda bi, pi, qi, pt: (bi, pi, 0, 0, 0)),
            whole((bd, N_HEADS, KV_LORA)),
            whole((bd, N_HEADS, QK_ROPE)),
            whole((bd, 1, KV_LORA)),
            whole((bd, 1, QK_ROPE)),
            pl.BlockSpec(memory_space=pl.ANY),
            pl.BlockSpec(memory_space=pl.ANY),
        ],
        out_specs=[
            pl.BlockSpec((1, 1, t, LANE), lambda bi, pi, qi, pt: (bi, pi, qi, 0)),
            whole((bd, N_HEADS, KV_LORA)),
        ],
        scratch_shapes=[
            pltpu.VMEM((CACHE_SLOTS, rows, KV_LORA), F32),
            pltpu.VMEM((CACHE_SLOTS, QK_ROPE, rows), F32),
            pltpu.SemaphoreType.DMA((2, CACHE_SLOTS)),
            pltpu.VMEM((N_HEADS, 1), F32),
            pltpu.VMEM((N_HEADS, 1), F32),
            pltpu.VMEM((N_HEADS, KV_LORA), F32),
        ],
    )
    return pl.pallas_call(
        kern,
        grid_spec=grid_spec,
        out_shape=[jax.ShapeDtypeStruct((b, N_PAIRS, s, LANE), F32),
                   jax.ShapeDtypeStruct((bd, N_HEADS, KV_LORA), F32)],
        compiler_params=_cparams(("arbitrary", "arbitrary", "arbitrary")),
        name="attn_fused",
    )(page_table.reshape(-1), qt, k, vt, qlat, qpe, c_new, k_new, cache_kv, cache_kr_t)


def _absorb_kernel(qt_ref, wlat_ref, wpe_ref, qlat_ref, qpe_ref):
    for h in range(N_HEADS):
        qt = qt_ref[h]
        qlat_ref[h] = _dot_tn(qt, wlat_ref[h])
        qpe_ref[h] = _dot_tn(qt, wpe_ref[...])


def _absorb(qt, wts):
    _, _, n = qt.shape
    return pl.pallas_call(
        _absorb_kernel,
        out_shape=[jax.ShapeDtypeStruct((N_HEADS, n, KV_LORA), F32),
                   jax.ShapeDtypeStruct((N_HEADS, n, QK_ROPE), F32)],
        name="absorb_q",
    )(qt, wts["w_uk_t"], wts["w_pe_sel"])


def _value_up_kernel(olat_ref, wv_ref, o_ref):
    w = 2 * KV_LORA
    for p in range(N_PAIRS):
        o_ref[0, p] = _dot(olat_ref[:, p * w:(p + 1) * w].astype(BF16), wv_ref[p])


def _value_up(olat2d, wts):
    n = olat2d.shape[0]
    return pl.pallas_call(
        _value_up_kernel,
        out_shape=jax.ShapeDtypeStruct((1, N_PAIRS, n, LANE), F32),
        name="value_up",
    )(olat2d, wts["w_uv_pair"])


def _merge_kernel(x_ref, att_ref, rec_ref, ag_ref, rgn_ref, wout_ref, g1_ref, b1_ref, wr_ref, bg_ref, be_ref,
                  x1_ref, gsel_ref, gate_ref, sel_ref, cnt_ref):
    tm = x_ref.shape[1]
    att = [att_ref[0, p] for p in range(N_PAIRS)]
    ss = att[0] * att[0]
    for p in range(1, N_PAIRS):
        ss = ss + att[p] * att[p]
    inv = lax.rsqrt(jnp.sum(ss, axis=-1, keepdims=True) / (N_PAIRS * LANE) + RMS_EPS)
    parts = [(att[p] * inv * ag_ref[:, p * LANE:(p + 1) * LANE]).astype(BF16) for p in range(N_PAIRS)]
    parts.append(_rmsnorm(rec_ref[0], rgn_ref[...]).astype(BF16))
    mixed = jnp.concatenate(parts, axis=-1)
    mix = _dot(mixed, wout_ref[...])
    x1 = _layernorm(ALPHA * x_ref[0] + mix, g1_ref[...], b1_ref[...])
    x1_ref[0] = x1

    n_r = wr_ref.shape[0] // 2
    x_hi = x1.astype(BF16)
    x_lo = (x1 - x_hi.astype(F32)).astype(BF16)
    both = _dot_nt(wr_ref[...], x_hi)
    lt = (both[:n_r] + both[n_r:]) + _dot_nt(wr_ref[0:n_r, :], x_lo)
    g = [lt[k:k + 1, :] for k in range(N_GROUPS)]
    gmax = functools.reduce(jnp.maximum, g)
    ex = [jnp.exp(gk - gmax) for gk in g]
    den = functools.reduce(lambda p, q: p + q, ex)
    best = g[0] + bg_ref[0:1, :]
    idx = jnp.zeros((1, tm), I32)
    for k in range(1, N_GROUPS):
        cand = g[k] + bg_ref[k:k + 1, :]
        upd = cand > best
        idx = jnp.where(upd, k, idx)
        best = jnp.where(upd, cand, best)
    gp = ex[0]
    e_sel = lt[SUBLANE:SUBLANE + EXPERTS_PER_GROUP, :]
    e_bias = jnp.broadcast_to(be_ref[0:EXPERTS_PER_GROUP, :], (EXPERTS_PER_GROUP, tm))
    for k in range(1, N_GROUPS):
        hit = idx == k
        lo = SUBLANE + k * EXPERTS_PER_GROUP
        gp = jnp.where(hit, ex[k], gp)
        e_sel = jnp.where(hit, lt[lo:lo + EXPERTS_PER_GROUP, :], e_sel)
        e_bias = jnp.where(hit, be_ref[k * EXPERTS_PER_GROUP:(k + 1) * EXPERTS_PER_GROUP, :], e_bias)
    g_prob = gp / den
    sc = e_sel + e_bias
    sub = lax.broadcasted_iota(I32, (EXPERTS_PER_GROUP, tm), 0)
    m1 = jnp.max(sc, axis=0, keepdims=True)
    i1 = jnp.min(jnp.where(sc == m1, sub, EXPERTS_PER_GROUP), axis=0, keepdims=True)
    mask1 = sub == i1
    sc2 = jnp.where(mask1, -jnp.inf, sc)
    m2 = jnp.max(sc2, axis=0, keepdims=True)
    i2 = jnp.min(jnp.where(sc2 == m2, sub, EXPERTS_PER_GROUP), axis=0, keepdims=True)
    mask2 = sub == i2
    v1 = jnp.sum(jnp.where(mask1, e_sel, 0.0), axis=0, keepdims=True)
    v2 = jnp.sum(jnp.where(mask2, e_sel, 0.0), axis=0, keepdims=True)
    vm = jnp.maximum(v1, v2)
    e1 = jnp.exp(v1 - vm)
    e2 = jnp.exp(v2 - vm)
    esum = e1 + e2
    gate = g_prob * (jnp.where(mask1, e1 / esum, 0.0) + jnp.where(mask2, e2 / esum, 0.0))
    gsel_ref[0] = idx
    gate_ref[0] = gate
    sel = jnp.where(jnp.logical_or(mask1, mask2), 1.0, 0.0)
    sel_ref[0] = sel
    for k in range(N_GROUPS):
        ck = jnp.sum(jnp.where(idx == k, sel, 0.0), axis=-1, keepdims=True)
        cnt_ref[0, k * EXPERTS_PER_GROUP:(k + 1) * EXPERTS_PER_GROUP, :] = jnp.broadcast_to(
            ck, (EXPERTS_PER_GROUP, LANE))


def _merge(x, att, rec, wts, tm):
    b, s, d = x.shape
    nt = s // tm
    mw = wts["w_out"].shape[0]
    n_r = wts["w_router_t"].shape[0]
    full = lambda shape: pl.BlockSpec(shape, lambda bi, si: (0,) * len(shape))
    return pl.pallas_call(
        _merge_kernel,
        grid=(b, nt),
        in_specs=[
            pl.BlockSpec((1, tm, d), lambda bi, si: (bi, si, 0)),
            pl.BlockSpec((1, N_PAIRS, tm, LANE), lambda bi, si: (bi, 0, si, 0)),
            pl.BlockSpec((1, tm, REC_WIDTH), lambda bi, si: (bi, si, 0)),
            full((1, N_PAIRS * LANE)), full((1, REC_WIDTH)), full((mw, d)), full((1, d)), full((1, d)),
            full((n_r, d)), full((N_GROUPS, 1)), full((N_GROUPS * EXPERTS_PER_GROUP, 1)),
        ],
        out_specs=[
            pl.BlockSpec((1, tm, d), lambda bi, si: (bi, si, 0)),
            pl.BlockSpec((1, 1, tm), lambda bi, si: (bi * nt + si, 0, 0)),
            pl.BlockSpec((1, EXPERTS_PER_GROUP, tm), lambda bi, si: (bi * nt + si, 0, 0)),
            pl.BlockSpec((1, EXPERTS_PER_GROUP, tm), lambda bi, si: (bi * nt + si, 0, 0)),
            pl.BlockSpec((1, N_EXPERTS, LANE), lambda bi, si: (bi * nt + si, 0, 0)),
        ],
        out_shape=[
            jax.ShapeDtypeStruct((b, s, d), F32),
            jax.ShapeDtypeStruct((b * nt, 1, tm), I32),
            jax.ShapeDtypeStruct((b * nt, EXPERTS_PER_GROUP, tm), F32),
            jax.ShapeDtypeStruct((b * nt, EXPERTS_PER_GROUP, tm), F32),
            jax.ShapeDtypeStruct((b * nt, N_EXPERTS, LANE), F32),
        ],
        compiler_params=_cparams(("parallel", "parallel")),
        name="merge_router",
    )(x, att, rec, wts["att_out_g"], wts["rec_out_g"], wts["w_out"], wts["ln1_g"], wts["ln1_b"],
      wts["w_router_t"], wts["b_group"], wts["b_expert"])


def _moe_kernel(nch_ref, x1_ref, gsel_ref, gate_ref, wgu_ref, wd_ref, g2_ref, b2_ref, o_ref, xb_ref, tri_ref,
                *, ch):
    tm = x1_ref.shape[0]
    ti = pl.program_id(0)
    gi = pl.program_id(1)
    hi = pl.program_id(2)
    n_half = wgu_ref.shape[2]

    @pl.when(jnp.logical_and(jnp.logical_and(ti == 0, gi == 0), hi == 0))
    def _():
        _build_tri(tri_ref)

    @pl.when(jnp.logical_and(gi == 0, hi == 0))
    def _():
        xb_ref[...] = x1_ref[...].astype(BF16)
        o_ref[...] = jnp.zeros_like(o_ref)

    in_group = gsel_ref[0] == gi
    member = jnp.broadcast_to(jnp.where(in_group, 1.0, 0.0), (SUBLANE, tm)).astype(BF16)
    before = _dot(member, tri_ref[...])
    rank = jnp.where(in_group, before[0:1, :].astype(I32), -1)
    gate = gate_ref[0]
    g_hi = gate.astype(BF16).astype(F32)
    g_mid = (gate - g_hi).astype(BF16).astype(F32)
    g_lo = (gate - g_hi) - g_mid
    n_terms = 3
    gate_terms = jnp.concatenate(
        [g_hi, g_mid, g_lo, jnp.zeros((LANE - n_terms * EXPERTS_PER_GROUP, tm), F32)], axis=0).astype(BF16)

    def chunk(c, carry, first_expert):
        slot_id = lax.broadcasted_iota(I32, (ch, tm), 0) + c * ch
        onehot_b = jnp.where(slot_id == rank, 1.0, 0.0).astype(BF16)
        xg = _dot(onehot_b, xb_ref[...]).astype(BF16)
        gt = _dot_nt(onehot_b, gate_terms)
        gc = gt
        for k in range(1, n_terms):
            gc = gc + pltpu.roll(gt, LANE - k * EXPERTS_PER_GROUP, 1)
        acc = jnp.zeros((ch, o_ref.shape[1]), F32)
        for jj in range(n_half):
            j = first_expert + jj
            gu = _dot(xg, wgu_ref[0, 0, jj].astype(BF16))
            hid = jax.nn.silu(gu[:, :D_EXPERT]) * gu[:, D_EXPERT:]
            acc = acc + gc[:, j:j + 1] * _dot(hid.astype(BF16), wd_ref[0, 0, jj].astype(BF16))
        o_ref[...] += _dot_tn(onehot_b, acc.astype(BF16))
        return carry

    for half in range(EXPERTS_PER_GROUP // n_half):
        @pl.when(hi == half)
        def _(half=half):
            lax.fori_loop(0, nch_ref[ti * N_GROUPS + gi], functools.partial(chunk, first_expert=half * n_half), 0)

    @pl.when(jnp.logical_and(gi == N_GROUPS - 1, hi == pl.num_programs(2) - 1))
    def _():
        o_ref[...] = _layernorm(ALPHA * x1_ref[...] + o_ref[...], g2_ref[...], b2_ref[...])


def _moe(x1, gsel, gate, nch, wts, tm, ch):
    n, d = x1.shape
    nt = n // tm
    e2 = 2 * D_EXPERT
    halves = 2
    per_half = EXPERTS_PER_GROUP // halves
    w_gu = wts["w_gate_up"].reshape(N_GROUPS, halves, per_half, d, e2)
    w_dn = wts["w_down"].reshape(N_GROUPS, halves, per_half, D_EXPERT, d)
    grid_spec = pltpu.PrefetchScalarGridSpec(
        num_scalar_prefetch=1,
        grid=(nt, N_GROUPS, halves),
        in_specs=[
            pl.BlockSpec((tm, d), lambda ti, gi, hi, nc: (ti, 0)),
            pl.BlockSpec((1, 1, tm), lambda ti, gi, hi, nc: (ti, 0, 0)),
            pl.BlockSpec((1, EXPERTS_PER_GROUP, tm), lambda ti, gi, hi, nc: (ti, 0, 0)),
            pl.BlockSpec((1, 1, per_half, d, e2), lambda ti, gi, hi, nc: (gi, hi, 0, 0, 0)),
            pl.BlockSpec((1, 1, per_half, D_EXPERT, d), lambda ti, gi, hi, nc: (gi, hi, 0, 0, 0)),
            pl.BlockSpec((1, d), lambda ti, gi, hi, nc: (0, 0)),
            pl.BlockSpec((1, d), lambda ti, gi, hi, nc: (0, 0)),
        ],
        out_specs=pl.BlockSpec((tm, d), lambda ti, gi, hi, nc: (ti, 0)),
        scratch_shapes=[pltpu.VMEM((tm, d), BF16), pltpu.VMEM((tm, tm), BF16)],
    )
    return pl.pallas_call(
        functools.partial(_moe_kernel, ch=ch),
        grid_spec=grid_spec,
        out_shape=jax.ShapeDtypeStruct((n, d), F32),
        compiler_params=_cparams(("arbitrary", "arbitrary", "arbitrary")),
        name="moe",
    )(nch, x1, gsel, gate, w_gu, w_dn, wts["ln2_g"], wts["ln2_b"])


def _build_tri(tri_ref):
    tm = tri_ref.shape[0]
    r = lax.broadcasted_iota(I32, (tm, tm), 0)
    c = lax.broadcasted_iota(I32, (tm, tm), 1)
    tri_ref[...] = jnp.where(r < c, 1.0, 0.0).astype(BF16)


def _sorted_positions(gsel, sel, start_col, tri_ref):
    member = jnp.concatenate([jnp.where(gsel == g, sel, 0.0) for g in range(N_GROUPS)], axis=0)
    before = _dot(member.astype(BF16), tri_ref[...])
    routed = member > 0.5
    pos = jnp.where(routed, start_col + before, -1.0)
    pos_a = jnp.max(pos, axis=0, keepdims=True)
    pos_b = jnp.sum(jnp.where(routed, start_col + before, 0.0), axis=0, keepdims=True) - pos_a
    return pos, pos_a.astype(I32), pos_b.astype(I32)


def _exact_terms(v):
    hi = v.astype(BF16).astype(F32)
    mid = (v - hi).astype(BF16).astype(F32)
    lo = (v - hi) - mid
    row = lax.broadcasted_iota(I32, (LANE, v.shape[1]), 0)
    return jnp.where(row == 0, hi, jnp.where(row == 1, mid, jnp.where(row == 2, lo, 0.0))).astype(BF16)


def _dispatch_kernel(udst_ref, nun_ref, nch_ref, tstart_ref, tunits_ref, rest_ref, x1_ref, gsel_ref, sel_ref,
                     gate_ref, start_ref, xs_hbm, gs_hbm, xb_ref, tri_ref, xsrt, gsrt, sem, *, rch, ech, umax):
    i = pl.program_id(0)
    tm = x1_ref.shape[0]

    @pl.when(i == 0)
    def _():
        _build_tri(tri_ref)
        unit = pl.ds(0, ROW_UNIT)
        xsrt[0, unit, :] = jnp.zeros((ROW_UNIT, xsrt.shape[2]), BF16)
        gsrt[0, unit, :] = jnp.zeros((ROW_UNIT, LANE), F32)

        def tail_copies(e, k):
            dst = pl.ds(pl.multiple_of(tstart_ref[e] + k * ROW_UNIT, ROW_UNIT), ROW_UNIT)
            return (pltpu.make_async_copy(xsrt.at[0, unit], xs_hbm.at[dst], sem.at[0, 0]),
                    pltpu.make_async_copy(gsrt.at[0, unit], gs_hbm.at[dst], sem.at[1, 0]))

        def per_expert(fn):
            def over_experts(e, carry):
                def over_units(k, c):
                    for cp in tail_copies(e, k):
                        fn(cp)
                    return c
                return lax.fori_loop(0, tunits_ref[e], over_units, carry)
            lax.fori_loop(0, N_EXPERTS, over_experts, 0)

        per_expert(lambda cp: cp.start())
        per_expert(lambda cp: cp.wait())

        blk = pl.ds(0, ech)
        xsrt[0, blk, :] = jnp.zeros((ech, xsrt.shape[2]), BF16)
        gsrt[0, blk, :] = jnp.zeros((ech, LANE), F32)

        def rest_copies(k):
            dst = pl.ds(pl.multiple_of(rest_ref[0] + k * ech, ech), ech)
            return (pltpu.make_async_copy(xsrt.at[0, blk], xs_hbm.at[dst], sem.at[0, 0]),
                    pltpu.make_async_copy(gsrt.at[0, blk], gs_hbm.at[dst], sem.at[1, 0]))

        def start_rest(k, c):
            for cp in rest_copies(k):
                cp.start()
            return c

        def wait_rest(k, c):
            for cp in rest_copies(k):
                cp.wait()
            return c

        lax.fori_loop(0, rest_ref[1], start_rest, 0)
        lax.fori_loop(0, rest_ref[1], wait_rest, 0)

    xb_ref[...] = x1_ref[...].astype(BF16)
    gsel = gsel_ref[0]
    pos, pos_a, pos_b = _sorted_positions(gsel, sel_ref[0], start_ref[0], tri_ref)
    gate32 = jnp.concatenate([jnp.where(gsel == g, gate_ref[0], 0.0) for g in range(N_GROUPS)], axis=0)
    gate_a = jnp.sum(jnp.where(pos == pos_a.astype(F32), gate32, 0.0), axis=0, keepdims=True)
    gate_b = jnp.sum(jnp.where(pos == pos_b.astype(F32), gate32, 0.0), axis=0, keepdims=True)
    terms_a = _exact_terms(gate_a)
    terms_b = _exact_terms(gate_b)
    slot = lax.rem(i, 2)

    def chunk(c, carry):
        r0 = pl.multiple_of(c * rch, rch)
        rid = lax.broadcasted_iota(I32, (rch, tm), 0) + r0
        hit_a = rid == pos_a
        hit_b = rid == pos_b
        onehot = jnp.where(jnp.logical_or(hit_a, hit_b), 1.0, 0.0).astype(BF16)
        xsrt[slot, pl.ds(r0, rch), :] = _dot(onehot, xb_ref[...]).astype(BF16)
        gt = (_dot_nt(jnp.where(hit_a, 1.0, 0.0).astype(BF16), terms_a)
              + _dot_nt(jnp.where(hit_b, 1.0, 0.0).astype(BF16), terms_b))
        gsrt[slot, pl.ds(r0, rch), :] = (gt + pltpu.roll(gt, LANE - 1, 1)) + pltpu.roll(gt, LANE - 2, 1)
        return carry

    lax.fori_loop(0, nch_ref[i], chunk, 0)

    def for_units(step, buf, fn):
        def body(u, carry):
            src = pl.ds(pl.multiple_of(u * ROW_UNIT, ROW_UNIT), ROW_UNIT)
            dst = pl.ds(pl.multiple_of(udst_ref[step * umax + u], ROW_UNIT), ROW_UNIT)
            fn(pltpu.make_async_copy(xsrt.at[buf, src], xs_hbm.at[dst], sem.at[0, buf]))
            fn(pltpu.make_async_copy(gsrt.at[buf, src], gs_hbm.at[dst], sem.at[1, buf]))
            return carry
        lax.fori_loop(0, nun_ref[step], body, 0)

    @pl.when(i > 0)
    def _():
        for_units(i - 1, 1 - slot, lambda cp: cp.wait())

    for_units(i, slot, lambda cp: cp.start())

    @pl.when(i == pl.num_programs(0) - 1)
    def _():
        for_units(i, slot, lambda cp: cp.wait())


def _dispatch(x1, gsel, sel, gate, start_col, tables, tails, total_rows, tm, rch, ech, umax):
    n, d = x1.shape
    nt = n // tm
    rows_max = umax * ROW_UNIT
    assert rows_max >= ech, "the sorted-tile buffer doubles as the zero source of one expert chunk"
    unit_dst, n_units, n_chunks = tables
    tail_start, tail_units, rest = tails
    grid_spec = pltpu.PrefetchScalarGridSpec(
        num_scalar_prefetch=6,
        grid=(nt,),
        in_specs=[
            pl.BlockSpec((tm, d), lambda i, *_: (i, 0)),
            pl.BlockSpec((1, 1, tm), lambda i, *_: (i, 0, 0)),
            pl.BlockSpec((1, EXPERTS_PER_GROUP, tm), lambda i, *_: (i, 0, 0)),
            pl.BlockSpec((1, EXPERTS_PER_GROUP, tm), lambda i, *_: (i, 0, 0)),
            pl.BlockSpec((1, N_EXPERTS, 1), lambda i, *_: (i, 0, 0)),
        ],
        out_specs=[pl.BlockSpec(memory_space=pl.ANY), pl.BlockSpec(memory_space=pl.ANY)],
        scratch_shapes=[
            pltpu.VMEM((tm, d), BF16),
            pltpu.VMEM((tm, tm), BF16),
            pltpu.VMEM((2, rows_max, d), BF16),
            pltpu.VMEM((2, rows_max, LANE), F32),
            pltpu.SemaphoreType.DMA((2, 2)),
        ],
    )
    return pl.pallas_call(
        functools.partial(_dispatch_kernel, rch=rch, ech=ech, umax=umax),
        grid_spec=grid_spec,
        out_shape=[jax.ShapeDtypeStruct((total_rows, d), BF16), jax.ShapeDtypeStruct((total_rows, LANE), F32)],
        compiler_params=_cparams(("arbitrary",)),
        name="moe_dispatch",
    )(unit_dst, n_units, n_chunks, tail_start, tail_units, rest, x1, gsel, sel, gate, start_col)


def _expert_kernel(ce_ref, x_ref, g_ref, wgu_ref, wd_ref, y_ref, wgu_b, wd_b):
    c = pl.program_id(0)
    expert = ce_ref[c]

    @pl.when(jnp.logical_and(expert >= 0, jnp.logical_or(c == 0, expert != ce_ref[jnp.maximum(c - 1, 0)])))
    def _():
        wgu_b[...] = wgu_ref[0].astype(BF16)
        wd_b[...] = wd_ref[0].astype(BF16)

    @pl.when(expert >= 0)
    def _():
        half = x_ref.shape[0] // 2
        for h in range(2):
            rows = pl.ds(h * half, half)
            gu = _dot(x_ref[rows, :], wgu_b[...])
            hid = jax.nn.silu(gu[:, :D_EXPERT]) * gu[:, D_EXPERT:]
            y_ref[rows, :] = (g_ref[rows, 0:1] * _dot(hid.astype(BF16), wd_b[...])).astype(BF16)

    @pl.when(ce_ref[c] < 0)
    def _():
        y_ref[...] = jnp.zeros_like(y_ref)


def _experts(xs, gs, chunk_expert, wts, rch):
    total_rows, d = xs.shape
    e2 = 2 * D_EXPERT
    w_gu = wts["w_gate_up"].reshape(N_EXPERTS, d, e2)
    w_dn = wts["w_down"].reshape(N_EXPERTS, D_EXPERT, d)
    grid_spec = pltpu.PrefetchScalarGridSpec(
        num_scalar_prefetch=1,
        grid=(total_rows // rch,),
        in_specs=[
            pl.BlockSpec((rch, d), lambda c, ce: (jnp.where(ce[c] >= 0, c, 0), 0)),
            pl.BlockSpec((rch, LANE), lambda c, ce: (jnp.where(ce[c] >= 0, c, 0), 0)),
            pl.BlockSpec((1, d, e2), lambda c, ce: (jnp.maximum(ce[c], 0), 0, 0)),
            pl.BlockSpec((1, D_EXPERT, d), lambda c, ce: (jnp.maximum(ce[c], 0), 0, 0)),
        ],
        out_specs=pl.BlockSpec((rch, d), lambda c, ce: (c, 0)),
        scratch_shapes=[pltpu.VMEM((d, e2), BF16), pltpu.VMEM((D_EXPERT, d), BF16)],
    )
    return pl.pallas_call(
        _expert_kernel,
        grid_spec=grid_spec,
        out_shape=jax.ShapeDtypeStruct((total_rows, d), BF16),
        compiler_params=_cparams(("arbitrary",)),
        name="moe_experts",
    )(chunk_expert, xs, gs, w_gu, w_dn)


def _combine_kernel(udst_ref, nun_ref, nch_ref, x1_ref, gsel_ref, sel_ref, start_ref, ys_hbm, g2_ref, b2_ref,
                    o_ref, tri_ref, ysrt, sem, *, rch, umax):
    i = pl.program_id(0)
    tm = x1_ref.shape[0]

    @pl.when(i == 0)
    def _():
        _build_tri(tri_ref)
        ysrt[...] = jnp.zeros_like(ysrt)

    def for_units(step, buf, fn):
        def body(u, carry):
            src = pl.ds(pl.multiple_of(udst_ref[step * umax + u], ROW_UNIT), ROW_UNIT)
            dst = pl.ds(pl.multiple_of(u * ROW_UNIT, ROW_UNIT), ROW_UNIT)
            fn(pltpu.make_async_copy(ys_hbm.at[src], ysrt.at[buf, dst], sem.at[buf]))
            return carry
        lax.fori_loop(0, nun_ref[step], body, 0)

    slot = lax.rem(i, 2)

    @pl.when(i == 0)
    def _():
        for_units(0, 0, lambda cp: cp.start())

    @pl.when(i + 1 < pl.num_programs(0))
    def _():
        for_units(i + 1, 1 - slot, lambda cp: cp.start())

    _, pos_a, pos_b = _sorted_positions(gsel_ref[0], sel_ref[0], start_ref[0], tri_ref)
    o_ref[...] = jnp.zeros_like(o_ref)
    for_units(i, slot, lambda cp: cp.wait())

    def chunk(c, carry):
        r0 = pl.multiple_of(c * rch, rch)
        rid = lax.broadcasted_iota(I32, (rch, tm), 0) + r0
        onehot = jnp.where(jnp.logical_or(rid == pos_a, rid == pos_b), 1.0, 0.0).astype(BF16)
        o_ref[...] += _dot_tn(onehot, ysrt[slot, pl.ds(r0, rch), :])
        return carry

    lax.fori_loop(0, nch_ref[i], chunk, 0)
    o_ref[...] = _layernorm(ALPHA * x1_ref[...] + o_ref[...], g2_ref[...], b2_ref[...])


def _combine(x1, gsel, sel, start_col, ys, tables, wts, tm, rch, umax):
    n, d = x1.shape
    nt = n // tm
    unit_dst, n_units, n_chunks = tables
    grid_spec = pltpu.PrefetchScalarGridSpec(
        num_scalar_prefetch=3,
        grid=(nt,),
        in_specs=[
            pl.BlockSpec((tm, d), lambda i, *_: (i, 0)),
            pl.BlockSpec((1, 1, tm), lambda i, *_: (i, 0, 0)),
            pl.BlockSpec((1, EXPERTS_PER_GROUP, tm), lambda i, *_: (i, 0, 0)),
            pl.BlockSpec((1, N_EXPERTS, 1), lambda i, *_: (i, 0, 0)),
            pl.BlockSpec(memory_space=pl.ANY),
            pl.BlockSpec((1, d), lambda i, *_: (0, 0)),
            pl.BlockSpec((1, d), lambda i, *_: (0, 0)),
        ],
        out_specs=pl.BlockSpec((tm, d), lambda i, *_: (i, 0)),
        scratch_shapes=[
            pltpu.VMEM((tm, tm), BF16),
            pltpu.VMEM((2, umax * ROW_UNIT, d), BF16),
            pltpu.SemaphoreType.DMA((2,)),
        ],
    )
    return pl.pallas_call(
        functools.partial(_combine_kernel, rch=rch, umax=umax),
        grid_spec=grid_spec,
        out_shape=jax.ShapeDtypeStruct((n, d), F32),
        compiler_params=_cparams(("arbitrary",)),
        name="moe_combine",
    )(unit_dst, n_units, n_chunks, x1, gsel, sel, start_col, ys, wts["ln2_g"], wts["ln2_b"])


def _routing_tables(cnt, tm, rch, ech):
    nt = cnt.shape[0]
    seg = (cnt + (ROW_UNIT - 1)) // ROW_UNIT * ROW_UNIT
    start = jnp.cumsum(seg, axis=1) - seg
    rows = seg.sum(axis=1)
    per_expert = seg.sum(axis=0)
    region = (per_expert + (ech - 1)) // ech * ech
    region_start = jnp.cumsum(region) - region
    seg_dst = region_start[None, :] + jnp.cumsum(seg, axis=0) - seg
    rows_max = -(-(TOP_K * tm + N_EXPERTS * (ROW_UNIT - 1)) // rch) * rch
    umax = rows_max // ROW_UNIT
    total_rows = -(-(TOP_K * tm * nt + nt * N_EXPERTS * (ROW_UNIT - 1) + N_EXPERTS * (ech - 1)) // ech) * ech
    u_row = (jnp.arange(umax, dtype=I32) * ROW_UNIT)[None, :, None]
    in_seg = jnp.logical_and(u_row >= start[:, None, :], u_row < (start + seg)[:, None, :])
    unit_dst = jnp.sum(jnp.where(in_seg, (seg_dst - start)[:, None, :] + u_row, 0), axis=-1)
    c_row = (jnp.arange(total_rows // ech, dtype=I32) * ech)[:, None]
    in_region = jnp.logical_and(c_row >= region_start[None, :], c_row < (region_start + per_expert)[None, :])
    chunk_expert = jnp.sum(jnp.where(in_region, jnp.arange(N_EXPERTS, dtype=I32)[None, :] + 1, 0), axis=-1) - 1
    tables = (unit_dst.reshape(-1).astype(I32), (rows // ROW_UNIT).astype(I32),
              ((rows + (rch - 1)) // rch).astype(I32))
    used_rows = region.sum()
    tails = ((region_start + per_expert).astype(I32), ((region - per_expert) // ROW_UNIT).astype(I32),
             jnp.stack([used_rows, (total_rows - used_rows) // ech]).astype(I32))
    return tables, tails, chunk_expert.astype(I32), start.astype(F32).reshape(nt, N_EXPERTS, 1), total_rows, umax


def _merge_and_routed_ffn(x, att, rec, wts, tm_merge, tm, rch):
    b, s, d = x.shape
    x1, gsel, gate, sel, cnt = _merge(x, att, rec, wts, tm_merge)
    n = b * s
    nt = n // tm
    f = tm // tm_merge
    regroup = lambda a: a.reshape(nt, f, EXPERTS_PER_GROUP, tm_merge).transpose(0, 2, 1, 3).reshape(
        nt, EXPERTS_PER_GROUP, tm)
    gsel, gate, sel = gsel.reshape(nt, 1, tm), regroup(gate), regroup(sel)
    counts = cnt[:, :, 0].reshape(nt, f, N_EXPERTS).sum(axis=1).astype(I32)
    ech = CH_EXPERT
    tables, tails, chunk_expert, start_col, total_rows, umax = _routing_tables(counts, tm, rch, ech)
    x1 = x1.reshape(n, d)
    xs, gs = _dispatch(x1, gsel, sel, gate, start_col, tables, tails, total_rows, tm, rch, ech, umax)
    ys = _experts(xs, gs, chunk_expert, wts, ech)
    y = _combine(x1, gsel, sel, start_col, ys, tables, wts, tm, rch, umax)
    return y.reshape(b, s, d)


def _merge_and_ffn(x, att, rec, wts, tm_merge, tm_moe, ch):
    b, s, d = x.shape
    x1, gsel, gate, _, cnt = _merge(x, att, rec, wts, tm_merge)
    n = b * s
    nt = n // tm_moe
    f = tm_moe // tm_merge
    gsel = gsel.reshape(nt, 1, tm_moe)
    gate = gate.reshape(nt, f, EXPERTS_PER_GROUP, tm_merge).transpose(0, 2, 1, 3).reshape(nt, EXPERTS_PER_GROUP, tm_moe)
    per_expert = cnt[:, :, 0].reshape(nt, f, N_GROUPS, EXPERTS_PER_GROUP)
    counts = (per_expert.sum(axis=(1, 3)) / TOP_K).astype(I32)
    nch = ((counts + (ch - 1)) // ch).reshape(-1)
    y = _moe(x1.reshape(n, d), gsel, gate, nch, wts, tm_moe, ch)
    return y.reshape(b, s, d)


def _swap_halves(w):
    half = QK_ROPE // 2
    return jnp.concatenate([w[..., half:], w[..., :half]], axis=-1)


def _prep_weights(w_in, q_norm_g, w_uq, kv_norm_g, w_uk, w_uv, conv_w, conv_b, w_rg, b_rg, w_ig, b_ig,
                  lru_lambda, att_out_g, rec_out_g, w_out, ln1_g, ln1_b, w_group, b_group, w_expert,
                  b_expert, w_gate_up, w_down, ln2_g, ln2_b):
    d = w_in.shape[0]
    o1, o2, o3, o4 = Q_LORA, Q_LORA + KV_LORA, Q_LORA + KV_LORA + QK_ROPE, Q_LORA + KV_LORA + QK_ROPE + REC_WIDTH
    w_kpe = w_in[:, o2:o3]
    pad_lo = jnp.zeros((d, QK_NOPE), F32)
    pad_hi = jnp.zeros((d, HEAD_PAD - QK_NOPE - QK_ROPE), F32)
    w_in_ext = jnp.concatenate([
        w_in[:, :o2],
        pad_lo, w_kpe, pad_hi,
        pad_lo, _swap_halves(w_kpe), pad_hi,
        w_in[:, o3:o4], w_in[:, o4:],
    ], axis=1).astype(BF16)

    nope, pe = w_uq[..., :QK_NOPE], w_uq[..., QK_NOPE:]
    zq = lambda n: jnp.zeros((Q_LORA, N_HEADS, n), F32)
    q_main = jnp.concatenate([nope, pe, zq(HEAD_PAD - QK_NOPE - QK_ROPE)], axis=-1)
    q_swap = jnp.concatenate([zq(QK_NOPE), _swap_halves(pe), zq(HEAD_PAD - QK_NOPE - QK_ROPE)], axis=-1)
    w_uq_t = jnp.concatenate([q_main.reshape(Q_LORA, -1), q_swap.reshape(Q_LORA, -1)], axis=1).T.astype(BF16)

    k_pad = jnp.concatenate([w_uk, jnp.zeros((KV_LORA, N_HEADS, HEAD_PAD - QK_NOPE), F32)], axis=-1)
    w_uk_pad = k_pad.reshape(KV_LORA, -1).astype(BF16)
    w_uv_t = w_uv.reshape(KV_LORA, -1).T.astype(BF16)

    w_uk_t = jnp.concatenate([w_uk.transpose(1, 2, 0),
                              jnp.zeros((N_HEADS, HEAD_PAD - QK_NOPE, KV_LORA), F32)], axis=1).astype(BF16)
    sel = jnp.zeros((HEAD_PAD, QK_ROPE), F32).at[QK_NOPE + jnp.arange(QK_ROPE), jnp.arange(QK_ROPE)].set(1.0)
    w_uv_h = w_uv.transpose(1, 0, 2)
    zero_v = jnp.zeros((KV_LORA, V_HEAD), F32)
    w_uv_pair = jnp.stack([
        jnp.concatenate([jnp.concatenate([w_uv_h[2 * p], zero_v], axis=1),
                         jnp.concatenate([zero_v, w_uv_h[2 * p + 1]], axis=1)], axis=0)
        for p in range(N_PAIRS)]).astype(BF16)

    def block_diag(w):
        eye = jnp.eye(REC_BLOCKS, dtype=F32)
        return jnp.einsum('nde,nm->ndme', w, eye).reshape(REC_WIDTH, REC_WIDTH)

    bd_r, bd_i = block_diag(w_rg), block_diag(w_ig)
    half = REC_WIDTH // 2
    w_gate = jnp.stack([
        jnp.concatenate([bd_r[j * half:(j + 1) * half, j * half:(j + 1) * half],
                         bd_i[j * half:(j + 1) * half, j * half:(j + 1) * half]], axis=1)
        for j in range(2)]).astype(BF16)

    w_router_t = jnp.concatenate([w_group.T, jnp.zeros((SUBLANE - N_GROUPS, d), F32), w_expert.T], axis=0)
    w_router_hi = w_router_t.astype(BF16)
    w_router_lo = (w_router_t - w_router_hi.astype(F32)).astype(BF16)
    w_router_t = jnp.concatenate([w_router_hi, w_router_lo], axis=0)
    row = lambda v: v.reshape(1, -1)
    return {
        "w_in": w_in_ext, "q_norm_g": row(q_norm_g), "w_uq_t": w_uq_t, "kv_norm_g": row(kv_norm_g),
        "w_uk": w_uk_pad, "w_uv_t": w_uv_t, "w_uk_t": w_uk_t, "w_pe_sel": sel.astype(BF16), "w_uv_pair": w_uv_pair,
        "conv_w": conv_w, "conv_b": row(conv_b), "w_gate": w_gate, "b_rg": row(b_rg), "b_ig": row(b_ig),
        "lru_lambda": row(lru_lambda), "att_out_g": row(att_out_g), "rec_out_g": row(rec_out_g),
        "w_out": w_out.astype(BF16), "ln1_g": row(ln1_g), "ln1_b": row(ln1_b),
        "w_router_t": w_router_t, "b_group": b_group.reshape(-1, 1), "b_expert": b_expert.reshape(-1, 1),
        "w_gate_up": w_gate_up, "w_down": w_down,
        "ln2_g": row(ln2_g), "ln2_b": row(ln2_b),
    }


def _rope_tables(pos):
    half = QK_ROPE // 2
    inv = ROPE_THETA ** (-(jnp.arange(half, dtype=F32) * 2.0 / QK_ROPE))
    ang = pos.astype(F32)[:, None] * inv[None, :]
    cos, sin = jnp.cos(ang), jnp.sin(ang)
    t = pos.shape[0]
    cos_t = jnp.concatenate([jnp.ones((t, QK_NOPE), F32), cos, cos,
                             jnp.zeros((t, HEAD_PAD - QK_NOPE - QK_ROPE), F32)], axis=1)
    sin_t = jnp.concatenate([jnp.zeros((t, QK_NOPE), F32), -sin, sin,
                             jnp.zeros((t, HEAD_PAD - QK_NOPE - QK_ROPE), F32)], axis=1)
    return cos_t, sin_t, cos_t.T, sin_t.T


def kernel(x_prompt, x_sample, cache_kv_latent, cache_k_rope, state_conv, state_rec, page_table,
           w_in, q_norm_g, w_uq, kv_norm_g, w_uk, w_uv, conv_w, conv_b, w_rg, b_rg, w_ig, b_ig,
           lru_lambda, att_out_g, rec_out_g, w_out, ln1_g, ln1_b, w_group, b_group, w_expert,
           b_expert, w_gate_up, w_down, ln2_g, ln2_b):
    wts = _prep_weights(w_in, q_norm_g, w_uq, kv_norm_g, w_uk, w_uv, conv_w, conv_b, w_rg, b_rg, w_ig, b_ig,
                        lru_lambda, att_out_g, rec_out_g, w_out, ln1_g, ln1_b, w_group, b_group, w_expert,
                        b_expert, w_gate_up, w_down, ln2_g, ln2_b)
    bp, sp, d = x_prompt.shape
    bd, td, _ = x_sample.shape
    assert td == 1, "the sample path handles one new token per sequence"
    past_len = page_table.shape[1] * PAGE_SIZE
    ko = QK_NOPE

    qt, k, vt, c_p, kpe_blk, rx, rg = _project(x_prompt, _rope_tables(jnp.arange(sp, dtype=I32)), wts,
                                               min(T_ATT, sp))
    xs = x_sample.reshape(1, bd, d)
    qt_s, _, _, c_s, kpe_s_blk, rx_s, rg_s = _project(xs, _rope_tables(jnp.full((bd,), past_len, I32)), wts, bd)
    kpe_s = kpe_s_blk[0, :, ko:ko + QK_ROPE]
    qlat, qpe = _absorb(qt_s[0, :, 0], wts)

    att_p, o_lat = _attention(qt, k, vt, page_table, qlat.transpose(1, 0, 2), qpe.transpose(1, 0, 2),
                              c_s.reshape(bd, 1, KV_LORA), kpe_s.reshape(bd, 1, QK_ROPE),
                              cache_kv_latent, cache_k_rope.transpose(0, 2, 1))

    rec_p, h_p = _lru_prompt(rx, rg, wts, min(TM_LRU, sp))
    tm_moe = min(TM_MOE, bp * sp)
    y_p = _merge_and_routed_ffn(x_prompt, att_p, rec_p, wts, min(TM_MERGE, sp, tm_moe), min(TM_ROUTED, tm_moe),
                                CH_MOE)
    kpe_p = kpe_blk[..., ko:ko + QK_ROPE]
    conv_p = rx[:, sp - (CONV_W - 1):, :]

    att_s = _value_up(o_lat.reshape(bd, N_HEADS * KV_LORA), wts)
    rec_s, conv_s_t, h_s = _lru_step(rx_s[0], rg_s[0], state_conv.transpose(1, 0, 2), state_rec, wts)
    y_s = _merge_and_ffn(xs, att_s, rec_s.reshape(1, bd, REC_WIDTH), wts, bd, bd, min(CH_MOE, bd))

    return (y_p, y_s.reshape(bd, 1, d), c_p, kpe_p, conv_p, h_p.reshape(bp, REC_WIDTH),
            c_s.reshape(bd, 1, KV_LORA), kpe_s.reshape(bd, 1, QK_ROPE), conv_s_t.transpose(1, 0, 2), h_s)
```

```python
import functools
import math

import jax
import jax.numpy as jnp
from jax import lax
from jax.experimental import pallas as pl
from jax.experimental.pallas import tpu as pltpu

F32 = jnp.float32
BF16 = jnp.bfloat16
I32 = jnp.int32

N_HEADS = 8
QK_NOPE = 64
QK_ROPE = 32
V_HEAD = 64
Q_LORA = 384
KV_LORA = 256
ROPE_THETA = 10000.0
SM_SCALE = (QK_NOPE + QK_ROPE) ** -0.5
REC_WIDTH = 512
REC_BLOCKS = 8
REC_BLOCK_W = REC_WIDTH // REC_BLOCKS
CONV_W = 4
LRU_C = 8.0
N_GROUPS = 4
EXPERTS_PER_GROUP = 8
N_EXPERTS = N_GROUPS * EXPERTS_PER_GROUP
TOP_K = 2
D_EXPERT = 256
DEPTH = 1
ALPHA = (2.0 * DEPTH) ** 0.25
LN_EPS = 1e-5
RMS_EPS = 1e-6
NEG_INF = -1e30
PAGE_SIZE = 128

LANE = 128
SUBLANE = 8
HEAD_PAD = LANE
N_PAIRS = N_HEADS * V_HEAD // LANE
VMEM_LIMIT = 56 * 1024 * 1024

T_ATT = 512
KEY_STRIP = 512
TM_LRU = 256
TM_MERGE = 512
TM_MOE = 1024
TM_ROUTED = 512
CH_EXPERT = 512
CH_MOE = 256
ROW_UNIT = 16
PAGES_PER_CHUNK = 32
CACHE_SLOTS = 3
EXP2_SCALE = SM_SCALE * math.log2(math.e)


def _dot(a, b):
    return jnp.dot(a, b, preferred_element_type=F32)


def _dot_nt(a, b, precision=None):
    return lax.dot_general(a, b, (((1,), (1,)), ((), ())), preferred_element_type=F32, precision=precision)


def _dot_tn(a, b):
    return lax.dot_general(a, b, (((0,), (0,)), ((), ())), preferred_element_type=F32)


def _cparams(semantics, flags=None):
    return pltpu.CompilerParams(dimension_semantics=semantics, vmem_limit_bytes=VMEM_LIMIT, flags=flags)


def _rmsnorm(x, g):
    return x * lax.rsqrt(jnp.mean(x * x, axis=-1, keepdims=True) + RMS_EPS) * g


def _layernorm(x, g, b):
    mu = jnp.mean(x, axis=-1, keepdims=True)
    xc = x - mu
    var = jnp.mean(xc * xc, axis=-1, keepdims=True)
    return xc * lax.rsqrt(var + LN_EPS) * g + b


def _gelu_tanh(x):
    return x * (0.5 * (1.0 + jnp.tanh(math.sqrt(2.0 / math.pi) * (x + 0.044715 * (x * x * x)))))


def _lru_coeffs(xc, pre_r, pre_i, b_r, b_i, lam):
    r = jax.nn.sigmoid(pre_r + b_r)
    i = jax.nn.sigmoid(pre_i + b_i)
    neg_lam = -lam
    softplus = jnp.maximum(neg_lam, 0.0) + jnp.log1p(jnp.exp(-jnp.abs(neg_lam)))
    log_a = (-LRU_C * softplus) * r
    a = jnp.exp(log_a)
    u = jnp.sqrt(-jnp.tanh(log_a) * (a * a + 1.0)) * (i * xc)
    return a, u


def _gate_preacts(xc, wg_ref):
    half = REC_WIDTH // 2
    g0 = _dot(xc[:, :half].astype(BF16), wg_ref[0])
    g1 = _dot(xc[:, half:].astype(BF16), wg_ref[1])
    pre_r = jnp.concatenate([g0[:, :half], g1[:, :half]], axis=1)
    pre_i = jnp.concatenate([g0[:, half:], g1[:, half:]], axis=1)
    return pre_r, pre_i


def _proj_kernel(x_ref, cos_ref, sin_ref, cost_ref, sint_ref, win_ref, qg_ref, wuqt_ref, kvg_ref, wuk_ref, wuvt_ref,
                 qt_ref, k_ref, vt_ref, ckv_ref, kpe_ref, rx_ref, rg_ref):
    x = x_ref[0].astype(BF16)
    z = _dot(x, win_ref[...])
    o_kv = Q_LORA
    o_ka = o_kv + KV_LORA
    o_kb = o_ka + LANE
    o_rx = o_kb + LANE
    o_rg = o_rx + REC_WIDTH
    qn = _rmsnorm(z[:, :o_kv], qg_ref[...]).astype(BF16)
    qq = _dot_nt(wuqt_ref[...], qn)
    cos_t = cost_ref[...]
    sin_t = sint_ref[...]
    sw = N_HEADS * HEAD_PAD
    for h in range(N_HEADS):
        lo = h * HEAD_PAD
        q_rot = qq[lo:lo + HEAD_PAD] * cos_t + qq[sw + lo:sw + lo + HEAD_PAD] * sin_t
        qt_ref[0, h, 0] = (q_rot * EXP2_SCALE).astype(BF16)
    ckv = _rmsnorm(z[:, o_kv:o_ka], kvg_ref[...])
    ckv_ref[0] = ckv
    kpe = z[:, o_ka:o_kb] * cos_ref[...] + z[:, o_kb:o_rx] * sin_ref[...]
    kpe_ref[0] = kpe
    ckv_b = ckv.astype(BF16)
    kn = _dot(ckv_b, wuk_ref[...])
    for h in range(N_HEADS):
        lo = h * HEAD_PAD
        k_ref[0, h] = (kn[:, lo:lo + HEAD_PAD] + kpe).astype(BF16)
    vt = _dot_nt(wuvt_ref[...], ckv_b)
    for p in range(N_PAIRS):
        vt_ref[0, p, 0] = vt[p * LANE:(p + 1) * LANE].astype(BF16)
    rx_ref[0] = z[:, o_rx:o_rg]
    rg_ref[0] = z[:, o_rg:o_rg + REC_WIDTH]


def _project(x, tables, wts, tm):
    b, s, d = x.shape
    nt = s // tm
    in_w = wts["w_in"].shape[1]
    full = lambda shape: pl.BlockSpec(shape, lambda bi, si: (0,) * len(shape))
    cos_n, sin_n, cos_t, sin_t = tables
    return pl.pallas_call(
        _proj_kernel,
        grid=(b, nt),
        in_specs=[
            pl.BlockSpec((1, tm, d), lambda bi, si: (bi, si, 0)),
            pl.BlockSpec((tm, LANE), lambda bi, si: (si, 0)),
            pl.BlockSpec((tm, LANE), lambda bi, si: (si, 0)),
            pl.BlockSpec((HEAD_PAD, tm), lambda bi, si: (0, si)),
            pl.BlockSpec((HEAD_PAD, tm), lambda bi, si: (0, si)),
            full((d, in_w)),
            full((1, Q_LORA)),
            full((2 * N_HEADS * HEAD_PAD, Q_LORA)),
            full((1, KV_LORA)),
            full((KV_LORA, N_HEADS * HEAD_PAD)),
            full((N_PAIRS * LANE, KV_LORA)),
        ],
        out_specs=[
            pl.BlockSpec((1, N_HEADS, 1, HEAD_PAD, tm), lambda bi, si: (bi, 0, si, 0, 0)),
            pl.BlockSpec((1, N_HEADS, tm, HEAD_PAD), lambda bi, si: (bi, 0, si, 0)),
            pl.BlockSpec((1, N_PAIRS, 1, LANE, tm), lambda bi, si: (bi, 0, si, 0, 0)),
            pl.BlockSpec((1, tm, KV_LORA), lambda bi, si: (bi, si, 0)),
            pl.BlockSpec((1, tm, LANE), lambda bi, si: (bi, si, 0)),
            pl.BlockSpec((1, tm, REC_WIDTH), lambda bi, si: (bi, si, 0)),
            pl.BlockSpec((1, tm, REC_WIDTH), lambda bi, si: (bi, si, 0)),
        ],
        out_shape=[
            jax.ShapeDtypeStruct((b, N_HEADS, nt, HEAD_PAD, tm), BF16),
            jax.ShapeDtypeStruct((b, N_HEADS, s, HEAD_PAD), BF16),
            jax.ShapeDtypeStruct((b, N_PAIRS, nt, LANE, tm), BF16),
            jax.ShapeDtypeStruct((b, s, KV_LORA), F32),
            jax.ShapeDtypeStruct((b, s, LANE), F32),
            jax.ShapeDtypeStruct((b, s, REC_WIDTH), F32),
            jax.ShapeDtypeStruct((b, s, REC_WIDTH), F32),
        ],
        compiler_params=_cparams(("parallel", "parallel")),
        name="proj",
    )(x, cos_n, sin_n, cos_t, sin_t, wts["w_in"], wts["q_norm_g"], wts["w_uq_t"], wts["kv_norm_g"],
      wts["w_uk"], wts["w_uv_t"])


def _lru_prompt_kernel(rx_ref, rg_ref, cw_ref, cb_ref, wg_ref, br_ref, bi_ref, lam_ref,
                       y_ref, hlast_ref, xp_ref, h_ref):
    tm = rx_ref.shape[1]
    si = pl.program_id(1)

    @pl.when(si == 0)
    def _():
        xp_ref[0:SUBLANE, :] = jnp.zeros((SUBLANE, REC_WIDTH), F32)
        h_ref[...] = jnp.zeros_like(h_ref)

    x = rx_ref[0]
    xp_ref[SUBLANE:SUBLANE + tm, :] = x
    xc = cb_ref[...] + x * cw_ref[CONV_W - 1:CONV_W, :]
    for m in range(1, CONV_W):
        xc = xc + xp_ref[pl.ds(SUBLANE - m, tm), :] * cw_ref[CONV_W - 1 - m:CONV_W - m, :]
    xp_ref[0:SUBLANE, :] = x[tm - SUBLANE:, :]

    pre_r, pre_i = _gate_preacts(xc, wg_ref)
    a, u = _lru_coeffs(xc, pre_r, pre_i, br_ref[...], bi_ref[...], lam_ref[...])

    row_in_group = lax.broadcasted_iota(I32, (tm, REC_WIDTH), 0) % SUBLANE
    d = 1
    while d < SUBLANE:
        keep = row_in_group >= d
        a_sh = jnp.where(keep, pltpu.roll(a, d, 0), 1.0)
        u_sh = jnp.where(keep, pltpu.roll(u, d, 0), 0.0)
        u = u + a * u_sh
        a = a * a_sh
        d *= 2
    carry = h_ref[...]
    groups = []
    for k in range(tm // SUBLANE):
        rows = slice(k * SUBLANE, (k + 1) * SUBLANE)
        hk = a[rows] * carry + u[rows]
        groups.append(hk)
        carry = hk[SUBLANE - 1:SUBLANE, :]
    h = jnp.concatenate(groups, axis=0)
    h_ref[...] = carry
    hlast_ref[0] = carry
    y_ref[0] = h * _gelu_tanh(rg_ref[0])


def _lru_prompt(rx, rg, wts, tm):
    b, s, w = rx.shape
    full = lambda shape: pl.BlockSpec(shape, lambda bi, si: (0,) * len(shape))
    return pl.pallas_call(
        _lru_prompt_kernel,
        grid=(b, s // tm),
        in_specs=[
            pl.BlockSpec((1, tm, w), lambda bi, si: (bi, si, 0)),
            pl.BlockSpec((1, tm, w), lambda bi, si: (bi, si, 0)),
            full((CONV_W, w)), full((1, w)), full((2, w // 2, w)), full((1, w)), full((1, w)), full((1, w)),
        ],
        out_specs=[
            pl.BlockSpec((1, tm, w), lambda bi, si: (bi, si, 0)),
            pl.BlockSpec((1, 1, w), lambda bi, si: (bi, 0, 0)),
        ],
        out_shape=[jax.ShapeDtypeStruct((b, s, w), F32), jax.ShapeDtypeStruct((b, 1, w), F32)],
        scratch_shapes=[pltpu.VMEM((tm + SUBLANE, w), F32), pltpu.VMEM((1, w), F32)],
        compiler_params=_cparams(("arbitrary", "arbitrary")),
        name="lru_prompt",
    )(rx, rg, wts["conv_w"], wts["conv_b"], wts["w_gate"], wts["b_rg"], wts["b_ig"], wts["lru_lambda"])


def _lru_step_kernel(rx_ref, rg_ref, conv_ref, h0_ref, cw_ref, cb_ref, wg_ref, br_ref, bi_ref, lam_ref,
                     y_ref, newconv_ref, h_ref):
    x = rx_ref[...]
    xc = cb_ref[...] + x * cw_ref[CONV_W - 1:CONV_W, :]
    for k in range(CONV_W - 1):
        xc = xc + conv_ref[k] * cw_ref[k:k + 1, :]
    pre_r, pre_i = _gate_preacts(xc, wg_ref)
    a, u = _lru_coeffs(xc, pre_r, pre_i, br_ref[...], bi_ref[...], lam_ref[...])
    h = a * h0_ref[...] + u
    h_ref[...] = h
    y_ref[...] = h * _gelu_tanh(rg_ref[...])
    for k in range(CONV_W - 2):
        newconv_ref[k] = conv_ref[k + 1]
    newconv_ref[CONV_W - 2] = x


def _lru_step(rx, rg, conv_t, h0, wts):
    n, w = rx.shape
    return pl.pallas_call(
        _lru_step_kernel,
        out_shape=[jax.ShapeDtypeStruct((n, w), F32),
                   jax.ShapeDtypeStruct((CONV_W - 1, n, w), F32),
                   jax.ShapeDtypeStruct((n, w), F32)],
        name="lru_step",
    )(rx, rg, conv_t, h0, wts["conv_w"], wts["conv_b"], wts["w_gate"], wts["b_rg"], wts["b_ig"], wts["lru_lambda"])


def _fused_attn_kernel(pt_ref, qt_ref, k_ref, vt_ref, qlat_ref, qpe_ref, cnew_ref, knew_ref, ckv_hbm, kr_hbm,
                       o_ref, olat_ref, cbuf, kbuf, sem, ms_ref, ls_ref, accs_ref,
                       *, n_batch, n_q, n_chunks, total_chunks):
    t = qt_ref.shape[-1]
    bi, pi, qi = pl.program_id(0), pl.program_id(1), pl.program_id(2)
    steps_per_group = n_q * (n_q + 1) // 2
    total_steps = n_batch * N_PAIRS * steps_per_group
    base = (bi * N_PAIRS + pi) * steps_per_group + (qi * (qi + 1)) // 2
    ks = KEY_STRIP
    key_i = lax.broadcasted_iota(I32, (ks, t), 0)
    qry_i = lax.broadcasted_iota(I32, (ks, t), 1)
    qts = [qt_ref[0, e, 0] for e in range(2)]

    cp = PAGES_PER_CHUNK
    ahead = CACHE_SLOTS - 1

    def copies(g, slot):
        out = []
        for p in range(cp):
            page = pt_ref[g * cp + p]
            rows = pl.ds(p * PAGE_SIZE, PAGE_SIZE)
            out.append(pltpu.make_async_copy(ckv_hbm.at[page], cbuf.at[slot, rows], sem.at[0, slot]))
            out.append(pltpu.make_async_copy(kr_hbm.at[page], kbuf.at[slot, :, rows], sem.at[1, slot]))
        return out

    def start(g, slot):
        for c in copies(g, slot):
            c.start()

    @pl.when(jnp.logical_and(jnp.logical_and(bi == 0, pi == 0), qi == 0))
    def _():
        for g0 in range(min(ahead, total_chunks)):
            start(g0, g0)

    def sample_chunk_stages(g):
        v = {}

        def fetch_and_score():
            @pl.when(g + ahead < total_chunks)
            def _():
                start(g + ahead, lax.rem(g + ahead, CACHE_SLOTS))

            slot = lax.rem(g, CACHE_SLOTS)
            for c in copies(g, slot):
                c.wait()
            v["b"] = lax.div(g, n_chunks)
            v["qlat"] = qlat_ref[v["b"]]
            v["qpe"] = qpe_ref[v["b"]]
            v["cb"] = cbuf[slot].astype(BF16)
            kb = kbuf[slot].astype(BF16)
            v["s"] = _dot_nt(v["qlat"].astype(BF16), v["cb"]) + _dot(v["qpe"].astype(BF16), kb)

        def softmax_and_values():
            first = lax.rem(g, n_chunks) == 0
            m = jnp.where(first, NEG_INF, ms_ref[...])
            l = jnp.where(first, 0.0, ls_ref[...])
            acc = jnp.where(first, 0.0, accs_ref[...])
            s = v["s"]
            m_new = jnp.maximum(m, jnp.max(s, axis=-1, keepdims=True))
            alpha = jnp.exp2((m - m_new))
            p = jnp.exp2((s - m_new))
            v["l"] = alpha * l + jnp.sum(p, axis=-1, keepdims=True)
            v["acc"] = alpha * acc + _dot(p.astype(BF16), v["cb"])
            v["m"] = m_new

        def finish():
            m_new, l, acc, b = v["m"], v["l"], v["acc"], v["b"]
            ms_ref[...] = m_new
            ls_ref[...] = l
            accs_ref[...] = acc
            cnew = cnew_ref[b]
            knew = knew_ref[b]
            s_new = (jnp.sum(v["qlat"] * cnew, axis=-1, keepdims=True)
                     + jnp.sum(v["qpe"] * knew, axis=-1, keepdims=True))
            m_fin = jnp.maximum(m_new, s_new)
            a_fin = jnp.exp2((m_new - m_fin))
            p_new = jnp.exp2((s_new - m_fin))
            olat_ref[b] = (a_fin * acc + p_new * cnew) / (a_fin * l + p_new)

        return [fetch_and_score, softmax_and_values, finish]

    def scores(j, r, e):
        kb = k_ref[0, e, pl.ds(pl.multiple_of(j * t + r * ks, ks), ks), :]
        return _dot(kb, qts[e])

    def step(j, carry, diagonal, with_chunk):
        stages = sample_chunk_stages(base + j) if with_chunk else []
        carry = list(carry)
        units = [(r, e) for r in range(t // ks) for e in range(2)]
        if stages:
            stages.pop(0)()
        st_next = scores(j, *units[0])
        for u, (r, e) in enumerate(units):
            st = st_next
            if u + 1 < len(units):
                st_next = scores(j, *units[u + 1])
            if stages and u > 0:
                stages.pop(0)()
            m, l, acc = carry[e]
            vb = vt_ref[0, 0, j, e * V_HEAD:(e + 1) * V_HEAD, r * ks:(r + 1) * ks]
            if diagonal:
                st = jnp.where(key_i + r * ks <= qry_i, st, NEG_INF)
            m_new = jnp.maximum(m, jnp.max(st, axis=0, keepdims=True))
            alpha = jnp.exp2((m - m_new))
            pt = jnp.exp2((st - m_new))
            l = alpha * l + jnp.sum(pt, axis=0, keepdims=True)
            acc = alpha * acc + _dot(vb, pt.astype(BF16))
            carry[e] = (m_new, l, acc)
        for stage in stages:
            stage()
        return tuple(carry)

    init = (jnp.full((1, t), NEG_INF, F32), jnp.zeros((1, t), F32), jnp.zeros((V_HEAD, t), F32))
    n_with = jnp.clip(total_chunks - base, 0, qi)
    carry = lax.fori_loop(0, n_with, functools.partial(step, diagonal=False, with_chunk=True), (init, init))
    carry = lax.fori_loop(n_with, qi, functools.partial(step, diagonal=False, with_chunk=False), carry)
    carry = lax.cond(base + qi < total_chunks,
                     functools.partial(step, qi, diagonal=True, with_chunk=True),
                     functools.partial(step, qi, diagonal=True, with_chunk=False), carry)
    halves = [acc / l for (_, l, acc) in carry]
    o_ref[0, 0] = jnp.concatenate(halves, axis=0).T

    if total_chunks > total_steps:
        @pl.when(jnp.logical_and(jnp.logical_and(bi == n_batch - 1, pi == N_PAIRS - 1), qi == n_q - 1))
        def _():
            def drain(g, c):
                for stage in sample_chunk_stages(g):
                    stage()
                return c
            lax.fori_loop(total_steps, total_chunks, drain, 0)


def _attention(qt, k, vt, page_table, qlat, qpe, c_new, k_new, cache_kv, cache_kr_t):
    b, _, nt, _, t = qt.shape
    s = nt * t
    bd, n_pages = page_table.shape
    n_chunks = n_pages // PAGES_PER_CHUNK
    rows = PAGES_PER_CHUNK * PAGE_SIZE
    kern = functools.partial(_fused_attn_kernel, n_batch=b, n_q=nt, n_chunks=n_chunks, total_chunks=bd * n_chunks)
    whole = lambda shape: pl.BlockSpec(shape, lambda bi, pi, qi, pt: (0,) * len(shape))
    grid_spec = pltpu.PrefetchScalarGridSpec(
        num_scalar_prefetch=1,
        grid=(b, N_PAIRS, nt),
        in_specs=[
            pl.BlockSpec((1, 2, 1, HEAD_PAD, t), lambda bi, pi, qi, pt: (bi, pi, qi, 0, 0)),
            pl.BlockSpec((1, 2, s, HEAD_PAD), lambda bi, pi, qi, pt: (bi, pi, 0, 0)),
            pl.BlockSpec((1, 1, nt, LANE, t), lambda bi, pi, qi, pt: (bi, pi, 0, 0, 0)),
            whole((bd, N_HEADS, KV_LORA)),
            whole((bd, N_HEADS, QK_ROPE)),
            whole((bd, 1, KV_LORA)),
            whole((bd, 1, QK_ROPE)),
            pl.BlockSpec(memory_space=pl.ANY),
            pl.BlockSpec(memory_space=pl.ANY),
        ],
        out_specs=[
            pl.BlockSpec((1, 1, t, LANE), lambda bi, pi, qi, pt: (bi, pi, qi, 0)),
            whole((bd, N_HEADS, KV_LORA)),
        ],
        scratch_shapes=[
            pltpu.VMEM((CACHE_SLOTS, rows, KV_LORA), F32),
            pltpu.VMEM((CACHE_SLOTS, QK_ROPE, rows), F32),
            pltpu.SemaphoreType.DMA((2, CACHE_SLOTS)),
            pltpu.VMEM((N_HEADS, 1), F32),
            pltpu.VMEM((N_HEADS, 1), F32),
            pltpu.VMEM((N_HEADS, KV_LORA), F32),
        ],
    )
    return pl.pallas_call(
        kern,
        grid_spec=grid_spec,
        out_shape=[jax.ShapeDtypeStruct((b, N_PAIRS, s, LANE), F32),
                   jax.ShapeDtypeStruct((bd, N_HEADS, KV_LORA), F32)],
        compiler_params=_cparams(("arbitrary", "arbitrary", "arbitrary")),
        name="attn_fused",
    )(page_table.reshape(-1), qt, k, vt, qlat, qpe, c_new, k_new, cache_kv, cache_kr_t)


def _absorb_kernel(qt_ref, wlat_ref, wpe_ref, qlat_ref, qpe_ref):
    for h in range(N_HEADS):
        qt = qt_ref[h]
        qlat_ref[h] = _dot_tn(qt, wlat_ref[h])
        qpe_ref[h] = _dot_tn(qt, wpe_ref[...])


def _absorb(qt, wts):
    _, _, n = qt.shape
    return pl.pallas_call(
        _absorb_kernel,
        out_shape=[jax.ShapeDtypeStruct((N_HEADS, n, KV_LORA), F32),
                   jax.ShapeDtypeStruct((N_HEADS, n, QK_ROPE), F32)],
        name="absorb_q",
    )(qt, wts["w_uk_t"], wts["w_pe_sel"])


def _value_up_kernel(olat_ref, wv_ref, o_ref):
    w = 2 * KV_LORA
    for p in range(N_PAIRS):
        o_ref[0, p] = _dot(olat_ref[:, p * w:(p + 1) * w].astype(BF16), wv_ref[p])


def _value_up(olat2d, wts):
    n = olat2d.shape[0]
    return pl.pallas_call(
        _value_up_kernel,
        out_shape=jax.ShapeDtypeStruct((1, N_PAIRS, n, LANE), F32),
        name="value_up",
    )(olat2d, wts["w_uv_pair"])


def _merge_kernel(x_ref, att_ref, rec_ref, ag_ref, rgn_ref, wout_ref, g1_ref, b1_ref, wr_ref, bg_ref, be_ref,
                  x1_ref, gsel_ref, gate_ref, sel_ref, cnt_ref):
    tm = x_ref.shape[1]
    att = [att_ref[0, p] for p in range(N_PAIRS)]
    ss = att[0] * att[0]
    for p in range(1, N_PAIRS):
        ss = ss + att[p] * att[p]
    inv = lax.rsqrt(jnp.sum(ss, axis=-1, keepdims=True) / (N_PAIRS * LANE) + RMS_EPS)
    parts = [(att[p] * inv * ag_ref[:, p * LANE:(p + 1) * LANE]).astype(BF16) for p in range(N_PAIRS)]
    parts.append(_rmsnorm(rec_ref[0], rgn_ref[...]).astype(BF16))
    mixed = jnp.concatenate(parts, axis=-1)
    mix = _dot(mixed, wout_ref[...])
    x1 = _layernorm(ALPHA * x_ref[0] + mix, g1_ref[...], b1_ref[...])
    x1_ref[0] = x1

    n_r = wr_ref.shape[0] // 2
    x_hi = x1.astype(BF16)
    x_lo = (x1 - x_hi.astype(F32)).astype(BF16)
    both = _dot_nt(wr_ref[...], x_hi)
    lt = (both[:n_r] + both[n_r:]) + _dot_nt(wr_ref[0:n_r, :], x_lo)
    g = [lt[k:k + 1, :] for k in range(N_GROUPS)]
    gmax = functools.reduce(jnp.maximum, g)
    ex = [jnp.exp(gk - gmax) for gk in g]
    den = functools.reduce(lambda p, q: p + q, ex)
    best = g[0] + bg_ref[0:1, :]
    idx = jnp.zeros((1, tm), I32)
    for k in range(1, N_GROUPS):
        cand = g[k] + bg_ref[k:k + 1, :]
        upd = cand > best
        idx = jnp.where(upd, k, idx)
        best = jnp.where(upd, cand, best)
    gp = ex[0]
    e_sel = lt[SUBLANE:SUBLANE + EXPERTS_PER_GROUP, :]
    e_bias = jnp.broadcast_to(be_ref[0:EXPERTS_PER_GROUP, :], (EXPERTS_PER_GROUP, tm))
    for k in range(1, N_GROUPS):
        hit = idx == k
        lo = SUBLANE + k * EXPERTS_PER_GROUP
        gp = jnp.where(hit, ex[k], gp)
        e_sel = jnp.where(hit, lt[lo:lo + EXPERTS_PER_GROUP, :], e_sel)
        e_bias = jnp.where(hit, be_ref[k * EXPERTS_PER_GROUP:(k + 1) * EXPERTS_PER_GROUP, :], e_bias)
    g_prob = gp / den
    sc = e_sel + e_bias
    sub = lax.broadcasted_iota(I32, (EXPERTS_PER_GROUP, tm), 0)
    m1 = jnp.max(sc, axis=0, keepdims=True)
    i1 = jnp.min(jnp.where(sc == m1, sub, EXPERTS_PER_GROUP), axis=0, keepdims=True)
    mask1 = sub == i1
    sc2 = jnp.where(mask1, -jnp.inf, sc)
    m2 = jnp.max(sc2, axis=0, keepdims=True)
    i2 = jnp.min(jnp.where(sc2 == m2, sub, EXPERTS_PER_GROUP), axis=0, keepdims=True)
    mask2 = sub == i2
    v1 = jnp.sum(jnp.where(mask1, e_sel, 0.0), axis=0, keepdims=True)
    v2 = jnp.sum(jnp.where(mask2, e_sel, 0.0), axis=0, keepdims=True)
    vm = jnp.maximum(v1, v2)
    e1 = jnp.exp(v1 - vm)
    e2 = jnp.exp(v2 - vm)
    esum = e1 + e2
    gate = g_prob * (jnp.where(mask1, e1 / esum, 0.0) + jnp.where(mask2, e2 / esum, 0.0))
    gsel_ref[0] = idx
    gate_ref[0] = gate
    sel = jnp.where(jnp.logical_or(mask1, mask2), 1.0, 0.0)
    sel_ref[0] = sel
    for k in range(N_GROUPS):
        ck = jnp.sum(jnp.where(idx == k, sel, 0.0), axis=-1, keepdims=True)
        cnt_ref[0, k * EXPERTS_PER_GROUP:(k + 1) * EXPERTS_PER_GROUP, :] = jnp.broadcast_to(
            ck, (EXPERTS_PER_GROUP, LANE))


def _merge(x, att, rec, wts, tm):
    b, s, d = x.shape
    nt = s // tm
    mw = wts["w_out"].shape[0]
    n_r = wts["w_router_t"].shape[0]
    full = lambda shape: pl.BlockSpec(shape, lambda bi, si: (0,) * len(shape))
    return pl.pallas_call(
        _merge_kernel,
        grid=(b, nt),
        in_specs=[
            pl.BlockSpec((1, tm, d), lambda bi, si: (bi, si, 0)),
            pl.BlockSpec((1, N_PAIRS, tm, LANE), lambda bi, si: (bi, 0, si, 0)),
            pl.BlockSpec((1, tm, REC_WIDTH), lambda bi, si: (bi, si, 0)),
            full((1, N_PAIRS * LANE)), full((1, REC_WIDTH)), full((mw, d)), full((1, d)), full((1, d)),
            full((n_r, d)), full((N_GROUPS, 1)), full((N_GROUPS * EXPERTS_PER_GROUP, 1)),
        ],
        out_specs=[
            pl.BlockSpec((1, tm, d), lambda bi, si: (bi, si, 0)),
            pl.BlockSpec((1, 1, tm), lambda bi, si: (bi * nt + si, 0, 0)),
            pl.BlockSpec((1, EXPERTS_PER_GROUP, tm), lambda bi, si: (bi * nt + si, 0, 0)),
            pl.BlockSpec((1, EXPERTS_PER_GROUP, tm), lambda bi, si: (bi * nt + si, 0, 0)),
            pl.BlockSpec((1, N_EXPERTS, LANE), lambda bi, si: (bi * nt + si, 0, 0)),
        ],
        out_shape=[
            jax.ShapeDtypeStruct((b, s, d), F32),
            jax.ShapeDtypeStruct((b * nt, 1, tm), I32),
            jax.ShapeDtypeStruct((b * nt, EXPERTS_PER_GROUP, tm), F32),
            jax.ShapeDtypeStruct((b * nt, EXPERTS_PER_GROUP, tm), F32),
            jax.ShapeDtypeStruct((b * nt, N_EXPERTS, LANE), F32),
        ],
        compiler_params=_cparams(("parallel", "parallel")),
        name="merge_router",
    )(x, att, rec, wts["att_out_g"], wts["rec_out_g"], wts["w_out"], wts["ln1_g"], wts["ln1_b"],
      wts["w_router_t"], wts["b_group"], wts["b_expert"])


def _moe_kernel(nch_ref, x1_ref, gsel_ref, gate_ref, wgu_ref, wd_ref, g2_ref, b2_ref, o_ref, xb_ref, tri_ref,
                *, ch):
    tm = x1_ref.shape[0]
    ti = pl.program_id(0)
    gi = pl.program_id(1)
    hi = pl.program_id(2)
    n_half = wgu_ref.shape[2]

    @pl.when(jnp.logical_and(jnp.logical_and(ti == 0, gi == 0), hi == 0))
    def _():
        _build_tri(tri_ref)

    @pl.when(jnp.logical_and(gi == 0, hi == 0))
    def _():
        xb_ref[...] = x1_ref[...].astype(BF16)
        o_ref[...] = jnp.zeros_like(o_ref)

    in_group = gsel_ref[0] == gi
    member = jnp.broadcast_to(jnp.where(in_group, 1.0, 0.0), (SUBLANE, tm)).astype(BF16)
    before = _dot(member, tri_ref[...])
    rank = jnp.where(in_group, before[0:1, :].astype(I32), -1)
    gate = gate_ref[0]
    g_hi = gate.astype(BF16).astype(F32)
    g_mid = (gate - g_hi).astype(BF16).astype(F32)
    g_lo = (gate - g_hi) - g_mid
    n_terms = 3
    gate_terms = jnp.concatenate(
        [g_hi, g_mid, g_lo, jnp.zeros((LANE - n_terms * EXPERTS_PER_GROUP, tm), F32)], axis=0).astype(BF16)

    def chunk(c, carry, first_expert):
        slot_id = lax.broadcasted_iota(I32, (ch, tm), 0) + c * ch
        onehot_b = jnp.where(slot_id == rank, 1.0, 0.0).astype(BF16)
        xg = _dot(onehot_b, xb_ref[...]).astype(BF16)
        gt = _dot_nt(onehot_b, gate_terms)
        gc = gt
        for k in range(1, n_terms):
            gc = gc + pltpu.roll(gt, LANE - k * EXPERTS_PER_GROUP, 1)
        acc = jnp.zeros((ch, o_ref.shape[1]), F32)
        for jj in range(n_half):
            j = first_expert + jj
            gu = _dot(xg, wgu_ref[0, 0, jj].astype(BF16))
            hid = jax.nn.silu(gu[:, :D_EXPERT]) * gu[:, D_EXPERT:]
            acc = acc + gc[:, j:j + 1] * _dot(hid.astype(BF16), wd_ref[0, 0, jj].astype(BF16))
        o_ref[...] += _dot_tn(onehot_b, acc.astype(BF16))
        return carry

    for half in range(EXPERTS_PER_GROUP // n_half):
        @pl.when(hi == half)
        def _(half=half):
            lax.fori_loop(0, nch_ref[ti * N_GROUPS + gi], functools.partial(chunk, first_expert=half * n_half), 0)

    @pl.when(jnp.logical_and(gi == N_GROUPS - 1, hi == pl.num_programs(2) - 1))
    def _():
        o_ref[...] = _layernorm(ALPHA * x1_ref[...] + o_ref[...], g2_ref[...], b2_ref[...])


def _moe(x1, gsel, gate, nch, wts, tm, ch):
    n, d = x1.shape
    nt = n // tm
    e2 = 2 * D_EXPERT
    halves = 2
    per_half = EXPERTS_PER_GROUP // halves
    w_gu = wts["w_gate_up"].reshape(N_GROUPS, halves, per_half, d, e2)
    w_dn = wts["w_down"].reshape(N_GROUPS, halves, per_half, D_EXPERT, d)
    grid_spec = pltpu.PrefetchScalarGridSpec(
        num_scalar_prefetch=1,
        grid=(nt, N_GROUPS, halves),
        in_specs=[
            pl.BlockSpec((tm, d), lambda ti, gi, hi, nc: (ti, 0)),
            pl.BlockSpec((1, 1, tm), lambda ti, gi, hi, nc: (ti, 0, 0)),
            pl.BlockSpec((1, EXPERTS_PER_GROUP, tm), lambda ti, gi, hi, nc: (ti, 0, 0)),
            pl.BlockSpec((1, 1, per_half, d, e2), lambda ti, gi, hi, nc: (gi, hi, 0, 0, 0)),
            pl.BlockSpec((1, 1, per_half, D_EXPERT, d), lambda ti, gi, hi, nc: (gi, hi, 0, 0, 0)),
            pl.BlockSpec((1, d), lambda ti, gi, hi, nc: (0, 0)),
            pl.BlockSpec((1, d), lambda ti, gi, hi, nc: (0, 0)),
        ],
        out_specs=pl.BlockSpec((tm, d), lambda ti, gi, hi, nc: (ti, 0)),
        scratch_shapes=[pltpu.VMEM((tm, d), BF16), pltpu.VMEM((tm, tm), BF16)],
    )
    return pl.pallas_call(
        functools.partial(_moe_kernel, ch=ch),
        grid_spec=grid_spec,
        out_shape=jax.ShapeDtypeStruct((n, d), F32),
        compiler_params=_cparams(("arbitrary", "arbitrary", "arbitrary")),
        name="moe",
    )(nch, x1, gsel, gate, w_gu, w_dn, wts["ln2_g"], wts["ln2_b"])


def _build_tri(tri_ref):
    tm = tri_ref.shape[0]
    r = lax.broadcasted_iota(I32, (tm, tm), 0)
    c = lax.broadcasted_iota(I32, (tm, tm), 1)
    tri_ref[...] = jnp.where(r < c, 1.0, 0.0).astype(BF16)


def _sorted_positions(gsel, sel, start_col, tri_ref):
    member = jnp.concatenate([jnp.where(gsel == g, sel, 0.0) for g in range(N_GROUPS)], axis=0)
    before = _dot(member.astype(BF16), tri_ref[...])
    routed = member > 0.5
    pos = jnp.where(routed, start_col + before, -1.0)
    pos_a = jnp.max(pos, axis=0, keepdims=True)
    pos_b = jnp.sum(jnp.where(routed, start_col + before, 0.0), axis=0, keepdims=True) - pos_a
    return pos, pos_a.astype(I32), pos_b.astype(I32)


def _exact_terms(v):
    hi = v.astype(BF16).astype(F32)
    mid = (v - hi).astype(BF16).astype(F32)
    lo = (v - hi) - mid
    row = lax.broadcasted_iota(I32, (LANE, v.shape[1]), 0)
    return jnp.where(row == 0, hi, jnp.where(row == 1, mid, jnp.where(row == 2, lo, 0.0))).astype(BF16)


def _dispatch_kernel(udst_ref, nun_ref, nch_ref, tstart_ref, tunits_ref, rest_ref, x1_ref, gsel_ref, sel_ref,
                     gate_ref, start_ref, xs_hbm, gs_hbm, xb_ref, tri_ref, xsrt, gsrt, sem, *, rch, ech, umax):
    i = pl.program_id(0)
    tm = x1_ref.shape[0]

    @pl.when(i == 0)
    def _():
        _build_tri(tri_ref)
        unit = pl.ds(0, ROW_UNIT)
        xsrt[0, unit, :] = jnp.zeros((ROW_UNIT, xsrt.shape[2]), BF16)
        gsrt[0, unit, :] = jnp.zeros((ROW_UNIT, LANE), F32)

        def tail_copies(e, k):
            dst = pl.ds(pl.multiple_of(tstart_ref[e] + k * ROW_UNIT, ROW_UNIT), ROW_UNIT)
            return (pltpu.make_async_copy(xsrt.at[0, unit], xs_hbm.at[dst], sem.at[0, 0]),
                    pltpu.make_async_copy(gsrt.at[0, unit], gs_hbm.at[dst], sem.at[1, 0]))

        def per_expert(fn):
            def over_experts(e, carry):
                def over_units(k, c):
                    for cp in tail_copies(e, k):
                        fn(cp)
                    return c
                return lax.fori_loop(0, tunits_ref[e], over_units, carry)
            lax.fori_loop(0, N_EXPERTS, over_experts, 0)

        per_expert(lambda cp: cp.start())
        per_expert(lambda cp: cp.wait())

        blk = pl.ds(0, ech)
        xsrt[0, blk, :] = jnp.zeros((ech, xsrt.shape[2]), BF16)
        gsrt[0, blk, :] = jnp.zeros((ech, LANE), F32)

        def rest_copies(k):
            dst = pl.ds(pl.multiple_of(rest_ref[0] + k * ech, ech), ech)
            return (pltpu.make_async_copy(xsrt.at[0, blk], xs_hbm.at[dst], sem.at[0, 0]),
                    pltpu.make_async_copy(gsrt.at[0, blk], gs_hbm.at[dst], sem.at[1, 0]))

        def start_rest(k, c):
            for cp in rest_copies(k):
                cp.start()
            return c

        def wait_rest(k, c):
            for cp in rest_copies(k):
                cp.wait()
            return c

        lax.fori_loop(0, rest_ref[1], start_rest, 0)
        lax.fori_loop(0, rest_ref[1], wait_rest, 0)

    xb_ref[...] = x1_ref[...].astype(BF16)
    gsel = gsel_ref[0]
    pos, pos_a, pos_b = _sorted_positions(gsel, sel_ref[0], start_ref[0], tri_ref)
    gate32 = jnp.concatenate([jnp.where(gsel == g, gate_ref[0], 0.0) for g in range(N_GROUPS)], axis=0)
    gate_a = jnp.sum(jnp.where(pos == pos_a.astype(F32), gate32, 0.0), axis=0, keepdims=True)
    gate_b = jnp.sum(jnp.where(pos == pos_b.astype(F32), gate32, 0.0), axis=0, keepdims=True)
    terms_a = _exact_terms(gate_a)
    terms_b = _exact_terms(gate_b)
    slot = lax.rem(i, 2)

    def chunk(c, carry):
        r0 = pl.multiple_of(c * rch, rch)
        rid = lax.broadcasted_iota(I32, (rch, tm), 0) + r0
        hit_a = rid == pos_a
        hit_b = rid == pos_b
        onehot = jnp.where(jnp.logical_or(hit_a, hit_b), 1.0, 0.0).astype(BF16)
        xsrt[slot, pl.ds(r0, rch), :] = _dot(onehot, xb_ref[...]).astype(BF16)
        gt = (_dot_nt(jnp.where(hit_a, 1.0, 0.0).astype(BF16), terms_a)
              + _dot_nt(jnp.where(hit_b, 1.0, 0.0).astype(BF16), terms_b))
        gsrt[slot, pl.ds(r0, rch), :] = (gt + pltpu.roll(gt, LANE - 1, 1)) + pltpu.roll(gt, LANE - 2, 1)
        return carry

    lax.fori_loop(0, nch_ref[i], chunk, 0)

    def for_units(step, buf, fn):
        def body(u, carry):
            src = pl.ds(pl.multiple_of(u * ROW_UNIT, ROW_UNIT), ROW_UNIT)
            dst = pl.ds(pl.multiple_of(udst_ref[step * umax + u], ROW_UNIT), ROW_UNIT)
            fn(pltpu.make_async_copy(xsrt.at[buf, src], xs_hbm.at[dst], sem.at[0, buf]))
            fn(pltpu.make_async_copy(gsrt.at[buf, src], gs_hbm.at[dst], sem.at[1, buf]))
            return carry
        lax.fori_loop(0, nun_ref[step], body, 0)

    @pl.when(i > 0)
    def _():
        for_units(i - 1, 1 - slot, lambda cp: cp.wait())

    for_units(i, slot, lambda cp: cp.start())

    @pl.when(i == pl.num_programs(0) - 1)
    def _():
        for_units(i, slot, lambda cp: cp.wait())


def _dispatch(x1, gsel, sel, gate, start_col, tables, tails, total_rows, tm, rch, ech, umax):
    n, d = x1.shape
    nt = n // tm
    rows_max = umax * ROW_UNIT
    assert rows_max >= ech, "the sorted-tile buffer doubles as the zero source of one expert chunk"
    unit_dst, n_units, n_chunks = tables
    tail_start, tail_units, rest = tails
    grid_spec = pltpu.PrefetchScalarGridSpec(
        num_scalar_prefetch=6,
        grid=(nt,),
        in_specs=[
            pl.BlockSpec((tm, d), lambda i, *_: (i, 0)),
            pl.BlockSpec((1, 1, tm), lambda i, *_: (i, 0, 0)),
            pl.BlockSpec((1, EXPERTS_PER_GROUP, tm), lambda i, *_: (i, 0, 0)),
            pl.BlockSpec((1, EXPERTS_PER_GROUP, tm), lambda i, *_: (i, 0, 0)),
            pl.BlockSpec((1, N_EXPERTS, 1), lambda i, *_: (i, 0, 0)),
        ],
        out_specs=[pl.BlockSpec(memory_space=pl.ANY), pl.BlockSpec(memory_space=pl.ANY)],
        scratch_shapes=[
            pltpu.VMEM((tm, d), BF16),
            pltpu.VMEM((tm, tm), BF16),
            pltpu.VMEM((2, rows_max, d), BF16),
            pltpu.VMEM((2, rows_max, LANE), F32),
            pltpu.SemaphoreType.DMA((2, 2)),
        ],
    )
    return pl.pallas_call(
        functools.partial(_dispatch_kernel, rch=rch, ech=ech, umax=umax),
        grid_spec=grid_spec,
        out_shape=[jax.ShapeDtypeStruct((total_rows, d), BF16), jax.ShapeDtypeStruct((total_rows, LANE), F32)],
        compiler_params=_cparams(("arbitrary",)),
        name="moe_dispatch",
    )(unit_dst, n_units, n_chunks, tail_start, tail_units, rest, x1, gsel, sel, gate, start_col)


def _expert_kernel(ce_ref, x_ref, g_ref, wgu_ref, wd_ref, y_ref, wgu_b, wd_b):
    c = pl.program_id(0)
    expert = ce_ref[c]

    @pl.when(jnp.logical_and(expert >= 0, jnp.logical_or(c == 0, expert != ce_ref[jnp.maximum(c - 1, 0)])))
    def _():
        wgu_b[...] = wgu_ref[0].astype(BF16)
        wd_b[...] = wd_ref[0].astype(BF16)

    @pl.when(expert >= 0)
    def _():
        half = x_ref.shape[0] // 2
        for h in range(2):
            rows = pl.ds(h * half, half)
            gu = _dot(x_ref[rows, :], wgu_b[...])
            hid = jax.nn.silu(gu[:, :D_EXPERT]) * gu[:, D_EXPERT:]
            y_ref[rows, :] = (g_ref[rows, 0:1] * _dot(hid.astype(BF16), wd_b[...])).astype(BF16)

    @pl.when(ce_ref[c] < 0)
    def _():
        y_ref[...] = jnp.zeros_like(y_ref)


def _experts(xs, gs, chunk_expert, wts, rch):
    total_rows, d = xs.shape
    e2 = 2 * D_EXPERT
    w_gu = wts["w_gate_up"].reshape(N_EXPERTS, d, e2)
    w_dn = wts["w_down"].reshape(N_EXPERTS, D_EXPERT, d)
    grid_spec = pltpu.PrefetchScalarGridSpec(
        num_scalar_prefetch=1,
        grid=(total_rows // rch,),
        in_specs=[
            pl.BlockSpec((rch, d), lambda c, ce: (jnp.where(ce[c] >= 0, c, 0), 0)),
            pl.BlockSpec((rch, LANE), lambda c, ce: (jnp.where(ce[c] >= 0, c, 0), 0)),
            pl.BlockSpec((1, d, e2), lambda c, ce: (jnp.maximum(ce[c], 0), 0, 0)),
            pl.BlockSpec((1, D_EXPERT, d), lambda c, ce: (jnp.maximum(ce[c], 0), 0, 0)),
        ],
        out_specs=pl.BlockSpec((rch, d), lambda c, ce: (c, 0)),
        scratch_shapes=[pltpu.VMEM((d, e2), BF16), pltpu.VMEM((D_EXPERT, d), BF16)],
    )
    return pl.pallas_call(
        _expert_kernel,
        grid_spec=grid_spec,
        out_shape=jax.ShapeDtypeStruct((total_rows, d), BF16),
        compiler_params=_cparams(("arbitrary",)),
        name="moe_experts",
    )(chunk_expert, xs, gs, w_gu, w_dn)


def _combine_kernel(udst_ref, nun_ref, nch_ref, x1_ref, gsel_ref, sel_ref, start_ref, ys_hbm, g2_ref, b2_ref,
                    o_ref, tri_ref, ysrt, sem, *, rch, umax):
    i = pl.program_id(0)
    tm = x1_ref.shape[0]

    @pl.when(i == 0)
    def _():
        _build_tri(tri_ref)
        ysrt[...] = jnp.zeros_like(ysrt)

    def for_units(step, buf, fn):
        def body(u, carry):
            src = pl.ds(pl.multiple_of(udst_ref[step * umax + u], ROW_UNIT), ROW_UNIT)
            dst = pl.ds(pl.multiple_of(u * ROW_UNIT, ROW_UNIT), ROW_UNIT)
            fn(pltpu.make_async_copy(ys_hbm.at[src], ysrt.at[buf, dst], sem.at[buf]))
            return carry
        lax.fori_loop(0, nun_ref[step], body, 0)

    slot = lax.rem(i, 2)

    @pl.when(i == 0)
    def _():
        for_units(0, 0, lambda cp: cp.start())

    @pl.when(i + 1 < pl.num_programs(0))
    def _():
        for_units(i + 1, 1 - slot, lambda cp: cp.start())

    _, pos_a, pos_b = _sorted_positions(gsel_ref[0], sel_ref[0], start_ref[0], tri_ref)
    o_ref[...] = jnp.zeros_like(o_ref)
    for_units(i, slot, lambda cp: cp.wait())

    def chunk(c, carry):
        r0 = pl.multiple_of(c * rch, rch)
        rid = lax.broadcasted_iota(I32, (rch, tm), 0) + r0
        onehot = jnp.where(jnp.logical_or(rid == pos_a, rid == pos_b), 1.0, 0.0).astype(BF16)
        o_ref[...] += _dot_tn(onehot, ysrt[slot, pl.ds(r0, rch), :])
        return carry

    lax.fori_loop(0, nch_ref[i], chunk, 0)
    o_ref[...] = _layernorm(ALPHA * x1_ref[...] + o_ref[...], g2_ref[...], b2_ref[...])


def _combine(x1, gsel, sel, start_col, ys, tables, wts, tm, rch, umax):
    n, d = x1.shape
    nt = n // tm
    unit_dst, n_units, n_chunks = tables
    grid_spec = pltpu.PrefetchScalarGridSpec(
        num_scalar_prefetch=3,
        grid=(nt,),
        in_specs=[
            pl.BlockSpec((tm, d), lambda i, *_: (i, 0)),
            pl.BlockSpec((1, 1, tm), lambda i, *_: (i, 0, 0)),
            pl.BlockSpec((1, EXPERTS_PER_GROUP, tm), lambda i, *_: (i, 0, 0)),
            pl.BlockSpec((1, N_EXPERTS, 1), lambda i, *_: (i, 0, 0)),
            pl.BlockSpec(memory_space=pl.ANY),
            pl.BlockSpec((1, d), lambda i, *_: (0, 0)),
            pl.BlockSpec((1, d), lambda i, *_: (0, 0)),
        ],
        out_specs=pl.BlockSpec((tm, d), lambda i, *_: (i, 0)),
        scratch_shapes=[
            pltpu.VMEM((tm, tm), BF16),
            pltpu.VMEM((2, umax * ROW_UNIT, d), BF16),
            pltpu.SemaphoreType.DMA((2,)),
        ],
    )
    return pl.pallas_call(
        functools.partial(_combine_kernel, rch=rch, umax=umax),
        grid_spec=grid_spec,
        out_shape=jax.ShapeDtypeStruct((n, d), F32),
        compiler_params=_cparams(("arbitrary",)),
        name="moe_combine",
    )(unit_dst, n_units, n_chunks, x1, gsel, sel, start_col, ys, wts["ln2_g"], wts["ln2_b"])


def _routing_tables(cnt, tm, rch, ech):
    nt = cnt.shape[0]
    seg = (cnt + (ROW_UNIT - 1)) // ROW_UNIT * ROW_UNIT
    start = jnp.cumsum(seg, axis=1) - seg
    rows = seg.sum(axis=1)
    per_expert = seg.sum(axis=0)
    region = (per_expert + (ech - 1)) // ech * ech
    region_start = jnp.cumsum(region) - region
    seg_dst = region_start[None, :] + jnp.cumsum(seg, axis=0) - seg
    rows_max = -(-(TOP_K * tm + N_EXPERTS * (ROW_UNIT - 1)) // rch) * rch
    umax = rows_max // ROW_UNIT
    total_rows = -(-(TOP_K * tm * nt + nt * N_EXPERTS * (ROW_UNIT - 1) + N_EXPERTS * (ech - 1)) // ech) * ech
    u_row = (jnp.arange(umax, dtype=I32) * ROW_UNIT)[None, :, None]
    in_seg = jnp.logical_and(u_row >= start[:, None, :], u_row < (start + seg)[:, None, :])
    unit_dst = jnp.sum(jnp.where(in_seg, (seg_dst - start)[:, None, :] + u_row, 0), axis=-1)
    c_row = (jnp.arange(total_rows // ech, dtype=I32) * ech)[:, None]
    in_region = jnp.logical_and(c_row >= region_start[None, :], c_row < (region_start + per_expert)[None, :])
    chunk_expert = jnp.sum(jnp.where(in_region, jnp.arange(N_EXPERTS, dtype=I32)[None, :] + 1, 0), axis=-1) - 1
    tables = (unit_dst.reshape(-1).astype(I32), (rows // ROW_UNIT).astype(I32),
              ((rows + (rch - 1)) // rch).astype(I32))
    used_rows = region.sum()
    tails = ((region_start + per_expert).astype(I32), ((region - per_expert) // ROW_UNIT).astype(I32),
             jnp.stack([used_rows, (total_rows - used_rows) // ech]).astype(I32))
    return tables, tails, chunk_expert.astype(I32), start.astype(F32).reshape(nt, N_EXPERTS, 1), total_rows, umax


def _merge_and_routed_ffn(x, att, rec, wts, tm_merge, tm, rch):
    b, s, d = x.shape
    x1, gsel, gate, sel, cnt = _merge(x, att, rec, wts, tm_merge)
    n = b * s
    nt = n // tm
    f = tm // tm_merge
    regroup = lambda a: a.reshape(nt, f, EXPERTS_PER_GROUP, tm_merge).transpose(0, 2, 1, 3).reshape(
        nt, EXPERTS_PER_GROUP, tm)
    gsel, gate, sel = gsel.reshape(nt, 1, tm), regroup(gate), regroup(sel)
    counts = cnt[:, :, 0].reshape(nt, f, N_EXPERTS).sum(axis=1).astype(I32)
    ech = CH_EXPERT
    tables, tails, chunk_expert, start_col, total_rows, umax = _routing_tables(counts, tm, rch, ech)
    x1 = x1.reshape(n, d)
    xs, gs = _dispatch(x1, gsel, sel, gate, start_col, tables, tails, total_rows, tm, rch, ech, umax)
    ys = _experts(xs, gs, chunk_expert, wts, ech)
    y = _combine(x1, gsel, sel, start_col, ys, tables, wts, tm, rch, umax)
    return y.reshape(b, s, d)


def _merge_and_ffn(x, att, rec, wts, tm_merge, tm_moe, ch):
    b, s, d = x.shape
    x1, gsel, gate, _, cnt = _merge(x, att, rec, wts, tm_merge)
    n = b * s
    nt = n // tm_moe
    f = tm_moe // tm_merge
    gsel = gsel.reshape(nt, 1, tm_moe)
    gate = gate.reshape(nt, f, EXPERTS_PER_GROUP, tm_merge).transpose(0, 2, 1, 3).reshape(nt, EXPERTS_PER_GROUP, tm_moe)
    per_expert = cnt[:, :, 0].reshape(nt, f, N_GROUPS, EXPERTS_PER_GROUP)
    counts = (per_expert.sum(axis=(1, 3)) / TOP_K).astype(I32)
    nch = ((counts + (ch - 1)) // ch).reshape(-1)
    y = _moe(x1.reshape(n, d), gsel, gate, nch, wts, tm_moe, ch)
    return y.reshape(b, s, d)


def _swap_halves(w):
    half = QK_ROPE // 2
    return jnp.concatenate([w[..., half:], w[..., :half]], axis=-1)


def _prep_weights(w_in, q_norm_g, w_uq, kv_norm_g, w_uk, w_uv, conv_w, conv_b, w_rg, b_rg, w_ig, b_ig,
                  lru_lambda, att_out_g, rec_out_g, w_out, ln1_g, ln1_b, w_group, b_group, w_expert,
                  b_expert, w_gate_up, w_down, ln2_g, ln2_b):
    d = w_in.shape[0]
    o1, o2, o3, o4 = Q_LORA, Q_LORA + KV_LORA, Q_LORA + KV_LORA + QK_ROPE, Q_LORA + KV_LORA + QK_ROPE + REC_WIDTH
    w_kpe = w_in[:, o2:o3]
    pad_lo = jnp.zeros((d, QK_NOPE), F32)
    pad_hi = jnp.zeros((d, HEAD_PAD - QK_NOPE - QK_ROPE), F32)
    w_in_ext = jnp.concatenate([
        w_in[:, :o2],
        pad_lo, w_kpe, pad_hi,
        pad_lo, _swap_halves(w_kpe), pad_hi,
        w_in[:, o3:o4], w_in[:, o4:],
    ], axis=1).astype(BF16)

    nope, pe = w_uq[..., :QK_NOPE], w_uq[..., QK_NOPE:]
    zq = lambda n: jnp.zeros((Q_LORA, N_HEADS, n), F32)
    q_main = jnp.concatenate([nope, pe, zq(HEAD_PAD - QK_NOPE - QK_ROPE)], axis=-1)
    q_swap = jnp.concatenate([zq(QK_NOPE), _swap_halves(pe), zq(HEAD_PAD - QK_NOPE - QK_ROPE)], axis=-1)
    w_uq_t = jnp.concatenate([q_main.reshape(Q_LORA, -1), q_swap.reshape(Q_LORA, -1)], axis=1).T.astype(BF16)

    k_pad = jnp.concatenate([w_uk, jnp.zeros((KV_LORA, N_HEADS, HEAD_PAD - QK_NOPE), F32)], axis=-1)
    w_uk_pad = k_pad.reshape(KV_LORA, -1).astype(BF16)
    w_uv_t = w_uv.reshape(KV_LORA, -1).T.astype(BF16)

    w_uk_t = jnp.concatenate([w_uk.transpose(1, 2, 0),
                              jnp.zeros((N_HEADS, HEAD_PAD - QK_NOPE, KV_LORA), F32)], axis=1).astype(BF16)
    sel = jnp.zeros((HEAD_PAD, QK_ROPE), F32).at[QK_NOPE + jnp.arange(QK_ROPE), jnp.arange(QK_ROPE)].set(1.0)
    w_uv_h = w_uv.transpose(1, 0, 2)
    zero_v = jnp.zeros((KV_LORA, V_HEAD), F32)
    w_uv_pair = jnp.stack([
        jnp.concatenate([jnp.concatenate([w_uv_h[2 * p], zero_v], axis=1),
                         jnp.concatenate([zero_v, w_uv_h[2 * p + 1]], axis=1)], axis=0)
        for p in range(N_PAIRS)]).astype(BF16)

    def block_diag(w):
        eye = jnp.eye(REC_BLOCKS, dtype=F32)
        return jnp.einsum('nde,nm->ndme', w, eye).reshape(REC_WIDTH, REC_WIDTH)

    bd_r, bd_i = block_diag(w_rg), block_diag(w_ig)
    half = REC_WIDTH // 2
    w_gate = jnp.stack([
        jnp.concatenate([bd_r[j * half:(j + 1) * half, j * half:(j + 1) * half],
                         bd_i[j * half:(j + 1) * half, j * half:(j + 1) * half]], axis=1)
        for j in range(2)]).astype(BF16)

    w_router_t = jnp.concatenate([w_group.T, jnp.zeros((SUBLANE - N_GROUPS, d), F32), w_expert.T], axis=0)
    w_router_hi = w_router_t.astype(BF16)
    w_router_lo = (w_router_t - w_router_hi.astype(F32)).astype(BF16)
    w_router_t = jnp.concatenate([w_router_hi, w_router_lo], axis=0)
    row = lambda v: v.reshape(1, -1)
    return {
        "w_in": w_in_ext, "q_norm_g": row(q_norm_g), "w_uq_t": w_uq_t, "kv_norm_g": row(kv_norm_g),
        "w_uk": w_uk_pad, "w_uv_t": w_uv_t, "w_uk_t": w_uk_t, "w_pe_sel": sel.astype(BF16), "w_uv_pair": w_uv_pair,
        "conv_w": conv_w, "conv_b": row(conv_b), "w_gate": w_gate, "b_rg": row(b_rg), "b_ig": row(b_ig),
        "lru_lambda": row(lru_lambda), "att_out_g": row(att_out_g), "rec_out_g": row(rec_out_g),
        "w_out": w_out.astype(BF16), "ln1_g": row(ln1_g), "ln1_b": row(ln1_b),
        "w_router_t": w_router_t, "b_group": b_group.reshape(-1, 1), "b_expert": b_expert.reshape(-1, 1),
        "w_gate_up": w_gate_up, "w_down": w_down,
        "ln2_g": row(ln2_g), "ln2_b": row(ln2_b),
    }


def _rope_tables(pos):
    half = QK_ROPE // 2
    inv = ROPE_THETA ** (-(jnp.arange(half, dtype=F32) * 2.0 / QK_ROPE))
    ang = pos.astype(F32)[:, None] * inv[None, :]
    cos, sin = jnp.cos(ang), jnp.sin(ang)
    t = pos.shape[0]
    cos_t = jnp.concatenate([jnp.ones((t, QK_NOPE), F32), cos, cos,
                             jnp.zeros((t, HEAD_PAD - QK_NOPE - QK_ROPE), F32)], axis=1)
    sin_t = jnp.concatenate([jnp.zeros((t, QK_NOPE), F32), -sin, sin,
                             jnp.zeros((t, HEAD_PAD - QK_NOPE - QK_ROPE), F32)], axis=1)
    return cos_t, sin_t, cos_t.T, sin_t.T


def kernel(x_prompt, x_sample, cache_kv_latent, cache_k_rope, state_conv, state_rec, page_table,
           w_in, q_norm_g, w_uq, kv_norm_g, w_uk, w_uv, conv_w, conv_b, w_rg, b_rg, w_ig, b_ig,
           lru_lambda, att_out_g, rec_out_g, w_out, ln1_g, ln1_b, w_group, b_group, w_expert,
           b_expert, w_gate_up, w_down, ln2_g, ln2_b):
    wts = _prep_weights(w_in, q_norm_g, w_uq, kv_norm_g, w_uk, w_uv, conv_w, conv_b, w_rg, b_rg, w_ig, b_ig,
                        lru_lambda, att_out_g, rec_out_g, w_out, ln1_g, ln1_b, w_group, b_group, w_expert,
                        b_expert, w_gate_up, w_down, ln2_g, ln2_b)
    bp, sp, d = x_prompt.shape
    bd, td, _ = x_sample.shape
    assert td == 1, "the sample path handles one new token per sequence"
    past_len = page_table.shape[1] * PAGE_SIZE
    ko = QK_NOPE

    qt, k, vt, c_p, kpe_blk, rx, rg = _project(x_prompt, _rope_tables(jnp.arange(sp, dtype=I32)), wts,
                                               min(T_ATT, sp))
    xs = x_sample.reshape(1, bd, d)
    qt_s, _, _, c_s, kpe_s_blk, rx_s, rg_s = _project(xs, _rope_tables(jnp.full((bd,), past_len, I32)), wts, bd)
    kpe_s = kpe_s_blk[0, :, ko:ko + QK_ROPE]
    qlat, qpe = _absorb(qt_s[0, :, 0], wts)

    att_p, o_lat = _attention(qt, k, vt, page_table, qlat.transpose(1, 0, 2), qpe.transpose(1, 0, 2),
                              c_s.reshape(bd, 1, KV_LORA), kpe_s.reshape(bd, 1, QK_ROPE),
                              cache_kv_latent, cache_k_rope.transpose(0, 2, 1))

    rec_p, h_p = _lru_prompt(rx, rg, wts, min(TM_LRU, sp))
    tm_moe = min(TM_MOE, bp * sp)
    y_p = _merge_and_routed_ffn(x_prompt, att_p, rec_p, wts, min(TM_MERGE, sp, tm_moe), min(TM_ROUTED, tm_moe),
                                CH_MOE)
    kpe_p = kpe_blk[..., ko:ko + QK_ROPE]
    conv_p = rx[:, sp - (CONV_W - 1):, :]

    att_s = _value_up(o_lat.reshape(bd, N_HEADS * KV_LORA), wts)
    rec_s, conv_s_t, h_s = _lru_step(rx_s[0], rg_s[0], state_conv.transpose(1, 0, 2), state_rec, wts)
    y_s = _merge_and_ffn(xs, att_s, rec_s.reshape(1, bd, REC_WIDTH), wts, bd, bd, min(CH_MOE, bd))

    return (y_p, y_s.reshape(bd, 1, d), c_p, kpe_p, conv_p, h_p.reshape(bp, REC_WIDTH),
            c_s.reshape(bd, 1, KV_LORA), kpe_s.reshape(bd, 1, QK_ROPE), conv_s_t.transpose(1, 0, 2), h_s)
```

```python
import functools
import math

import jax
import jax.numpy as jnp
from jax import lax
from jax.experimental import pallas as pl
from jax.experimental.pallas import tpu as pltpu

F32 = jnp.float32
BF16 = jnp.bfloat16
I32 = jnp.int32

N_HEADS = 8
QK_NOPE = 64
QK_ROPE = 32
V_HEAD = 64
Q_LORA = 384
KV_LORA = 256
ROPE_THETA = 10000.0
SM_SCALE = (QK_NOPE + QK_ROPE) ** -0.5
REC_WIDTH = 512
REC_BLOCKS = 8
REC_BLOCK_W = REC_WIDTH // REC_BLOCKS
CONV_W = 4
LRU_C = 8.0
N_GROUPS = 4
EXPERTS_PER_GROUP = 8
N_EXPERTS = N_GROUPS * EXPERTS_PER_GROUP
TOP_K = 2
D_EXPERT = 256
DEPTH = 1
ALPHA = (2.0 * DEPTH) ** 0.25
LN_EPS = 1e-5
RMS_EPS = 1e-6
NEG_INF = -1e30
PAGE_SIZE = 128

LANE = 128
SUBLANE = 8
HEAD_PAD = LANE
N_PAIRS = N_HEADS * V_HEAD // LANE
VMEM_LIMIT = 56 * 1024 * 1024

T_ATT = 512
KEY_STRIP = 512
TM_LRU = 256
TM_MERGE = 512
TM_MOE = 1024
TM_ROUTED = 512
CH_EXPERT = 512
CH_MOE = 256
ROW_UNIT = 16
PAGES_PER_CHUNK = 32
CACHE_SLOTS = 3
EXP2_SCALE = SM_SCALE * math.log2(math.e)


def _dot(a, b):
    return jnp.dot(a, b, preferred_element_type=F32)


def _dot_nt(a, b, precision=None):
    return lax.dot_general(a, b, (((1,), (1,)), ((), ())), preferred_element_type=F32, precision=precision)


def _dot_tn(a, b):
    return lax.dot_general(a, b, (((0,), (0,)), ((), ())), preferred_element_type=F32)


def _cparams(semantics, flags=None):
    return pltpu.CompilerParams(dimension_semantics=semantics, vmem_limit_bytes=VMEM_LIMIT, flags=flags)


def _rmsnorm(x, g):
    return x * lax.rsqrt(jnp.mean(x * x, axis=-1, keepdims=True) + RMS_EPS) * g


def _layernorm(x, g, b):
    mu = jnp.mean(x, axis=-1, keepdims=True)
    xc = x - mu
    var = jnp.mean(xc * xc, axis=-1, keepdims=True)
    return xc * lax.rsqrt(var + LN_EPS) * g + b


def _gelu_tanh(x):
    return x * (0.5 * (1.0 + jnp.tanh(math.sqrt(2.0 / math.pi) * (x + 0.044715 * (x * x * x)))))


def _lru_coeffs(xc, pre_r, pre_i, b_r, b_i, lam):
    r = jax.nn.sigmoid(pre_r + b_r)
    i = jax.nn.sigmoid(pre_i + b_i)
    neg_lam = -lam
    softplus = jnp.maximum(neg_lam, 0.0) + jnp.log1p(jnp.exp(-jnp.abs(neg_lam)))
    log_a = (-LRU_C * softplus) * r
    a = jnp.exp(log_a)
    u = jnp.sqrt(-jnp.tanh(log_a) * (a * a + 1.0)) * (i * xc)
    return a, u


def _gate_preacts(xc, wg_ref):
    half = REC_WIDTH // 2
    g0 = _dot(xc[:, :half].astype(BF16), wg_ref[0])
    g1 = _dot(xc[:, half:].astype(BF16), wg_ref[1])
    pre_r = jnp.concatenate([g0[:, :half], g1[:, :half]], axis=1)
    pre_i = jnp.concatenate([g0[:, half:], g1[:, half:]], axis=1)
    return pre_r, pre_i


def _proj_kernel(x_ref, cos_ref, sin_ref, cost_ref, sint_ref, win_ref, qg_ref, wuqt_ref, kvg_ref, wuk_ref, wuvt_ref,
                 qt_ref, k_ref, vt_ref, ckv_ref, kpe_ref, rx_ref, rg_ref):
    x = x_ref[0].astype(BF16)
    z = _dot(x, win_ref[...])
    o_kv = Q_LORA
    o_ka = o_kv + KV_LORA
    o_kb = o_ka + LANE
    o_rx = o_kb + LANE
    o_rg = o_rx + REC_WIDTH
    qn = _rmsnorm(z[:, :o_kv], qg_ref[...]).astype(BF16)
    qq = _dot_nt(wuqt_ref[...], qn)
    cos_t = cost_ref[...]
    sin_t = sint_ref[...]
    sw = N_HEADS * HEAD_PAD
    for h in range(N_HEADS):
        lo = h * HEAD_PAD
        q_rot = qq[lo:lo + HEAD_PAD] * cos_t + qq[sw + lo:sw + lo + HEAD_PAD] * sin_t
        qt_ref[0, h, 0] = (q_rot * EXP2_SCALE).astype(BF16)
    ckv = _rmsnorm(z[:, o_kv:o_ka], kvg_ref[...])
    ckv_ref[0] = ckv
    kpe = z[:, o_ka:o_kb] * cos_ref[...] + z[:, o_kb:o_rx] * sin_ref[...]
    kpe_ref[0] = kpe
    ckv_b = ckv.astype(BF16)
    kn = _dot(ckv_b, wuk_ref[...])
    for h in range(N_HEADS):
        lo = h * HEAD_PAD
        k_ref[0, h] = (kn[:, lo:lo + HEAD_PAD] + kpe).astype(BF16)
    vt = _dot_nt(wuvt_ref[...], ckv_b)
    for p in range(N_PAIRS):
        vt_ref[0, p, 0] = vt[p * LANE:(p + 1) * LANE].astype(BF16)
    rx_ref[0] = z[:, o_rx:o_rg]
    rg_ref[0] = z[:, o_rg:o_rg + REC_WIDTH]


def _project(x, tables, wts, tm):
    b, s, d = x.shape
    nt = s // tm
    in_w = wts["w_in"].shape[1]
    full = lambda shape: pl.BlockSpec(shape, lambda bi, si: (0,) * len(shape))
    cos_n, sin_n, cos_t, sin_t = tables
    return pl.pallas_call(
        _proj_kernel,
        grid=(b, nt),
        in_specs=[
            pl.BlockSpec((1, tm, d), lambda bi, si: (bi, si, 0)),
            pl.BlockSpec((tm, LANE), lambda bi, si: (si, 0)),
            pl.BlockSpec((tm, LANE), lambda bi, si: (si, 0)),
            pl.BlockSpec((HEAD_PAD, tm), lambda bi, si: (0, si)),
            pl.BlockSpec((HEAD_PAD, tm), lambda bi, si: (0, si)),
            full((d, in_w)),
            full((1, Q_LORA)),
            full((2 * N_HEADS * HEAD_PAD, Q_LORA)),
            full((1, KV_LORA)),
            full((KV_LORA, N_HEADS * HEAD_PAD)),
            full((N_PAIRS * LANE, KV_LORA)),
        ],
        out_specs=[
            pl.BlockSpec((1, N_HEADS, 1, HEAD_PAD, tm), lambda bi, si: (bi, 0, si, 0, 0)),
            pl.BlockSpec((1, N_HEADS, tm, HEAD_PAD), lambda bi, si: (bi, 0, si, 0)),
            pl.BlockSpec((1, N_PAIRS, 1, LANE, tm), lambda bi, si: (bi, 0, si, 0, 0)),
            pl.BlockSpec((1, tm, KV_LORA), lambda bi, si: (bi, si, 0)),
            pl.BlockSpec((1, tm, LANE), lambda bi, si: (bi, si, 0)),
            pl.BlockSpec((1, tm, REC_WIDTH), lambda bi, si: (bi, si, 0)),
            pl.BlockSpec((1, tm, REC_WIDTH), lambda bi, si: (bi, si, 0)),
        ],
        out_shape=[
            jax.ShapeDtypeStruct((b, N_HEADS, nt, HEAD_PAD, tm), BF16),
            jax.ShapeDtypeStruct((b, N_HEADS, s, HEAD_PAD), BF16),
            jax.ShapeDtypeStruct((b, N_PAIRS, nt, LANE, tm), BF16),
            jax.ShapeDtypeStruct((b, s, KV_LORA), F32),
            jax.ShapeDtypeStruct((b, s, LANE), F32),
            jax.ShapeDtypeStruct((b, s, REC_WIDTH), F32),
            jax.ShapeDtypeStruct((b, s, REC_WIDTH), F32),
        ],
        compiler_params=_cparams(("parallel", "parallel")),
        name="proj",
    )(x, cos_n, sin_n, cos_t, sin_t, wts["w_in"], wts["q_norm_g"], wts["w_uq_t"], wts["kv_norm_g"],
      wts["w_uk"], wts["w_uv_t"])


def _lru_prompt_kernel(rx_ref, rg_ref, cw_ref, cb_ref, wg_ref, br_ref, bi_ref, lam_ref,
                       y_ref, hlast_ref, xp_ref, h_ref):
    tm = rx_ref.shape[1]
    si = pl.program_id(1)

    @pl.when(si == 0)
    def _():
        xp_ref[0:SUBLANE, :] = jnp.zeros((SUBLANE, REC_WIDTH), F32)
        h_ref[...] = jnp.zeros_like(h_ref)

    x = rx_ref[0]
    xp_ref[SUBLANE:SUBLANE + tm, :] = x
    xc = cb_ref[...] + x * cw_ref[CONV_W - 1:CONV_W, :]
    for m in range(1, CONV_W):
        xc = xc + xp_ref[pl.ds(SUBLANE - m, tm), :] * cw_ref[CONV_W - 1 - m:CONV_W - m, :]
    xp_ref[0:SUBLANE, :] = x[tm - SUBLANE:, :]

    pre_r, pre_i = _gate_preacts(xc, wg_ref)
    a, u = _lru_coeffs(xc, pre_r, pre_i, br_ref[...], bi_ref[...], lam_ref[...])

    row_in_group = lax.broadcasted_iota(I32, (tm, REC_WIDTH), 0) % SUBLANE
    d = 1
    while d < SUBLANE:
        keep = row_in_group >= d
        a_sh = jnp.where(keep, pltpu.roll(a, d, 0), 1.0)
        u_sh = jnp.where(keep, pltpu.roll(u, d, 0), 0.0)
        u = u + a * u_sh
        a = a * a_sh
        d *= 2
    carry = h_ref[...]
    groups = []
    for k in range(tm // SUBLANE):
        rows = slice(k * SUBLANE, (k + 1) * SUBLANE)
        hk = a[rows] * carry + u[rows]
        groups.append(hk)
        carry = hk[SUBLANE - 1:SUBLANE, :]
    h = jnp.concatenate(groups, axis=0)
    h_ref[...] = carry
    hlast_ref[0] = carry
    y_ref[0] = h * _gelu_tanh(rg_ref[0])


def _lru_prompt(rx, rg, wts, tm):
    b, s, w = rx.shape
    full = lambda shape: pl.BlockSpec(shape, lambda bi, si: (0,) * len(shape))
    return pl.pallas_call(
        _lru_prompt_kernel,
        grid=(b, s // tm),
        in_specs=[
            pl.BlockSpec((1, tm, w), lambda bi, si: (bi, si, 0)),
            pl.BlockSpec((1, tm, w), lambda bi, si: (bi, si, 0)),
            full((CONV_W, w)), full((1, w)), full((2, w // 2, w)), full((1, w)), full((1, w)), full((1, w)),
        ],
        out_specs=[
            pl.BlockSpec((1, tm, w), lambda bi, si: (bi, si, 0)),
            pl.BlockSpec((1, 1, w), lambda bi, si: (bi, 0, 0)),
        ],
        out_shape=[jax.ShapeDtypeStruct((b, s, w), F32), jax.ShapeDtypeStruct((b, 1, w), F32)],
        scratch_shapes=[pltpu.VMEM((tm + SUBLANE, w), F32), pltpu.VMEM((1, w), F32)],
        compiler_params=_cparams(("arbitrary", "arbitrary")),
        name="lru_prompt",
    )(rx, rg, wts["conv_w"], wts["conv_b"], wts["w_gate"], wts["b_rg"], wts["b_ig"], wts["lru_lambda"])


def _lru_step_kernel(rx_ref, rg_ref, conv_ref, h0_ref, cw_ref, cb_ref, wg_ref, br_ref, bi_ref, lam_ref,
                     y_ref, newconv_ref, h_ref):
    x = rx_ref[...]
    xc = cb_ref[...] + x * cw_ref[CONV_W - 1:CONV_W, :]
    for k in range(CONV_W - 1):
        xc = xc + conv_ref[k] * cw_ref[k:k + 1, :]
    pre_r, pre_i = _gate_preacts(xc, wg_ref)
    a, u = _lru_coeffs(xc, pre_r, pre_i, br_ref[...], bi_ref[...], lam_ref[...])
    h = a * h0_ref[...] + u
    h_ref[...] = h
    y_ref[...] = h * _gelu_tanh(rg_ref[...])
    for k in range(CONV_W - 2):
        newconv_ref[k] = conv_ref[k + 1]
    newconv_ref[CONV_W - 2] = x


def _lru_step(rx, rg, conv_t, h0, wts):
    n, w = rx.shape
    return pl.pallas_call(
        _lru_step_kernel,
        out_shape=[jax.ShapeDtypeStruct((n, w), F32),
                   jax.ShapeDtypeStruct((CONV_W - 1, n, w), F32),
                   jax.ShapeDtypeStruct((n, w), F32)],
        name="lru_step",
    )(rx, rg, conv_t, h0, wts["conv_w"], wts["conv_b"], wts["w_gate"], wts["b_rg"], wts["b_ig"], wts["lru_lambda"])


def _fused_attn_kernel(pt_ref, qt_ref, k_ref, vt_ref, qlat_ref, qpe_ref, cnew_ref, knew_ref, ckv_hbm, kr_hbm,
                       o_ref, olat_ref, cbuf, kbuf, sem, ms_ref, ls_ref, accs_ref,
                       *, n_batch, n_q, n_chunks, total_chunks):
    t = qt_ref.shape[-1]
    bi, pi, qi = pl.program_id(0), pl.program_id(1), pl.program_id(2)
    steps_per_group = n_q * (n_q + 1) // 2
    total_steps = n_batch * N_PAIRS * steps_per_group
    base = (bi * N_PAIRS + pi) * steps_per_group + (qi * (qi + 1)) // 2
    ks = KEY_STRIP
    key_i = lax.broadcasted_iota(I32, (ks, t), 0)
    qry_i = lax.broadcasted_iota(I32, (ks, t), 1)
    qts = [qt_ref[0, e, 0] for e in range(2)]

    cp = PAGES_PER_CHUNK
    ahead = CACHE_SLOTS - 1

    def copies(g, slot):
        out = []
        for p in range(cp):
            page = pt_ref[g * cp + p]
            rows = pl.ds(p * PAGE_SIZE, PAGE_SIZE)
            out.append(pltpu.make_async_copy(ckv_hbm.at[page], cbuf.at[slot, rows], sem.at[0, slot]))
            out.append(pltpu.make_async_copy(kr_hbm.at[page], kbuf.at[slot, :, rows], sem.at[1, slot]))
        return out

    def start(g, slot):
        for c in copies(g, slot):
            c.start()

    @pl.when(jnp.logical_and(jnp.logical_and(bi == 0, pi == 0), qi == 0))
    def _():
        for g0 in range(min(ahead, total_chunks)):
            start(g0, g0)

    def sample_chunk_stages(g):
        v = {}

        def fetch_and_score():
            @pl.when(g + ahead < total_chunks)
            def _():
                start(g + ahead, lax.rem(g + ahead, CACHE_SLOTS))

            slot = lax.rem(g, CACHE_SLOTS)
            for c in copies(g, slot):
                c.wait()
            v["b"] = lax.div(g, n_chunks)
            v["qlat"] = qlat_ref[v["b"]]
            v["qpe"] = qpe_ref[v["b"]]
            v["cb"] = cbuf[slot].astype(BF16)
            kb = kbuf[slot].astype(BF16)
            v["s"] = _dot_nt(v["qlat"].astype(BF16), v["cb"]) + _dot(v["qpe"].astype(BF16), kb)

        def softmax_and_values():
            first = lax.rem(g, n_chunks) == 0
            m = jnp.where(first, NEG_INF, ms_ref[...])
            l = jnp.where(first, 0.0, ls_ref[...])
            acc = jnp.where(first, 0.0, accs_ref[...])
            s = v["s"]
            m_new = jnp.maximum(m, jnp.max(s, axis=-1, keepdims=True))
            alpha = jnp.exp2((m - m_new))
            p = jnp.exp2((s - m_new))
            v["l"] = alpha * l + jnp.sum(p, axis=-1, keepdims=True)
            v["acc"] = alpha * acc + _dot(p.astype(BF16), v["cb"])
            v["m"] = m_new

        def finish():
            m_new, l, acc, b = v["m"], v["l"], v["acc"], v["b"]
            ms_ref[...] = m_new
            ls_ref[...] = l
            accs_ref[...] = acc
            cnew = cnew_ref[b]
            knew = knew_ref[b]
            s_new = (jnp.sum(v["qlat"] * cnew, axis=-1, keepdims=True)
                     + jnp.sum(v["qpe"] * knew, axis=-1, keepdims=True))
            m_fin = jnp.maximum(m_new, s_new)
            a_fin = jnp.exp2((m_new - m_fin))
            p_new = jnp.exp2((s_new - m_fin))
            olat_ref[b] = (a_fin * acc + p_new * cnew) / (a_fin * l + p_new)

        return [fetch_and_score, softmax_and_values, finish]

    def scores(j, r, e):
        kb = k_ref[0, e, pl.ds(pl.multiple_of(j * t + r * ks, ks), ks), :]
        return _dot(kb, qts[e])

    def step(j, carry, diagonal, with_chunk):
        stages = sample_chunk_stages(base + j) if with_chunk else []
        carry = list(carry)
        units = [(r, e) for r in range(t // ks) for e in range(2)]
        if stages:
            stages.pop(0)()
        st_next = scores(j, *units[0])
        for u, (r, e) in enumerate(units):
            st = st_next
            if u + 1 < len(units):
                st_next = scores(j, *units[u + 1])
            if stages and u > 0:
                stages.pop(0)()
            m, l, acc = carry[e]
            vb = vt_ref[0, 0, j, e * V_HEAD:(e + 1) * V_HEAD, r * ks:(r + 1) * ks]
            if diagonal:
                st = jnp.where(key_i + r * ks <= qry_i, st, NEG_INF)
            m_new = jnp.maximum(m, jnp.max(st, axis=0, keepdims=True))
            alpha = jnp.exp2((m - m_new))
            pt = jnp.exp2((st - m_new))
            l = alpha * l + jnp.sum(pt, axis=0, keepdims=True)
            acc = alpha * acc + _dot(vb, pt.astype(BF16))
            carry[e] = (m_new, l, acc)
        for stage in stages:
            stage()
        return tuple(carry)

    init = (jnp.full((1, t), NEG_INF, F32), jnp.zeros((1, t), F32), jnp.zeros((V_HEAD, t), F32))
    n_with = jnp.clip(total_chunks - base, 0, qi)
    carry = lax.fori_loop(0, n_with, functools.partial(step, diagonal=False, with_chunk=True), (init, init))
    carry = lax.fori_loop(n_with, qi, functools.partial(step, diagonal=False, with_chunk=False), carry)
    carry = lax.cond(base + qi < total_chunks,
                     functools.partial(step, qi, diagonal=True, with_chunk=True),
                     functools.partial(step, qi, diagonal=True, with_chunk=False), carry)
    halves = [acc / l for (_, l, acc) in carry]
    o_ref[0, 0] = jnp.concatenate(halves, axis=0).T

    if total_chunks > total_steps:
        @pl.when(jnp.logical_and(jnp.logical_and(bi == n_batch - 1, pi == N_PAIRS - 1), qi == n_q - 1))
        def _():
            def drain(g, c):
                for stage in sample_chunk_stages(g):
                    stage()
                return c
            lax.fori_loop(total_steps, total_chunks, drain, 0)


def _attention(qt, k, vt, page_table, qlat, qpe, c_new, k_new, cache_kv, cache_kr_t):
    b, _, nt, _, t = qt.shape
    s = nt * t
    bd, n_pages = page_table.shape
    n_chunks = n_pages // PAGES_PER_CHUNK
    rows = PAGES_PER_CHUNK * PAGE_SIZE
    kern = functools.partial(_fused_attn_kernel, n_batch=b, n_q=nt, n_chunks=n_chunks, total_chunks=bd * n_chunks)
    whole = lambda shape: pl.BlockSpec(shape, lambda bi, pi, qi, pt: (0,) * len(shape))
    grid_spec = pltpu.PrefetchScalarGridSpec(
        num_scalar_prefetch=1,
        grid=(b, N_PAIRS, nt),
        in_specs=[
            pl.BlockSpec((1, 2, 1, HEAD_PAD, t), lambda bi, pi, qi, pt: (bi, pi, qi, 0, 0)),
            pl.BlockSpec((1, 2, s, HEAD_PAD), lambda bi, pi, qi, pt: (bi, pi, 0, 0)),
            pl.BlockSpec((1, 1, nt, LANE, t), lambda bi, pi, qi, pt: (bi, pi, 0, 0, 0)),
            whole((bd, N_HEADS, KV_LORA)),
            whole((bd, N_HEADS, QK_ROPE)),
            whole((bd, 1, KV_LORA)),
            whole((bd, 1, QK_ROPE)),
            pl.BlockSpec(memory_space=pl.ANY),
            pl.BlockSpec(memory_space=pl.ANY),
        ],
        out_specs=[
            pl.BlockSpec((1, 1, t, LANE), lambda bi, pi, qi, pt: (bi, pi, qi, 0)),
            whole((bd, N_HEADS, KV_LORA)),
        ],
        scratch_shapes=[
            pltpu.VMEM((CACHE_SLOTS, rows, KV_LORA), F32),
            pltpu.VMEM((CACHE_SLOTS, QK_ROPE, rows), F32),
            pltpu.SemaphoreType.DMA((2, CACHE_SLOTS)),
            pltpu.VMEM((N_HEADS, 1), F32),
            pltpu.VMEM((N_HEADS, 1), F32),
            pltpu.VMEM((N_HEADS, KV_LORA), F32),
        ],
    )
    return pl.pallas_call(
        kern,
        grid_spec=grid_spec,
        out_shape=[jax.ShapeDtypeStruct((b, N_PAIRS, s, LANE), F32),
                   jax.ShapeDtypeStruct((bd, N_HEADS, KV_LORA), F32)],
        compiler_params=_cparams(("arbitrary", "arbitrary", "arbitrary")),
        name="attn_fused",
    )(page_table.reshape(-1), qt, k, vt, qlat, qpe, c_new, k_new, cache_kv, cache_kr_t)


def _absorb_kernel(qt_ref, wlat_ref, wpe_ref, qlat_ref, qpe_ref):
    for h in range(N_HEADS):
        qt = qt_ref[h]
        qlat_ref[h] = _dot_tn(qt, wlat_ref[h])
        qpe_ref[h] = _dot_tn(qt, wpe_ref[...])


def _absorb(qt, wts):
    _, _, n = qt.shape
    return pl.pallas_call(
        _absorb_kernel,
        out_shape=[jax.ShapeDtypeStruct((N_HEADS, n, KV_LORA), F32),
                   jax.ShapeDtypeStruct((N_HEADS, n, QK_ROPE), F32)],
        name="absorb_q",
    )(qt, wts["w_uk_t"], wts["w_pe_sel"])


def _value_up_kernel(olat_ref, wv_ref, o_ref):
    w = 2 * KV_LORA
    for p in range(N_PAIRS):
        o_ref[0, p] = _dot(olat_ref[:, p * w:(p + 1) * w].astype(BF16), wv_ref[p])


def _value_up(olat2d, wts):
    n = olat2d.shape[0]
    return pl.pallas_call(
        _value_up_kernel,
        out_shape=jax.ShapeDtypeStruct((1, N_PAIRS, n, LANE), F32),
        name="value_up",
    )(olat2d, wts["w_uv_pair"])


def _merge_kernel(x_ref, att_ref, rec_ref, ag_ref, rgn_ref, wout_ref, g1_ref, b1_ref, wr_ref, bg_ref, be_ref,
                  x1_ref, gsel_ref, gate_ref, sel_ref, cnt_ref):
    tm = x_ref.shape[1]
    att = [att_ref[0, p] for p in range(N_PAIRS)]
    ss = att[0] * att[0]
    for p in range(1, N_PAIRS):
        ss = ss + att[p] * att[p]
    inv = lax.rsqrt(jnp.sum(ss, axis=-1, keepdims=True) / (N_PAIRS * LANE) + RMS_EPS)
    parts = [(att[p] * inv * ag_ref[:, p * LANE:(p + 1) * LANE]).astype(BF16) for p in range(N_PAIRS)]
    parts.append(_rmsnorm(rec_ref[0], rgn_ref[...]).astype(BF16))
    mixed = jnp.concatenate(parts, axis=-1)
    mix = _dot(mixed, wout_ref[...])
    x1 = _layernorm(ALPHA * x_ref[0] + mix, g1_ref[...], b1_ref[...])
    x1_ref[0] = x1

    n_r = wr_ref.shape[0] // 2
    x_hi = x1.astype(BF16)
    x_lo = (x1 - x_hi.astype(F32)).astype(BF16)
    both = _dot_nt(wr_ref[...], x_hi)
    lt = (both[:n_r] + both[n_r:]) + _dot_nt(wr_ref[0:n_r, :], x_lo)
    g = [lt[k:k + 1, :] for k in range(N_GROUPS)]
    gmax = functools.reduce(jnp.maximum, g)
    ex = [jnp.exp(gk - gmax) for gk in g]
    den = functools.reduce(lambda p, q: p + q, ex)
    best = g[0] + bg_ref[0:1, :]
    idx = jnp.zeros((1, tm), I32)
    for k in range(1, N_GROUPS):
        cand = g[k] + bg_ref[k:k + 1, :]
        upd = cand > best
        idx = jnp.where(upd, k, idx)
        best = jnp.where(upd, cand, best)
    gp = ex[0]
    e_sel = lt[SUBLANE:SUBLANE + EXPERTS_PER_GROUP, :]
    e_bias = jnp.broadcast_to(be_ref[0:EXPERTS_PER_GROUP, :], (EXPERTS_PER_GROUP, tm))
    for k in range(1, N_GROUPS):
        hit = idx == k
        lo = SUBLANE + k * EXPERTS_PER_GROUP
        gp = jnp.where(hit, ex[k], gp)
        e_sel = jnp.where(hit, lt[lo:lo + EXPERTS_PER_GROUP, :], e_sel)
        e_bias = jnp.where(hit, be_ref[k * EXPERTS_PER_GROUP:(k + 1) * EXPERTS_PER_GROUP, :], e_bias)
    g_prob = gp / den
    sc = e_sel + e_bias
    sub = lax.broadcasted_iota(I32, (EXPERTS_PER_GROUP, tm), 0)
    m1 = jnp.max(sc, axis=0, keepdims=True)
    i1 = jnp.min(jnp.where(sc == m1, sub, EXPERTS_PER_GROUP), axis=0, keepdims=True)
    mask1 = sub == i1
    sc2 = jnp.where(mask1, -jnp.inf, sc)
    m2 = jnp.max(sc2, axis=0, keepdims=True)
    i2 = jnp.min(jnp.where(sc2 == m2, sub, EXPERTS_PER_GROUP), axis=0, keepdims=True)
    mask2 = sub == i2
    v1 = jnp.sum(jnp.where(mask1, e_sel, 0.0), axis=0, keepdims=True)
    v2 = jnp.sum(jnp.where(mask2, e_sel, 0.0), axis=0, keepdims=True)
    vm = jnp.maximum(v1, v2)
    e1 = jnp.exp(v1 - vm)
    e2 = jnp.exp(v2 - vm)
    esum = e1 + e2
    gate = g_prob * (jnp.where(mask1, e1 / esum, 0.0) + jnp.where(mask2, e2 / esum, 0.0))
    gsel_ref[0] = idx
    gate_ref[0] = gate
    sel = jnp.where(jnp.logical_or(mask1, mask2), 1.0, 0.0)
    sel_ref[0] = sel
    for k in range(N_GROUPS):
        ck = jnp.sum(jnp.where(idx == k, sel, 0.0), axis=-1, keepdims=True)
        cnt_ref[0, k * EXPERTS_PER_GROUP:(k + 1) * EXPERTS_PER_GROUP, :] = jnp.broadcast_to(
            ck, (EXPERTS_PER_GROUP, LANE))


def _merge(x, att, rec, wts, tm):
    b, s, d = x.shape
    nt = s // tm
    mw = wts["w_out"].shape[0]
    n_r = wts["w_router_t"].shape[0]
    full = lambda shape: pl.BlockSpec(shape, lambda bi, si: (0,) * len(shape))
    return pl.pallas_call(
        _merge_kernel,
        grid=(b, nt),
        in_specs=[
            pl.BlockSpec((1, tm, d), lambda bi, si: (bi, si, 0)),
            pl.BlockSpec((1, N_PAIRS, tm, LANE), lambda bi, si: (bi, 0, si, 0)),
            pl.BlockSpec((1, tm, REC_WIDTH), lambda bi, si: (bi, si, 0)),
            full((1, N_PAIRS * LANE)), full((1, REC_WIDTH)), full((mw, d)), full((1, d)), full((1, d)),
            full((n_r, d)), full((N_GROUPS, 1)), full((N_GROUPS * EXPERTS_PER_GROUP, 1)),
        ],
        out_specs=[
            pl.BlockSpec((1, tm, d), lambda bi, si: (bi, si, 0)),
            pl.BlockSpec((1, 1, tm), lambda bi, si: (bi * nt + si, 0, 0)),
            pl.BlockSpec((1, EXPERTS_PER_GROUP, tm), lambda bi, si: (bi * nt + si, 0, 0)),
            pl.BlockSpec((1, EXPERTS_PER_GROUP, tm), lambda bi, si: (bi * nt + si, 0, 0)),
            pl.BlockSpec((1, N_EXPERTS, LANE), lambda bi, si: (bi * nt + si, 0, 0)),
        ],
        out_shape=[
            jax.ShapeDtypeStruct((b, s, d), F32),
            jax.ShapeDtypeStruct((b * nt, 1, tm), I32),
            jax.ShapeDtypeStruct((b * nt, EXPERTS_PER_GROUP, tm), F32),
            jax.ShapeDtypeStruct((b * nt, EXPERTS_PER_GROUP, tm), F32),
            jax.ShapeDtypeStruct((b * nt, N_EXPERTS, LANE), F32),
        ],
        compiler_params=_cparams(("parallel", "parallel")),
        name="merge_router",
    )(x, att, rec, wts["att_out_g"], wts["rec_out_g"], wts["w_out"], wts["ln1_g"], wts["ln1_b"],
      wts["w_router_t"], wts["b_group"], wts["b_expert"])


def _moe_kernel(nch_ref, x1_ref, gsel_ref, gate_ref, wgu_ref, wd_ref, g2_ref, b2_ref, o_ref, xb_ref, tri_ref,
                *, ch):
    tm = x1_ref.shape[0]
    ti = pl.program_id(0)
    gi = pl.program_id(1)
    hi = pl.program_id(2)
    n_half = wgu_ref.shape[2]

    @pl.when(jnp.logical_and(jnp.logical_and(ti == 0, gi == 0), hi == 0))
    def _():
        _build_tri(tri_ref)

    @pl.when(jnp.logical_and(gi == 0, hi == 0))
    def _():
        xb_ref[...] = x1_ref[...].astype(BF16)
        o_ref[...] = jnp.zeros_like(o_ref)

    in_group = gsel_ref[0] == gi
    member = jnp.broadcast_to(jnp.where(in_group, 1.0, 0.0), (SUBLANE, tm)).astype(BF16)
    before = _dot(member, tri_ref[...])
    rank = jnp.where(in_group, before[0:1, :].astype(I32), -1)
    gate = gate_ref[0]
    g_hi = gate.astype(BF16).astype(F32)
    g_mid = (gate - g_hi).astype(BF16).astype(F32)
    g_lo = (gate - g_hi) - g_mid
    n_terms = 3
    gate_terms = jnp.concatenate(
        [g_hi, g_mid, g_lo, jnp.zeros((LANE - n_terms * EXPERTS_PER_GROUP, tm), F32)], axis=0).astype(BF16)

    def chunk(c, carry, first_expert):
        slot_id = lax.broadcasted_iota(I32, (ch, tm), 0) + c * ch
        onehot_b = jnp.where(slot_id == rank, 1.0, 0.0).astype(BF16)
        xg = _dot(onehot_b, xb_ref[...]).astype(BF16)
        gt = _dot_nt(onehot_b, gate_terms)
        gc = gt
        for k in range(1, n_terms):
            gc = gc + pltpu.roll(gt, LANE - k * EXPERTS_PER_GROUP, 1)
        acc = jnp.zeros((ch, o_ref.shape[1]), F32)
        for jj in range(n_half):
            j = first_expert + jj
            gu = _dot(xg, wgu_ref[0, 0, jj].astype(BF16))
            hid = jax.nn.silu(gu[:, :D_EXPERT]) * gu[:, D_EXPERT:]
            acc = acc + gc[:, j:j + 1] * _dot(hid.astype(BF16), wd_ref[0, 0, jj].astype(BF16))
        o_ref[...] += _dot_tn(onehot_b, acc.astype(BF16))
        return carry

    for half in range(EXPERTS_PER_GROUP // n_half):
        @pl.when(hi == half)
        def _(half=half):
            lax.fori_loop(0, nch_ref[ti * N_GROUPS + gi], functools.partial(chunk, first_expert=half * n_half), 0)

    @pl.when(jnp.logical_and(gi == N_GROUPS - 1, hi == pl.num_programs(2) - 1))
    def _():
        o_ref[...] = _layernorm(ALPHA * x1_ref[...] + o_ref[...], g2_ref[...], b2_ref[...])


def _moe(x1, gsel, gate, nch, wts, tm, ch):
    n, d = x1.shape
    nt = n // tm
    e2 = 2 * D_EXPERT
    halves = 2
    per_half = EXPERTS_PER_GROUP // halves
    w_gu = wts["w_gate_up"].reshape(N_GROUPS, halves, per_half, d, e2)
    w_dn = wts["w_down"].reshape(N_GROUPS, halves, per_half, D_EXPERT, d)
    grid_spec = pltpu.PrefetchScalarGridSpec(
        num_scalar_prefetch=1,
        grid=(nt, N_GROUPS, halves),
        in_specs=[
            pl.BlockSpec((tm, d), lambda ti, gi, hi, nc: (ti, 0)),
            pl.BlockSpec((1, 1, tm), lambda ti, gi, hi, nc: (ti, 0, 0)),
            pl.BlockSpec((1, EXPERTS_PER_GROUP, tm), lambda ti, gi, hi, nc: (ti, 0, 0)),
            pl.BlockSpec((1, 1, per_half, d, e2), lambda ti, gi, hi, nc: (gi, hi, 0, 0, 0)),
            pl.BlockSpec((1, 1, per_half, D_EXPERT, d), lambda ti, gi, hi, nc: (gi, hi, 0, 0, 0)),
            pl.BlockSpec((1, d), lambda ti, gi, hi, nc: (0, 0)),
            pl.BlockSpec((1, d), lambda ti, gi, hi, nc: (0, 0)),
        ],
        out_specs=pl.BlockSpec((tm, d), lambda ti, gi, hi, nc: (ti, 0)),
        scratch_shapes=[pltpu.VMEM((tm, d), BF16), pltpu.VMEM((tm, tm), BF16)],
    )
    return pl.pallas_call(
        functools.partial(_moe_kernel, ch=ch),
        grid_spec=grid_spec,
        out_shape=jax.ShapeDtypeStruct((n, d), F32),
        compiler_params=_cparams(("arbitrary", "arbitrary", "arbitrary")),
        name="moe",
    )(nch, x1, gsel, gate, w_gu, w_dn, wts["ln2_g"], wts["ln2_b"])


def _build_tri(tri_ref):
    tm = tri_ref.shape[0]
    r = lax.broadcasted_iota(I32, (tm, tm), 0)
    c = lax.broadcasted_iota(I32, (tm, tm), 1)
    tri_ref[...] = jnp.where(r < c, 1.0, 0.0).astype(BF16)


def _sorted_positions(gsel, sel, start_col, tri_ref):
    member = jnp.concatenate([jnp.where(gsel == g, sel, 0.0) for g in range(N_GROUPS)], axis=0)
    before = _dot(member.astype(BF16), tri_ref[...])
    routed = member > 0.5
    pos = jnp.where(routed, start_col + before, -1.0)
    pos_a = jnp.max(pos, axis=0, keepdims=True)
    pos_b = jnp.sum(jnp.where(routed, start_col + before, 0.0), axis=0, keepdims=True) - pos_a
    return pos, pos_a.astype(I32), pos_b.astype(I32)


def _exact_terms(v):
    hi = v.astype(BF16).astype(F32)
    mid = (v - hi).astype(BF16).astype(F32)
    lo = (v - hi) - mid
    row = lax.broadcasted_iota(I32, (LANE, v.shape[1]), 0)
    return jnp.where(row == 0, hi, jnp.where(row == 1, mid, jnp.where(row == 2, lo, 0.0))).astype(BF16)


def _dispatch_kernel(udst_ref, nun_ref, nch_ref, tstart_ref, tunits_ref, rest_ref, x1_ref, gsel_ref, sel_ref,
                     gate_ref, start_ref, xs_hbm, gs_hbm, xb_ref, tri_ref, xsrt, gsrt, sem, *, rch, ech, umax):
    i = pl.program_id(0)
    tm = x1_ref.shape[0]

    @pl.when(i == 0)
    def _():
        _build_tri(tri_ref)
        unit = pl.ds(0, ROW_UNIT)
        xsrt[0, unit, :] = jnp.zeros((ROW_UNIT, xsrt.shape[2]), BF16)
        gsrt[0, unit, :] = jnp.zeros((ROW_UNIT, LANE), F32)

        def tail_copies(e, k):
            dst = pl.ds(pl.multiple_of(tstart_ref[e] + k * ROW_UNIT, ROW_UNIT), ROW_UNIT)
            return (pltpu.make_async_copy(xsrt.at[0, unit], xs_hbm.at[dst], sem.at[0, 0]),
                    pltpu.make_async_copy(gsrt.at[0, unit], gs_hbm.at[dst], sem.at[1, 0]))

        def per_expert(fn):
            def over_experts(e, carry):
                def over_units(k, c):
                    for cp in tail_copies(e, k):
                        fn(cp)
                    return c
                return lax.fori_loop(0, tunits_ref[e], over_units, carry)
            lax.fori_loop(0, N_EXPERTS, over_experts, 0)

        per_expert(lambda cp: cp.start())
        per_expert(lambda cp: cp.wait())

        blk = pl.ds(0, ech)
        xsrt[0, blk, :] = jnp.zeros((ech, xsrt.shape[2]), BF16)
        gsrt[0, blk, :] = jnp.zeros((ech, LANE), F32)

        def rest_copies(k):
            dst = pl.ds(pl.multiple_of(rest_ref[0] + k * ech, ech), ech)
            return (pltpu.make_async_copy(xsrt.at[0, blk], xs_hbm.at[dst], sem.at[0, 0]),
                    pltpu.make_async_copy(gsrt.at[0, blk], gs_hbm.at[dst], sem.at[1, 0]))

        def start_rest(k, c):
            for cp in rest_copies(k):
                cp.start()
            return c

        def wait_rest(k, c):
            for cp in rest_copies(k):
                cp.wait()
            return c

        lax.fori_loop(0, rest_ref[1], start_rest, 0)
        lax.fori_loop(0, rest_ref[1], wait_rest, 0)

    xb_ref[...] = x1_ref[...].astype(BF16)
    gsel = gsel_ref[0]
    pos, pos_a, pos_b = _sorted_positions(gsel, sel_ref[0], start_ref[0], tri_ref)
    gate32 = jnp.concatenate([jnp.where(gsel == g, gate_ref[0], 0.0) for g in range(N_GROUPS)], axis=0)
    gate_a = jnp.sum(jnp.where(pos == pos_a.astype(F32), gate32, 0.0), axis=0, keepdims=True)
    gate_b = jnp.sum(jnp.where(pos == pos_b.astype(F32), gate32, 0.0), axis=0, keepdims=True)
    terms_a = _exact_terms(gate_a)
    terms_b = _exact_terms(gate_b)
    slot = lax.rem(i, 2)

    def chunk(c, carry):
        r0 = pl.multiple_of(c * rch, rch)
        rid = lax.broadcasted_iota(I32, (rch, tm), 0) + r0
        hit_a = rid == pos_a
        hit_b = rid == pos_b
        onehot = jnp.where(jnp.logical_or(hit_a, hit_b), 1.0, 0.0).astype(BF16)
        xsrt[slot, pl.ds(r0, rch), :] = _dot(onehot, xb_ref[...]).astype(BF16)
        gt = (_dot_nt(jnp.where(hit_a, 1.0, 0.0).astype(BF16), terms_a)
              + _dot_nt(jnp.where(hit_b, 1.0, 0.0).astype(BF16), terms_b))
        gsrt[slot, pl.ds(r0, rch), :] = (gt + pltpu.roll(gt, LANE - 1, 1)) + pltpu.roll(gt, LANE - 2, 1)
        return carry

    lax.fori_loop(0, nch_ref[i], chunk, 0)

    def for_units(step, buf, fn):
        def body(u, carry):
            src = pl.ds(pl.multiple_of(u * ROW_UNIT, ROW_UNIT), ROW_UNIT)
            dst = pl.ds(pl.multiple_of(udst_ref[step * umax + u], ROW_UNIT), ROW_UNIT)
            fn(pltpu.make_async_copy(xsrt.at[buf, src], xs_hbm.at[dst], sem.at[0, buf]), 0)
            fn(pltpu.make_async_copy(gsrt.at[buf, src], gs_hbm.at[dst], sem.at[1, buf]), 1)
            return carry
        lax.fori_loop(0, nun_ref[step], body, 0)

    @pl.when(i > 0)
    def _():
        for_units(i - 1, 1 - slot, lambda cp, prio: cp.wait())

    for_units(i, slot, lambda cp, prio: cp.start(priority=prio))

    @pl.when(i == pl.num_programs(0) - 1)
    def _():
        for_units(i, slot, lambda cp, prio: cp.wait())


def _dispatch(x1, gsel, sel, gate, start_col, tables, tails, total_rows, tm, rch, ech, umax):
    n, d = x1.shape
    nt = n // tm
    rows_max = umax * ROW_UNIT
    assert rows_max >= ech, "the sorted-tile buffer doubles as the zero source of one expert chunk"
    unit_dst, n_units, n_chunks = tables
    tail_start, tail_units, rest = tails
    grid_spec = pltpu.PrefetchScalarGridSpec(
        num_scalar_prefetch=6,
        grid=(nt,),
        in_specs=[
            pl.BlockSpec((tm, d), lambda i, *_: (i, 0)),
            pl.BlockSpec((1, 1, tm), lambda i, *_: (i, 0, 0)),
            pl.BlockSpec((1, EXPERTS_PER_GROUP, tm), lambda i, *_: (i, 0, 0)),
            pl.BlockSpec((1, EXPERTS_PER_GROUP, tm), lambda i, *_: (i, 0, 0)),
            pl.BlockSpec((1, N_EXPERTS, 1), lambda i, *_: (i, 0, 0)),
        ],
        out_specs=[pl.BlockSpec(memory_space=pl.ANY), pl.BlockSpec(memory_space=pl.ANY)],
        scratch_shapes=[
            pltpu.VMEM((tm, d), BF16),
            pltpu.VMEM((tm, tm), BF16),
            pltpu.VMEM((2, rows_max, d), BF16),
            pltpu.VMEM((2, rows_max, LANE), F32),
            pltpu.SemaphoreType.DMA((2, 2)),
        ],
    )
    return pl.pallas_call(
        functools.partial(_dispatch_kernel, rch=rch, ech=ech, umax=umax),
        grid_spec=grid_spec,
        out_shape=[jax.ShapeDtypeStruct((total_rows, d), BF16), jax.ShapeDtypeStruct((total_rows, LANE), F32)],
        compiler_params=_cparams(("arbitrary",)),
        name="moe_dispatch",
    )(unit_dst, n_units, n_chunks, tail_start, tail_units, rest, x1, gsel, sel, gate, start_col)


def _expert_kernel(ce_ref, x_ref, g_ref, wgu_ref, wd_ref, y_ref, wgu_b, wd_b):
    c = pl.program_id(0)
    expert = ce_ref[c]

    @pl.when(jnp.logical_and(expert >= 0, jnp.logical_or(c == 0, expert != ce_ref[jnp.maximum(c - 1, 0)])))
    def _():
        wgu_b[...] = wgu_ref[0].astype(BF16)
        wd_b[...] = wd_ref[0].astype(BF16)

    @pl.when(expert >= 0)
    def _():
        half = x_ref.shape[0] // 2
        for h in range(2):
            rows = pl.ds(h * half, half)
            gu = _dot(x_ref[rows, :], wgu_b[...])
            hid = jax.nn.silu(gu[:, :D_EXPERT]) * gu[:, D_EXPERT:]
            y_ref[rows, :] = (g_ref[rows, 0:1] * _dot(hid.astype(BF16), wd_b[...])).astype(BF16)

    @pl.when(ce_ref[c] < 0)
    def _():
        y_ref[...] = jnp.zeros_like(y_ref)


def _experts(xs, gs, chunk_expert, wts, rch):
    total_rows, d = xs.shape
    e2 = 2 * D_EXPERT
    w_gu = wts["w_gate_up"].reshape(N_EXPERTS, d, e2)
    w_dn = wts["w_down"].reshape(N_EXPERTS, D_EXPERT, d)
    grid_spec = pltpu.PrefetchScalarGridSpec(
        num_scalar_prefetch=1,
        grid=(total_rows // rch,),
        in_specs=[
            pl.BlockSpec((rch, d), lambda c, ce: (jnp.where(ce[c] >= 0, c, 0), 0)),
            pl.BlockSpec((rch, LANE), lambda c, ce: (jnp.where(ce[c] >= 0, c, 0), 0)),
            pl.BlockSpec((1, d, e2), lambda c, ce: (jnp.maximum(ce[c], 0), 0, 0)),
            pl.BlockSpec((1, D_EXPERT, d), lambda c, ce: (jnp.maximum(ce[c], 0), 0, 0)),
        ],
        out_specs=pl.BlockSpec((rch, d), lambda c, ce: (c, 0)),
        scratch_shapes=[pltpu.VMEM((d, e2), BF16), pltpu.VMEM((D_EXPERT, d), BF16)],
    )
    return pl.pallas_call(
        _expert_kernel,
        grid_spec=grid_spec,
        out_shape=jax.ShapeDtypeStruct((total_rows, d), BF16),
        compiler_params=_cparams(("arbitrary",)),
        name="moe_experts",
    )(chunk_expert, xs, gs, w_gu, w_dn)


def _combine_kernel(udst_ref, nun_ref, nch_ref, x1_ref, gsel_ref, sel_ref, start_ref, ys_hbm, g2_ref, b2_ref,
                    o_ref, tri_ref, ysrt, sem, *, rch, umax):
    i = pl.program_id(0)
    tm = x1_ref.shape[0]

    @pl.when(i == 0)
    def _():
        _build_tri(tri_ref)
        ysrt[...] = jnp.zeros_like(ysrt)

    def for_units(step, buf, fn):
        n = nun_ref[step]
        for parity in range(2):
            def body(k, carry, parity=parity):
                u = 2 * k + parity
                src = pl.ds(pl.multiple_of(udst_ref[step * umax + u], ROW_UNIT), ROW_UNIT)
                dst = pl.ds(pl.multiple_of(u * ROW_UNIT, ROW_UNIT), ROW_UNIT)
                fn(pltpu.make_async_copy(ys_hbm.at[src], ysrt.at[buf, dst], sem.at[buf]), parity)
                return carry
            lax.fori_loop(0, (n + 1 - parity) // 2, body, 0)

    slot = lax.rem(i, 2)

    @pl.when(i == 0)
    def _():
        for_units(0, 0, lambda cp, prio: cp.start(priority=prio))

    @pl.when(i + 1 < pl.num_programs(0))
    def _():
        for_units(i + 1, 1 - slot, lambda cp, prio: cp.start(priority=prio))

    _, pos_a, pos_b = _sorted_positions(gsel_ref[0], sel_ref[0], start_ref[0], tri_ref)
    o_ref[...] = jnp.zeros_like(o_ref)
    for_units(i, slot, lambda cp, prio: cp.wait())

    def chunk(c, carry):
        r0 = pl.multiple_of(c * rch, rch)
        rid = lax.broadcasted_iota(I32, (rch, tm), 0) + r0
        onehot = jnp.where(jnp.logical_or(rid == pos_a, rid == pos_b), 1.0, 0.0).astype(BF16)
        o_ref[...] += _dot_tn(onehot, ysrt[slot, pl.ds(r0, rch), :])
        return carry

    lax.fori_loop(0, nch_ref[i], chunk, 0)
    o_ref[...] = _layernorm(ALPHA * x1_ref[...] + o_ref[...], g2_ref[...], b2_ref[...])


def _combine(x1, gsel, sel, start_col, ys, tables, wts, tm, rch, umax):
    n, d = x1.shape
    nt = n // tm
    unit_dst, n_units, n_chunks = tables
    grid_spec = pltpu.PrefetchScalarGridSpec(
        num_scalar_prefetch=3,
        grid=(nt,),
        in_specs=[
            pl.BlockSpec((tm, d), lambda i, *_: (i, 0)),
            pl.BlockSpec((1, 1, tm), lambda i, *_: (i, 0, 0)),
            pl.BlockSpec((1, EXPERTS_PER_GROUP, tm), lambda i, *_: (i, 0, 0)),
            pl.BlockSpec((1, N_EXPERTS, 1), lambda i, *_: (i, 0, 0)),
            pl.BlockSpec(memory_space=pl.ANY),
            pl.BlockSpec((1, d), lambda i, *_: (0, 0)),
            pl.BlockSpec((1, d), lambda i, *_: (0, 0)),
        ],
        out_specs=pl.BlockSpec((tm, d), lambda i, *_: (i, 0)),
        scratch_shapes=[
            pltpu.VMEM((tm, tm), BF16),
            pltpu.VMEM((2, umax * ROW_UNIT, d), BF16),
            pltpu.SemaphoreType.DMA((2,)),
        ],
    )
    return pl.pallas_call(
        functools.partial(_combine_kernel, rch=rch, umax=umax),
        grid_spec=grid_spec,
        out_shape=jax.ShapeDtypeStruct((n, d), F32),
        compiler_params=_cparams(("arbitrary",)),
        name="moe_combine",
    )(unit_dst, n_units, n_chunks, x1, gsel, sel, start_col, ys, wts["ln2_g"], wts["ln2_b"])


def _routing_tables(cnt, tm, rch, ech):
    nt = cnt.shape[0]
    seg = (cnt + (ROW_UNIT - 1)) // ROW_UNIT * ROW_UNIT
    start = jnp.cumsum(seg, axis=1) - seg
    rows = seg.sum(axis=1)
    per_expert = seg.sum(axis=0)
    region = (per_expert + (ech - 1)) // ech * ech
    region_start = jnp.cumsum(region) - region
    seg_dst = region_start[None, :] + jnp.cumsum(seg, axis=0) - seg
    rows_max = -(-(TOP_K * tm + N_EXPERTS * (ROW_UNIT - 1)) // rch) * rch
    umax = rows_max // ROW_UNIT
    total_rows = -(-(TOP_K * tm * nt + nt * N_EXPERTS * (ROW_UNIT - 1) + N_EXPERTS * (ech - 1)) // ech) * ech
    u_row = (jnp.arange(umax, dtype=I32) * ROW_UNIT)[None, :, None]
    in_seg = jnp.logical_and(u_row >= start[:, None, :], u_row < (start + seg)[:, None, :])
    unit_dst = jnp.sum(jnp.where(in_seg, (seg_dst - start)[:, None, :] + u_row, 0), axis=-1)
    c_row = (jnp.arange(total_rows // ech, dtype=I32) * ech)[:, None]
    in_region = jnp.logical_and(c_row >= region_start[None, :], c_row < (region_start + per_expert)[None, :])
    chunk_expert = jnp.sum(jnp.where(in_region, jnp.arange(N_EXPERTS, dtype=I32)[None, :] + 1, 0), axis=-1) - 1
    tables = (unit_dst.reshape(-1).astype(I32), (rows // ROW_UNIT).astype(I32),
              ((rows + (rch - 1)) // rch).astype(I32))
    used_rows = region.sum()
    tails = ((region_start + per_expert).astype(I32), ((region - per_expert) // ROW_UNIT).astype(I32),
             jnp.stack([used_rows, (total_rows - used_rows) // ech]).astype(I32))
    return tables, tails, chunk_expert.astype(I32), start.astype(F32).reshape(nt, N_EXPERTS, 1), total_rows, umax


def _merge_and_routed_ffn(x, att, rec, wts, tm_merge, tm, rch):
    b, s, d = x.shape
    x1, gsel, gate, sel, cnt = _merge(x, att, rec, wts, tm_merge)
    n = b * s
    nt = n // tm
    f = tm // tm_merge
    regroup = lambda a: a.reshape(nt, f, EXPERTS_PER_GROUP, tm_merge).transpose(0, 2, 1, 3).reshape(
        nt, EXPERTS_PER_GROUP, tm)
    gsel, gate, sel = gsel.reshape(nt, 1, tm), regroup(gate), regroup(sel)
    counts = cnt[:, :, 0].reshape(nt, f, N_EXPERTS).sum(axis=1).astype(I32)
    ech = CH_EXPERT
    tables, tails, chunk_expert, start_col, total_rows, umax = _routing_tables(counts, tm, rch, ech)
    x1 = x1.reshape(n, d)
    xs, gs = _dispatch(x1, gsel, sel, gate, start_col, tables, tails, total_rows, tm, rch, ech, umax)
    ys = _experts(xs, gs, chunk_expert, wts, ech)
    y = _combine(x1, gsel, sel, start_col, ys, tables, wts, tm, rch, umax)
    return y.reshape(b, s, d)


def _merge_and_ffn(x, att, rec, wts, tm_merge, tm_moe, ch):
    b, s, d = x.shape
    x1, gsel, gate, _, cnt = _merge(x, att, rec, wts, tm_merge)
    n = b * s
    nt = n // tm_moe
    f = tm_moe // tm_merge
    gsel = gsel.reshape(nt, 1, tm_moe)
    gate = gate.reshape(nt, f, EXPERTS_PER_GROUP, tm_merge).transpose(0, 2, 1, 3).reshape(nt, EXPERTS_PER_GROUP, tm_moe)
    per_expert = cnt[:, :, 0].reshape(nt, f, N_GROUPS, EXPERTS_PER_GROUP)
    counts = (per_expert.sum(axis=(1, 3)) / TOP_K).astype(I32)
    nch = ((counts + (ch - 1)) // ch).reshape(-1)
    y = _moe(x1.reshape(n, d), gsel, gate, nch, wts, tm_moe, ch)
    return y.reshape(b, s, d)


def _swap_halves(w):
    half = QK_ROPE // 2
    return jnp.concatenate([w[..., half:], w[..., :half]], axis=-1)


def _prep_weights(w_in, q_norm_g, w_uq, kv_norm_g, w_uk, w_uv, conv_w, conv_b, w_rg, b_rg, w_ig, b_ig,
                  lru_lambda, att_out_g, rec_out_g, w_out, ln1_g, ln1_b, w_group, b_group, w_expert,
                  b_expert, w_gate_up, w_down, ln2_g, ln2_b):
    d = w_in.shape[0]
    o1, o2, o3, o4 = Q_LORA, Q_LORA + KV_LORA, Q_LORA + KV_LORA + QK_ROPE, Q_LORA + KV_LORA + QK_ROPE + REC_WIDTH
    w_kpe = w_in[:, o2:o3]
    pad_lo = jnp.zeros((d, QK_NOPE), F32)
    pad_hi = jnp.zeros((d, HEAD_PAD - QK_NOPE - QK_ROPE), F32)
    w_in_ext = jnp.concatenate([
        w_in[:, :o2],
        pad_lo, w_kpe, pad_hi,
        pad_lo, _swap_halves(w_kpe), pad_hi,
        w_in[:, o3:o4], w_in[:, o4:],
    ], axis=1).astype(BF16)

    nope, pe = w_uq[..., :QK_NOPE], w_uq[..., QK_NOPE:]
    zq = lambda n: jnp.zeros((Q_LORA, N_HEADS, n), F32)
    q_main = jnp.concatenate([nope, pe, zq(HEAD_PAD - QK_NOPE - QK_ROPE)], axis=-1)
    q_swap = jnp.concatenate([zq(QK_NOPE), _swap_halves(pe), zq(HEAD_PAD - QK_NOPE - QK_ROPE)], axis=-1)
    w_uq_t = jnp.concatenate([q_main.reshape(Q_LORA, -1), q_swap.reshape(Q_LORA, -1)], axis=1).T.astype(BF16)

    k_pad = jnp.concatenate([w_uk, jnp.zeros((KV_LORA, N_HEADS, HEAD_PAD - QK_NOPE), F32)], axis=-1)
    w_uk_pad = k_pad.reshape(KV_LORA, -1).astype(BF16)
    w_uv_t = w_uv.reshape(KV_LORA, -1).T.astype(BF16)

    w_uk_t = jnp.concatenate([w_uk.transpose(1, 2, 0),
                              jnp.zeros((N_HEADS, HEAD_PAD - QK_NOPE, KV_LORA), F32)], axis=1).astype(BF16)
    sel = jnp.zeros((HEAD_PAD, QK_ROPE), F32).at[QK_NOPE + jnp.arange(QK_ROPE), jnp.arange(QK_ROPE)].set(1.0)
    w_uv_h = w_uv.transpose(1, 0, 2)
    zero_v = jnp.zeros((KV_LORA, V_HEAD), F32)
    w_uv_pair = jnp.stack([
        jnp.concatenate([jnp.concatenate([w_uv_h[2 * p], zero_v], axis=1),
                         jnp.concatenate([zero_v, w_uv_h[2 * p + 1]], axis=1)], axis=0)
        for p in range(N_PAIRS)]).astype(BF16)

    def block_diag(w):
        eye = jnp.eye(REC_BLOCKS, dtype=F32)
        return jnp.einsum('nde,nm->ndme', w, eye).reshape(REC_WIDTH, REC_WIDTH)

    bd_r, bd_i = block_diag(w_rg), block_diag(w_ig)
    half = REC_WIDTH // 2
    w_gate = jnp.stack([
        jnp.concatenate([bd_r[j * half:(j + 1) * half, j * half:(j + 1) * half],
                         bd_i[j * half:(j + 1) * half, j * half:(j + 1) * half]], axis=1)
        for j in range(2)]).astype(BF16)

    w_router_t = jnp.concatenate([w_group.T, jnp.zeros((SUBLANE - N_GROUPS, d), F32), w_expert.T], axis=0)
    w_router_hi = w_router_t.astype(BF16)
    w_router_lo = (w_router_t - w_router_hi.astype(F32)).astype(BF16)
    w_router_t = jnp.concatenate([w_router_hi, w_router_lo], axis=0)
    row = lambda v: v.reshape(1, -1)
    return {
        "w_in": w_in_ext, "q_norm_g": row(q_norm_g), "w_uq_t": w_uq_t, "kv_norm_g": row(kv_norm_g),
        "w_uk": w_uk_pad, "w_uv_t": w_uv_t, "w_uk_t": w_uk_t, "w_pe_sel": sel.astype(BF16), "w_uv_pair": w_uv_pair,
        "conv_w": conv_w, "conv_b": row(conv_b), "w_gate": w_gate, "b_rg": row(b_rg), "b_ig": row(b_ig),
        "lru_lambda": row(lru_lambda), "att_out_g": row(att_out_g), "rec_out_g": row(rec_out_g),
        "w_out": w_out.astype(BF16), "ln1_g": row(ln1_g), "ln1_b": row(ln1_b),
        "w_router_t": w_router_t, "b_group": b_group.reshape(-1, 1), "b_expert": b_expert.reshape(-1, 1),
        "w_gate_up": w_gate_up, "w_down": w_down,
        "ln2_g": row(ln2_g), "ln2_b": row(ln2_b),
    }


def _rope_tables(pos):
    half = QK_ROPE // 2
    inv = ROPE_THETA ** (-(jnp.arange(half, dtype=F32) * 2.0 / QK_ROPE))
    ang = pos.astype(F32)[:, None] * inv[None, :]
    cos, sin = jnp.cos(ang), jnp.sin(ang)
    t = pos.shape[0]
    cos_t = jnp.concatenate([jnp.ones((t, QK_NOPE), F32), cos, cos,
                             jnp.zeros((t, HEAD_PAD - QK_NOPE - QK_ROPE), F32)], axis=1)
    sin_t = jnp.concatenate([jnp.zeros((t, QK_NOPE), F32), -sin, sin,
                             jnp.zeros((t, HEAD_PAD - QK_NOPE - QK_ROPE), F32)], axis=1)
    return cos_t, sin_t, cos_t.T, sin_t.T


def kernel(x_prompt, x_sample, cache_kv_latent, cache_k_rope, state_conv, state_rec, page_table,
           w_in, q_norm_g, w_uq, kv_norm_g, w_uk, w_uv, conv_w, conv_b, w_rg, b_rg, w_ig, b_ig,
           lru_lambda, att_out_g, rec_out_g, w_out, ln1_g, ln1_b, w_group, b_group, w_expert,
           b_expert, w_gate_up, w_down, ln2_g, ln2_b):
    wts = _prep_weights(w_in, q_norm_g, w_uq, kv_norm_g, w_uk, w_uv, conv_w, conv_b, w_rg, b_rg, w_ig, b_ig,
                        lru_lambda, att_out_g, rec_out_g, w_out, ln1_g, ln1_b, w_group, b_group, w_expert,
                        b_expert, w_gate_up, w_down, ln2_g, ln2_b)
    bp, sp, d = x_prompt.shape
    bd, td, _ = x_sample.shape
    assert td == 1, "the sample path handles one new token per sequence"
    past_len = page_table.shape[1] * PAGE_SIZE
    ko = QK_NOPE

    qt, k, vt, c_p, kpe_blk, rx, rg = _project(x_prompt, _rope_tables(jnp.arange(sp, dtype=I32)), wts,
                                               min(T_ATT, sp))
    xs = x_sample.reshape(1, bd, d)
    qt_s, _, _, c_s, kpe_s_blk, rx_s, rg_s = _project(xs, _rope_tables(jnp.full((bd,), past_len, I32)), wts, bd)
    kpe_s = kpe_s_blk[0, :, ko:ko + QK_ROPE]
    qlat, qpe = _absorb(qt_s[0, :, 0], wts)

    att_p, o_lat = _attention(qt, k, vt, page_table, qlat.transpose(1, 0, 2), qpe.transpose(1, 0, 2),
                              c_s.reshape(bd, 1, KV_LORA), kpe_s.reshape(bd, 1, QK_ROPE),
                              cache_kv_latent, cache_k_rope.transpose(0, 2, 1))

    rec_p, h_p = _lru_prompt(rx, rg, wts, min(TM_LRU, sp))
    tm_moe = min(TM_MOE, bp * sp)
    y_p = _merge_and_routed_ffn(x_prompt, att_p, rec_p, wts, min(TM_MERGE, sp, tm_moe), min(TM_ROUTED, tm_moe),
                                CH_MOE)
    kpe_p = kpe_blk[..., ko:ko + QK_ROPE]
    conv_p = rx[:, sp - (CONV_W - 1):, :]

    att_s = _value_up(o_lat.reshape(bd, N_HEADS * KV_LORA), wts)
    rec_s, conv_s_t, h_s = _lru_step(rx_s[0], rg_s[0], state_conv.transpose(1, 0, 2), state_rec, wts)
    y_s = _merge_and_ffn(xs, att_s, rec_s.reshape(1, bd, REC_WIDTH), wts, bd, bd, min(CH_MOE, bd))

    return (y_p, y_s.reshape(bd, 1, d), c_p, kpe_p, conv_p, h_p.reshape(bp, REC_WIDTH),
            c_s.reshape(bd, 1, KV_LORA), kpe_s.reshape(bd, 1, QK_ROPE), conv_s_t.transpose(1, 0, 2), h_s)
```
